```python
import math
import jax
import jax.numpy as jnp
from jax import lax
import numpy as np

D_MODEL = 1024
BATCH = 32
SEQ = 256
DEPTH = 4
DEC_BATCH = 2
DEC_SEQ = 4096
PAST_LEN = 512

GRID_W = 64
N_MIXERS = 3
EPS = 1e-6
ROPE_BASE = 10000.0
Q_BLOCK = 128
NEG_INF = -1e30

DA_HEADS = 8
DA_QK = 64
DA_V = 2 * DA_QK

NA_HEADS = 16
NA_HD = 64
NA_KH = 8
NA_KW = 16

MLA_HEADS = 16
MLA_NOPE = 64
MLA_ROPE = 32
MLA_V = 64
MLA_Q_RANK = 384
MLA_KV_RANK = 256

PEER_HEADS = 8
PEER_NKEYS = 128
PEER_EXPERTS = PEER_NKEYS * PEER_NKEYS
PEER_DK = 256
PEER_TOPK = 16
PEER_BLOCK = 128

kernel_name = 'hybrid_diffusion_diffattn_natten_mla_peer_step'


def _rms(x, g):
    xf = x.astype(jnp.float32)
    y = xf * lax.rsqrt(jnp.mean(xf * xf, axis=-1, keepdims=True) + EPS)
    return (y * g.astype(jnp.float32)).astype(x.dtype)


def _softmax32(s):
    return jax.nn.softmax(s.astype(jnp.float32), axis=-1)


def _rope_axis(x, pos):
    half = x.shape[-1] // 2
    inv = ROPE_BASE ** (-jnp.arange(half, dtype=jnp.float32) / half)
    ang = pos.astype(jnp.float32)[:, None] * inv[None, :]
    cos = jnp.cos(ang)[None, :, None, :]
    sin = jnp.sin(ang)[None, :, None, :]
    xf = x.astype(jnp.float32)
    x1, x2 = xf[..., :half], xf[..., half:]
    return jnp.concatenate([x1 * cos - x2 * sin, x1 * sin + x2 * cos], axis=-1).astype(x.dtype)


def _rope_2d(x):
    t = jnp.arange(x.shape[1])
    n = x.shape[-1] // 2
    return jnp.concatenate([_rope_axis(x[..., :n], t // GRID_W), _rope_axis(x[..., n:], t % GRID_W)], axis=-1)


def _over_query_blocks(fn, qs):
    B, S = qs[0].shape[:2]
    nb = S // Q_BLOCK
    split = lambda a: jnp.moveaxis(a.reshape((B, nb, Q_BLOCK) + a.shape[2:]), 1, 0)
    out = lax.map(fn, tuple(split(a) for a in qs))
    return jnp.moveaxis(out, 0, 1).reshape((B, S) + out.shape[3:])


def _attend(q, k, v):
    scale = q.shape[-1] ** -0.5
    def blk(qb):
        (a,) = qb
        pr = _softmax32(jnp.einsum('bqhd,bkhd->bhqk', a, k) * scale).astype(v.dtype)
        return jnp.einsum('bhqk,bkhd->bqhd', pr, v)
    return _over_query_blocks(blk, (q,))


def _diff_lambda(p, layer):
    lam_init = 0.8 - 0.6 * math.exp(-0.3 * layer)
    f = lambda a, b: jnp.exp(jnp.sum(a.astype(jnp.float32) * b.astype(jnp.float32)))
    return f(p['lam_q1'], p['lam_k1']) - f(p['lam_q2'], p['lam_k2']) + lam_init, lam_init


def _diff_qkv(p, h):
    B, S, _ = h.shape
    qkv = (h @ p['w_qkv']).reshape(B, S, 3, DA_HEADS, DA_V)
    return qkv[:, :, 0], qkv[:, :, 1], qkv[:, :, 2]


def _diff_attend(q1, q2, k1, k2, v, lam):
    scale = DA_QK ** -0.5
    lam = lam.astype(jnp.float32)
    def blk(qb):
        a, b = qb
        p1 = _softmax32(jnp.einsum('bqhd,bkhd->bhqk', a, k1) * scale)
        p2 = _softmax32(jnp.einsum('bqhd,bkhd->bhqk', b, k2) * scale)
        return jnp.einsum('bhqk,bkhd->bqhd', (p1 - lam * p2).astype(v.dtype), v)
    return _over_query_blocks(blk, (q1, q2))


def _diff_out(p, o, lam_init):
    B, S = o.shape[:2]
    o = _rms(o, p['subln_g']) * (1.0 - lam_init)
    return o.reshape(B, S, DA_HEADS * DA_V) @ p['w_o']


def _diff_ctx(p, layer, h):
    lam, lam_init = _diff_lambda(p, layer)
    q, k, v = _diff_qkv(p, h)
    o = _diff_attend(q[..., :DA_QK], q[..., DA_QK:], k[..., :DA_QK], k[..., DA_QK:], v, lam)
    return _diff_out(p, o, lam_init), (k, v)


def _diff_lat(p, layer, h, k_ctx, v_ctx):
    lam, lam_init = _diff_lambda(p, layer)
    q, k, v = _diff_qkv(p, h)
    q1, q2 = _rope_2d(q[..., :DA_QK]), _rope_2d(q[..., DA_QK:])
    k1 = jnp.concatenate([_rope_2d(k[..., :DA_QK]), k_ctx[..., :DA_QK]], axis=1)
    k2 = jnp.concatenate([_rope_2d(k[..., DA_QK:]), k_ctx[..., DA_QK:]], axis=1)
    vv = jnp.concatenate([v, v_ctx], axis=1)
    o = _diff_attend(q1, q2, k1, k2, vv, lam)
    return _diff_out(p, o, lam_init)


def _na_qkv(p, h):
    B, S, _ = h.shape
    qkv = (h @ p['w_qkv']).reshape(B, S, 3, NA_HEADS, NA_HD)
    return qkv[:, :, 0], qkv[:, :, 1], qkv[:, :, 2]


def _na_ctx(p, h):
    B, S, _ = h.shape
    q, k, v = _na_qkv(p, h)
    o = _attend(q, k, v)
    return o.reshape(B, S, NA_HEADS * NA_HD) @ p['w_o'], (k, v)


def _na_lat(p, h, k_ctx, v_ctx):
    B, S, _ = h.shape
    R = S // GRID_W
    kh = min(NA_KH, R)
    q, k, v = _na_qkv(p, h)
    grid = lambda a: a.reshape(B, R, GRID_W, NA_HEADS, NA_HD)
    qg, kg, vg = grid(q), grid(k), grid(v)
    rows = jnp.arange(R)
    row_idx = jnp.clip(rows - kh // 2, 0, R - kh)[:, None] + jnp.arange(kh)[None, :]
    dr = row_idx - rows[:, None] + (NA_KH - 1)
    cols = jnp.arange(GRID_W)
    col_start = jnp.clip(cols - NA_KW // 2, 0, GRID_W - NA_KW)
    col_mask = (cols[None, :] >= col_start[:, None]) & (cols[None, :] < col_start[:, None] + NA_KW)
    dc = jnp.clip(cols[None, :] - cols[:, None], -(NA_KW - 1), NA_KW - 1) + (NA_KW - 1)
    scale = NA_HD ** -0.5
    n_loc = kh * GRID_W
    rpb = p['rpb'].astype(jnp.float32)

    def row_fn(args):
        qr, ridx, dr_r = args
        k_r = jnp.take(kg, ridx, axis=1)
        v_r = jnp.take(vg, ridx, axis=1)
        bias = rpb[:, dr_r[None, :, None], dc[:, None, :]]
        s_loc = jnp.einsum('bqhd,bkwhd->bhqkw', qr, k_r).astype(jnp.float32) * scale + bias
        s_loc = jnp.where(col_mask[:, None, :], s_loc, NEG_INF)
        s_ctx = jnp.einsum('bqhd,bphd->bhqp', qr, k_ctx).astype(jnp.float32) * scale
        s_all = jnp.concatenate([s_loc.reshape(B, NA_HEADS, GRID_W, n_loc), s_ctx], axis=-1)
        pr = jax.nn.softmax(s_all, axis=-1).astype(v.dtype)
        p_loc = pr[..., :n_loc].reshape(B, NA_HEADS, GRID_W, kh, GRID_W)
        return (jnp.einsum('bhqkw,bkwhd->bqhd', p_loc, v_r)
                + jnp.einsum('bhqp,bphd->bqhd', pr[..., n_loc:], v_ctx))

    o = lax.map(row_fn, (jnp.moveaxis(qg, 1, 0), row_idx, dr))
    return jnp.moveaxis(o, 0, 1).reshape(B, S, NA_HEADS * NA_HD) @ p['w_o']


def _mla_proj(p, h):
    B, S, _ = h.shape
    z = h @ p['w_in']
    cq = _rms(z[..., :MLA_Q_RANK], p['q_norm_g'])
    ckv = _rms(z[..., MLA_Q_RANK:MLA_Q_RANK + MLA_KV_RANK], p['kv_norm_g'])
    kpe = z[..., MLA_Q_RANK + MLA_KV_RANK:]
    q = (cq @ p['w_uq']).reshape(B, S, MLA_HEADS, MLA_NOPE + MLA_ROPE)
    return q, ckv, kpe


def _mla_keys(p, ckv, kpe):
    B, L, _ = ckv.shape
    kv = (ckv @ p['w_ukv']).reshape(B, L, MLA_HEADS, MLA_NOPE + MLA_V)
    k_nope, v = kv[..., :MLA_NOPE], kv[..., MLA_NOPE:]
    k_pe = jnp.broadcast_to(kpe[:, :, None, :], (B, L, MLA_HEADS, MLA_ROPE))
    return jnp.concatenate([k_nope, k_pe], axis=-1), v


def _mla_ctx(p, h):
    B, S, _ = h.shape
    q, ckv, kpe = _mla_proj(p, h)
    k, v = _mla_keys(p, ckv, kpe)
    o = _attend(q, k, v)
    return o.reshape(B, S, MLA_HEADS * MLA_V) @ p['w_o'], (ckv, kpe)


def _mla_lat(p, h, ckv_ctx, kpe_ctx):
    B, S, _ = h.shape
    q, ckv, kpe = _mla_proj(p, h)
    q = jnp.concatenate([q[..., :MLA_NOPE], _rope_2d(q[..., MLA_NOPE:])], axis=-1)
    kpe = _rope_2d(kpe[:, :, None, :])[:, :, 0]
    k, v = _mla_keys(p, jnp.concatenate([ckv, ckv_ctx], axis=1), jnp.concatenate([kpe, kpe_ctx], axis=1))
    o = _attend(q, k, v)
    return o.reshape(B, S, MLA_HEADS * MLA_V) @ p['w_o']


def _peer(p, h):
    B, S, D = h.shape
    half = PEER_DK // 2
    k1 = p['peer_k1'].astype(jnp.float32)
    k2 = p['peer_k2'].astype(jnp.float32)

    def blk(x):
        q = (x @ p['peer_wq']).astype(jnp.float32).reshape(PEER_BLOCK, PEER_HEADS, PEER_DK)
        s1 = jnp.einsum('thd,hnd->thn', q[..., :half], k1)
        s2 = jnp.einsum('thd,hnd->thn', q[..., half:], k2)
        v1, i1 = lax.top_k(s1, PEER_TOPK)
        v2, i2 = lax.top_k(s2, PEER_TOPK)
        cand = (v1[..., :, None] + v2[..., None, :]).reshape(PEER_BLOCK, PEER_HEADS, PEER_TOPK * PEER_TOPK)
        vals, idx = lax.top_k(cand, PEER_TOPK)
        e = (jnp.take_along_axis(i1, idx // PEER_TOPK, axis=-1) * PEER_NKEYS
             + jnp.take_along_axis(i2, idx % PEER_TOPK, axis=-1))
        g = jax.nn.softmax(vals, axis=-1)
        u_e = jnp.take(p['peer_u'], e, axis=0)
        v_e = jnp.take(p['peer_v'], e, axis=0)
        act = jax.nn.gelu(jnp.einsum('td,thkd->thk', x, u_e).astype(jnp.float32), approximate=False)
        return jnp.einsum('thk,thkd->td', (g * act).astype(x.dtype), v_e)

    out = lax.map(blk, h.reshape(B * S // PEER_BLOCK, PEER_BLOCK, D))
    return out.reshape(B, S, D)


def _modulation(p, cond):
    m = jax.nn.silu(cond) @ p['ada_w'] + p['ada_b']
    return jnp.split(m, 6, axis=-1)


def _ctx_layer(layer, p, x, c_ctx):
    kind = layer % N_MIXERS
    sh1, sc1, g1, sh2, sc2, g2 = _modulation(p, c_ctx)
    h = _rms(x, p['norm1_g']) * (1 + sc1) + sh1
    if kind == 0:
        o, st = _diff_ctx(p, layer, h)
    elif kind == 1:
        o, st = _na_ctx(p, h)
    else:
        o, st = _mla_ctx(p, h)
    x = x + g1 * o
    x = x + g2 * _peer(p, _rms(x, p['norm2_g']) * (1 + sc2) + sh2)
    return x, st


def _lat_layer(layer, p, x, c, cache):
    kind = layer % N_MIXERS
    sh1, sc1, g1, sh2, sc2, g2 = _modulation(p, c[:, None, :])
    h = _rms(x, p['norm1_g']) * (1 + sc1) + sh1
    if kind == 0:
        o = _diff_lat(p, layer, h, cache[0], cache[1])
    elif kind == 1:
        o = _na_lat(p, h, cache[0], cache[1])
    else:
        o = _mla_lat(p, h, cache[0], cache[1])
    x = x + g1 * o
    x = x + g2 * _peer(p, _rms(x, p['norm2_g']) * (1 + sc2) + sh2)
    return x


def _normal(key, shape, scale):
    return scale * jax.random.normal(key, shape, jnp.float32)


def _gain(key, n):
    return 1.0 + 0.02 * jax.random.normal(key, (n,), jnp.float32)


def _layer_params(key, layer):
    kind = layer % N_MIXERS
    ks = iter(jax.random.split(key, 24))
    D = D_MODEL
    pre = f'l{layer}_'
    p = {}
    p[pre + 'norm1_g'] = _gain(next(ks), D)
    p[pre + 'norm2_g'] = _gain(next(ks), D)
    p[pre + 'ada_w'] = _normal(next(ks), (D, 6 * D), 0.5 * D ** -0.5)
    p[pre + 'ada_b'] = _normal(next(ks), (6 * D,), 0.02)
    if kind == 0:
        p[pre + 'w_qkv'] = _normal(next(ks), (D, 3 * DA_HEADS * DA_V), D ** -0.5)
        p[pre + 'w_o'] = _normal(next(ks), (DA_HEADS * DA_V, D), (DA_HEADS * DA_V) ** -0.5)
        for n in ('lam_q1', 'lam_k1', 'lam_q2', 'lam_k2'):
            p[pre + n] = _normal(next(ks), (DA_QK,), 0.1)
        p[pre + 'subln_g'] = _gain(next(ks), DA_V)
    elif kind == 1:
        p[pre + 'w_qkv'] = _normal(next(ks), (D, 3 * NA_HEADS * NA_HD), D ** -0.5)
        p[pre + 'w_o'] = _normal(next(ks), (NA_HEADS * NA_HD, D), (NA_HEADS * NA_HD) ** -0.5)
        p[pre + 'rpb'] = _normal(next(ks), (NA_HEADS, 2 * NA_KH - 1, 2 * NA_KW - 1), 0.1)
    else:
        p[pre + 'w_in'] = _normal(next(ks), (D, MLA_Q_RANK + MLA_KV_RANK + MLA_ROPE), D ** -0.5)
        p[pre + 'q_norm_g'] = _gain(next(ks), MLA_Q_RANK)
        p[pre + 'w_uq'] = _normal(next(ks), (MLA_Q_RANK, MLA_HEADS * (MLA_NOPE + MLA_ROPE)), MLA_Q_RANK ** -0.5)
        p[pre + 'kv_norm_g'] = _gain(next(ks), MLA_KV_RANK)
        p[pre + 'w_ukv'] = _normal(next(ks), (MLA_KV_RANK, MLA_HEADS * (MLA_NOPE + MLA_V)), MLA_KV_RANK ** -0.5)
        p[pre + 'w_o'] = _normal(next(ks), (MLA_HEADS * MLA_V, D), (MLA_HEADS * MLA_V) ** -0.5)
    p[pre + 'peer_wq'] = _normal(next(ks), (D, PEER_HEADS * PEER_DK), D ** -0.5)
    p[pre + 'peer_k1'] = _normal(next(ks), (PEER_HEADS, PEER_NKEYS, PEER_DK // 2), (PEER_DK // 2) ** -0.5)
    p[pre + 'peer_k2'] = _normal(next(ks), (PEER_HEADS, PEER_NKEYS, PEER_DK // 2), (PEER_DK // 2) ** -0.5)
    p[pre + 'peer_u'] = _normal(next(ks), (PEER_EXPERTS, D), D ** -0.5)
    p[pre + 'peer_v'] = _normal(next(ks), (PEER_EXPERTS, D), 0.5)
    return p


def setup_inputs(seed: int = 0) -> dict:
    key = jax.random.key(seed)
    k_in, k_lay = jax.random.split(key)
    ki = jax.random.split(k_in, 12)
    D = D_MODEL
    inp = {
        'x_prompt': _normal(ki[0], (BATCH, SEQ, D), 1.0),
        'x_sample': _normal(ki[1], (DEC_BATCH, DEC_SEQ, D), 1.0),
        'cache_l0_k': _normal(ki[2], (DEC_BATCH, PAST_LEN, DA_HEADS, 2 * DA_QK), 1.0),
        'cache_l0_v': _normal(ki[3], (DEC_BATCH, PAST_LEN, DA_HEADS, DA_V), 1.0),
        'cache_l1_k': _normal(ki[4], (DEC_BATCH, PAST_LEN, NA_HEADS, NA_HD), 1.0),
        'cache_l1_v': _normal(ki[5], (DEC_BATCH, PAST_LEN, NA_HEADS, NA_HD), 1.0),
        'cache_l2_ckv': _normal(ki[6], (DEC_BATCH, PAST_LEN, MLA_KV_RANK), 1.0),
        'cache_l2_kpe': _normal(ki[7], (DEC_BATCH, PAST_LEN, MLA_ROPE), 1.0),
        'cache_l3_k': _normal(ki[8], (DEC_BATCH, PAST_LEN, DA_HEADS, 2 * DA_QK), 1.0),
        'cache_l3_v': _normal(ki[9], (DEC_BATCH, PAST_LEN, DA_HEADS, DA_V), 1.0),
        'c': _normal(ki[10], (DEC_BATCH, D), 1.0),
        'c_ctx': _normal(ki[11], (D,), 1.0),
    }
    lk = jax.random.split(k_lay, DEPTH + 1)
    for layer in range(DEPTH):
        inp.update(_layer_params(lk[layer], layer))
    inp['final_norm_g'] = _gain(lk[DEPTH], D)
    return inp


def reference(x_prompt, x_sample,
              cache_l0_k, cache_l0_v, cache_l1_k, cache_l1_v,
              cache_l2_ckv, cache_l2_kpe, cache_l3_k, cache_l3_v,
              c, c_ctx,
              l0_norm1_g, l0_norm2_g, l0_ada_w, l0_ada_b, l0_w_qkv, l0_w_o,
              l0_lam_q1, l0_lam_k1, l0_lam_q2, l0_lam_k2, l0_subln_g,
              l0_peer_wq, l0_peer_k1, l0_peer_k2, l0_peer_u, l0_peer_v,
              l1_norm1_g, l1_norm2_g, l1_ada_w, l1_ada_b, l1_w_qkv, l1_w_o, l1_rpb,
              l1_peer_wq, l1_peer_k1, l1_peer_k2, l1_peer_u, l1_peer_v,
              l2_norm1_g, l2_norm2_g, l2_ada_w, l2_ada_b, l2_w_in, l2_q_norm_g, l2_w_uq,
              l2_kv_norm_g, l2_w_ukv, l2_w_o,
              l2_peer_wq, l2_peer_k1, l2_peer_k2, l2_peer_u, l2_peer_v,
              l3_norm1_g, l3_norm2_g, l3_ada_w, l3_ada_b, l3_w_qkv, l3_w_o,
              l3_lam_q1, l3_lam_k1, l3_lam_q2, l3_lam_k2, l3_subln_g,
              l3_peer_wq, l3_peer_k1, l3_peer_k2, l3_peer_u, l3_peer_v,
              final_norm_g):
    p0 = dict(norm1_g=l0_norm1_g, norm2_g=l0_norm2_g, ada_w=l0_ada_w, ada_b=l0_ada_b,
              w_qkv=l0_w_qkv, w_o=l0_w_o, lam_q1=l0_lam_q1, lam_k1=l0_lam_k1,
              lam_q2=l0_lam_q2, lam_k2=l0_lam_k2, subln_g=l0_subln_g,
              peer_wq=l0_peer_wq, peer_k1=l0_peer_k1, peer_k2=l0_peer_k2, peer_u=l0_peer_u, peer_v=l0_peer_v)
    p1 = dict(norm1_g=l1_norm1_g, norm2_g=l1_norm2_g, ada_w=l1_ada_w, ada_b=l1_ada_b,
              w_qkv=l1_w_qkv, w_o=l1_w_o, rpb=l1_rpb,
              peer_wq=l1_peer_wq, peer_k1=l1_peer_k1, peer_k2=l1_peer_k2, peer_u=l1_peer_u, peer_v=l1_peer_v)
    p2 = dict(norm1_g=l2_norm1_g, norm2_g=l2_norm2_g, ada_w=l2_ada_w, ada_b=l2_ada_b,
              w_in=l2_w_in, q_norm_g=l2_q_norm_g, w_uq=l2_w_uq, kv_norm_g=l2_kv_norm_g,
              w_ukv=l2_w_ukv, w_o=l2_w_o,
              peer_wq=l2_peer_wq, peer_k1=l2_peer_k1, peer_k2=l2_peer_k2, peer_u=l2_peer_u, peer_v=l2_peer_v)
    p3 = dict(norm1_g=l3_norm1_g, norm2_g=l3_norm2_g, ada_w=l3_ada_w, ada_b=l3_ada_b,
              w_qkv=l3_w_qkv, w_o=l3_w_o, lam_q1=l3_lam_q1, lam_k1=l3_lam_k1,
              lam_q2=l3_lam_q2, lam_k2=l3_lam_k2, subln_g=l3_subln_g,
              peer_wq=l3_peer_wq, peer_k1=l3_peer_k1, peer_k2=l3_peer_k2, peer_u=l3_peer_u, peer_v=l3_peer_v)
    params = (p0, p1, p2, p3)
    caches = ((cache_l0_k, cache_l0_v), (cache_l1_k, cache_l1_v),
              (cache_l2_ckv, cache_l2_kpe), (cache_l3_k, cache_l3_v))

    xp = x_prompt
    new_state = []
    for layer in range(DEPTH):
        xp, st = _ctx_layer(layer, params[layer], xp, c_ctx)
        new_state.extend(st)

    xs = x_sample
    for layer in range(DEPTH):
        xs = _lat_layer(layer, params[layer], xs, c, caches[layer])

    y_prompt = _rms(xp, final_norm_g)
    y_sample = _rms(xs, final_norm_g)
    return (y_prompt, y_sample, *new_state)
```

```python
import functools
import math

import numpy as np
import jax
import jax.numpy as jnp
from jax import lax
from jax.experimental import pallas as pl
from jax.experimental.pallas import tpu as pltpu

F32 = jnp.float32
BF16 = jnp.bfloat16

D_MODEL = 1024
BATCH = 32
SEQ = 256
DEPTH = 4
DEC_BATCH = 2
DEC_SEQ = 4096
PAST_LEN = 512
GRID_W = 64
GRID_R = DEC_SEQ // GRID_W
EPS = 1e-6
ROPE_BASE = 10000.0
NEG_INF = -1e30

DA_HEADS = 8
DA_QK = 64
DA_V = 128
NA_HEADS = 16
NA_HD = 64
NA_KH = 8
NA_KW = 16
MLA_HEADS = 16
MLA_NOPE = 64
MLA_ROPE = 32
MLA_V = 64
MLA_Q_RANK = 384
MLA_KV_RANK = 256
PEER_HEADS = 8
PEER_NKEYS = 128
PEER_EXPERTS = PEER_NKEYS * PEER_NKEYS
PEER_DK = 256
PEER_TOPK = 16

N_CTX = BATCH * SEQ
N_LAT = DEC_BATCH * DEC_SEQ
N_TOK = N_CTX + N_LAT
LAT_KEYS = DEC_SEQ + PAST_LEN

LANES = 128
VMEM_LIMIT = 48 << 20

TM = 256
TQ = 256
TK = 512
NA_RB = 8
PEER_T = 256
PEER_EC = 2048
PEER_SUB = 32

_NT = (((1,), (1,)), ((), ()))


def _cparams(sem):
  return pltpu.CompilerParams(dimension_semantics=sem, vmem_limit_bytes=VMEM_LIMIT)


def _mod_row(i, tm):
  nb_ctx = N_CTX // tm
  nb_bat = DEC_SEQ // tm
  return jnp.where(i < nb_ctx, 0, 1 + (i - nb_ctx) // nb_bat)


def _rope_blk(i, tm):
  nb_ctx = N_CTX // tm
  nb_bat = DEC_SEQ // tm
  return jnp.where(i < nb_ctx, nb_bat, (i - nb_ctx) % nb_bat)


def _rms_mod(x, g, sc, sh):
  y = x * lax.rsqrt(jnp.mean(x * x, axis=-1, keepdims=True) + EPS)
  return (y * g) * (1.0 + sc) + sh


def _rms(x, g):
  return x * lax.rsqrt(jnp.mean(x * x, axis=-1, keepdims=True) + EPS) * g


def _row_spec(width, tm=TM):
  return pl.BlockSpec((tm, width), lambda i: (i, 0))


def _full_spec(shape):
  return pl.BlockSpec(shape, lambda i: (0,) * len(shape))


def _mod_spec(tm=TM):
  return pl.BlockSpec((None, 1, D_MODEL), lambda i: (_mod_row(i, tm), 0, 0))


def _mod_body(c_ref, w_ref, b_ref, o_ref):
  c = c_ref[...]
  s = c / (1.0 + jnp.exp(-c))
  o_ref[...] = jnp.dot(s, w_ref[...], precision=lax.Precision.HIGHEST,
                       preferred_element_type=F32) + b_ref[...]


def _modulation(cond8, ada_w, ada_b):
  n = ada_w.shape[1]
  tn = 1536
  out = pl.pallas_call(
      _mod_body,
      grid=(n // tn,),
      in_specs=[pl.BlockSpec((8, D_MODEL), lambda j: (0, 0)),
                pl.BlockSpec((D_MODEL, tn), lambda j: (0, j)),
                pl.BlockSpec((1, tn), lambda j: (0, j))],
      out_specs=pl.BlockSpec((8, tn), lambda j: (0, j)),
      out_shape=jax.ShapeDtypeStruct((8, n), F32),
      compiler_params=_cparams(("parallel",)),
      name="modulation",
  )(cond8, ada_w, ada_b.reshape(1, n))
  return [out[:, k * D_MODEL:(k + 1) * D_MODEL].reshape(8, 1, D_MODEL) for k in range(6)]


def _tile_lanes(t, reps):
  return jnp.concatenate([t] * reps, axis=1)


def _qkv_rope_body(x_ref, g_ref, sc_ref, sh_ref, w_ref, cos_ref, sin_ref,
                   q_ref, k_ref, v_ref, *, q_scale):
  h = _rms_mod(x_ref[...], g_ref[...], sc_ref[...], sh_ref[...]).astype(BF16)
  reps = D_MODEL // LANES
  cos = _tile_lanes(cos_ref[...], reps)
  sin = _tile_lanes(sin_ref[...], reps)
  d = D_MODEL
  dot = lambda a, b: jnp.dot(h, w_ref[:, a:b], preferred_element_type=F32)
  q_ref[...] = (dot(0, d) * cos + dot(d, 2 * d) * sin) * q_scale
  k_ref[...] = dot(2 * d, 3 * d) * cos + dot(3 * d, 4 * d) * sin
  v_ref[...] = dot(4 * d, 5 * d)


def _qkv_rope(x, g, sc, sh, w5, cos, sin, q_scale):
  n = x.shape[0]
  rope_spec = pl.BlockSpec((TM, LANES), lambda i: (_rope_blk(i, TM), 0))
  return pl.pallas_call(
      functools.partial(_qkv_rope_body, q_scale=q_scale),
      grid=(n // TM,),
      in_specs=[_row_spec(D_MODEL), _full_spec((1, D_MODEL)), _mod_spec(), _mod_spec(),
                _full_spec(w5.shape), rope_spec, rope_spec],
      out_specs=[_row_spec(D_MODEL)] * 3,
      out_shape=[jax.ShapeDtypeStruct((n, D_MODEL), F32)] * 3,
      compiler_params=_cparams(("parallel",)),
      name="qkv_rope_proj",
  )(x, g, sc, sh, w5, cos, sin)


def _qkv_plain_body(x_ref, g_ref, sc_ref, sh_ref, w_ref, q_ref, k_ref, v_ref, *, q_scale):
  h = _rms_mod(x_ref[...], g_ref[...], sc_ref[...], sh_ref[...]).astype(BF16)
  d = D_MODEL
  dot = lambda a, b: jnp.dot(h, w_ref[:, a:b], preferred_element_type=F32)
  q_ref[...] = dot(0, d) * q_scale
  k_ref[...] = dot(d, 2 * d)
  v_ref[...] = dot(2 * d, 3 * d)


def _qkv_plain(x, g, sc, sh, w3, q_scale):
  n = x.shape[0]
  return pl.pallas_call(
      functools.partial(_qkv_plain_body, q_scale=q_scale),
      grid=(n // TM,),
      in_specs=[_row_spec(D_MODEL), _full_spec((1, D_MODEL)), _mod_spec(), _mod_spec(),
                _full_spec(w3.shape)],
      out_specs=[_row_spec(D_MODEL)] * 3,
      out_shape=[jax.ShapeDtypeStruct((n, D_MODEL), F32)] * 3,
      compiler_params=_cparams(("parallel",)),
      name="qkv_proj",
  )(x, g, sc, sh, w3)


def _mla_in_body(x_ref, g_ref, sc_ref, sh_ref, w_ref, qg_ref, kvg_ref, cos_ref, sin_ref,
                 cq_ref, ckv_ref, kpe_ref):
  h = _rms_mod(x_ref[...], g_ref[...], sc_ref[...], sh_ref[...]).astype(BF16)
  z = jnp.dot(h, w_ref[...], preferred_element_type=F32)
  a, b = MLA_Q_RANK, MLA_Q_RANK + MLA_KV_RANK
  cq_ref[...] = _rms(z[:, :a], qg_ref[...]).astype(BF16)
  ckv_ref[...] = _rms(z[:, a:b], kvg_ref[...])
  kpe_ref[...] = z[:, b:b + LANES] * cos_ref[...] + z[:, b + LANES:] * sin_ref[...]


def _mla_in(x, g, sc, sh, w_in, qg, kvg, cos, sin):
  n = x.shape[0]
  rope_spec = pl.BlockSpec((TM, LANES), lambda i: (_rope_blk(i, TM), 0))
  return pl.pallas_call(
      _mla_in_body,
      grid=(n // TM,),
      in_specs=[_row_spec(D_MODEL), _full_spec((1, D_MODEL)), _mod_spec(), _mod_spec(),
                _full_spec(w_in.shape), _full_spec((1, MLA_Q_RANK)),
                _full_spec((1, MLA_KV_RANK)), rope_spec, rope_spec],
      out_specs=[_row_spec(MLA_Q_RANK), _row_spec(MLA_KV_RANK), _row_spec(LANES)],
      out_shape=[jax.ShapeDtypeStruct((n, MLA_Q_RANK), BF16),
                 jax.ShapeDtypeStruct((n, MLA_KV_RANK), F32),
                 jax.ShapeDtypeStruct((n, LANES), F32)],
      compiler_params=_cparams(("parallel",)),
      name="mla_in_proj",
  )(x, g, sc, sh, w_in, qg, kvg, cos, sin)


def _mla_q_body(cq_ref, w_ref, cos_ref, sin_ref, qn_ref, qp_ref):
  z = jnp.dot(cq_ref[...], w_ref[...], preferred_element_type=F32)
  pe = MLA_HEADS * MLA_ROPE
  reps = pe // LANES
  cos = _tile_lanes(cos_ref[...], reps)
  sin = _tile_lanes(sin_ref[...], reps)
  qn_ref[...] = z[:, :D_MODEL]
  qp_ref[...] = z[:, D_MODEL:D_MODEL + pe] * cos + z[:, D_MODEL + pe:] * sin


def _mla_q(cq, w_uq, cos, sin):
  n = cq.shape[0]
  pe = MLA_HEADS * MLA_ROPE
  rope_spec = pl.BlockSpec((TM, LANES), lambda i: (_rope_blk(i, TM), 0))
  return pl.pallas_call(
      _mla_q_body,
      grid=(n // TM,),
      in_specs=[_row_spec(MLA_Q_RANK), _full_spec(w_uq.shape), rope_spec, rope_spec],
      out_specs=[_row_spec(D_MODEL), _row_spec(pe)],
      out_shape=[jax.ShapeDtypeStruct((n, D_MODEL), F32), jax.ShapeDtypeStruct((n, pe), F32)],
      compiler_params=_cparams(("parallel",)),
      name="mla_q_proj",
  )(cq, w_uq, cos, sin)


def _mla_kv_body(c_ref, w_ref, kn_ref, v_ref):
  z = jnp.dot(c_ref[...].astype(BF16), w_ref[...], preferred_element_type=F32)
  kn_ref[...] = z[:, :D_MODEL]
  v_ref[...] = z[:, D_MODEL:]


def _mla_kv(ckv, w_ukv):
  n = ckv.shape[0]
  return pl.pallas_call(
      _mla_kv_body,
      grid=(n // TM,),
      in_specs=[_row_spec(MLA_KV_RANK), _full_spec(w_ukv.shape)],
      out_specs=[_row_spec(D_MODEL)] * 2,
      out_shape=[jax.ShapeDtypeStruct((n, D_MODEL), F32)] * 2,
      compiler_params=_cparams(("parallel",)),
      name="mla_kv_proj",
  )(ckv, w_ukv)


def _out_proj_body(o_ref, w_ref, x_ref, gate_ref, y_ref):
  y_ref[...] = x_ref[...] + gate_ref[...] * jnp.dot(
      o_ref[...].astype(BF16), w_ref[...], preferred_element_type=F32)


def _out_proj(o, w_o, x, gate):
  n = x.shape[0]
  return pl.pallas_call(
      _out_proj_body,
      grid=(n // TM,),
      in_specs=[_row_spec(D_MODEL), _full_spec(w_o.shape), _row_spec(D_MODEL), _mod_spec()],
      out_specs=_row_spec(D_MODEL),
      out_shape=jax.ShapeDtypeStruct((n, D_MODEL), F32),
      compiler_params=_cparams(("parallel",)),
      name="out_proj",
  )(o, w_o, x, gate)


def _final_norm_body(x_ref, g_ref, y_ref):
  y_ref[...] = _rms(x_ref[...], g_ref[...])


def _final_norm(x, g):
  n = x.shape[0]
  return pl.pallas_call(
      _final_norm_body,
      grid=(n // TM,),
      in_specs=[_row_spec(D_MODEL), _full_spec((1, D_MODEL))],
      out_specs=_row_spec(D_MODEL),
      out_shape=jax.ShapeDtypeStruct((n, D_MODEL), F32),
      compiler_params=_cparams(("parallel",)),
      name="final_norm",
  )(x, g)


def _online_update(s, m, l, acc, vb):
  m_new = jnp.maximum(m, jnp.max(s, axis=-1, keepdims=True))
  alpha = jnp.exp(m - m_new)
  p = jnp.exp(s - m_new)
  l_new = alpha * l + jnp.sum(p, axis=-1, keepdims=True)
  acc_new = alpha * acc + jnp.dot(p.astype(BF16), vb, preferred_element_type=F32)
  return m_new, l_new, acc_new


def _softmax_state(tq):
  return (jnp.full((tq, 1), NEG_INF, F32), jnp.zeros((tq, 1), F32), jnp.zeros((tq, LANES), F32))


def _diff_attn_body(lam_ref, g_ref, q_ref, k_ref, v_ref, o_ref, *, n_keys, tk, lam_init):
  q = q_ref[...]
  tq = q.shape[0]
  lane = lax.broadcasted_iota(jnp.int32, q.shape, 1)
  q1 = jnp.where(lane < DA_QK, q, 0.0).astype(BF16)
  q2 = jnp.where(lane >= DA_QK, q, 0.0).astype(BF16)
  lv = lam_ref[...]
  lam = (jnp.exp(jnp.sum(lv[0:1] * lv[1:2], axis=-1, keepdims=True))
         - jnp.exp(jnp.sum(lv[2:3] * lv[3:4], axis=-1, keepdims=True)) + lam_init)

  def chunk(c, carry):
    st1, st2 = carry
    off = pl.multiple_of(c * tk, tk)
    kb = k_ref[pl.ds(off, tk), :].astype(BF16)
    vb = v_ref[pl.ds(off, tk), :].astype(BF16)
    s1 = lax.dot_general(q1, kb, _NT, preferred_element_type=F32)
    s2 = lax.dot_general(q2, kb, _NT, preferred_element_type=F32)
    return _online_update(s1, *st1, vb), _online_update(s2, *st2, vb)

  st1, st2 = lax.fori_loop(0, n_keys // tk, chunk, (_softmax_state(tq), _softmax_state(tq)))
  o = st1[2] / st1[1] - lam * (st2[2] / st2[1])
  o_ref[...] = _rms(o, g_ref[...]) * (1.0 - lam_init)


def _diff_attention(q, k, v, lam_rows, subln_g, layer):
  b, sq, _ = q.shape
  n_keys = k.shape[1]
  tq = min(TQ, sq)
  tk = min(TK, n_keys)
  lam_init = 0.8 - 0.6 * math.exp(-0.3 * layer)
  qspec = pl.BlockSpec((None, tq, LANES), lambda bi, h, qi: (bi, qi, h))
  kspec = pl.BlockSpec((None, n_keys, LANES), lambda bi, h, qi: (bi, 0, h))
  return pl.pallas_call(
      functools.partial(_diff_attn_body, n_keys=n_keys, tk=tk, lam_init=lam_init),
      grid=(b, DA_HEADS, sq // tq),
      in_specs=[pl.BlockSpec((8, LANES), lambda bi, h, qi: (0, 0)),
                pl.BlockSpec((1, LANES), lambda bi, h, qi: (0, 0)),
                qspec, kspec, kspec],
      out_specs=qspec,
      out_shape=jax.ShapeDtypeStruct((b, sq, D_MODEL), F32),
      compiler_params=_cparams(("parallel", "parallel", "arbitrary")),
      name="diff_attention",
  )(lam_rows, subln_g, q, k, v)


def _pair_attn_body(*refs, n_keys, tk, scale, with_pe):
  if with_pe:
    q_ref, qp_ref, k_ref, kp_ref, v_ref, o_ref = refs
  else:
    q_ref, k_ref, v_ref, o_ref = refs
  q = q_ref[...]
  tq = q.shape[0]
  lane = lax.broadcasted_iota(jnp.int32, q.shape, 1)
  halves = (lane < NA_HD, lane >= NA_HD)
  qs = [jnp.where(hm, q, 0.0).astype(BF16) for hm in halves]
  if with_pe:
    qp = qp_ref[...]
    base = (pl.program_id(1) % 2) * (2 * MLA_ROPE)
    qps = [jnp.where((lane >= base + a * MLA_ROPE) & (lane < base + (a + 1) * MLA_ROPE),
                     qp, 0.0).astype(BF16) for a in range(2)]

  def chunk(c, carry):
    off = pl.multiple_of(c * tk, tk)
    kb = k_ref[pl.ds(off, tk), :].astype(BF16)
    v = v_ref[pl.ds(off, tk), :]
    if with_pe:
      kpb = kp_ref[pl.ds(off, tk), :].astype(BF16)
    new = []
    for a in range(2):
      s = lax.dot_general(qs[a], kb, _NT, preferred_element_type=F32)
      if with_pe:
        s = s + lax.dot_general(qps[a], kpb, _NT, preferred_element_type=F32)
      if scale != 1.0:
        s = s * scale
      vb = jnp.where(halves[a][:1], v, 0.0).astype(BF16)
      new.append(_online_update(s, *carry[a], vb))
    return tuple(new)

  st = lax.fori_loop(0, n_keys // tk, chunk, (_softmax_state(tq), _softmax_state(tq)))
  o_ref[...] = st[0][2] / st[0][1] + st[1][2] / st[1][1]


def _pair_attention(q, k, v, scale=1.0, q_pe=None, k_pe=None):
  b, sq, _ = q.shape
  n_keys = k.shape[1]
  tq = min(TQ, sq)
  tk = min(TK, n_keys)
  with_pe = q_pe is not None
  qspec = pl.BlockSpec((None, tq, LANES), lambda bi, h, qi: (bi, qi, h))
  kspec = pl.BlockSpec((None, n_keys, LANES), lambda bi, h, qi: (bi, 0, h))
  if with_pe:
    qpspec = pl.BlockSpec((None, tq, LANES), lambda bi, h, qi: (bi, qi, h // 2))
    kpspec = pl.BlockSpec((None, n_keys, LANES), lambda bi, h, qi: (bi, 0, 0))
    in_specs = [qspec, qpspec, kspec, kpspec, kspec]
    args = (q, q_pe, k, k_pe, v)
  else:
    in_specs = [qspec, kspec, kspec]
    args = (q, k, v)
  return pl.pallas_call(
      functools.partial(_pair_attn_body, n_keys=n_keys, tk=tk, scale=scale, with_pe=with_pe),
      grid=(b, D_MODEL // LANES, sq // tq),
      in_specs=in_specs,
      out_specs=qspec,
      out_shape=jax.ShapeDtypeStruct((b, sq, D_MODEL), F32),
      compiler_params=_cparams(("parallel", "parallel", "arbitrary")),
      name="pair_attention_pe" if with_pe else "pair_attention",
  )(*args)


def _na_lat_body(q_ref, k_ref, v_ref, kc_ref, vc_ref, bias_ref, o_ref):
  rb = pl.program_id(2)
  n_loc = NA_KH * GRID_W
  lane = lax.broadcasted_iota(jnp.int32, (1, LANES), 1)
  halves = (lane < NA_HD, lane >= NA_HD)
  kc = kc_ref[...].astype(BF16)
  vc = vc_ref[...]
  vcs = [jnp.where(hm, vc, 0.0).astype(BF16) for hm in halves]
  for rr in range(NA_RB):
    r = rb * NA_RB + rr
    start = jnp.clip(r - NA_KH // 2, 0, GRID_R - NA_KH)
    pat = jnp.where(r < NA_KH // 2, 1 + r,
                    jnp.where(r > GRID_R - NA_KH // 2, r - (GRID_R - NA_KH), 0))
    off = pl.multiple_of(start * GRID_W, GRID_W)
    kw = k_ref[pl.ds(off, n_loc), :].astype(BF16)
    vw = v_ref[pl.ds(off, n_loc), :]
    q = q_ref[rr * GRID_W:(rr + 1) * GRID_W, :]
    out = jnp.zeros((GRID_W, LANES), F32)
    for a in range(2):
      qa = jnp.where(halves[a], q, 0.0).astype(BF16)
      s_loc = lax.dot_general(qa, kw, _NT, preferred_element_type=F32) + bias_ref[pat, a]
      s_ctx = lax.dot_general(qa, kc, _NT, preferred_element_type=F32)
      m = jnp.maximum(jnp.max(s_loc, axis=-1, keepdims=True),
                      jnp.max(s_ctx, axis=-1, keepdims=True))
      p_loc = jnp.exp(s_loc - m)
      p_ctx = jnp.exp(s_ctx - m)
      l = jnp.sum(p_loc, axis=-1, keepdims=True) + jnp.sum(p_ctx, axis=-1, keepdims=True)
      vwa = jnp.where(halves[a], vw, 0.0).astype(BF16)
      o = (jnp.dot(p_loc.astype(BF16), vwa, preferred_element_type=F32)
           + jnp.dot(p_ctx.astype(BF16), vcs[a], preferred_element_type=F32))
      out = out + o / l
    o_ref[rr * GRID_W:(rr + 1) * GRID_W, :] = out


def _na_lat_attention(q, k, v, kc, vc, bias):
  b = q.shape[0]
  n_pat = bias.shape[0]
  blk = NA_RB * GRID_W
  qspec = pl.BlockSpec((None, blk, LANES), lambda bi, h, r: (bi, r, h))
  kspec = pl.BlockSpec((None, DEC_SEQ, LANES), lambda bi, h, r: (bi, 0, h))
  cspec = pl.BlockSpec((None, PAST_LEN, LANES), lambda bi, h, r: (bi, 0, h))
  bspec = pl.BlockSpec((n_pat, 2, GRID_W, NA_KH * GRID_W), lambda bi, h, r: (0, h, 0, 0))
  return pl.pallas_call(
      _na_lat_body,
      grid=(b, D_MODEL // LANES, GRID_R // NA_RB),
      in_specs=[qspec, kspec, kspec, cspec, cspec, bspec],
      out_specs=qspec,
      out_shape=jax.ShapeDtypeStruct((b, DEC_SEQ, D_MODEL), F32),
      compiler_params=_cparams(("parallel", "parallel", "arbitrary")),
      name="na_lat_attention",
  )(q, k, v, kc, vc, bias)


def _top16_sorted(s):
  t = s.shape[1]
  rows = lax.broadcasted_iota(jnp.int32, (PEER_TOPK, t), 0)

  def step(r, carry):
    s, v = carry
    m = jnp.max(s, axis=0, keepdims=True)
    return jnp.where(s == m, NEG_INF, s), jnp.where(rows == r, m, v)

  _, v = lax.fori_loop(0, PEER_TOPK, step, (s, jnp.zeros((PEER_TOPK, t), F32)), unroll=True)
  return v


def _router_body(x_ref, g_ref, sc_ref, sh_ref, wq_ref, k1_ref, k2_ref,
                 hb_ref, a1_ref, b2_ref, th_ref, h_s):
  @pl.when(pl.program_id(1) == 0)
  def _():
    h = _rms_mod(x_ref[...], g_ref[...], sc_ref[...], sh_ref[...]).astype(BF16)
    h_s[...] = h
    hb_ref[...] = h

  qh = jnp.dot(h_s[...], wq_ref[...], preferred_element_type=F32)
  half = PEER_DK // 2
  s1 = lax.dot_general(k1_ref[...], qh[:, :half].astype(BF16), _NT, preferred_element_type=F32)
  s2 = lax.dot_general(k2_ref[...], qh[:, half:].astype(BF16), _NT, preferred_element_type=F32)
  s1 = s1 - jnp.max(s1, axis=0, keepdims=True)
  s2 = s2 - jnp.max(s2, axis=0, keepdims=True)
  v1 = _top16_sorted(s1)
  v2 = _top16_sorted(s2)
  cand = jnp.concatenate([v1[r:r + 1] + v2 for r in range(PEER_TOPK)], axis=0)

  def step(_, carry):
    c, th = carry
    m = jnp.max(c, axis=0, keepdims=True)
    return jnp.where(c == m, NEG_INF, c), m

  _, th = lax.fori_loop(0, PEER_TOPK, step, (cand, jnp.zeros_like(cand[:1])), unroll=True)
  sel = cand >= th
  z = jnp.sum(jnp.where(sel, jnp.exp(cand), 0.0), axis=0, keepdims=True)
  inv_z = 1.0 / z
  e1 = jnp.exp(v1)
  e2 = jnp.exp(v2) * inv_z
  gate = jnp.concatenate([e1[r:r + 1] * e2 for r in range(PEER_TOPK)], axis=0)
  a1_ref[...] = jnp.exp(s1)
  b2_ref[...] = jnp.exp(s2) * inv_z
  th_ref[...] = jnp.min(jnp.where(sel, gate, jnp.inf), axis=0, keepdims=True)


def _peer_router(x, g, sc, sh, wq, k1, k2):
  n = x.shape[0]
  tm = PEER_T
  row = lambda w: pl.BlockSpec((tm, w), lambda i, h: (i, 0))
  full = lambda shape: pl.BlockSpec(shape, lambda i, h: (0,) * len(shape))
  modspec = pl.BlockSpec((None, 1, D_MODEL), lambda i, h: (_mod_row(i, tm), 0, 0))
  keyspec = pl.BlockSpec((None, PEER_NKEYS, PEER_DK // 2), lambda i, h: (h, 0, 0))
  tspec = pl.BlockSpec((None, PEER_NKEYS, tm), lambda i, h: (h, 0, i))
  return pl.pallas_call(
      _router_body,
      grid=(n // tm, PEER_HEADS),
      in_specs=[row(D_MODEL), full((1, D_MODEL)), modspec, modspec,
                pl.BlockSpec((D_MODEL, PEER_DK), lambda i, h: (0, h)), keyspec, keyspec],
      out_specs=[row(D_MODEL), tspec, tspec,
                 pl.BlockSpec((None, 1, tm), lambda i, h: (h, 0, i))],
      out_shape=[jax.ShapeDtypeStruct((n, D_MODEL), BF16),
                 jax.ShapeDtypeStruct((PEER_HEADS, PEER_NKEYS, n), F32),
                 jax.ShapeDtypeStruct((PEER_HEADS, PEER_NKEYS, n), F32),
                 jax.ShapeDtypeStruct((PEER_HEADS, 1, n), F32)],
      scratch_shapes=[pltpu.VMEM((tm, D_MODEL), BF16)],
      compiler_params=_cparams(("parallel", "arbitrary")),
      name="peer_router",
  )(x, g, sc, sh, wq, k1, k2)


def _gelu(x):
  return 0.5 * x * (1.0 + lax.erf(x * np.float32(math.sqrt(0.5))))


def _peer_mix_body(hb_ref, u_ref, vt_ref, a1_ref, b2_ref, th_ref, x_ref, gate_ref,
                   y_ref, acc_s, s_s, p_s):
  c = pl.program_id(1)

  @pl.when(c == 0)
  def _():
    acc_s[...] = jnp.zeros_like(acc_s)

  s_s[...] = lax.dot_general(u_ref[...], hb_ref[...], _NT, preferred_element_type=F32)
  n_i1 = PEER_EC // PEER_NKEYS
  t = hb_ref.shape[0]

  def per_i1(i1, carry):
    for sub in range(PEER_NKEYS // PEER_SUB):
      w = jnp.zeros((PEER_SUB, t), F32)
      for h in range(PEER_HEADS):
        a_row = a1_ref[h, pl.ds(i1, 1), :]
        prod = a_row * b2_ref[h, sub * PEER_SUB:(sub + 1) * PEER_SUB, :]
        w = w + jnp.where(prod >= th_ref[h], prod, 0.0)
      row0 = pl.multiple_of(i1 * PEER_NKEYS + sub * PEER_SUB, PEER_SUB)
      sv = s_s[pl.ds(row0, PEER_SUB), :]
      p_s[pl.ds(row0, PEER_SUB), :] = (w * _gelu(sv)).astype(BF16)
    return carry

  lax.fori_loop(0, n_i1, per_i1, 0)
  acc_s[...] += jnp.dot(vt_ref[...], p_s[...], preferred_element_type=F32)

  @pl.when(c == pl.num_programs(1) - 1)
  def _():
    y_ref[...] = x_ref[...] + gate_ref[...] * acc_s[...].T


def _peer_mix(hb, u, vt, a1, b2, th, x, gate):
  n = x.shape[0]
  t = PEER_T
  n_i1 = PEER_EC // PEER_NKEYS
  row = lambda w: pl.BlockSpec((t, w), lambda i, c: (i, 0))
  return pl.pallas_call(
      _peer_mix_body,
      grid=(n // t, PEER_EXPERTS // PEER_EC),
      in_specs=[row(D_MODEL),
                pl.BlockSpec((PEER_EC, D_MODEL), lambda i, c: (c, 0)),
                pl.BlockSpec((D_MODEL, PEER_EC), lambda i, c: (0, c)),
                pl.BlockSpec((PEER_HEADS, n_i1, t), lambda i, c: (0, c, i)),
                pl.BlockSpec((PEER_HEADS, PEER_NKEYS, t), lambda i, c: (0, 0, i)),
                pl.BlockSpec((PEER_HEADS, 1, t), lambda i, c: (0, 0, i)),
                row(D_MODEL),
                pl.BlockSpec((None, 1, D_MODEL), lambda i, c: (_mod_row(i, t), 0, 0))],
      out_specs=row(D_MODEL),
      out_shape=jax.ShapeDtypeStruct((n, D_MODEL), F32),
      scratch_shapes=[pltpu.VMEM((D_MODEL, t), F32),
                      pltpu.VMEM((PEER_EC, t), F32),
                      pltpu.VMEM((PEER_EC, t), BF16)],
      compiler_params=_cparams(("parallel", "arbitrary")),
      name="peer_mix",
  )(hb, u, vt, a1, b2, th, x, gate)


def _peer_layer(x, p, sc, sh, gate):
  hb, a1, b2, th = _peer_router(x, p['norm2_g'], sc, sh, p['peer_wq'], p['peer_k1'], p['peer_k2'])
  return _peer_mix(hb, p['peer_u'], p['peer_vt'], a1, b2, th, x, gate)


def _rot_perm(width, group):
  j = np.arange(width)
  half = group // 2
  first = (j % group) < half
  return np.where(first, j + half, j - half), np.where(first, -1.0, 1.0).astype(np.float32)


def _rope_table(rot_dims, reps):
  half = rot_dims // 2
  t = jnp.arange(DEC_SEQ)
  inv = ROPE_BASE ** (-jnp.arange(half, dtype=F32) / half)
  parts_c, parts_s = [], []
  for pos in (t // GRID_W, t % GRID_W):
    ang = pos.astype(F32)[:, None] * inv[None, :]
    parts_c += [jnp.cos(ang), jnp.cos(ang)]
    parts_s += [jnp.sin(ang), jnp.sin(ang)]
  cos = jnp.tile(jnp.concatenate(parts_c, axis=1), (1, reps))
  sin = jnp.tile(jnp.concatenate(parts_s, axis=1), (1, reps))
  cos = jnp.concatenate([cos, jnp.ones((TM, LANES), F32)], axis=0)
  sin = jnp.concatenate([sin, jnp.zeros((TM, LANES), F32)], axis=0)
  return cos, sin


def _na_bias_table(rpb):
  reps = [NA_KH // 2] + list(range(NA_KH // 2)) + list(range(GRID_R - NA_KH // 2 + 1, GRID_R))
  cols = np.arange(GRID_W)
  col_start = np.clip(cols - NA_KW // 2, 0, GRID_W - NA_KW)
  col_mask = (cols[None, :] >= col_start[:, None]) & (cols[None, :] < col_start[:, None] + NA_KW)
  dc = np.clip(cols[None, :] - cols[:, None], -(NA_KW - 1), NA_KW - 1) + (NA_KW - 1)
  tabs = []
  for r in reps:
    ridx = np.clip(r - NA_KH // 2, 0, GRID_R - NA_KH) + np.arange(NA_KH)
    dr = ridx - r + (NA_KH - 1)
    bias = rpb[:, dr[None, :, None], dc[:, None, :]]
    bias = jnp.where(col_mask[None, :, None, :], bias, NEG_INF)
    tabs.append(bias.reshape(NA_HEADS, GRID_W, NA_KH * GRID_W))
  return jnp.stack(tabs, axis=0)


def _split_passes(a):
  return a[:N_CTX].reshape(BATCH, SEQ, -1), a[N_CTX:].reshape(DEC_BATCH, DEC_SEQ, -1)


def _join_passes(c, l):
  return jnp.concatenate([c.reshape(N_CTX, -1), l.reshape(N_LAT, -1)], axis=0)


def _diff_mixer(layer, p, x, sc, sh, cache, tables):
  w = p['w_qkv']
  d = D_MODEL
  wq, wk, wv = w[:, :d], w[:, d:2 * d], w[:, 2 * d:]
  rot = lambda m: _rotated_tiles(m, DA_QK // 2)
  w5 = jnp.concatenate([wq, rot(wq), wk, rot(wk), wv], axis=1).astype(BF16)
  cos, sin = tables['da']
  q, k, v = _qkv_rope(x, p['norm1_g'], sc, sh, w5, cos, sin, DA_QK ** -0.5)
  qc, ql = _split_passes(q)
  kc, kl = _split_passes(k)
  vc, vl = _split_passes(v)
  lam_rows = jnp.zeros((8, LANES), F32)
  for i, nme in enumerate(('lam_q1', 'lam_k1', 'lam_q2', 'lam_k2')):
    lam_rows = lam_rows.at[i, :DA_QK].set(p[nme])
  g = p['subln_g'].reshape(1, DA_V)
  oc = _diff_attention(qc, kc, vc, lam_rows, g, layer)
  kl = jnp.concatenate([kl, cache[0].reshape(DEC_BATCH, PAST_LEN, d)], axis=1)
  vl = jnp.concatenate([vl, cache[1].reshape(DEC_BATCH, PAST_LEN, d)], axis=1)
  ol = _diff_attention(ql, kl, vl, lam_rows, g, layer)
  state = (kc.reshape(BATCH, SEQ, DA_HEADS, 2 * DA_QK), vc.reshape(BATCH, SEQ, DA_HEADS, DA_V))
  return _join_passes(oc, ol), state


def _rotated_tiles(w, group):
  src, sign = _rot_perm(w.shape[1], group)
  return w[:, src] * sign[None, :]


def _na_mixer(p, x, sc, sh, cache):
  d = D_MODEL
  q, k, v = _qkv_plain(x, p['norm1_g'], sc, sh, p['w_qkv'].astype(BF16), NA_HD ** -0.5)
  qc, ql = _split_passes(q)
  kc, kl = _split_passes(k)
  vc, vl = _split_passes(v)
  oc = _pair_attention(qc, kc, vc)
  bias = _na_bias_table(p['rpb'])
  ol = _na_lat_attention(ql, kl, vl, cache[0].reshape(DEC_BATCH, PAST_LEN, d),
                         cache[1].reshape(DEC_BATCH, PAST_LEN, d), bias)
  state = (kc.reshape(BATCH, SEQ, NA_HEADS, NA_HD), vc.reshape(BATCH, SEQ, NA_HEADS, NA_HD))
  return _join_passes(oc, ol), state


def _mla_mixer(p, x, sc, sh, cache, tables):
  a, b = MLA_Q_RANK, MLA_Q_RANK + MLA_KV_RANK
  w_in = p['w_in']
  kpe_w = w_in[:, b:]
  w_in_x = jnp.concatenate([w_in[:, :b], jnp.tile(kpe_w, (1, 4)),
                            jnp.tile(_rotated_tiles(kpe_w, MLA_ROPE // 2), (1, 4))],
                           axis=1).astype(BF16)
  cos, sin = tables['mla']
  cq, ckv, kpe = _mla_in(x, p['norm1_g'], sc, sh, w_in_x, p['q_norm_g'].reshape(1, a),
                         p['kv_norm_g'].reshape(1, MLA_KV_RANK), cos, sin)
  w_uq = p['w_uq'].reshape(a, MLA_HEADS, MLA_NOPE + MLA_ROPE)
  w_qn = w_uq[:, :, :MLA_NOPE].reshape(a, -1)
  w_qp = w_uq[:, :, MLA_NOPE:].reshape(a, -1)
  w_uq_x = jnp.concatenate([w_qn, w_qp, _rotated_tiles(w_qp, MLA_ROPE // 2)], axis=1).astype(BF16)
  qn, qp = _mla_q(cq, w_uq_x, cos, sin)
  w_ukv = p['w_ukv'].reshape(MLA_KV_RANK, MLA_HEADS, MLA_NOPE + MLA_V)
  w_ukv_x = jnp.concatenate([w_ukv[:, :, :MLA_NOPE].reshape(MLA_KV_RANK, -1),
                             w_ukv[:, :, MLA_NOPE:].reshape(MLA_KV_RANK, -1)], axis=1).astype(BF16)
  ckv_c, ckv_l = _split_passes(ckv)
  kpe_c, kpe_l = _split_passes(kpe)
  ckv_keys = jnp.concatenate([ckv_l, cache[0]], axis=1)
  kpe_keys = jnp.concatenate([kpe_l, jnp.tile(cache[1], (1, 1, 4))], axis=1)
  ckv_all = jnp.concatenate([ckv[:N_CTX], ckv_keys.reshape(-1, MLA_KV_RANK)], axis=0)
  kn, v = _mla_kv(ckv_all, w_ukv_x)
  qn_c, qn_l = _split_passes(qn)
  qp_c, qp_l = _split_passes(qp)
  scale = (MLA_NOPE + MLA_ROPE) ** -0.5
  oc = _pair_attention(qn_c, kn[:N_CTX].reshape(BATCH, SEQ, -1), v[:N_CTX].reshape(BATCH, SEQ, -1),
                       scale, qp_c, kpe_c)
  ol = _pair_attention(qn_l, kn[N_CTX:].reshape(DEC_BATCH, LAT_KEYS, -1),
                       v[N_CTX:].reshape(DEC_BATCH, LAT_KEYS, -1), scale, qp_l, kpe_keys)
  state = (ckv_c, kpe_c[:, :, :MLA_ROPE])
  return _join_passes(oc, ol), state


def _layer(layer, p, x, cond8, cache, tables):
  sh1, sc1, g1, sh2, sc2, g2 = _modulation(cond8, p['ada_w'], p['ada_b'])
  kind = layer % 3
  if kind == 0:
    o, state = _diff_mixer(layer, p, x, sc1, sh1, cache, tables)
  elif kind == 1:
    o, state = _na_mixer(p, x, sc1, sh1, cache)
  else:
    o, state = _mla_mixer(p, x, sc1, sh1, cache, tables)
  x = _out_proj(o, p['w_o'].astype(BF16), x, g1)
  x = _peer_layer(x, p, sc2, sh2, g2)
  return x, state


def kernel(x_prompt, x_sample, cache_l0_k, cache_l0_v, cache_l1_k, cache_l1_v, cache_l2_ckv, cache_l2_kpe, cache_l3_k, cache_l3_v, c, c_ctx, l0_norm1_g, l0_norm2_g, l0_ada_w, l0_ada_b, l0_w_qkv, l0_w_o, l0_lam_q1, l0_lam_k1, l0_lam_q2, l0_lam_k2, l0_subln_g, l0_peer_wq, l0_peer_k1, l0_peer_k2, l0_peer_u, l0_peer_v, l1_norm1_g, l1_norm2_g, l1_ada_w, l1_ada_b, l1_w_qkv, l1_w_o, l1_rpb, l1_peer_wq, l1_peer_k1, l1_peer_k2, l1_peer_u, l1_peer_v, l2_norm1_g, l2_norm2_g, l2_ada_w, l2_ada_b, l2_w_in, l2_q_norm_g, l2_w_uq, l2_kv_norm_g, l2_w_ukv, l2_w_o, l2_peer_wq, l2_peer_k1, l2_peer_k2, l2_peer_u, l2_peer_v, l3_norm1_g, l3_norm2_g, l3_ada_w, l3_ada_b, l3_w_qkv, l3_w_o, l3_lam_q1, l3_lam_k1, l3_lam_q2, l3_lam_k2, l3_subln_g, l3_peer_wq, l3_peer_k1, l3_peer_k2, l3_peer_u, l3_peer_v, final_norm_g):
  common = lambda n1, n2, aw, ab, wq, k1, k2, u, v: dict(
      norm1_g=n1.reshape(1, D_MODEL), norm2_g=n2.reshape(1, D_MODEL), ada_w=aw, ada_b=ab,
      peer_wq=wq.astype(BF16), peer_k1=k1.astype(BF16), peer_k2=k2.astype(BF16),
      peer_u=u.astype(BF16), peer_vt=v.T.astype(BF16))
  p0 = dict(common(l0_norm1_g, l0_norm2_g, l0_ada_w, l0_ada_b, l0_peer_wq, l0_peer_k1, l0_peer_k2,
                   l0_peer_u, l0_peer_v),
            w_qkv=l0_w_qkv, w_o=l0_w_o, lam_q1=l0_lam_q1, lam_k1=l0_lam_k1, lam_q2=l0_lam_q2,
            lam_k2=l0_lam_k2, subln_g=l0_subln_g)
  p1 = dict(common(l1_norm1_g, l1_norm2_g, l1_ada_w, l1_ada_b, l1_peer_wq, l1_peer_k1, l1_peer_k2,
                   l1_peer_u, l1_peer_v),
            w_qkv=l1_w_qkv, w_o=l1_w_o, rpb=l1_rpb)
  p2 = dict(common(l2_norm1_g, l2_norm2_g, l2_ada_w, l2_ada_b, l2_peer_wq, l2_peer_k1, l2_peer_k2,
                   l2_peer_u, l2_peer_v),
            w_in=l2_w_in, q_norm_g=l2_q_norm_g, w_uq=l2_w_uq, kv_norm_g=l2_kv_norm_g,
            w_ukv=l2_w_ukv, w_o=l2_w_o)
  p3 = dict(common(l3_norm1_g, l3_norm2_g, l3_ada_w, l3_ada_b, l3_peer_wq, l3_peer_k1, l3_peer_k2,
                   l3_peer_u, l3_peer_v),
            w_qkv=l3_w_qkv, w_o=l3_w_o, lam_q1=l3_lam_q1, lam_k1=l3_lam_k1, lam_q2=l3_lam_q2,
            lam_k2=l3_lam_k2, subln_g=l3_subln_g)
  params = (p0, p1, p2, p3)
  caches = ((cache_l0_k, cache_l0_v), (cache_l1_k, cache_l1_v),
            (cache_l2_ckv, cache_l2_kpe), (cache_l3_k, cache_l3_v))
  tables = dict(da=_rope_table(DA_QK // 2, LANES // DA_QK),
                mla=_rope_table(MLA_ROPE // 2, LANES // MLA_ROPE))

  cond8 = jnp.zeros((8, D_MODEL), F32).at[0].set(c_ctx).at[1:1 + DEC_BATCH].set(c)
  x = jnp.concatenate([x_prompt.reshape(N_CTX, D_MODEL), x_sample.reshape(N_LAT, D_MODEL)], axis=0)
  states = []
  for layer in range(DEPTH):
    x, st = _layer(layer, params[layer], x, cond8, caches[layer], tables)
    states.extend(st)
  y = _final_norm(x, final_norm_g.reshape(1, D_MODEL))
  return (y[:N_CTX].reshape(BATCH, SEQ, D_MODEL), y[N_CTX:].reshape(DEC_BATCH, DEC_SEQ, D_MODEL),
          *states)
```

```python
import functools
import math

import numpy as np
import jax
import jax.numpy as jnp
from jax import lax
from jax.experimental import pallas as pl
from jax.experimental.pallas import tpu as pltpu

F32 = jnp.float32
BF16 = jnp.bfloat16

D_MODEL = 1024
BATCH = 32
SEQ = 256
DEPTH = 4
DEC_BATCH = 2
DEC_SEQ = 4096
PAST_LEN = 512
GRID_W = 64
GRID_R = DEC_SEQ // GRID_W
EPS = 1e-6
ROPE_BASE = 10000.0
NEG_INF = -1e30

DA_HEADS = 8
DA_QK = 64
DA_V = 128
NA_HEADS = 16
NA_HD = 64
NA_KH = 8
NA_KW = 16
MLA_HEADS = 16
MLA_NOPE = 64
MLA_ROPE = 32
MLA_V = 64
MLA_Q_RANK = 384
MLA_KV_RANK = 256
PEER_HEADS = 8
PEER_NKEYS = 128
PEER_EXPERTS = PEER_NKEYS * PEER_NKEYS
PEER_DK = 256
PEER_TOPK = 16

N_CTX = BATCH * SEQ
N_LAT = DEC_BATCH * DEC_SEQ
N_TOK = N_CTX + N_LAT
LAT_KEYS = DEC_SEQ + PAST_LEN

LANES = 128
VMEM_LIMIT = 48 << 20

TM = 256
TQ = 256
TK = 512
NA_RB = 8
PEER_T = 256
PEER_EC = 2048
PEER_SUBC = 256

_NT = (((1,), (1,)), ((), ()))


def _cparams(sem):
  return pltpu.CompilerParams(dimension_semantics=sem, vmem_limit_bytes=VMEM_LIMIT)


def _mod_row(i, tm):
  nb_ctx = N_CTX // tm
  nb_bat = DEC_SEQ // tm
  return jnp.where(i < nb_ctx, 0, 1 + (i - nb_ctx) // nb_bat)


def _rope_blk(i, tm):
  nb_ctx = N_CTX // tm
  nb_bat = DEC_SEQ // tm
  return jnp.where(i < nb_ctx, nb_bat, (i - nb_ctx) % nb_bat)


def _rms_mod(x, g, sc, sh):
  y = x * lax.rsqrt(jnp.mean(x * x, axis=-1, keepdims=True) + EPS)
  return (y * g) * (1.0 + sc) + sh


def _rms(x, g):
  return x * lax.rsqrt(jnp.mean(x * x, axis=-1, keepdims=True) + EPS) * g


def _row_spec(width, tm=TM):
  return pl.BlockSpec((tm, width), lambda i: (i, 0))


def _full_spec(shape):
  return pl.BlockSpec(shape, lambda i: (0,) * len(shape))


def _mod_spec(tm=TM):
  return pl.BlockSpec((None, 1, D_MODEL), lambda i: (_mod_row(i, tm), 0, 0))


def _mod_body(c_ref, w_ref, b_ref, o_ref):
  c = c_ref[...]
  s = c / (1.0 + jnp.exp(-c))
  o_ref[...] = jnp.dot(s, w_ref[...], precision=lax.Precision.HIGHEST,
                       preferred_element_type=F32) + b_ref[...]


def _modulation(cond8, ada_w, ada_b):
  n = ada_w.shape[1]
  tn = 1536
  out = pl.pallas_call(
      _mod_body,
      grid=(n // tn,),
      in_specs=[pl.BlockSpec((8, D_MODEL), lambda j: (0, 0)),
                pl.BlockSpec((D_MODEL, tn), lambda j: (0, j)),
                pl.BlockSpec((1, tn), lambda j: (0, j))],
      out_specs=pl.BlockSpec((8, tn), lambda j: (0, j)),
      out_shape=jax.ShapeDtypeStruct((8, n), F32),
      compiler_params=_cparams(("parallel",)),
      name="modulation",
  )(cond8, ada_w, ada_b.reshape(1, n))
  return [out[:, k * D_MODEL:(k + 1) * D_MODEL].reshape(8, 1, D_MODEL) for k in range(6)]


def _tile_lanes(t, reps):
  return jnp.concatenate([t] * reps, axis=1)


def _qkv_rope_body(x_ref, g_ref, sc_ref, sh_ref, w_ref, cos_ref, sin_ref,
                   q_ref, k_ref, v_ref, *, q_scale):
  h = _rms_mod(x_ref[...], g_ref[...], sc_ref[...], sh_ref[...]).astype(BF16)
  reps = D_MODEL // LANES
  cos = _tile_lanes(cos_ref[...], reps)
  sin = _tile_lanes(sin_ref[...], reps)
  d = D_MODEL
  dot = lambda a, b: jnp.dot(h, w_ref[:, a:b], preferred_element_type=F32)
  q_ref[...] = (dot(0, d) * cos + dot(d, 2 * d) * sin) * q_scale
  k_ref[...] = dot(2 * d, 3 * d) * cos + dot(3 * d, 4 * d) * sin
  v_ref[...] = dot(4 * d, 5 * d)


def _qkv_rope(x, g, sc, sh, w5, cos, sin, q_scale):
  n = x.shape[0]
  rope_spec = pl.BlockSpec((TM, LANES), lambda i: (_rope_blk(i, TM), 0))
  return pl.pallas_call(
      functools.partial(_qkv_rope_body, q_scale=q_scale),
      grid=(n // TM,),
      in_specs=[_row_spec(D_MODEL), _full_spec((1, D_MODEL)), _mod_spec(), _mod_spec(),
                _full_spec(w5.shape), rope_spec, rope_spec],
      out_specs=[_row_spec(D_MODEL)] * 3,
      out_shape=[jax.ShapeDtypeStruct((n, D_MODEL), F32)] * 3,
      compiler_params=_cparams(("parallel",)),
      name="qkv_rope_proj",
  )(x, g, sc, sh, w5, cos, sin)


def _qkv_plain_body(x_ref, g_ref, sc_ref, sh_ref, w_ref, q_ref, k_ref, v_ref, *, q_scale):
  h = _rms_mod(x_ref[...], g_ref[...], sc_ref[...], sh_ref[...]).astype(BF16)
  d = D_MODEL
  dot = lambda a, b: jnp.dot(h, w_ref[:, a:b], preferred_element_type=F32)
  q_ref[...] = dot(0, d) * q_scale
  k_ref[...] = dot(d, 2 * d)
  v_ref[...] = dot(2 * d, 3 * d)


def _qkv_plain(x, g, sc, sh, w3, q_scale):
  n = x.shape[0]
  return pl.pallas_call(
      functools.partial(_qkv_plain_body, q_scale=q_scale),
      grid=(n // TM,),
      in_specs=[_row_spec(D_MODEL), _full_spec((1, D_MODEL)), _mod_spec(), _mod_spec(),
                _full_spec(w3.shape)],
      out_specs=[_row_spec(D_MODEL)] * 3,
      out_shape=[jax.ShapeDtypeStruct((n, D_MODEL), F32)] * 3,
      compiler_params=_cparams(("parallel",)),
      name="qkv_proj",
  )(x, g, sc, sh, w3)


def _mla_in_body(x_ref, g_ref, sc_ref, sh_ref, w_ref, qg_ref, kvg_ref, cos_ref, sin_ref,
                 cq_ref, ckv_ref, kpe_ref):
  h = _rms_mod(x_ref[...], g_ref[...], sc_ref[...], sh_ref[...]).astype(BF16)
  z = jnp.dot(h, w_ref[...], preferred_element_type=F32)
  a, b = MLA_Q_RANK, MLA_Q_RANK + MLA_KV_RANK
  cq_ref[...] = _rms(z[:, :a], qg_ref[...]).astype(BF16)
  ckv_ref[...] = _rms(z[:, a:b], kvg_ref[...])
  kpe_ref[...] = z[:, b:b + LANES] * cos_ref[...] + z[:, b + LANES:] * sin_ref[...]


def _mla_in(x, g, sc, sh, w_in, qg, kvg, cos, sin):
  n = x.shape[0]
  rope_spec = pl.BlockSpec((TM, LANES), lambda i: (_rope_blk(i, TM), 0))
  return pl.pallas_call(
      _mla_in_body,
      grid=(n // TM,),
      in_specs=[_row_spec(D_MODEL), _full_spec((1, D_MODEL)), _mod_spec(), _mod_spec(),
                _full_spec(w_in.shape), _full_spec((1, MLA_Q_RANK)),
                _full_spec((1, MLA_KV_RANK)), rope_spec, rope_spec],
      out_specs=[_row_spec(MLA_Q_RANK), _row_spec(MLA_KV_RANK), _row_spec(LANES)],
      out_shape=[jax.ShapeDtypeStruct((n, MLA_Q_RANK), BF16),
                 jax.ShapeDtypeStruct((n, MLA_KV_RANK), F32),
                 jax.ShapeDtypeStruct((n, LANES), F32)],
      compiler_params=_cparams(("parallel",)),
      name="mla_in_proj",
  )(x, g, sc, sh, w_in, qg, kvg, cos, sin)


def _mla_q_body(cq_ref, w_ref, cos_ref, sin_ref, qn_ref, qp_ref):
  z = jnp.dot(cq_ref[...], w_ref[...], preferred_element_type=F32)
  pe = MLA_HEADS * MLA_ROPE
  reps = pe // LANES
  cos = _tile_lanes(cos_ref[...], reps)
  sin = _tile_lanes(sin_ref[...], reps)
  qn_ref[...] = z[:, :D_MODEL]
  qp_ref[...] = z[:, D_MODEL:D_MODEL + pe] * cos + z[:, D_MODEL + pe:] * sin


def _mla_q(cq, w_uq, cos, sin):
  n = cq.shape[0]
  pe = MLA_HEADS * MLA_ROPE
  rope_spec = pl.BlockSpec((TM, LANES), lambda i: (_rope_blk(i, TM), 0))
  return pl.pallas_call(
      _mla_q_body,
      grid=(n // TM,),
      in_specs=[_row_spec(MLA_Q_RANK), _full_spec(w_uq.shape), rope_spec, rope_spec],
      out_specs=[_row_spec(D_MODEL), _row_spec(pe)],
      out_shape=[jax.ShapeDtypeStruct((n, D_MODEL), F32), jax.ShapeDtypeStruct((n, pe), F32)],
      compiler_params=_cparams(("parallel",)),
      name="mla_q_proj",
  )(cq, w_uq, cos, sin)


def _mla_kv_body(c_ref, w_ref, kn_ref, v_ref):
  z = jnp.dot(c_ref[...].astype(BF16), w_ref[...], preferred_element_type=F32)
  kn_ref[...] = z[:, :D_MODEL]
  v_ref[...] = z[:, D_MODEL:]


def _mla_kv(ckv, w_ukv):
  n = ckv.shape[0]
  return pl.pallas_call(
      _mla_kv_body,
      grid=(n // TM,),
      in_specs=[_row_spec(MLA_KV_RANK), _full_spec(w_ukv.shape)],
      out_specs=[_row_spec(D_MODEL)] * 2,
      out_shape=[jax.ShapeDtypeStruct((n, D_MODEL), F32)] * 2,
      compiler_params=_cparams(("parallel",)),
      name="mla_kv_proj",
  )(ckv, w_ukv)


def _out_proj_body(o_ref, w_ref, x_ref, gate_ref, y_ref):
  y_ref[...] = x_ref[...] + gate_ref[...] * jnp.dot(
      o_ref[...].astype(BF16), w_ref[...], preferred_element_type=F32)


def _out_proj(o, w_o, x, gate):
  n = x.shape[0]
  return pl.pallas_call(
      _out_proj_body,
      grid=(n // TM,),
      in_specs=[_row_spec(D_MODEL), _full_spec(w_o.shape), _row_spec(D_MODEL), _mod_spec()],
      out_specs=_row_spec(D_MODEL),
      out_shape=jax.ShapeDtypeStruct((n, D_MODEL), F32),
      compiler_params=_cparams(("parallel",)),
      name="out_proj",
  )(o, w_o, x, gate)


def _final_norm_body(x_ref, g_ref, y_ref):
  y_ref[...] = _rms(x_ref[...], g_ref[...])


def _final_norm(x, g):
  n = x.shape[0]
  return pl.pallas_call(
      _final_norm_body,
      grid=(n // TM,),
      in_specs=[_row_spec(D_MODEL), _full_spec((1, D_MODEL))],
      out_specs=_row_spec(D_MODEL),
      out_shape=jax.ShapeDtypeStruct((n, D_MODEL), F32),
      compiler_params=_cparams(("parallel",)),
      name="final_norm",
  )(x, g)


def _online_update(s, m, l, acc, vb):
  m_new = jnp.maximum(m, jnp.max(s, axis=-1, keepdims=True))
  alpha = jnp.exp(m - m_new)
  p = jnp.exp(s - m_new)
  l_new = alpha * l + jnp.sum(p, axis=-1, keepdims=True)
  acc_new = alpha * acc + jnp.dot(p.astype(BF16), vb, preferred_element_type=F32)
  return m_new, l_new, acc_new


def _softmax_state(tq):
  return (jnp.full((tq, 1), NEG_INF, F32), jnp.zeros((tq, 1), F32), jnp.zeros((tq, LANES), F32))


def _diff_attn_body(lam_ref, g_ref, q_ref, k_ref, v_ref, o_ref, kb_s, vb_s, *,
                    n_keys, tk, lam_init):
  @pl.when(pl.program_id(2) == 0)
  def _():
    kb_s[...] = k_ref[...].astype(BF16)
    vb_s[...] = v_ref[...].astype(BF16)

  q = q_ref[...]
  tq = q.shape[0]
  lane = lax.broadcasted_iota(jnp.int32, q.shape, 1)
  q1 = jnp.where(lane < DA_QK, q, 0.0).astype(BF16)
  q2 = jnp.where(lane >= DA_QK, q, 0.0).astype(BF16)
  lv = lam_ref[...]
  lam = (jnp.exp(jnp.sum(lv[0:1] * lv[1:2], axis=-1, keepdims=True))
         - jnp.exp(jnp.sum(lv[2:3] * lv[3:4], axis=-1, keepdims=True)) + lam_init)

  def chunk(c, carry):
    st1, st2 = carry
    off = pl.multiple_of(c * tk, tk)
    kb = kb_s[pl.ds(off, tk), :]
    vb = vb_s[pl.ds(off, tk), :]
    s1 = lax.dot_general(q1, kb, _NT, preferred_element_type=F32)
    s2 = lax.dot_general(q2, kb, _NT, preferred_element_type=F32)
    return _online_update(s1, *st1, vb), _online_update(s2, *st2, vb)

  st1, st2 = lax.fori_loop(0, n_keys // tk, chunk, (_softmax_state(tq), _softmax_state(tq)))
  o = st1[2] / st1[1] - lam * (st2[2] / st2[1])
  o_ref[...] = _rms(o, g_ref[...]) * (1.0 - lam_init)


def _diff_attention(q, k, v, lam_rows, subln_g, layer):
  b, sq, _ = q.shape
  n_keys = k.shape[1]
  tq = min(TQ, sq)
  tk = min(TK, n_keys)
  lam_init = 0.8 - 0.6 * math.exp(-0.3 * layer)
  qspec = pl.BlockSpec((None, tq, LANES), lambda bi, h, qi: (bi, qi, h))
  kspec = pl.BlockSpec((None, n_keys, LANES), lambda bi, h, qi: (bi, 0, h))
  return pl.pallas_call(
      functools.partial(_diff_attn_body, n_keys=n_keys, tk=tk, lam_init=lam_init),
      grid=(b, DA_HEADS, sq // tq),
      in_specs=[pl.BlockSpec((8, LANES), lambda bi, h, qi: (0, 0)),
                pl.BlockSpec((1, LANES), lambda bi, h, qi: (0, 0)),
                qspec, kspec, kspec],
      out_specs=qspec,
      out_shape=jax.ShapeDtypeStruct((b, sq, D_MODEL), F32),
      scratch_shapes=[pltpu.VMEM((n_keys, LANES), BF16)] * 2,
      compiler_params=_cparams(("parallel", "parallel", "arbitrary")),
      name="diff_attention",
  )(lam_rows, subln_g, q, k, v)


def _pair_attn_body(*refs, n_keys, tk, scale, with_pe):
  if with_pe:
    q_ref, qp_ref, k_ref, kp_ref, v_ref, o_ref, kb_s, vb_s = refs
  else:
    q_ref, k_ref, v_ref, o_ref, kb_s, vb_s = refs

  @pl.when(pl.program_id(2) == 0)
  def _():
    kb_s[:, :LANES] = k_ref[...].astype(BF16)
    if with_pe:
      kb_s[:, LANES:] = kp_ref[...].astype(BF16)
    vb_s[...] = v_ref[...].astype(BF16)

  q = q_ref[...] * scale
  tq = q.shape[0]
  lane = lax.broadcasted_iota(jnp.int32, q.shape, 1)
  halves = (lane < NA_HD, lane >= NA_HD)
  qs = [jnp.where(hm, q, 0.0).astype(BF16) for hm in halves]
  if with_pe:
    qp = qp_ref[...] * scale
    base = (pl.program_id(1) % 2) * (2 * MLA_ROPE)
    qs = [jnp.concatenate(
        [qs[a], jnp.where((lane >= base + a * MLA_ROPE) & (lane < base + (a + 1) * MLA_ROPE),
                          qp, 0.0).astype(BF16)], axis=1) for a in range(2)]

  def chunk(c, carry):
    off = pl.multiple_of(c * tk, tk)
    kb = kb_s[pl.ds(off, tk), :]
    vb = vb_s[pl.ds(off, tk), :]
    return tuple(
        _online_update(lax.dot_general(qs[a], kb, _NT, preferred_element_type=F32), *carry[a], vb)
        for a in range(2))

  st = lax.fori_loop(0, n_keys // tk, chunk, (_softmax_state(tq), _softmax_state(tq)))
  o_ref[...] = jnp.where(halves[0], st[0][2] / st[0][1], st[1][2] / st[1][1])


def _pair_attention(q, k, v, scale=1.0, q_pe=None, k_pe=None):
  b, sq, _ = q.shape
  n_keys = k.shape[1]
  tq = min(TQ, sq)
  tk = min(TK, n_keys)
  with_pe = q_pe is not None
  qspec = pl.BlockSpec((None, tq, LANES), lambda bi, h, qi: (bi, qi, h))
  kspec = pl.BlockSpec((None, n_keys, LANES), lambda bi, h, qi: (bi, 0, h))
  if with_pe:
    qpspec = pl.BlockSpec((None, tq, LANES), lambda bi, h, qi: (bi, qi, h // 2))
    kpspec = pl.BlockSpec((None, n_keys, LANES), lambda bi, h, qi: (bi, 0, 0))
    in_specs = [qspec, qpspec, kspec, kpspec, kspec]
    args = (q, q_pe, k, k_pe, v)
  else:
    in_specs = [qspec, kspec, kspec]
    args = (q, k, v)
  return pl.pallas_call(
      functools.partial(_pair_attn_body, n_keys=n_keys, tk=tk, scale=scale, with_pe=with_pe),
      grid=(b, D_MODEL // LANES, sq // tq),
      in_specs=in_specs,
      out_specs=qspec,
      out_shape=jax.ShapeDtypeStruct((b, sq, D_MODEL), F32),
      scratch_shapes=[pltpu.VMEM((n_keys, 2 * LANES if with_pe else LANES), BF16),
                      pltpu.VMEM((n_keys, LANES), BF16)],
      compiler_params=_cparams(("parallel", "parallel", "arbitrary")),
      name="pair_attention_pe" if with_pe else "pair_attention",
  )(*args)


def _na_lat_body(q_ref, k_ref, v_ref, kc_ref, vc_ref, bias_ref, o_ref, kb_s, vb_s, kcb_s, vcb_s):
  rb = pl.program_id(2)

  @pl.when(rb == 0)
  def _():
    kb_s[...] = k_ref[...].astype(BF16)
    vb_s[...] = v_ref[...].astype(BF16)
    kcb_s[...] = kc_ref[...].astype(BF16)
    vcb_s[...] = vc_ref[...].astype(BF16)

  n_loc = NA_KH * GRID_W
  lane = lax.broadcasted_iota(jnp.int32, (GRID_W, LANES), 1)
  halves = (lane < NA_HD, lane >= NA_HD)
  kc = kcb_s[...]
  vc = vcb_s[...]
  for rr in range(NA_RB):
    r = rb * NA_RB + rr
    start = jnp.clip(r - NA_KH // 2, 0, GRID_R - NA_KH)
    pat = jnp.where(r < NA_KH // 2, 1 + r,
                    jnp.where(r > GRID_R - NA_KH // 2, r - (GRID_R - NA_KH), 0))
    off = pl.multiple_of(start * GRID_W, GRID_W)
    kw = kb_s[pl.ds(off, n_loc), :]
    vw = vb_s[pl.ds(off, n_loc), :]
    q = q_ref[rr * GRID_W:(rr + 1) * GRID_W, :]
    outs = []
    for a in range(2):
      qa = jnp.where(halves[a], q, 0.0).astype(BF16)
      s_loc = lax.dot_general(qa, kw, _NT, preferred_element_type=F32) + bias_ref[pat, a]
      s_ctx = lax.dot_general(qa, kc, _NT, preferred_element_type=F32)
      m = jnp.maximum(jnp.max(s_loc, axis=-1, keepdims=True),
                      jnp.max(s_ctx, axis=-1, keepdims=True))
      p_loc = jnp.exp(s_loc - m)
      p_ctx = jnp.exp(s_ctx - m)
      l = jnp.sum(p_loc, axis=-1, keepdims=True) + jnp.sum(p_ctx, axis=-1, keepdims=True)
      o = (jnp.dot(p_loc.astype(BF16), vw, preferred_element_type=F32)
           + jnp.dot(p_ctx.astype(BF16), vc, preferred_element_type=F32))
      outs.append(o / l)
    o_ref[rr * GRID_W:(rr + 1) * GRID_W, :] = jnp.where(halves[0], outs[0], outs[1])


def _na_lat_attention(q, k, v, kc, vc, bias):
  b = q.shape[0]
  n_pat = bias.shape[0]
  blk = NA_RB * GRID_W
  qspec = pl.BlockSpec((None, blk, LANES), lambda bi, h, r: (bi, r, h))
  kspec = pl.BlockSpec((None, DEC_SEQ, LANES), lambda bi, h, r: (bi, 0, h))
  cspec = pl.BlockSpec((None, PAST_LEN, LANES), lambda bi, h, r: (bi, 0, h))
  bspec = pl.BlockSpec((n_pat, 2, GRID_W, NA_KH * GRID_W), lambda bi, h, r: (0, h, 0, 0))
  return pl.pallas_call(
      _na_lat_body,
      grid=(b, D_MODEL // LANES, GRID_R // NA_RB),
      in_specs=[qspec, kspec, kspec, cspec, cspec, bspec],
      out_specs=qspec,
      out_shape=jax.ShapeDtypeStruct((b, DEC_SEQ, D_MODEL), F32),
      scratch_shapes=[pltpu.VMEM((DEC_SEQ, LANES), BF16)] * 2
      + [pltpu.VMEM((PAST_LEN, LANES), BF16)] * 2,
      compiler_params=_cparams(("parallel", "parallel", "arbitrary")),
      name="na_lat_attention",
  )(q, k, v, kc, vc, bias)


def _top16_sorted(s):
  t = s.shape[1]
  rows = lax.broadcasted_iota(jnp.int32, (PEER_TOPK, t), 0)

  def step(r, carry):
    s, v = carry
    m = jnp.max(s, axis=0, keepdims=True)
    return jnp.where(s == m, NEG_INF, s), jnp.where(rows == r, m, v)

  _, v = lax.fori_loop(0, PEER_TOPK, step, (s, jnp.zeros((PEER_TOPK, t), F32)), unroll=True)
  return v


def _router_body(x_ref, g_ref, sc_ref, sh_ref, wq_ref, k1_ref, k2_ref,
                 hb_ref, a1_ref, n1_ref, b2_ref, r2_ref, h_s):
  @pl.when(pl.program_id(1) == 0)
  def _():
    h = _rms_mod(x_ref[...], g_ref[...], sc_ref[...], sh_ref[...]).astype(BF16)
    h_s[...] = h
    hb_ref[...] = h

  qh = jnp.dot(h_s[...], wq_ref[...], preferred_element_type=F32)
  half = PEER_DK // 2
  s1 = lax.dot_general(k1_ref[...], qh[:, :half].astype(BF16), _NT, preferred_element_type=F32)
  s2 = lax.dot_general(k2_ref[...], qh[:, half:].astype(BF16), _NT, preferred_element_type=F32)
  s1 = s1 - jnp.max(s1, axis=0, keepdims=True)
  s2 = s2 - jnp.max(s2, axis=0, keepdims=True)
  v1 = _top16_sorted(s1)
  v2 = _top16_sorted(s2)
  hk = PEER_TOPK // 2
  cand = jnp.concatenate([v1[r:r + 1] + v2 for r in range(hk)] + [v1[hk:] + v2[0:1]], axis=0)

  def step(_, carry):
    c, th = carry
    m = jnp.max(c, axis=0, keepdims=True)
    return jnp.where(c == m, NEG_INF, c), m

  _, th = lax.fori_loop(0, PEER_TOPK, step, (cand, jnp.zeros_like(cand[:1])), unroll=True)
  z = jnp.sum(jnp.where(cand >= th, jnp.exp(cand), 0.0), axis=0, keepdims=True)
  n1 = jnp.zeros_like(s1)
  r2 = jnp.zeros_like(s2)
  for r in range(PEER_TOPK):
    n1 = n1 + jnp.where(s1 + v2[r:r + 1] >= th, 1.0, 0.0)
    r2 = r2 + jnp.where(v2[r:r + 1] > s2, 1.0, 0.0)
  a1_ref[...] = jnp.exp(s1)
  n1_ref[...] = n1
  b2_ref[...] = (jnp.exp(s2) * (1.0 / z)).astype(BF16)
  r2_ref[...] = r2.astype(BF16)


def _peer_router(x, g, sc, sh, wq, k1, k2):
  n = x.shape[0]
  tm = PEER_T
  row = lambda w: pl.BlockSpec((tm, w), lambda i, h: (i, 0))
  full = lambda shape: pl.BlockSpec(shape, lambda i, h: (0,) * len(shape))
  modspec = pl.BlockSpec((None, 1, D_MODEL), lambda i, h: (_mod_row(i, tm), 0, 0))
  keyspec = pl.BlockSpec((None, PEER_NKEYS, PEER_DK // 2), lambda i, h: (h, 0, 0))
  tspec = pl.BlockSpec((None, PEER_NKEYS, tm), lambda i, h: (h, 0, i))
  return pl.pallas_call(
      _router_body,
      grid=(n // tm, PEER_HEADS),
      in_specs=[row(D_MODEL), full((1, D_MODEL)), modspec, modspec,
                pl.BlockSpec((D_MODEL, PEER_DK), lambda i, h: (0, h)), keyspec, keyspec],
      out_specs=[row(D_MODEL), tspec, tspec, tspec, tspec],
      out_shape=[jax.ShapeDtypeStruct((n, D_MODEL), BF16)]
      + [jax.ShapeDtypeStruct((PEER_HEADS, PEER_NKEYS, n), dt) for dt in (F32, F32, BF16, BF16)],
      scratch_shapes=[pltpu.VMEM((tm, D_MODEL), BF16)],
      compiler_params=_cparams(("parallel", "arbitrary")),
      name="peer_router",
  )(x, g, sc, sh, wq, k1, k2)


def _gelu(x):
  return 0.5 * x * (1.0 + lax.erf(x * np.float32(math.sqrt(0.5))))


def _peer_mix_body(hb_ref, u_ref, vt_ref, a1_ref, n1_ref, b2_ref, r2_ref, x_ref, gate_ref,
                   y_ref, acc_s, s_s, p_s):
  c = pl.program_id(1)

  @pl.when(c == 0)
  def _():
    acc_s[...] = jnp.zeros_like(acc_s)

  hb = hb_ref[...]
  t = hb.shape[0]
  pk = 16
  for j in range(PEER_EC // PEER_SUBC):
    rows = slice(j * PEER_SUBC, (j + 1) * PEER_SUBC)
    s_s[rows, :] = lax.dot_general(u_ref[rows, :], hb, _NT, preferred_element_type=F32)
  for j in range(PEER_EC // PEER_SUBC):
    for ii in range(PEER_SUBC // PEER_NKEYS):
      i1 = j * (PEER_SUBC // PEER_NKEYS) + ii
      w = [jnp.zeros((pk, t), BF16) for _ in range(PEER_NKEYS // pk)]
      for h in range(PEER_HEADS):
        a_row = jnp.broadcast_to(a1_ref[h, i1:i1 + 1, :], (pk, t)).astype(BF16)
        n_row = jnp.broadcast_to(n1_ref[h, i1:i1 + 1, :], (pk, t)).astype(BF16)
        for sub in range(PEER_NKEYS // pk):
          i2 = slice(sub * pk, (sub + 1) * pk)
          prod = a_row * b2_ref[h, i2, :]
          w[sub] = w[sub] + jnp.where(r2_ref[h, i2, :] < n_row, prod, jnp.zeros_like(prod))
      for sub in range(PEER_NKEYS // pk):
        row0 = i1 * PEER_NKEYS + sub * pk
        p_s[row0:row0 + pk, :] = w[sub] * _gelu(s_s[row0:row0 + pk, :]).astype(BF16)
    rows = slice(j * PEER_SUBC, (j + 1) * PEER_SUBC)
    acc_s[...] += jnp.dot(vt_ref[:, rows], p_s[rows, :], preferred_element_type=F32)

  @pl.when(c == pl.num_programs(1) - 1)
  def _():
    y_ref[...] = x_ref[...] + gate_ref[...] * acc_s[...].T


def _peer_mix(hb, u, vt, a1, n1, b2, r2, x, gate):
  n = x.shape[0]
  t = PEER_T
  n_i1 = PEER_EC // PEER_NKEYS
  row = lambda w: pl.BlockSpec((t, w), lambda i, c: (i, 0))
  i1spec = pl.BlockSpec((PEER_HEADS, n_i1, t), lambda i, c: (0, c, i))
  i2spec = pl.BlockSpec((PEER_HEADS, PEER_NKEYS, t), lambda i, c: (0, 0, i))
  return pl.pallas_call(
      _peer_mix_body,
      grid=(n // t, PEER_EXPERTS // PEER_EC),
      in_specs=[row(D_MODEL),
                pl.BlockSpec((PEER_EC, D_MODEL), lambda i, c: (c, 0)),
                pl.BlockSpec((D_MODEL, PEER_EC), lambda i, c: (0, c)),
                i1spec, i1spec, i2spec, i2spec,
                row(D_MODEL),
                pl.BlockSpec((None, 1, D_MODEL), lambda i, c: (_mod_row(i, t), 0, 0))],
      out_specs=row(D_MODEL),
      out_shape=jax.ShapeDtypeStruct((n, D_MODEL), F32),
      scratch_shapes=[pltpu.VMEM((D_MODEL, t), F32),
                      pltpu.VMEM((PEER_EC, t), F32),
                      pltpu.VMEM((PEER_EC, t), BF16)],
      compiler_params=_cparams(("parallel", "arbitrary")),
      name="peer_mix",
  )(hb, u, vt, a1, n1, b2, r2, x, gate)


def _peer_layer(x, p, sc, sh, gate):
  hb, a1, n1, b2, r2 = _peer_router(x, p['norm2_g'], sc, sh, p['peer_wq'], p['peer_k1'],
                                    p['peer_k2'])
  return _peer_mix(hb, p['peer_u'], p['peer_vt'], a1, n1, b2, r2, x, gate)


def _rot_perm(width, group):
  j = np.arange(width)
  half = group // 2
  first = (j % group) < half
  return np.where(first, j + half, j - half), np.where(first, -1.0, 1.0).astype(np.float32)


def _rope_table(rot_dims, reps):
  half = rot_dims // 2
  t = jnp.arange(DEC_SEQ)
  inv = ROPE_BASE ** (-jnp.arange(half, dtype=F32) / half)
  parts_c, parts_s = [], []
  for pos in (t // GRID_W, t % GRID_W):
    ang = pos.astype(F32)[:, None] * inv[None, :]
    parts_c += [jnp.cos(ang), jnp.cos(ang)]
    parts_s += [jnp.sin(ang), jnp.sin(ang)]
  cos = jnp.tile(jnp.concatenate(parts_c, axis=1), (1, reps))
  sin = jnp.tile(jnp.concatenate(parts_s, axis=1), (1, reps))
  cos = jnp.concatenate([cos, jnp.ones((TM, LANES), F32)], axis=0)
  sin = jnp.concatenate([sin, jnp.zeros((TM, LANES), F32)], axis=0)
  return cos, sin


def _na_bias_table(rpb):
  reps = [NA_KH // 2] + list(range(NA_KH // 2)) + list(range(GRID_R - NA_KH // 2 + 1, GRID_R))
  cols = np.arange(GRID_W)
  col_start = np.clip(cols - NA_KW // 2, 0, GRID_W - NA_KW)
  col_mask = (cols[None, :] >= col_start[:, None]) & (cols[None, :] < col_start[:, None] + NA_KW)
  dc = np.clip(cols[None, :] - cols[:, None], -(NA_KW - 1), NA_KW - 1) + (NA_KW - 1)
  tabs = []
  for r in reps:
    ridx = np.clip(r - NA_KH // 2, 0, GRID_R - NA_KH) + np.arange(NA_KH)
    dr = ridx - r + (NA_KH - 1)
    bias = rpb[:, dr[None, :, None], dc[:, None, :]]
    bias = jnp.where(col_mask[None, :, None, :], bias, NEG_INF)
    tabs.append(bias.reshape(NA_HEADS, GRID_W, NA_KH * GRID_W))
  return jnp.stack(tabs, axis=0)


def _split_passes(a):
  return a[:N_CTX].reshape(BATCH, SEQ, -1), a[N_CTX:].reshape(DEC_BATCH, DEC_SEQ, -1)


def _join_passes(c, l):
  return jnp.concatenate([c.reshape(N_CTX, -1), l.reshape(N_LAT, -1)], axis=0)


def _diff_mixer(layer, p, x, sc, sh, cache, tables):
  w = p['w_qkv']
  d = D_MODEL
  wq, wk, wv = w[:, :d], w[:, d:2 * d], w[:, 2 * d:]
  rot = lambda m: _rotated_tiles(m, DA_QK // 2)
  w5 = jnp.concatenate([wq, rot(wq), wk, rot(wk), wv], axis=1).astype(BF16)
  cos, sin = tables['da']
  q, k, v = _qkv_rope(x, p['norm1_g'], sc, sh, w5, cos, sin, DA_QK ** -0.5)
  qc, ql = _split_passes(q)
  kc, kl = _split_passes(k)
  vc, vl = _split_passes(v)
  lam_rows = jnp.zeros((8, LANES), F32)
  for i, nme in enumerate(('lam_q1', 'lam_k1', 'lam_q2', 'lam_k2')):
    lam_rows = lam_rows.at[i, :DA_QK].set(p[nme])
  g = p['subln_g'].reshape(1, DA_V)
  oc = _diff_attention(qc, kc, vc, lam_rows, g, layer)
  kl = jnp.concatenate([kl, cache[0].reshape(DEC_BATCH, PAST_LEN, d)], axis=1)
  vl = jnp.concatenate([vl, cache[1].reshape(DEC_BATCH, PAST_LEN, d)], axis=1)
  ol = _diff_attention(ql, kl, vl, lam_rows, g, layer)
  state = (kc.reshape(BATCH, SEQ, DA_HEADS, 2 * DA_QK), vc.reshape(BATCH, SEQ, DA_HEADS, DA_V))
  return _join_passes(oc, ol), state


def _rotated_tiles(w, group):
  src, sign = _rot_perm(w.shape[1], group)
  return w[:, src] * sign[None, :]


def _na_mixer(p, x, sc, sh, cache):
  d = D_MODEL
  q, k, v = _qkv_plain(x, p['norm1_g'], sc, sh, p['w_qkv'].astype(BF16), NA_HD ** -0.5)
  qc, ql = _split_passes(q)
  kc, kl = _split_passes(k)
  vc, vl = _split_passes(v)
  oc = _pair_attention(qc, kc, vc)
  bias = _na_bias_table(p['rpb'])
  ol = _na_lat_attention(ql, kl, vl, cache[0].reshape(DEC_BATCH, PAST_LEN, d),
                         cache[1].reshape(DEC_BATCH, PAST_LEN, d), bias)
  state = (kc.reshape(BATCH, SEQ, NA_HEADS, NA_HD), vc.reshape(BATCH, SEQ, NA_HEADS, NA_HD))
  return _join_passes(oc, ol), state


def _mla_mixer(p, x, sc, sh, cache, tables):
  a, b = MLA_Q_RANK, MLA_Q_RANK + MLA_KV_RANK
  w_in = p['w_in']
  kpe_w = w_in[:, b:]
  w_in_x = jnp.concatenate([w_in[:, :b], jnp.tile(kpe_w, (1, 4)),
                            jnp.tile(_rotated_tiles(kpe_w, MLA_ROPE // 2), (1, 4))],
                           axis=1).astype(BF16)
  cos, sin = tables['mla']
  cq, ckv, kpe = _mla_in(x, p['norm1_g'], sc, sh, w_in_x, p['q_norm_g'].reshape(1, a),
                         p['kv_norm_g'].reshape(1, MLA_KV_RANK), cos, sin)
  w_uq = p['w_uq'].reshape(a, MLA_HEADS, MLA_NOPE + MLA_ROPE)
  w_qn = w_uq[:, :, :MLA_NOPE].reshape(a, -1)
  w_qp = w_uq[:, :, MLA_NOPE:].reshape(a, -1)
  w_uq_x = jnp.concatenate([w_qn, w_qp, _rotated_tiles(w_qp, MLA_ROPE // 2)], axis=1).astype(BF16)
  qn, qp = _mla_q(cq, w_uq_x, cos, sin)
  w_ukv = p['w_ukv'].reshape(MLA_KV_RANK, MLA_HEADS, MLA_NOPE + MLA_V)
  w_ukv_x = jnp.concatenate([w_ukv[:, :, :MLA_NOPE].reshape(MLA_KV_RANK, -1),
                             w_ukv[:, :, MLA_NOPE:].reshape(MLA_KV_RANK, -1)], axis=1).astype(BF16)
  ckv_c, ckv_l = _split_passes(ckv)
  kpe_c, kpe_l = _split_passes(kpe)
  ckv_keys = jnp.concatenate([ckv_l, cache[0]], axis=1)
  kpe_keys = jnp.concatenate([kpe_l, jnp.tile(cache[1], (1, 1, 4))], axis=1)
  ckv_all = jnp.concatenate([ckv[:N_CTX], ckv_keys.reshape(-1, MLA_KV_RANK)], axis=0)
  kn, v = _mla_kv(ckv_all, w_ukv_x)
  qn_c, qn_l = _split_passes(qn)
  qp_c, qp_l = _split_passes(qp)
  scale = (MLA_NOPE + MLA_ROPE) ** -0.5
  oc = _pair_attention(qn_c, kn[:N_CTX].reshape(BATCH, SEQ, -1), v[:N_CTX].reshape(BATCH, SEQ, -1),
                       scale, qp_c, kpe_c)
  ol = _pair_attention(qn_l, kn[N_CTX:].reshape(DEC_BATCH, LAT_KEYS, -1),
                       v[N_CTX:].reshape(DEC_BATCH, LAT_KEYS, -1), scale, qp_l, kpe_keys)
  state = (ckv_c, kpe_c[:, :, :MLA_ROPE])
  return _join_passes(oc, ol), state


def _layer(layer, p, x, cond8, cache, tables):
  sh1, sc1, g1, sh2, sc2, g2 = _modulation(cond8, p['ada_w'], p['ada_b'])
  kind = layer % 3
  if kind == 0:
    o, state = _diff_mixer(layer, p, x, sc1, sh1, cache, tables)
  elif kind == 1:
    o, state = _na_mixer(p, x, sc1, sh1, cache)
  else:
    o, state = _mla_mixer(p, x, sc1, sh1, cache, tables)
  x = _out_proj(o, p['w_o'].astype(BF16), x, g1)
  x = _peer_layer(x, p, sc2, sh2, g2)
  return x, state


def kernel(x_prompt, x_sample, cache_l0_k, cache_l0_v, cache_l1_k, cache_l1_v, cache_l2_ckv, cache_l2_kpe, cache_l3_k, cache_l3_v, c, c_ctx, l0_norm1_g, l0_norm2_g, l0_ada_w, l0_ada_b, l0_w_qkv, l0_w_o, l0_lam_q1, l0_lam_k1, l0_lam_q2, l0_lam_k2, l0_subln_g, l0_peer_wq, l0_peer_k1, l0_peer_k2, l0_peer_u, l0_peer_v, l1_norm1_g, l1_norm2_g, l1_ada_w, l1_ada_b, l1_w_qkv, l1_w_o, l1_rpb, l1_peer_wq, l1_peer_k1, l1_peer_k2, l1_peer_u, l1_peer_v, l2_norm1_g, l2_norm2_g, l2_ada_w, l2_ada_b, l2_w_in, l2_q_norm_g, l2_w_uq, l2_kv_norm_g, l2_w_ukv, l2_w_o, l2_peer_wq, l2_peer_k1, l2_peer_k2, l2_peer_u, l2_peer_v, l3_norm1_g, l3_norm2_g, l3_ada_w, l3_ada_b, l3_w_qkv, l3_w_o, l3_lam_q1, l3_lam_k1, l3_lam_q2, l3_lam_k2, l3_subln_g, l3_peer_wq, l3_peer_k1, l3_peer_k2, l3_peer_u, l3_peer_v, final_norm_g):
  common = lambda n1, n2, aw, ab, wq, k1, k2, u, v: dict(
      norm1_g=n1.reshape(1, D_MODEL), norm2_g=n2.reshape(1, D_MODEL), ada_w=aw, ada_b=ab,
      peer_wq=wq.astype(BF16), peer_k1=k1.astype(BF16), peer_k2=k2.astype(BF16),
      peer_u=u.astype(BF16), peer_vt=v.T.astype(BF16))
  p0 = dict(common(l0_norm1_g, l0_norm2_g, l0_ada_w, l0_ada_b, l0_peer_wq, l0_peer_k1, l0_peer_k2,
                   l0_peer_u, l0_peer_v),
            w_qkv=l0_w_qkv, w_o=l0_w_o, lam_q1=l0_lam_q1, lam_k1=l0_lam_k1, lam_q2=l0_lam_q2,
            lam_k2=l0_lam_k2, subln_g=l0_subln_g)
  p1 = dict(common(l1_norm1_g, l1_norm2_g, l1_ada_w, l1_ada_b, l1_peer_wq, l1_peer_k1, l1_peer_k2,
                   l1_peer_u, l1_peer_v),
            w_qkv=l1_w_qkv, w_o=l1_w_o, rpb=l1_rpb)
  p2 = dict(common(l2_norm1_g, l2_norm2_g, l2_ada_w, l2_ada_b, l2_peer_wq, l2_peer_k1, l2_peer_k2,
                   l2_peer_u, l2_peer_v),
            w_in=l2_w_in, q_norm_g=l2_q_norm_g, w_uq=l2_w_uq, kv_norm_g=l2_kv_norm_g,
            w_ukv=l2_w_ukv, w_o=l2_w_o)
  p3 = dict(common(l3_norm1_g, l3_norm2_g, l3_ada_w, l3_ada_b, l3_peer_wq, l3_peer_k1, l3_peer_k2,
                   l3_peer_u, l3_peer_v),
            w_qkv=l3_w_qkv, w_o=l3_w_o, lam_q1=l3_lam_q1, lam_k1=l3_lam_k1, lam_q2=l3_lam_q2,
            lam_k2=l3_lam_k2, subln_g=l3_subln_g)
  params = (p0, p1, p2, p3)
  caches = ((cache_l0_k, cache_l0_v), (cache_l1_k, cache_l1_v),
            (cache_l2_ckv, cache_l2_kpe), (cache_l3_k, cache_l3_v))
  tables = dict(da=_rope_table(DA_QK // 2, LANES // DA_QK),
                mla=_rope_table(MLA_ROPE // 2, LANES // MLA_ROPE))

  cond8 = jnp.zeros((8, D_MODEL), F32).at[0].set(c_ctx).at[1:1 + DEC_BATCH].set(c)
  x = jnp.concatenate([x_prompt.reshape(N_CTX, D_MODEL), x_sample.reshape(N_LAT, D_MODEL)], axis=0)
  states = []
  for layer in range(DEPTH):
    x, st = _layer(layer, params[layer], x, cond8, caches[layer], tables)
    states.extend(st)
  y = _final_norm(x, final_norm_g.reshape(1, D_MODEL))
  return (y[:N_CTX].reshape(BATCH, SEQ, D_MODEL), y[N_CTX:].reshape(DEC_BATCH, DEC_SEQ, D_MODEL),
          *states)
```

```python
import functools
import math

import numpy as np
import jax
import jax.numpy as jnp
from jax import lax
from jax.experimental import pallas as pl
from jax.experimental.pallas import tpu as pltpu

F32 = jnp.float32
BF16 = jnp.bfloat16

D_MODEL = 1024
BATCH = 32
SEQ = 256
DEPTH = 4
DEC_BATCH = 2
DEC_SEQ = 4096
PAST_LEN = 512
GRID_W = 64
GRID_R = DEC_SEQ // GRID_W
EPS = 1e-6
ROPE_BASE = 10000.0
NEG_INF = -1e30

DA_HEADS = 8
DA_QK = 64
DA_V = 128
NA_HEADS = 16
NA_HD = 64
NA_KH = 8
NA_KW = 16
MLA_HEADS = 16
MLA_NOPE = 64
MLA_ROPE = 32
MLA_V = 64
MLA_Q_RANK = 384
MLA_KV_RANK = 256
PEER_HEADS = 8
PEER_NKEYS = 128
PEER_EXPERTS = PEER_NKEYS * PEER_NKEYS
PEER_DK = 256
PEER_TOPK = 16

N_CTX = BATCH * SEQ
N_LAT = DEC_BATCH * DEC_SEQ
N_TOK = N_CTX + N_LAT
LAT_KEYS = DEC_SEQ + PAST_LEN

LANES = 128
VMEM_LIMIT = 48 << 20
VMEM_LIMIT_PEER = 56 << 20

TM = 256
TQ = 512
TK = 512
ATTN_UNROLL = 3
NA_RB = 8
PEER_T = 512
PEER_EC = 2048
PEER_SUBC = 256

_NT = (((1,), (1,)), ((), ()))


def _cparams(sem, vmem=VMEM_LIMIT):
  return pltpu.CompilerParams(dimension_semantics=sem, vmem_limit_bytes=vmem)


def _mod_row(i, tm):
  nb_ctx = N_CTX // tm
  nb_bat = DEC_SEQ // tm
  return jnp.where(i < nb_ctx, 0, 1 + (i - nb_ctx) // nb_bat)


def _rope_blk(i, tm):
  nb_ctx = N_CTX // tm
  nb_bat = DEC_SEQ // tm
  return jnp.where(i < nb_ctx, nb_bat, (i - nb_ctx) % nb_bat)


def _rms_mod(x, g, sc, sh):
  y = x * lax.rsqrt(jnp.mean(x * x, axis=-1, keepdims=True) + EPS)
  return (y * g) * (1.0 + sc) + sh


def _rms(x, g):
  return x * lax.rsqrt(jnp.mean(x * x, axis=-1, keepdims=True) + EPS) * g


def _row_spec(width, tm=TM):
  return pl.BlockSpec((tm, width), lambda i: (i, 0))


def _full_spec(shape):
  return pl.BlockSpec(shape, lambda i: (0,) * len(shape))


def _mod_spec(tm=TM):
  return pl.BlockSpec((None, 1, D_MODEL), lambda i: (_mod_row(i, tm), 0, 0))


def _mod_body(c_ref, w_ref, b_ref, o_ref):
  c = c_ref[...]
  s = c / (1.0 + jnp.exp(-c))
  o_ref[...] = jnp.dot(s, w_ref[...], precision=lax.Precision.HIGHEST,
                       preferred_element_type=F32) + b_ref[...]


def _modulation(cond8, ada_w, ada_b):
  n = ada_w.shape[1]
  tn = 1536
  out = pl.pallas_call(
      _mod_body,
      grid=(n // tn,),
      in_specs=[pl.BlockSpec((8, D_MODEL), lambda j: (0, 0)),
                pl.BlockSpec((D_MODEL, tn), lambda j: (0, j)),
                pl.BlockSpec((1, tn), lambda j: (0, j))],
      out_specs=pl.BlockSpec((8, tn), lambda j: (0, j)),
      out_shape=jax.ShapeDtypeStruct((8, n), F32),
      compiler_params=_cparams(("parallel",)),
      name="modulation",
  )(cond8, ada_w, ada_b.reshape(1, n))
  return [out[:, k * D_MODEL:(k + 1) * D_MODEL].reshape(8, 1, D_MODEL) for k in range(6)]


def _tile_lanes(t, reps):
  return jnp.concatenate([t] * reps, axis=1)


def _qkv_rope_body(x_ref, g_ref, sc_ref, sh_ref, w_ref, cos_ref, sin_ref,
                   q_ref, k_ref, v_ref, *, q_scale):
  h = _rms_mod(x_ref[...], g_ref[...], sc_ref[...], sh_ref[...]).astype(BF16)
  reps = D_MODEL // LANES
  cos = _tile_lanes(cos_ref[...], reps)
  sin = _tile_lanes(sin_ref[...], reps)
  d = D_MODEL
  dot = lambda a, b: jnp.dot(h, w_ref[:, a:b], preferred_element_type=F32)
  q_ref[...] = (dot(0, d) * cos + dot(d, 2 * d) * sin) * q_scale
  k_ref[...] = dot(2 * d, 3 * d) * cos + dot(3 * d, 4 * d) * sin
  v_ref[...] = dot(4 * d, 5 * d)


def _qkv_rope(x, g, sc, sh, w5, cos, sin, q_scale):
  n = x.shape[0]
  rope_spec = pl.BlockSpec((TM, LANES), lambda i: (_rope_blk(i, TM), 0))
  return pl.pallas_call(
      functools.partial(_qkv_rope_body, q_scale=q_scale),
      grid=(n // TM,),
      in_specs=[_row_spec(D_MODEL), _full_spec((1, D_MODEL)), _mod_spec(), _mod_spec(),
                _full_spec(w5.shape), rope_spec, rope_spec],
      out_specs=[_row_spec(D_MODEL)] * 3,
      out_shape=[jax.ShapeDtypeStruct((n, D_MODEL), F32)] * 3,
      compiler_params=_cparams(("parallel",)),
      name="qkv_rope_proj",
  )(x, g, sc, sh, w5, cos, sin)


def _qkv_plain_body(x_ref, g_ref, sc_ref, sh_ref, w_ref, q_ref, k_ref, v_ref, *, q_scale):
  h = _rms_mod(x_ref[...], g_ref[...], sc_ref[...], sh_ref[...]).astype(BF16)
  d = D_MODEL
  dot = lambda a, b: jnp.dot(h, w_ref[:, a:b], preferred_element_type=F32)
  q_ref[...] = dot(0, d) * q_scale
  k_ref[...] = dot(d, 2 * d)
  v_ref[...] = dot(2 * d, 3 * d)


def _qkv_plain(x, g, sc, sh, w3, q_scale):
  n = x.shape[0]
  return pl.pallas_call(
      functools.partial(_qkv_plain_body, q_scale=q_scale),
      grid=(n // TM,),
      in_specs=[_row_spec(D_MODEL), _full_spec((1, D_MODEL)), _mod_spec(), _mod_spec(),
                _full_spec(w3.shape)],
      out_specs=[_row_spec(D_MODEL)] * 3,
      out_shape=[jax.ShapeDtypeStruct((n, D_MODEL), F32)] * 3,
      compiler_params=_cparams(("parallel",)),
      name="qkv_proj",
  )(x, g, sc, sh, w3)


def _mla_in_body(x_ref, g_ref, sc_ref, sh_ref, w_ref, qg_ref, kvg_ref, cos_ref, sin_ref,
                 cq_ref, ckv_ref, kpe_ref):
  h = _rms_mod(x_ref[...], g_ref[...], sc_ref[...], sh_ref[...]).astype(BF16)
  z = jnp.dot(h, w_ref[...], preferred_element_type=F32)
  a, b = MLA_Q_RANK, MLA_Q_RANK + MLA_KV_RANK
  cq_ref[...] = _rms(z[:, :a], qg_ref[...]).astype(BF16)
  ckv_ref[...] = _rms(z[:, a:b], kvg_ref[...])
  kpe_ref[...] = z[:, b:b + LANES] * cos_ref[...] + z[:, b + LANES:] * sin_ref[...]


def _mla_in(x, g, sc, sh, w_in, qg, kvg, cos, sin):
  n = x.shape[0]
  rope_spec = pl.BlockSpec((TM, LANES), lambda i: (_rope_blk(i, TM), 0))
  return pl.pallas_call(
      _mla_in_body,
      grid=(n // TM,),
      in_specs=[_row_spec(D_MODEL), _full_spec((1, D_MODEL)), _mod_spec(), _mod_spec(),
                _full_spec(w_in.shape), _full_spec((1, MLA_Q_RANK)),
                _full_spec((1, MLA_KV_RANK)), rope_spec, rope_spec],
      out_specs=[_row_spec(MLA_Q_RANK), _row_spec(MLA_KV_RANK), _row_spec(LANES)],
      out_shape=[jax.ShapeDtypeStruct((n, MLA_Q_RANK), BF16),
                 jax.ShapeDtypeStruct((n, MLA_KV_RANK), F32),
                 jax.ShapeDtypeStruct((n, LANES), F32)],
      compiler_params=_cparams(("parallel",)),
      name="mla_in_proj",
  )(x, g, sc, sh, w_in, qg, kvg, cos, sin)


def _mla_q_body(cq_ref, w_ref, cos_ref, sin_ref, qn_ref, qp_ref):
  z = jnp.dot(cq_ref[...], w_ref[...], preferred_element_type=F32)
  pe = MLA_HEADS * MLA_ROPE
  reps = pe // LANES
  cos = _tile_lanes(cos_ref[...], reps)
  sin = _tile_lanes(sin_ref[...], reps)
  qn_ref[...] = z[:, :D_MODEL]
  qp_ref[...] = z[:, D_MODEL:D_MODEL + pe] * cos + z[:, D_MODEL + pe:] * sin


def _mla_q(cq, w_uq, cos, sin):
  n = cq.shape[0]
  pe = MLA_HEADS * MLA_ROPE
  rope_spec = pl.BlockSpec((TM, LANES), lambda i: (_rope_blk(i, TM), 0))
  return pl.pallas_call(
      _mla_q_body,
      grid=(n // TM,),
      in_specs=[_row_spec(MLA_Q_RANK), _full_spec(w_uq.shape), rope_spec, rope_spec],
      out_specs=[_row_spec(D_MODEL), _row_spec(pe)],
      out_shape=[jax.ShapeDtypeStruct((n, D_MODEL), F32), jax.ShapeDtypeStruct((n, pe), F32)],
      compiler_params=_cparams(("parallel",)),
      name="mla_q_proj",
  )(cq, w_uq, cos, sin)


def _mla_kv_body(c_ref, w_ref, kn_ref, v_ref):
  z = jnp.dot(c_ref[...].astype(BF16), w_ref[...], preferred_element_type=F32)
  kn_ref[...] = z[:, :D_MODEL]
  v_ref[...] = z[:, D_MODEL:]


def _mla_kv(ckv, w_ukv):
  n = ckv.shape[0]
  return pl.pallas_call(
      _mla_kv_body,
      grid=(n // TM,),
      in_specs=[_row_spec(MLA_KV_RANK), _full_spec(w_ukv.shape)],
      out_specs=[_row_spec(D_MODEL)] * 2,
      out_shape=[jax.ShapeDtypeStruct((n, D_MODEL), F32)] * 2,
      compiler_params=_cparams(("parallel",)),
      name="mla_kv_proj",
  )(ckv, w_ukv)


def _out_proj_body(oc_ref, ol_ref, w_ref, x_ref, gate_ref, y_ref):
  def run(o_ref):
    y_ref[...] = x_ref[...] + gate_ref[...] * jnp.dot(
        o_ref[...].astype(BF16), w_ref[...], preferred_element_type=F32)

  is_ctx = pl.program_id(0) < N_CTX // TM
  pl.when(is_ctx)(lambda: run(oc_ref))
  pl.when(jnp.logical_not(is_ctx))(lambda: run(ol_ref))


def _out_proj(oc, ol, w_o, x, gate):
  n = x.shape[0]
  nb_ctx = N_CTX // TM
  return pl.pallas_call(
      _out_proj_body,
      grid=(n // TM,),
      in_specs=[pl.BlockSpec((TM, D_MODEL), lambda i: (jnp.minimum(i, nb_ctx - 1), 0)),
                pl.BlockSpec((TM, D_MODEL), lambda i: (jnp.maximum(i - nb_ctx, 0), 0)),
                _full_spec(w_o.shape), _row_spec(D_MODEL), _mod_spec()],
      out_specs=_row_spec(D_MODEL),
      out_shape=jax.ShapeDtypeStruct((n, D_MODEL), F32),
      compiler_params=_cparams(("parallel",)),
      name="out_proj",
  )(oc.reshape(N_CTX, D_MODEL), ol.reshape(N_LAT, D_MODEL), w_o, x, gate)


def _final_norm_body(x_ref, g_ref, y_ref):
  y_ref[...] = _rms(x_ref[...], g_ref[...])


def _final_norm(x, g):
  n = x.shape[0]
  return pl.pallas_call(
      _final_norm_body,
      grid=(n // TM,),
      in_specs=[_row_spec(D_MODEL), _full_spec((1, D_MODEL))],
      out_specs=_row_spec(D_MODEL),
      out_shape=jax.ShapeDtypeStruct((n, D_MODEL), F32),
      compiler_params=_cparams(("parallel",)),
      name="final_norm",
  )(x, g)


def _online_update(s, m, l, acc, vb):
  m_new = jnp.maximum(m, jnp.max(s, axis=-1, keepdims=True))
  alpha = jnp.exp(m - m_new)
  p = jnp.exp(s - m_new)
  l_new = alpha * l + jnp.sum(p, axis=-1, keepdims=True)
  acc_new = alpha * acc + jnp.dot(p.astype(BF16), vb, preferred_element_type=F32)
  return m_new, l_new, acc_new


def _softmax_state(tq):
  return (jnp.full((tq, 1), NEG_INF, F32), jnp.zeros((tq, 1), F32), jnp.zeros((tq, LANES), F32))


def _pass_view(a, lat):
  w = a.shape[-1]
  if lat:
    return a.reshape(N_TOK // DEC_SEQ, DEC_SEQ, w), N_CTX // DEC_SEQ, DEC_BATCH
  return a.reshape(N_TOK // SEQ, SEQ, w), 0, BATCH


def _stage_keys(dst, lanes, own_ref, cache_ref):
  n_own = own_ref.shape[0]
  dst[:n_own, lanes] = own_ref[...].astype(BF16)
  if cache_ref is not None:
    dst[n_own:, lanes] = cache_ref[...].astype(BF16)


def _key_chunks(n_keys):
  tk = min(TK, n_keys)
  n = n_keys // tk
  return tk, n, (ATTN_UNROLL if n % ATTN_UNROLL == 0 else 1)


def _attn_specs(q, lat):
  qv, b0, nb = _pass_view(q, lat)
  sq = qv.shape[1]
  tq = min(TQ, sq)
  grid = (nb, D_MODEL // LANES, sq // tq)
  qspec = pl.BlockSpec((None, tq, LANES), lambda bi, h, qi: (bi + b0, qi, h))
  kspec = pl.BlockSpec((None, sq, LANES), lambda bi, h, qi: (bi + b0, 0, h))
  cspec = pl.BlockSpec((None, PAST_LEN, LANES), lambda bi, h, qi: (bi, 0, h))
  ospec = pl.BlockSpec((None, tq, LANES), lambda bi, h, qi: (bi, qi, h))
  n_keys = sq + (PAST_LEN if lat else 0)
  return grid, qspec, kspec, cspec, ospec, n_keys, (nb, sq, D_MODEL)


def _diff_attn_body(*refs, lam_init, cached):
  if cached:
    lam_ref, g_ref, q_ref, k_ref, v_ref, kc_ref, vc_ref, o_ref, kb_s, vb_s = refs
  else:
    lam_ref, g_ref, q_ref, k_ref, v_ref, o_ref, kb_s, vb_s = refs
    kc_ref = vc_ref = None

  @pl.when(pl.program_id(2) == 0)
  def _():
    _stage_keys(kb_s, slice(None), k_ref, kc_ref)
    _stage_keys(vb_s, slice(None), v_ref, vc_ref)

  q = q_ref[...]
  tq = q.shape[0]
  lane = lax.broadcasted_iota(jnp.int32, q.shape, 1)
  q1 = jnp.where(lane < DA_QK, q, 0.0).astype(BF16)
  q2 = jnp.where(lane >= DA_QK, q, 0.0).astype(BF16)
  lv = lam_ref[...]
  lam = (jnp.exp(jnp.sum(lv[0:1] * lv[1:2], axis=-1, keepdims=True))
         - jnp.exp(jnp.sum(lv[2:3] * lv[3:4], axis=-1, keepdims=True)) + lam_init)
  tk, n_chunks, unroll = _key_chunks(kb_s.shape[0])

  def chunk(c, carry):
    st1, st2 = carry
    off = pl.multiple_of(c * tk, tk)
    kb = kb_s[pl.ds(off, tk), :]
    vb = vb_s[pl.ds(off, tk), :]
    s1 = lax.dot_general(q1, kb, _NT, preferred_element_type=F32)
    s2 = lax.dot_general(q2, kb, _NT, preferred_element_type=F32)
    return _online_update(s1, *st1, vb), _online_update(s2, *st2, vb)

  st1, st2 = lax.fori_loop(0, n_chunks, chunk, (_softmax_state(tq), _softmax_state(tq)),
                           unroll=unroll)
  o = st1[2] / st1[1] - lam * (st2[2] / st2[1])
  o_ref[...] = _rms(o, g_ref[...]) * (1.0 - lam_init)


def _diff_attention(q, k, v, cache, lam_rows, subln_g, layer, lat):
  grid, qspec, kspec, cspec, ospec, n_keys, oshape = _attn_specs(q, lat)
  view = lambda a: _pass_view(a, lat)[0]
  lam_init = 0.8 - 0.6 * math.exp(-0.3 * layer)
  const = lambda shape: pl.BlockSpec(shape, lambda bi, h, qi: (0, 0))
  in_specs = [const((8, LANES)), const((1, LANES)), qspec, kspec, kspec]
  args = [lam_rows, subln_g, view(q), view(k), view(v)]
  if lat:
    in_specs += [cspec, cspec]
    args += [cache[0].reshape(DEC_BATCH, PAST_LEN, D_MODEL),
             cache[1].reshape(DEC_BATCH, PAST_LEN, D_MODEL)]
  return pl.pallas_call(
      functools.partial(_diff_attn_body, lam_init=lam_init, cached=lat),
      grid=grid, in_specs=in_specs, out_specs=ospec,
      out_shape=jax.ShapeDtypeStruct(oshape, F32),
      scratch_shapes=[pltpu.VMEM((n_keys, LANES), BF16)] * 2,
      compiler_params=_cparams(("parallel", "parallel", "arbitrary")),
      name="diff_attention",
  )(*args)


def _pair_attn_body(*refs, scale, with_pe, cached):
  refs = list(refs)
  q_ref = refs.pop(0)
  qp_ref = refs.pop(0) if with_pe else None
  k_ref = refs.pop(0)
  kp_ref = refs.pop(0) if with_pe else None
  v_ref = refs.pop(0)
  kc_ref = refs.pop(0) if cached else None
  kpc_ref = refs.pop(0) if (cached and with_pe) else None
  vc_ref = refs.pop(0) if cached else None
  o_ref, kb_s, vb_s = refs

  @pl.when(pl.program_id(2) == 0)
  def _():
    _stage_keys(kb_s, slice(0, LANES), k_ref, kc_ref)
    if with_pe:
      _stage_keys(kb_s, slice(LANES, 2 * LANES), kp_ref, kpc_ref)
    _stage_keys(vb_s, slice(None), v_ref, vc_ref)

  q = q_ref[...] * scale
  tq = q.shape[0]
  lane = lax.broadcasted_iota(jnp.int32, q.shape, 1)
  halves = (lane < NA_HD, lane >= NA_HD)
  qs = [jnp.where(hm, q, 0.0).astype(BF16) for hm in halves]
  if with_pe:
    qp = qp_ref[...] * scale
    base = (pl.program_id(1) % 2) * (2 * MLA_ROPE)
    qs = [jnp.concatenate(
        [qs[a], jnp.where((lane >= base + a * MLA_ROPE) & (lane < base + (a + 1) * MLA_ROPE),
                          qp, 0.0).astype(BF16)], axis=1) for a in range(2)]
  tk, n_chunks, unroll = _key_chunks(kb_s.shape[0])

  def chunk(c, carry):
    off = pl.multiple_of(c * tk, tk)
    kb = kb_s[pl.ds(off, tk), :]
    vb = vb_s[pl.ds(off, tk), :]
    return tuple(
        _online_update(lax.dot_general(qs[a], kb, _NT, preferred_element_type=F32), *carry[a], vb)
        for a in range(2))

  st = lax.fori_loop(0, n_chunks, chunk, (_softmax_state(tq), _softmax_state(tq)), unroll=unroll)
  o_ref[...] = jnp.where(halves[0], st[0][2] / st[0][1], st[1][2] / st[1][1])


def _pair_attention(q, k, v, lat, cache=None, scale=1.0, q_pe=None, k_pe=None):
  grid, qspec, kspec, cspec, ospec, n_keys, oshape = _attn_specs(q, lat)
  view = lambda a: _pass_view(a, lat)[0]
  b0 = _pass_view(q, lat)[1]
  with_pe = q_pe is not None
  tq, sq = qspec.block_shape[1], kspec.block_shape[1]
  in_specs, args = [qspec], [view(q)]
  if with_pe:
    in_specs.append(pl.BlockSpec((None, tq, LANES), lambda bi, h, qi: (bi + b0, qi, h // 2)))
    args.append(view(q_pe))
  in_specs.append(kspec)
  args.append(view(k))
  if with_pe:
    in_specs.append(pl.BlockSpec((None, sq, LANES), lambda bi, h, qi: (bi + b0, 0, 0)))
    args.append(view(k_pe))
  in_specs.append(kspec)
  args.append(view(v))
  if lat:
    in_specs.append(cspec)
    args.append(cache[0])
    if with_pe:
      in_specs.append(pl.BlockSpec((None, PAST_LEN, LANES), lambda bi, h, qi: (bi, 0, 0)))
      args.append(cache[1])
    in_specs.append(cspec)
    args.append(cache[-1])
  return pl.pallas_call(
      functools.partial(_pair_attn_body, scale=scale, with_pe=with_pe, cached=lat),
      grid=grid, in_specs=in_specs, out_specs=ospec,
      out_shape=jax.ShapeDtypeStruct(oshape, F32),
      scratch_shapes=[pltpu.VMEM((n_keys, 2 * LANES if with_pe else LANES), BF16),
                      pltpu.VMEM((n_keys, LANES), BF16)],
      compiler_params=_cparams(("parallel", "parallel", "arbitrary")),
      name="pair_attention_pe" if with_pe else "pair_attention",
  )(*args)


def _na_lat_body(q_ref, k_ref, v_ref, kc_ref, vc_ref, bias_ref, o_ref, kb_s, vb_s, kcb_s, vcb_s):
  rb = pl.program_id(2)

  @pl.when(rb == 0)
  def _():
    kb_s[...] = k_ref[...].astype(BF16)
    vb_s[...] = v_ref[...].astype(BF16)
    kcb_s[...] = kc_ref[...].astype(BF16)
    vcb_s[...] = vc_ref[...].astype(BF16)

  n_loc = NA_KH * GRID_W
  lane = lax.broadcasted_iota(jnp.int32, (GRID_W, LANES), 1)
  halves = (lane < NA_HD, lane >= NA_HD)
  kc = kcb_s[...]
  vc = vcb_s[...]
  for rr in range(NA_RB):
    r = rb * NA_RB + rr
    start = jnp.clip(r - NA_KH // 2, 0, GRID_R - NA_KH)
    pat = jnp.where(r < NA_KH // 2, 1 + r,
                    jnp.where(r > GRID_R - NA_KH // 2, r - (GRID_R - NA_KH), 0))
    off = pl.multiple_of(start * GRID_W, GRID_W)
    kw = kb_s[pl.ds(off, n_loc), :]
    vw = vb_s[pl.ds(off, n_loc), :]
    q = q_ref[rr * GRID_W:(rr + 1) * GRID_W, :]
    outs = []
    for a in range(2):
      qa = jnp.where(halves[a], q, 0.0).astype(BF16)
      s_loc = lax.dot_general(qa, kw, _NT, preferred_element_type=F32) + bias_ref[pat, a]
      s_ctx = lax.dot_general(qa, kc, _NT, preferred_element_type=F32)
      m = jnp.maximum(jnp.max(s_loc, axis=-1, keepdims=True),
                      jnp.max(s_ctx, axis=-1, keepdims=True))
      p_loc = jnp.exp(s_loc - m)
      p_ctx = jnp.exp(s_ctx - m)
      l = jnp.sum(p_loc, axis=-1, keepdims=True) + jnp.sum(p_ctx, axis=-1, keepdims=True)
      o = (jnp.dot(p_loc.astype(BF16), vw, preferred_element_type=F32)
           + jnp.dot(p_ctx.astype(BF16), vc, preferred_element_type=F32))
      outs.append(o / l)
    o_ref[rr * GRID_W:(rr + 1) * GRID_W, :] = jnp.where(halves[0], outs[0], outs[1])


def _na_lat_attention(q, k, v, kc, vc, bias):
  qv, b0, b = _pass_view(q, True)
  kv, vv = _pass_view(k, True)[0], _pass_view(v, True)[0]
  n_pat = bias.shape[0]
  blk = NA_RB * GRID_W
  qspec = pl.BlockSpec((None, blk, LANES), lambda bi, h, r: (bi + b0, r, h))
  kspec = pl.BlockSpec((None, DEC_SEQ, LANES), lambda bi, h, r: (bi + b0, 0, h))
  cspec = pl.BlockSpec((None, PAST_LEN, LANES), lambda bi, h, r: (bi, 0, h))
  bspec = pl.BlockSpec((n_pat, 2, GRID_W, NA_KH * GRID_W), lambda bi, h, r: (0, h, 0, 0))
  return pl.pallas_call(
      _na_lat_body,
      grid=(b, D_MODEL // LANES, GRID_R // NA_RB),
      in_specs=[qspec, kspec, kspec, cspec, cspec, bspec],
      out_specs=pl.BlockSpec((None, blk, LANES), lambda bi, h, r: (bi, r, h)),
      out_shape=jax.ShapeDtypeStruct((b, DEC_SEQ, D_MODEL), F32),
      scratch_shapes=[pltpu.VMEM((DEC_SEQ, LANES), BF16)] * 2
      + [pltpu.VMEM((PAST_LEN, LANES), BF16)] * 2,
      compiler_params=_cparams(("parallel", "parallel", "arbitrary")),
      name="na_lat_attention",
  )(qv, kv, vv, kc, vc, bias)


def _top16_sorted(s):
  t = s.shape[1]
  rows = lax.broadcasted_iota(jnp.int32, (PEER_TOPK, t), 0)

  def step(r, carry):
    s, v = carry
    m = jnp.max(s, axis=0, keepdims=True)
    return jnp.where(s == m, NEG_INF, s), jnp.where(rows == r, m, v)

  _, v = lax.fori_loop(0, PEER_TOPK, step, (s, jnp.zeros((PEER_TOPK, t), F32)), unroll=True)
  return v


def _router_body(x_ref, g_ref, sc_ref, sh_ref, wq_ref, k1_ref, k2_ref,
                 hb_ref, a1_ref, n1_ref, b2_ref, r2_ref, h_s):
  @pl.when(pl.program_id(1) == 0)
  def _():
    h = _rms_mod(x_ref[...], g_ref[...], sc_ref[...], sh_ref[...]).astype(BF16)
    h_s[...] = h
    hb_ref[...] = h

  qh = jnp.dot(h_s[...], wq_ref[...], preferred_element_type=F32)
  half = PEER_DK // 2
  s1 = lax.dot_general(k1_ref[...], qh[:, :half].astype(BF16), _NT, preferred_element_type=F32)
  s2 = lax.dot_general(k2_ref[...], qh[:, half:].astype(BF16), _NT, preferred_element_type=F32)
  s1 = s1 - jnp.max(s1, axis=0, keepdims=True)
  s2 = s2 - jnp.max(s2, axis=0, keepdims=True)
  v1 = _top16_sorted(s1)
  v2 = _top16_sorted(s2)
  hk = PEER_TOPK // 2
  cand = jnp.concatenate([v1[r:r + 1] + v2 for r in range(hk)] + [v1[hk:] + v2[0:1]], axis=0)

  def step(_, carry):
    c, th = carry
    m = jnp.max(c, axis=0, keepdims=True)
    return jnp.where(c == m, NEG_INF, c), m

  _, th = lax.fori_loop(0, PEER_TOPK, step, (cand, jnp.zeros_like(cand[:1])), unroll=True)
  z = jnp.sum(jnp.where(cand >= th, jnp.exp(cand), 0.0), axis=0, keepdims=True)
  n1 = jnp.zeros_like(s1)
  r2 = jnp.zeros_like(s2)
  for r in range(PEER_TOPK):
    n1 = n1 + jnp.where(s1 + v2[r:r + 1] >= th, 1.0, 0.0)
    r2 = r2 + jnp.where(v2[r:r + 1] > s2, 1.0, 0.0)
  a1_ref[...] = jnp.exp(s1)
  n1_ref[...] = n1
  b2_ref[...] = pltpu.bitcast((jnp.exp(s2) * (1.0 / z)).astype(BF16), jnp.uint32)
  r2_ref[...] = pltpu.bitcast(r2.astype(BF16), jnp.uint32)


def _peer_router(x, g, sc, sh, wq, k1, k2):
  n = x.shape[0]
  tm = PEER_T
  row = lambda w: pl.BlockSpec((tm, w), lambda i, h: (i, 0))
  full = lambda shape: pl.BlockSpec(shape, lambda i, h: (0,) * len(shape))
  modspec = pl.BlockSpec((None, 1, D_MODEL), lambda i, h: (_mod_row(i, tm), 0, 0))
  keyspec = pl.BlockSpec((None, PEER_NKEYS, PEER_DK // 2), lambda i, h: (h, 0, 0))
  tspec = pl.BlockSpec((None, PEER_NKEYS, tm), lambda i, h: (h, 0, i))
  pspec = pl.BlockSpec((None, PEER_NKEYS // 2, tm), lambda i, h: (h, 0, i))
  return pl.pallas_call(
      _router_body,
      grid=(n // tm, PEER_HEADS),
      in_specs=[row(D_MODEL), full((1, D_MODEL)), modspec, modspec,
                pl.BlockSpec((D_MODEL, PEER_DK), lambda i, h: (0, h)), keyspec, keyspec],
      out_specs=[row(D_MODEL), tspec, tspec, pspec, pspec],
      out_shape=[jax.ShapeDtypeStruct((n, D_MODEL), BF16)]
      + [jax.ShapeDtypeStruct((PEER_HEADS, PEER_NKEYS, n), F32)] * 2
      + [jax.ShapeDtypeStruct((PEER_HEADS, PEER_NKEYS // 2, n), jnp.uint32)] * 2,
      scratch_shapes=[pltpu.VMEM((tm, D_MODEL), BF16)],
      compiler_params=_cparams(("parallel", "arbitrary")),
      name="peer_router",
  )(x, g, sc, sh, wq, k1, k2)


def _gelu(x):
  return 0.5 * x * (1.0 + lax.erf(x * np.float32(math.sqrt(0.5))))


def _peer_mix_body(hb_ref, u_ref, vt_ref, a1_ref, n1_ref, b2_ref, r2_ref, x_ref, gate_ref,
                   y_ref, acc_s, s_s, p_s):
  c = pl.program_id(1)

  @pl.when(c == 0)
  def _():
    acc_s[...] = jnp.zeros_like(acc_s)

  hb = hb_ref[...]
  t = hb.shape[0]
  pk = 16
  for j in range(PEER_EC // PEER_SUBC):
    rows = slice(j * PEER_SUBC, (j + 1) * PEER_SUBC)
    s_s[rows, :] = lax.dot_general(u_ref[rows, :], hb, _NT, preferred_element_type=F32)
  for j in range(PEER_EC // PEER_SUBC):
    for ii in range(PEER_SUBC // PEER_NKEYS):
      i1 = j * (PEER_SUBC // PEER_NKEYS) + ii
      w = [jnp.zeros((pk, t), BF16) for _ in range(PEER_NKEYS // pk)]
      for h in range(PEER_HEADS):
        a_row = jnp.broadcast_to(a1_ref[h, i1:i1 + 1, :], (pk, t)).astype(BF16)
        n_row = jnp.broadcast_to(n1_ref[h, i1:i1 + 1, :], (pk, t)).astype(BF16)
        for sub in range(PEER_NKEYS // pk):
          words = slice(sub * pk // 2, (sub + 1) * pk // 2)
          prod = a_row * pltpu.bitcast(b2_ref[h, words, :], BF16)
          rank = pltpu.bitcast(r2_ref[h, words, :], BF16)
          w[sub] = w[sub] + jnp.where(rank < n_row, prod, jnp.zeros_like(prod))
      for sub in range(PEER_NKEYS // pk):
        row0 = i1 * PEER_NKEYS + sub * pk
        p_s[row0:row0 + pk, :] = w[sub] * _gelu(s_s[row0:row0 + pk, :]).astype(BF16)
    rows = slice(j * PEER_SUBC, (j + 1) * PEER_SUBC)
    acc_s[...] += jnp.dot(vt_ref[:, rows], p_s[rows, :], preferred_element_type=F32)

  @pl.when(c == pl.num_programs(1) - 1)
  def _():
    y_ref[...] = x_ref[...] + gate_ref[...] * acc_s[...].T


def _peer_mix(hb, u, vt, a1, n1, b2, r2, x, gate):
  n = x.shape[0]
  t = PEER_T
  n_i1 = PEER_EC // PEER_NKEYS
  row = lambda w: pl.BlockSpec((t, w), lambda i, c: (i, 0))
  i1spec = pl.BlockSpec((PEER_HEADS, n_i1, t), lambda i, c: (0, c, i))
  i2spec = pl.BlockSpec((PEER_HEADS, PEER_NKEYS // 2, t), lambda i, c: (0, 0, i))
  return pl.pallas_call(
      _peer_mix_body,
      grid=(n // t, PEER_EXPERTS // PEER_EC),
      in_specs=[row(D_MODEL),
                pl.BlockSpec((PEER_EC, D_MODEL), lambda i, c: (c, 0)),
                pl.BlockSpec((D_MODEL, PEER_EC), lambda i, c: (0, c)),
                i1spec, i1spec, i2spec, i2spec,
                row(D_MODEL),
                pl.BlockSpec((None, 1, D_MODEL), lambda i, c: (_mod_row(i, t), 0, 0))],
      out_specs=row(D_MODEL),
      out_shape=jax.ShapeDtypeStruct((n, D_MODEL), F32),
      scratch_shapes=[pltpu.VMEM((D_MODEL, t), F32),
                      pltpu.VMEM((PEER_EC, t), F32),
                      pltpu.VMEM((PEER_EC, t), BF16)],
      compiler_params=_cparams(("parallel", "arbitrary"), VMEM_LIMIT_PEER),
      name="peer_mix",
  )(hb, u, vt, a1, n1, b2, r2, x, gate)


def _peer_layer(x, p, sc, sh, gate):
  hb, a1, n1, b2, r2 = _peer_router(x, p['norm2_g'], sc, sh, p['peer_wq'], p['peer_k1'],
                                    p['peer_k2'])
  return _peer_mix(hb, p['peer_u'], p['peer_vt'], a1, n1, b2, r2, x, gate)


def _rotated_tiles(w, group):
  k, n = w.shape
  g = w.reshape(k, n // group, 2, group // 2)
  return jnp.concatenate([-g[:, :, 1:], g[:, :, :1]], axis=2).reshape(k, n)


def _rope_table(rot_dims, reps):
  half = rot_dims // 2
  t = jnp.arange(DEC_SEQ)
  inv = ROPE_BASE ** (-jnp.arange(half, dtype=F32) / half)
  parts_c, parts_s = [], []
  for pos in (t // GRID_W, t % GRID_W):
    ang = pos.astype(F32)[:, None] * inv[None, :]
    parts_c += [jnp.cos(ang), jnp.cos(ang)]
    parts_s += [jnp.sin(ang), jnp.sin(ang)]
  cos = jnp.tile(jnp.concatenate(parts_c, axis=1), (1, reps))
  sin = jnp.tile(jnp.concatenate(parts_s, axis=1), (1, reps))
  cos = jnp.concatenate([cos, jnp.ones((TM, LANES), F32)], axis=0)
  sin = jnp.concatenate([sin, jnp.zeros((TM, LANES), F32)], axis=0)
  return cos, sin


def _na_bias_table(rpb):
  reps = [NA_KH // 2] + list(range(NA_KH // 2)) + list(range(GRID_R - NA_KH // 2 + 1, GRID_R))
  cols = np.arange(GRID_W)
  col_start = np.clip(cols - NA_KW // 2, 0, GRID_W - NA_KW)
  col_mask = (cols[None, :] >= col_start[:, None]) & (cols[None, :] < col_start[:, None] + NA_KW)
  dc = np.clip(cols[None, :] - cols[:, None], -(NA_KW - 1), NA_KW - 1) + (NA_KW - 1)
  dr = np.stack([np.clip(r - NA_KH // 2, 0, GRID_R - NA_KH) + np.arange(NA_KH) - r + (NA_KH - 1)
                 for r in reps])
  onehot = (dc[:, :, None] == np.arange(2 * NA_KW - 1)).astype(np.float32)
  bias = jnp.einsum('hpjc,qwc->phqjw', rpb[:, dr, :], onehot, precision=lax.Precision.HIGHEST)
  bias = jnp.where(col_mask[None, None, :, None, :], bias, NEG_INF)
  return bias.reshape(len(reps), NA_HEADS, GRID_W, NA_KH * GRID_W)


def _ctx_rows(a, *shape):
  return a[:N_CTX].reshape(BATCH, SEQ, *shape)


def _diff_mixer(layer, p, x, sc, sh, cache, tables):
  w = p['w_qkv']
  d = D_MODEL
  wq, wk, wv = w[:, :d], w[:, d:2 * d], w[:, 2 * d:]
  rot = lambda m: _rotated_tiles(m, DA_QK // 2)
  w5 = jnp.concatenate([wq, rot(wq), wk, rot(wk), wv], axis=1).astype(BF16)
  cos, sin = tables['da']
  q, k, v = _qkv_rope(x, p['norm1_g'], sc, sh, w5, cos, sin, DA_QK ** -0.5)
  lam_rows = jnp.zeros((8, LANES), F32)
  for i, nme in enumerate(('lam_q1', 'lam_k1', 'lam_q2', 'lam_k2')):
    lam_rows = lam_rows.at[i, :DA_QK].set(p[nme])
  g = p['subln_g'].reshape(1, DA_V)
  oc = _diff_attention(q, k, v, None, lam_rows, g, layer, lat=False)
  ol = _diff_attention(q, k, v, cache, lam_rows, g, layer, lat=True)
  state = (_ctx_rows(k, DA_HEADS, 2 * DA_QK), _ctx_rows(v, DA_HEADS, DA_V))
  return oc, ol, state


def _na_mixer(p, x, sc, sh, cache):
  d = D_MODEL
  q, k, v = _qkv_plain(x, p['norm1_g'], sc, sh, p['w_qkv'].astype(BF16), NA_HD ** -0.5)
  oc = _pair_attention(q, k, v, lat=False)
  bias = _na_bias_table(p['rpb'])
  ol = _na_lat_attention(q, k, v, cache[0].reshape(DEC_BATCH, PAST_LEN, d),
                         cache[1].reshape(DEC_BATCH, PAST_LEN, d), bias)
  state = (_ctx_rows(k, NA_HEADS, NA_HD), _ctx_rows(v, NA_HEADS, NA_HD))
  return oc, ol, state


def _mla_mixer(p, x, sc, sh, cache, tables):
  a, b = MLA_Q_RANK, MLA_Q_RANK + MLA_KV_RANK
  w_in = p['w_in']
  kpe_w = w_in[:, b:]
  w_in_x = jnp.concatenate([w_in[:, :b], jnp.tile(kpe_w, (1, 4)),
                            jnp.tile(_rotated_tiles(kpe_w, MLA_ROPE // 2), (1, 4))],
                           axis=1).astype(BF16)
  cos, sin = tables['mla']
  cq, ckv, kpe = _mla_in(x, p['norm1_g'], sc, sh, w_in_x, p['q_norm_g'].reshape(1, a),
                         p['kv_norm_g'].reshape(1, MLA_KV_RANK), cos, sin)
  w_uq = p['w_uq'].reshape(a, MLA_HEADS, MLA_NOPE + MLA_ROPE)
  w_qn = w_uq[:, :, :MLA_NOPE].reshape(a, -1)
  w_qp = w_uq[:, :, MLA_NOPE:].reshape(a, -1)
  w_uq_x = jnp.concatenate([w_qn, w_qp, _rotated_tiles(w_qp, MLA_ROPE // 2)], axis=1).astype(BF16)
  qn, qp = _mla_q(cq, w_uq_x, cos, sin)
  w_ukv = p['w_ukv'].reshape(MLA_KV_RANK, MLA_HEADS, MLA_NOPE + MLA_V)
  w_ukv_x = jnp.concatenate([w_ukv[:, :, :MLA_NOPE].reshape(MLA_KV_RANK, -1),
                             w_ukv[:, :, MLA_NOPE:].reshape(MLA_KV_RANK, -1)], axis=1).astype(BF16)
  kn, v = _mla_kv(ckv, w_ukv_x)
  knc, vc = _mla_kv(cache[0].reshape(DEC_BATCH * PAST_LEN, MLA_KV_RANK), w_ukv_x)
  cached = (knc.reshape(DEC_BATCH, PAST_LEN, D_MODEL), jnp.tile(cache[1], (1, 1, LANES // MLA_ROPE)),
            vc.reshape(DEC_BATCH, PAST_LEN, D_MODEL))
  scale = (MLA_NOPE + MLA_ROPE) ** -0.5
  oc = _pair_attention(qn, kn, v, lat=False, scale=scale, q_pe=qp, k_pe=kpe)
  ol = _pair_attention(qn, kn, v, lat=True, cache=cached, scale=scale, q_pe=qp, k_pe=kpe)
  state = (_ctx_rows(ckv, MLA_KV_RANK), _ctx_rows(kpe[:, :MLA_ROPE], MLA_ROPE))
  return oc, ol, state


def _layer(layer, p, x, cond8, cache, tables):
  sh1, sc1, g1, sh2, sc2, g2 = _modulation(cond8, p['ada_w'], p['ada_b'])
  kind = layer % 3
  if kind == 0:
    oc, ol, state = _diff_mixer(layer, p, x, sc1, sh1, cache, tables)
  elif kind == 1:
    oc, ol, state = _na_mixer(p, x, sc1, sh1, cache)
  else:
    oc, ol, state = _mla_mixer(p, x, sc1, sh1, cache, tables)
  x = _out_proj(oc, ol, p['w_o'].astype(BF16), x, g1)
  x = _peer_layer(x, p, sc2, sh2, g2)
  return x, state


def kernel(x_prompt, x_sample, cache_l0_k, cache_l0_v, cache_l1_k, cache_l1_v, cache_l2_ckv, cache_l2_kpe, cache_l3_k, cache_l3_v, c, c_ctx, l0_norm1_g, l0_norm2_g, l0_ada_w, l0_ada_b, l0_w_qkv, l0_w_o, l0_lam_q1, l0_lam_k1, l0_lam_q2, l0_lam_k2, l0_subln_g, l0_peer_wq, l0_peer_k1, l0_peer_k2, l0_peer_u, l0_peer_v, l1_norm1_g, l1_norm2_g, l1_ada_w, l1_ada_b, l1_w_qkv, l1_w_o, l1_rpb, l1_peer_wq, l1_peer_k1, l1_peer_k2, l1_peer_u, l1_peer_v, l2_norm1_g, l2_norm2_g, l2_ada_w, l2_ada_b, l2_w_in, l2_q_norm_g, l2_w_uq, l2_kv_norm_g, l2_w_ukv, l2_w_o, l2_peer_wq, l2_peer_k1, l2_peer_k2, l2_peer_u, l2_peer_v, l3_norm1_g, l3_norm2_g, l3_ada_w, l3_ada_b, l3_w_qkv, l3_w_o, l3_lam_q1, l3_lam_k1, l3_lam_q2, l3_lam_k2, l3_subln_g, l3_peer_wq, l3_peer_k1, l3_peer_k2, l3_peer_u, l3_peer_v, final_norm_g):
  common = lambda n1, n2, aw, ab, wq, k1, k2, u, v: dict(
      norm1_g=n1.reshape(1, D_MODEL), norm2_g=n2.reshape(1, D_MODEL), ada_w=aw, ada_b=ab,
      peer_wq=wq.astype(BF16), peer_k1=k1.astype(BF16), peer_k2=k2.astype(BF16),
      peer_u=u.astype(BF16), peer_vt=v.T.astype(BF16))
  p0 = dict(common(l0_norm1_g, l0_norm2_g, l0_ada_w, l0_ada_b, l0_peer_wq, l0_peer_k1, l0_peer_k2,
                   l0_peer_u, l0_peer_v),
            w_qkv=l0_w_qkv, w_o=l0_w_o, lam_q1=l0_lam_q1, lam_k1=l0_lam_k1, lam_q2=l0_lam_q2,
            lam_k2=l0_lam_k2, subln_g=l0_subln_g)
  p1 = dict(common(l1_norm1_g, l1_norm2_g, l1_ada_w, l1_ada_b, l1_peer_wq, l1_peer_k1, l1_peer_k2,
                   l1_peer_u, l1_peer_v),
            w_qkv=l1_w_qkv, w_o=l1_w_o, rpb=l1_rpb)
  p2 = dict(common(l2_norm1_g, l2_norm2_g, l2_ada_w, l2_ada_b, l2_peer_wq, l2_peer_k1, l2_peer_k2,
                   l2_peer_u, l2_peer_v),
            w_in=l2_w_in, q_norm_g=l2_q_norm_g, w_uq=l2_w_uq, kv_norm_g=l2_kv_norm_g,
            w_ukv=l2_w_ukv, w_o=l2_w_o)
  p3 = dict(common(l3_norm1_g, l3_norm2_g, l3_ada_w, l3_ada_b, l3_peer_wq, l3_peer_k1, l3_peer_k2,
                   l3_peer_u, l3_peer_v),
            w_qkv=l3_w_qkv, w_o=l3_w_o, lam_q1=l3_lam_q1, lam_k1=l3_lam_k1, lam_q2=l3_lam_q2,
            lam_k2=l3_lam_k2, subln_g=l3_subln_g)
  params = (p0, p1, p2, p3)
  caches = ((cache_l0_k, cache_l0_v), (cache_l1_k, cache_l1_v),
            (cache_l2_ckv, cache_l2_kpe), (cache_l3_k, cache_l3_v))
  tables = dict(da=_rope_table(DA_QK // 2, LANES // DA_QK),
                mla=_rope_table(MLA_ROPE // 2, LANES // MLA_ROPE))

  cond8 = jnp.zeros((8, D_MODEL), F32).at[0].set(c_ctx).at[1:1 + DEC_BATCH].set(c)
  x = jnp.concatenate([x_prompt.reshape(N_CTX, D_MODEL), x_sample.reshape(N_LAT, D_MODEL)], axis=0)
  states = []
  for layer in range(DEPTH):
    x, st = _layer(layer, params[layer], x, cond8, caches[layer], tables)
    states.extend(st)
  y = _final_norm(x, final_norm_g.reshape(1, D_MODEL))
  return (y[:N_CTX].reshape(BATCH, SEQ, D_MODEL), y[N_CTX:].reshape(DEC_BATCH, DEC_SEQ, D_MODEL),
          *states)
```

```python
import functools
import math

import numpy as np
import jax
import jax.numpy as jnp
from jax import lax
from jax.experimental import pallas as pl
from jax.experimental.pallas import tpu as pltpu

F32 = jnp.float32
BF16 = jnp.bfloat16

D_MODEL = 1024
BATCH = 32
SEQ = 256
DEPTH = 4
DEC_BATCH = 2
DEC_SEQ = 4096
PAST_LEN = 512
GRID_W = 64
GRID_R = DEC_SEQ // GRID_W
EPS = 1e-6
ROPE_BASE = 10000.0
NEG_INF = -1e30

DA_HEADS = 8
DA_QK = 64
DA_V = 128
NA_HEADS = 16
NA_HD = 64
NA_KH = 8
NA_KW = 16
MLA_HEADS = 16
MLA_NOPE = 64
MLA_ROPE = 32
MLA_V = 64
MLA_Q_RANK = 384
MLA_KV_RANK = 256
PEER_HEADS = 8
PEER_NKEYS = 128
PEER_EXPERTS = PEER_NKEYS * PEER_NKEYS
PEER_DK = 256
PEER_TOPK = 16

N_CTX = BATCH * SEQ
N_LAT = DEC_BATCH * DEC_SEQ
N_TOK = N_CTX + N_LAT
LAT_KEYS = DEC_SEQ + PAST_LEN

LANES = 128
VMEM_LIMIT = 48 << 20
VMEM_LIMIT_PEER = 56 << 20

TM = 256
TQ = 512
TK = 512
ATTN_UNROLL = 3
NA_RB = 8
PEER_T = 512
PEER_EC = 2048
PEER_SUBC = 256

_NT = (((1,), (1,)), ((), ()))


def _cparams(sem, vmem=VMEM_LIMIT):
  return pltpu.CompilerParams(dimension_semantics=sem, vmem_limit_bytes=vmem)


def _mod_row(i, tm):
  nb_ctx = N_CTX // tm
  nb_bat = DEC_SEQ // tm
  return jnp.where(i < nb_ctx, 0, 1 + (i - nb_ctx) // nb_bat)


def _rope_blk(i, tm):
  nb_ctx = N_CTX // tm
  nb_bat = DEC_SEQ // tm
  return jnp.where(i < nb_ctx, nb_bat, (i - nb_ctx) % nb_bat)


def _rms_mod(x, g, sc, sh):
  y = x * lax.rsqrt(jnp.mean(x * x, axis=-1, keepdims=True) + EPS)
  return (y * g) * (1.0 + sc) + sh


def _rms(x, g):
  return x * lax.rsqrt(jnp.mean(x * x, axis=-1, keepdims=True) + EPS) * g


def _row_spec(width, tm=TM):
  return pl.BlockSpec((tm, width), lambda i: (i, 0))


def _full_spec(shape):
  return pl.BlockSpec(shape, lambda i: (0,) * len(shape))


def _mod_spec(tm=TM):
  return pl.BlockSpec((None, 1, D_MODEL), lambda i: (_mod_row(i, tm), 0, 0))


def _mod_body(c_ref, w_ref, b_ref, o_ref):
  c = c_ref[...]
  s = c / (1.0 + jnp.exp(-c))
  o_ref[...] = jnp.dot(s, w_ref[...], precision=lax.Precision.HIGHEST,
                       preferred_element_type=F32) + b_ref[...]


def _modulation(cond8, ada_w, ada_b):
  n = ada_w.shape[1]
  tn = 1536
  out = pl.pallas_call(
      _mod_body,
      grid=(n // tn,),
      in_specs=[pl.BlockSpec((8, D_MODEL), lambda j: (0, 0)),
                pl.BlockSpec((D_MODEL, tn), lambda j: (0, j)),
                pl.BlockSpec((1, tn), lambda j: (0, j))],
      out_specs=pl.BlockSpec((8, tn), lambda j: (0, j)),
      out_shape=jax.ShapeDtypeStruct((8, n), F32),
      compiler_params=_cparams(("parallel",)),
      name="modulation",
  )(cond8, ada_w, ada_b.reshape(1, n))
  return [out[:, k * D_MODEL:(k + 1) * D_MODEL].reshape(8, 1, D_MODEL) for k in range(6)]


def _tile_lanes(t, reps):
  return jnp.concatenate([t] * reps, axis=1)


def _qkv_rope_body(x_ref, g_ref, sc_ref, sh_ref, w_ref, cos_ref, sin_ref,
                   q_ref, k_ref, v_ref, *, q_scale):
  h = _rms_mod(x_ref[...], g_ref[...], sc_ref[...], sh_ref[...]).astype(BF16)
  reps = D_MODEL // LANES
  cos = _tile_lanes(cos_ref[...], reps)
  sin = _tile_lanes(sin_ref[...], reps)
  d = D_MODEL
  dot = lambda a, b: jnp.dot(h, w_ref[:, a:b], preferred_element_type=F32)
  q_ref[...] = (dot(0, d) * cos + dot(d, 2 * d) * sin) * q_scale
  k_ref[...] = dot(2 * d, 3 * d) * cos + dot(3 * d, 4 * d) * sin
  v_ref[...] = dot(4 * d, 5 * d)


def _qkv_rope(x, g, sc, sh, w5, cos, sin, q_scale):
  n = x.shape[0]
  rope_spec = pl.BlockSpec((TM, LANES), lambda i: (_rope_blk(i, TM), 0))
  return pl.pallas_call(
      functools.partial(_qkv_rope_body, q_scale=q_scale),
      grid=(n // TM,),
      in_specs=[_row_spec(D_MODEL), _full_spec((1, D_MODEL)), _mod_spec(), _mod_spec(),
                _full_spec(w5.shape), rope_spec, rope_spec],
      out_specs=[_row_spec(D_MODEL)] * 3,
      out_shape=[jax.ShapeDtypeStruct((n, D_MODEL), F32)] * 3,
      compiler_params=_cparams(("parallel",)),
      name="qkv_rope_proj",
  )(x, g, sc, sh, w5, cos, sin)


def _qkv_plain_body(x_ref, g_ref, sc_ref, sh_ref, w_ref, q_ref, k_ref, v_ref, *, q_scale):
  h = _rms_mod(x_ref[...], g_ref[...], sc_ref[...], sh_ref[...]).astype(BF16)
  d = D_MODEL
  dot = lambda a, b: jnp.dot(h, w_ref[:, a:b], preferred_element_type=F32)
  q_ref[...] = dot(0, d) * q_scale
  k_ref[...] = dot(d, 2 * d)
  v_ref[...] = dot(2 * d, 3 * d)


def _qkv_plain(x, g, sc, sh, w3, q_scale):
  n = x.shape[0]
  return pl.pallas_call(
      functools.partial(_qkv_plain_body, q_scale=q_scale),
      grid=(n // TM,),
      in_specs=[_row_spec(D_MODEL), _full_spec((1, D_MODEL)), _mod_spec(), _mod_spec(),
                _full_spec(w3.shape)],
      out_specs=[_row_spec(D_MODEL)] * 3,
      out_shape=[jax.ShapeDtypeStruct((n, D_MODEL), F32)] * 3,
      compiler_params=_cparams(("parallel",)),
      name="qkv_proj",
  )(x, g, sc, sh, w3)


def _mla_in_body(x_ref, g_ref, sc_ref, sh_ref, w_ref, qg_ref, kvg_ref, cos_ref, sin_ref,
                 cq_ref, ckv_ref, kpe_ref):
  h = _rms_mod(x_ref[...], g_ref[...], sc_ref[...], sh_ref[...]).astype(BF16)
  z = jnp.dot(h, w_ref[...], preferred_element_type=F32)
  a, b = MLA_Q_RANK, MLA_Q_RANK + MLA_KV_RANK
  cq_ref[...] = _rms(z[:, :a], qg_ref[...]).astype(BF16)
  ckv_ref[...] = _rms(z[:, a:b], kvg_ref[...])
  kpe_ref[...] = z[:, b:b + LANES] * cos_ref[...] + z[:, b + LANES:] * sin_ref[...]


def _mla_in(x, g, sc, sh, w_in, qg, kvg, cos, sin):
  n = x.shape[0]
  rope_spec = pl.BlockSpec((TM, LANES), lambda i: (_rope_blk(i, TM), 0))
  return pl.pallas_call(
      _mla_in_body,
      grid=(n // TM,),
      in_specs=[_row_spec(D_MODEL), _full_spec((1, D_MODEL)), _mod_spec(), _mod_spec(),
                _full_spec(w_in.shape), _full_spec((1, MLA_Q_RANK)),
                _full_spec((1, MLA_KV_RANK)), rope_spec, rope_spec],
      out_specs=[_row_spec(MLA_Q_RANK), _row_spec(MLA_KV_RANK), _row_spec(LANES)],
      out_shape=[jax.ShapeDtypeStruct((n, MLA_Q_RANK), BF16),
                 jax.ShapeDtypeStruct((n, MLA_KV_RANK), F32),
                 jax.ShapeDtypeStruct((n, LANES), F32)],
      compiler_params=_cparams(("parallel",)),
      name="mla_in_proj",
  )(x, g, sc, sh, w_in, qg, kvg, cos, sin)


def _mla_q_body(cq_ref, w_ref, cos_ref, sin_ref, qn_ref, qp_ref):
  z = jnp.dot(cq_ref[...], w_ref[...], preferred_element_type=F32)
  pe = MLA_HEADS * MLA_ROPE
  reps = pe // LANES
  cos = _tile_lanes(cos_ref[...], reps)
  sin = _tile_lanes(sin_ref[...], reps)
  qn_ref[...] = z[:, :D_MODEL]
  qp_ref[...] = z[:, D_MODEL:D_MODEL + pe] * cos + z[:, D_MODEL + pe:] * sin


def _mla_q(cq, w_uq, cos, sin):
  n = cq.shape[0]
  pe = MLA_HEADS * MLA_ROPE
  rope_spec = pl.BlockSpec((TM, LANES), lambda i: (_rope_blk(i, TM), 0))
  return pl.pallas_call(
      _mla_q_body,
      grid=(n // TM,),
      in_specs=[_row_spec(MLA_Q_RANK), _full_spec(w_uq.shape), rope_spec, rope_spec],
      out_specs=[_row_spec(D_MODEL), _row_spec(pe)],
      out_shape=[jax.ShapeDtypeStruct((n, D_MODEL), F32), jax.ShapeDtypeStruct((n, pe), F32)],
      compiler_params=_cparams(("parallel",)),
      name="mla_q_proj",
  )(cq, w_uq, cos, sin)


def _mla_kv_body(c_ref, w_ref, kn_ref, v_ref):
  z = jnp.dot(c_ref[...].astype(BF16), w_ref[...], preferred_element_type=F32)
  kn_ref[...] = z[:, :D_MODEL]
  v_ref[...] = z[:, D_MODEL:]


def _mla_kv(ckv, w_ukv):
  n = ckv.shape[0]
  return pl.pallas_call(
      _mla_kv_body,
      grid=(n // TM,),
      in_specs=[_row_spec(MLA_KV_RANK), _full_spec(w_ukv.shape)],
      out_specs=[_row_spec(D_MODEL)] * 2,
      out_shape=[jax.ShapeDtypeStruct((n, D_MODEL), F32)] * 2,
      compiler_params=_cparams(("parallel",)),
      name="mla_kv_proj",
  )(ckv, w_ukv)


def _out_proj_body(oc_ref, ol_ref, w_ref, x_ref, gate_ref, y_ref):
  def run(o_ref):
    y_ref[...] = x_ref[...] + gate_ref[...] * jnp.dot(
        o_ref[...].astype(BF16), w_ref[...], preferred_element_type=F32)

  is_ctx = pl.program_id(0) < N_CTX // TM
  pl.when(is_ctx)(lambda: run(oc_ref))
  pl.when(jnp.logical_not(is_ctx))(lambda: run(ol_ref))


def _out_proj(oc, ol, w_o, x, gate):
  n = x.shape[0]
  nb_ctx = N_CTX // TM
  return pl.pallas_call(
      _out_proj_body,
      grid=(n // TM,),
      in_specs=[pl.BlockSpec((TM, D_MODEL), lambda i: (jnp.minimum(i, nb_ctx - 1), 0)),
                pl.BlockSpec((TM, D_MODEL), lambda i: (jnp.maximum(i - nb_ctx, 0), 0)),
                _full_spec(w_o.shape), _row_spec(D_MODEL), _mod_spec()],
      out_specs=_row_spec(D_MODEL),
      out_shape=jax.ShapeDtypeStruct((n, D_MODEL), F32),
      compiler_params=_cparams(("parallel",)),
      name="out_proj",
  )(oc.reshape(N_CTX, D_MODEL), ol.reshape(N_LAT, D_MODEL), w_o, x, gate)


def _final_norm_body(x_ref, g_ref, y_ref):
  y_ref[...] = _rms(x_ref[...], g_ref[...])


def _final_norm(x, g):
  n = x.shape[0]
  return pl.pallas_call(
      _final_norm_body,
      grid=(n // TM,),
      in_specs=[_row_spec(D_MODEL), _full_spec((1, D_MODEL))],
      out_specs=_row_spec(D_MODEL),
      out_shape=jax.ShapeDtypeStruct((n, D_MODEL), F32),
      compiler_params=_cparams(("parallel",)),
      name="final_norm",
  )(x, g)


def _online_update(s, m, l, acc, vb):
  m_new = jnp.maximum(m, jnp.max(s, axis=-1, keepdims=True))
  alpha = jnp.exp(m - m_new)
  p = jnp.exp(s - m_new)
  l_new = alpha * l + jnp.sum(p, axis=-1, keepdims=True)
  acc_new = alpha * acc + jnp.dot(p.astype(BF16), vb, preferred_element_type=F32)
  return m_new, l_new, acc_new


def _softmax_state(tq):
  return (jnp.full((tq, 1), NEG_INF, F32), jnp.zeros((tq, 1), F32), jnp.zeros((tq, LANES), F32))


def _pass_view(a, lat):
  w = a.shape[-1]
  if lat:
    return a.reshape(N_TOK // DEC_SEQ, DEC_SEQ, w), N_CTX // DEC_SEQ, DEC_BATCH
  return a.reshape(N_TOK // SEQ, SEQ, w), 0, BATCH


def _stage_keys(dst, lanes, own_ref, cache_ref):
  n_own = own_ref.shape[0]
  dst[:n_own, lanes] = own_ref[...].astype(BF16)
  if cache_ref is not None:
    dst[n_own:, lanes] = cache_ref[...].astype(BF16)


def _key_chunks(n_keys):
  tk = min(TK, n_keys)
  n = n_keys // tk
  return tk, n, (ATTN_UNROLL if n % ATTN_UNROLL == 0 else 1)


def _attn_specs(q, lat):
  qv, b0, nb = _pass_view(q, lat)
  sq = qv.shape[1]
  tq = min(TQ, sq)
  grid = (nb, D_MODEL // LANES, sq // tq)
  qspec = pl.BlockSpec((None, tq, LANES), lambda bi, h, qi: (bi + b0, qi, h))
  kspec = pl.BlockSpec((None, sq, LANES), lambda bi, h, qi: (bi + b0, 0, h))
  cspec = pl.BlockSpec((None, PAST_LEN, LANES), lambda bi, h, qi: (bi, 0, h))
  ospec = pl.BlockSpec((None, tq, LANES), lambda bi, h, qi: (bi, qi, h))
  n_keys = sq + (PAST_LEN if lat else 0)
  return grid, qspec, kspec, cspec, ospec, n_keys, (nb, sq, D_MODEL)


def _diff_attn_body(*refs, lam_init, cached):
  if cached:
    lam_ref, g_ref, q_ref, k_ref, v_ref, kc_ref, vc_ref, o_ref, kb_s, vb_s = refs
  else:
    lam_ref, g_ref, q_ref, k_ref, v_ref, o_ref, kb_s, vb_s = refs
    kc_ref = vc_ref = None

  @pl.when(pl.program_id(2) == 0)
  def _():
    _stage_keys(kb_s, slice(None), k_ref, kc_ref)
    _stage_keys(vb_s, slice(None), v_ref, vc_ref)

  q = q_ref[...]
  tq = q.shape[0]
  lane = lax.broadcasted_iota(jnp.int32, q.shape, 1)
  q1 = jnp.where(lane < DA_QK, q, 0.0).astype(BF16)
  q2 = jnp.where(lane >= DA_QK, q, 0.0).astype(BF16)
  lv = lam_ref[...]
  lam = (jnp.exp(jnp.sum(lv[0:1] * lv[1:2], axis=-1, keepdims=True))
         - jnp.exp(jnp.sum(lv[2:3] * lv[3:4], axis=-1, keepdims=True)) + lam_init)
  tk, n_chunks, unroll = _key_chunks(kb_s.shape[0])

  def chunk(c, carry):
    st1, st2 = carry
    off = pl.multiple_of(c * tk, tk)
    kb = kb_s[pl.ds(off, tk), :]
    vb = vb_s[pl.ds(off, tk), :]
    s1 = lax.dot_general(q1, kb, _NT, preferred_element_type=F32)
    s2 = lax.dot_general(q2, kb, _NT, preferred_element_type=F32)
    return _online_update(s1, *st1, vb), _online_update(s2, *st2, vb)

  st1, st2 = lax.fori_loop(0, n_chunks, chunk, (_softmax_state(tq), _softmax_state(tq)),
                           unroll=unroll)
  o = st1[2] / st1[1] - lam * (st2[2] / st2[1])
  o_ref[...] = _rms(o, g_ref[...]) * (1.0 - lam_init)


def _diff_attention(q, k, v, cache, lam_rows, subln_g, layer, lat):
  grid, qspec, kspec, cspec, ospec, n_keys, oshape = _attn_specs(q, lat)
  view = lambda a: _pass_view(a, lat)[0]
  lam_init = 0.8 - 0.6 * math.exp(-0.3 * layer)
  const = lambda shape: pl.BlockSpec(shape, lambda bi, h, qi: (0, 0))
  in_specs = [const((8, LANES)), const((1, LANES)), qspec, kspec, kspec]
  args = [lam_rows, subln_g, view(q), view(k), view(v)]
  if lat:
    in_specs += [cspec, cspec]
    args += [cache[0].reshape(DEC_BATCH, PAST_LEN, D_MODEL),
             cache[1].reshape(DEC_BATCH, PAST_LEN, D_MODEL)]
  return pl.pallas_call(
      functools.partial(_diff_attn_body, lam_init=lam_init, cached=lat),
      grid=grid, in_specs=in_specs, out_specs=ospec,
      out_shape=jax.ShapeDtypeStruct(oshape, F32),
      scratch_shapes=[pltpu.VMEM((n_keys, LANES), BF16)] * 2,
      compiler_params=_cparams(("parallel", "parallel", "arbitrary")),
      name="diff_attention",
  )(*args)


def _pair_attn_body(*refs, scale, with_pe, cached):
  refs = list(refs)
  q_ref = refs.pop(0)
  qp_ref = refs.pop(0) if with_pe else None
  k_ref = refs.pop(0)
  kp_ref = refs.pop(0) if with_pe else None
  v_ref = refs.pop(0)
  kc_ref = refs.pop(0) if cached else None
  kpc_ref = refs.pop(0) if (cached and with_pe) else None
  vc_ref = refs.pop(0) if cached else None
  o_ref, kb_s, vb_s = refs

  @pl.when(pl.program_id(2) == 0)
  def _():
    _stage_keys(kb_s, slice(0, LANES), k_ref, kc_ref)
    if with_pe:
      _stage_keys(kb_s, slice(LANES, 2 * LANES), kp_ref, kpc_ref)
    _stage_keys(vb_s, slice(None), v_ref, vc_ref)

  q = q_ref[...] * scale
  tq = q.shape[0]
  lane = lax.broadcasted_iota(jnp.int32, q.shape, 1)
  halves = (lane < NA_HD, lane >= NA_HD)
  qs = [jnp.where(hm, q, 0.0).astype(BF16) for hm in halves]
  if with_pe:
    qp = qp_ref[...] * scale
    base = (pl.program_id(1) % 2) * (2 * MLA_ROPE)
    qs = [jnp.concatenate(
        [qs[a], jnp.where((lane >= base + a * MLA_ROPE) & (lane < base + (a + 1) * MLA_ROPE),
                          qp, 0.0).astype(BF16)], axis=1) for a in range(2)]
  tk, n_chunks, unroll = _key_chunks(kb_s.shape[0])

  def chunk(c, carry):
    off = pl.multiple_of(c * tk, tk)
    kb = kb_s[pl.ds(off, tk), :]
    vb = vb_s[pl.ds(off, tk), :]
    return tuple(
        _online_update(lax.dot_general(qs[a], kb, _NT, preferred_element_type=F32), *carry[a], vb)
        for a in range(2))

  st = lax.fori_loop(0, n_chunks, chunk, (_softmax_state(tq), _softmax_state(tq)), unroll=unroll)
  o_ref[...] = jnp.where(halves[0], st[0][2] / st[0][1], st[1][2] / st[1][1])


def _pair_attention(q, k, v, lat, cache=None, scale=1.0, q_pe=None, k_pe=None):
  grid, qspec, kspec, cspec, ospec, n_keys, oshape = _attn_specs(q, lat)
  view = lambda a: _pass_view(a, lat)[0]
  b0 = _pass_view(q, lat)[1]
  with_pe = q_pe is not None
  tq, sq = qspec.block_shape[1], kspec.block_shape[1]
  in_specs, args = [qspec], [view(q)]
  if with_pe:
    in_specs.append(pl.BlockSpec((None, tq, LANES), lambda bi, h, qi: (bi + b0, qi, h // 2)))
    args.append(view(q_pe))
  in_specs.append(kspec)
  args.append(view(k))
  if with_pe:
    in_specs.append(pl.BlockSpec((None, sq, LANES), lambda bi, h, qi: (bi + b0, 0, 0)))
    args.append(view(k_pe))
  in_specs.append(kspec)
  args.append(view(v))
  if lat:
    in_specs.append(cspec)
    args.append(cache[0])
    if with_pe:
      in_specs.append(pl.BlockSpec((None, PAST_LEN, LANES), lambda bi, h, qi: (bi, 0, 0)))
      args.append(cache[1])
    in_specs.append(cspec)
    args.append(cache[-1])
  return pl.pallas_call(
      functools.partial(_pair_attn_body, scale=scale, with_pe=with_pe, cached=lat),
      grid=grid, in_specs=in_specs, out_specs=ospec,
      out_shape=jax.ShapeDtypeStruct(oshape, F32),
      scratch_shapes=[pltpu.VMEM((n_keys, 2 * LANES if with_pe else LANES), BF16),
                      pltpu.VMEM((n_keys, LANES), BF16)],
      compiler_params=_cparams(("parallel", "parallel", "arbitrary")),
      name="pair_attention_pe" if with_pe else "pair_attention",
  )(*args)


def _na_lat_body(q_ref, k_ref, v_ref, kc_ref, vc_ref, bias_ref, o_ref, kb_s, vb_s, kcb_s, vcb_s):
  rb = pl.program_id(2)

  @pl.when(rb == 0)
  def _():
    kb_s[...] = k_ref[...].astype(BF16)
    vb_s[...] = v_ref[...].astype(BF16)
    kcb_s[...] = kc_ref[...].astype(BF16)
    vcb_s[...] = vc_ref[...].astype(BF16)

  n_loc = NA_KH * GRID_W
  lane = lax.broadcasted_iota(jnp.int32, (GRID_W, LANES), 1)
  halves = (lane < NA_HD, lane >= NA_HD)
  kc = kcb_s[...]
  vc = vcb_s[...]
  for rr in range(NA_RB):
    r = rb * NA_RB + rr
    start = jnp.clip(r - NA_KH // 2, 0, GRID_R - NA_KH)
    pat = jnp.where(r < NA_KH // 2, 1 + r,
                    jnp.where(r > GRID_R - NA_KH // 2, r - (GRID_R - NA_KH), 0))
    off = pl.multiple_of(start * GRID_W, GRID_W)
    kw = kb_s[pl.ds(off, n_loc), :]
    vw = vb_s[pl.ds(off, n_loc), :]
    q = q_ref[rr * GRID_W:(rr + 1) * GRID_W, :]
    outs = []
    for a in range(2):
      qa = jnp.where(halves[a], q, 0.0).astype(BF16)
      s_loc = lax.dot_general(qa, kw, _NT, preferred_element_type=F32) + bias_ref[pat, a]
      s_ctx = lax.dot_general(qa, kc, _NT, preferred_element_type=F32)
      m = jnp.maximum(jnp.max(s_loc, axis=-1, keepdims=True),
                      jnp.max(s_ctx, axis=-1, keepdims=True))
      p_loc = jnp.exp(s_loc - m)
      p_ctx = jnp.exp(s_ctx - m)
      l = jnp.sum(p_loc, axis=-1, keepdims=True) + jnp.sum(p_ctx, axis=-1, keepdims=True)
      o = (jnp.dot(p_loc.astype(BF16), vw, preferred_element_type=F32)
           + jnp.dot(p_ctx.astype(BF16), vc, preferred_element_type=F32))
      outs.append(o / l)
    o_ref[rr * GRID_W:(rr + 1) * GRID_W, :] = jnp.where(halves[0], outs[0], outs[1])


def _na_lat_attention(q, k, v, kc, vc, bias):
  qv, b0, b = _pass_view(q, True)
  kv, vv = _pass_view(k, True)[0], _pass_view(v, True)[0]
  n_pat = bias.shape[0]
  blk = NA_RB * GRID_W
  qspec = pl.BlockSpec((None, blk, LANES), lambda bi, h, r: (bi + b0, r, h))
  kspec = pl.BlockSpec((None, DEC_SEQ, LANES), lambda bi, h, r: (bi + b0, 0, h))
  cspec = pl.BlockSpec((None, PAST_LEN, LANES), lambda bi, h, r: (bi, 0, h))
  bspec = pl.BlockSpec((n_pat, 2, GRID_W, NA_KH * GRID_W), lambda bi, h, r: (0, h, 0, 0))
  return pl.pallas_call(
      _na_lat_body,
      grid=(b, D_MODEL // LANES, GRID_R // NA_RB),
      in_specs=[qspec, kspec, kspec, cspec, cspec, bspec],
      out_specs=pl.BlockSpec((None, blk, LANES), lambda bi, h, r: (bi, r, h)),
      out_shape=jax.ShapeDtypeStruct((b, DEC_SEQ, D_MODEL), F32),
      scratch_shapes=[pltpu.VMEM((DEC_SEQ, LANES), BF16)] * 2
      + [pltpu.VMEM((PAST_LEN, LANES), BF16)] * 2,
      compiler_params=_cparams(("parallel", "parallel", "arbitrary")),
      name="na_lat_attention",
  )(qv, kv, vv, kc, vc, bias)


SUBLANES = 8


def _sort_network(n):
  pairs = []
  p = 1
  while p < n:
    k = p
    while k >= 1:
      for j in range(k % p, n - k, 2 * k):
        for i in range(min(k, n - j - k)):
          if (i + j) // (2 * p) == (i + j + k) // (2 * p):
            pairs.append((i + j, i + j + k))
      k //= 2
    p *= 2
  return pairs


def _pop_heads(lists, hit, depth):
  for k in range(depth):
    lists[k] = jnp.where(hit, lists[k + 1], lists[k])


def _top16_sorted(s):
  t = s.shape[1]
  n = s.shape[0] // SUBLANES
  xs = [s[SUBLANES * k:SUBLANES * (k + 1), :] for k in range(n)]
  for i, j in _sort_network(n):
    xs[i], xs[j] = jnp.maximum(xs[i], xs[j]), jnp.minimum(xs[i], xs[j])
  rows = lax.broadcasted_iota(jnp.int32, (PEER_TOPK, t), 0)
  v = jnp.zeros((PEER_TOPK, t), F32)
  for r in range(PEER_TOPK):
    m = jnp.max(xs[0], axis=0, keepdims=True)
    v = jnp.where(rows == r, m, v)
    _pop_heads(xs, xs[0] == m, PEER_TOPK - 1 - r)
  return v


def _router_body(x_ref, g_ref, sc_ref, sh_ref, wq_ref, k1_ref, k2_ref,
                 hb_ref, a1_ref, n1_ref, b2_ref, r2_ref, h_s):
  @pl.when(pl.program_id(1) == 0)
  def _():
    h = _rms_mod(x_ref[...], g_ref[...], sc_ref[...], sh_ref[...]).astype(BF16)
    h_s[...] = h
    hb_ref[...] = h

  qh = jnp.dot(h_s[...], wq_ref[...], preferred_element_type=F32)
  half = PEER_DK // 2
  s1 = lax.dot_general(k1_ref[...], qh[:, :half].astype(BF16), _NT, preferred_element_type=F32)
  s2 = lax.dot_general(k2_ref[...], qh[:, half:].astype(BF16), _NT, preferred_element_type=F32)
  s1 = s1 - jnp.max(s1, axis=0, keepdims=True)
  s2 = s2 - jnp.max(s2, axis=0, keepdims=True)
  v1 = _top16_sorted(s1)
  v2 = _top16_sorted(s2)
  cand = [v1[:SUBLANES] + v2[k:k + 1] for k in range(PEER_TOPK)]
  tail = v1[SUBLANES:] + v2[0:1]
  z = jnp.zeros_like(v2[0:1])
  for r in range(PEER_TOPK):
    th = jnp.max(jnp.maximum(cand[0], tail), axis=0, keepdims=True)
    z = z + jnp.exp(th)
    tail = jnp.where(tail == th, NEG_INF, tail)
    _pop_heads(cand, cand[0] == th, PEER_TOPK - 1 - r)
  n1 = jnp.zeros_like(s1)
  r2 = jnp.zeros_like(s2)
  for r in range(PEER_TOPK):
    n1 = n1 + jnp.where(s1 + v2[r:r + 1] >= th, 1.0, 0.0)
    r2 = r2 + jnp.where(v2[r:r + 1] > s2, 1.0, 0.0)
  a1_ref[...] = jnp.exp(s1)
  n1_ref[...] = n1
  b2_ref[...] = pltpu.bitcast((jnp.exp(s2) * (1.0 / z)).astype(BF16), jnp.uint32)
  r2_ref[...] = pltpu.bitcast(r2.astype(BF16), jnp.uint32)


def _peer_router(x, g, sc, sh, wq, k1, k2):
  n = x.shape[0]
  tm = PEER_T
  row = lambda w: pl.BlockSpec((tm, w), lambda i, h: (i, 0))
  full = lambda shape: pl.BlockSpec(shape, lambda i, h: (0,) * len(shape))
  modspec = pl.BlockSpec((None, 1, D_MODEL), lambda i, h: (_mod_row(i, tm), 0, 0))
  keyspec = pl.BlockSpec((None, PEER_NKEYS, PEER_DK // 2), lambda i, h: (h, 0, 0))
  tspec = pl.BlockSpec((None, PEER_NKEYS, tm), lambda i, h: (h, 0, i))
  pspec = pl.BlockSpec((None, PEER_NKEYS // 2, tm), lambda i, h: (h, 0, i))
  return pl.pallas_call(
      _router_body,
      grid=(n // tm, PEER_HEADS),
      in_specs=[row(D_MODEL), full((1, D_MODEL)), modspec, modspec,
                pl.BlockSpec((D_MODEL, PEER_DK), lambda i, h: (0, h)), keyspec, keyspec],
      out_specs=[row(D_MODEL), tspec, tspec, pspec, pspec],
      out_shape=[jax.ShapeDtypeStruct((n, D_MODEL), BF16)]
      + [jax.ShapeDtypeStruct((PEER_HEADS, PEER_NKEYS, n), F32)] * 2
      + [jax.ShapeDtypeStruct((PEER_HEADS, PEER_NKEYS // 2, n), jnp.uint32)] * 2,
      scratch_shapes=[pltpu.VMEM((tm, D_MODEL), BF16)],
      compiler_params=_cparams(("parallel", "arbitrary")),
      name="peer_router",
  )(x, g, sc, sh, wq, k1, k2)


def _gelu(x):
  return 0.5 * x * (1.0 + lax.erf(x * np.float32(math.sqrt(0.5))))


def _peer_mix_body(hb_ref, u_ref, vt_ref, a1_ref, n1_ref, b2_ref, r2_ref, x_ref, gate_ref,
                   y_ref, acc_s, s_s, p_s):
  c = pl.program_id(1)

  @pl.when(c == 0)
  def _():
    acc_s[...] = jnp.zeros_like(acc_s)

  hb = hb_ref[...]
  t = hb.shape[0]
  pk = 16
  for j in range(PEER_EC // PEER_SUBC):
    rows = slice(j * PEER_SUBC, (j + 1) * PEER_SUBC)
    s_s[rows, :] = lax.dot_general(u_ref[rows, :], hb, _NT, preferred_element_type=F32)
  for j in range(PEER_EC // PEER_SUBC):
    for ii in range(PEER_SUBC // PEER_NKEYS):
      i1 = j * (PEER_SUBC // PEER_NKEYS) + ii
      w = [jnp.zeros((pk, t), BF16) for _ in range(PEER_NKEYS // pk)]
      for h in range(PEER_HEADS):
        a_row = jnp.broadcast_to(a1_ref[h, i1:i1 + 1, :], (pk, t)).astype(BF16)
        n_row = jnp.broadcast_to(n1_ref[h, i1:i1 + 1, :], (pk, t)).astype(BF16)
        for sub in range(PEER_NKEYS // pk):
          words = slice(sub * pk // 2, (sub + 1) * pk // 2)
          prod = a_row * pltpu.bitcast(b2_ref[h, words, :], BF16)
          rank = pltpu.bitcast(r2_ref[h, words, :], BF16)
          w[sub] = w[sub] + jnp.where(rank < n_row, prod, jnp.zeros_like(prod))
      for sub in range(PEER_NKEYS // pk):
        row0 = i1 * PEER_NKEYS + sub * pk
        p_s[row0:row0 + pk, :] = w[sub] * _gelu(s_s[row0:row0 + pk, :]).astype(BF16)
    rows = slice(j * PEER_SUBC, (j + 1) * PEER_SUBC)
    acc_s[...] += jnp.dot(vt_ref[:, rows], p_s[rows, :], preferred_element_type=F32)

  @pl.when(c == pl.num_programs(1) - 1)
  def _():
    y_ref[...] = x_ref[...] + gate_ref[...] * acc_s[...].T


def _peer_mix(hb, u, vt, a1, n1, b2, r2, x, gate):
  n = x.shape[0]
  t = PEER_T
  n_i1 = PEER_EC // PEER_NKEYS
  row = lambda w: pl.BlockSpec((t, w), lambda i, c: (i, 0))
  i1spec = pl.BlockSpec((PEER_HEADS, n_i1, t), lambda i, c: (0, c, i))
  i2spec = pl.BlockSpec((PEER_HEADS, PEER_NKEYS // 2, t), lambda i, c: (0, 0, i))
  return pl.pallas_call(
      _peer_mix_body,
      grid=(n // t, PEER_EXPERTS // PEER_EC),
      in_specs=[row(D_MODEL),
                pl.BlockSpec((PEER_EC, D_MODEL), lambda i, c: (c, 0)),
                pl.BlockSpec((D_MODEL, PEER_EC), lambda i, c: (0, c)),
                i1spec, i1spec, i2spec, i2spec,
                row(D_MODEL),
                pl.BlockSpec((None, 1, D_MODEL), lambda i, c: (_mod_row(i, t), 0, 0))],
      out_specs=row(D_MODEL),
      out_shape=jax.ShapeDtypeStruct((n, D_MODEL), F32),
      scratch_shapes=[pltpu.VMEM((D_MODEL, t), F32),
                      pltpu.VMEM((PEER_EC, t), F32),
                      pltpu.VMEM((PEER_EC, t), BF16)],
      compiler_params=_cparams(("parallel", "arbitrary"), VMEM_LIMIT_PEER),
      name="peer_mix",
  )(hb, u, vt, a1, n1, b2, r2, x, gate)


def _peer_layer(x, p, sc, sh, gate):
  hb, a1, n1, b2, r2 = _peer_router(x, p['norm2_g'], sc, sh, p['peer_wq'], p['peer_k1'],
                                    p['peer_k2'])
  return _peer_mix(hb, p['peer_u'], p['peer_vt'], a1, n1, b2, r2, x, gate)


def _rotated_tiles(w, group):
  k, n = w.shape
  g = w.reshape(k, n // group, 2, group // 2)
  return jnp.concatenate([-g[:, :, 1:], g[:, :, :1]], axis=2).reshape(k, n)


def _rope_table(rot_dims, reps):
  half = rot_dims // 2
  t = jnp.arange(DEC_SEQ)
  inv = ROPE_BASE ** (-jnp.arange(half, dtype=F32) / half)
  parts_c, parts_s = [], []
  for pos in (t // GRID_W, t % GRID_W):
    ang = pos.astype(F32)[:, None] * inv[None, :]
    parts_c += [jnp.cos(ang), jnp.cos(ang)]
    parts_s += [jnp.sin(ang), jnp.sin(ang)]
  cos = jnp.tile(jnp.concatenate(parts_c, axis=1), (1, reps))
  sin = jnp.tile(jnp.concatenate(parts_s, axis=1), (1, reps))
  cos = jnp.concatenate([cos, jnp.ones((TM, LANES), F32)], axis=0)
  sin = jnp.concatenate([sin, jnp.zeros((TM, LANES), F32)], axis=0)
  return cos, sin


def _na_bias_table(rpb):
  reps = [NA_KH // 2] + list(range(NA_KH // 2)) + list(range(GRID_R - NA_KH // 2 + 1, GRID_R))
  cols = np.arange(GRID_W)
  col_start = np.clip(cols - NA_KW // 2, 0, GRID_W - NA_KW)
  col_mask = (cols[None, :] >= col_start[:, None]) & (cols[None, :] < col_start[:, None] + NA_KW)
  dc = np.clip(cols[None, :] - cols[:, None], -(NA_KW - 1), NA_KW - 1) + (NA_KW - 1)
  dr = np.stack([np.clip(r - NA_KH // 2, 0, GRID_R - NA_KH) + np.arange(NA_KH) - r + (NA_KH - 1)
                 for r in reps])
  onehot = (dc[:, :, None] == np.arange(2 * NA_KW - 1)).astype(np.float32)
  bias = jnp.einsum('hpjc,qwc->phqjw', rpb[:, dr, :], onehot, precision=lax.Precision.HIGHEST)
  bias = jnp.where(col_mask[None, None, :, None, :], bias, NEG_INF)
  return bias.reshape(len(reps), NA_HEADS, GRID_W, NA_KH * GRID_W)


def _ctx_rows(a, *shape):
  return a[:N_CTX].reshape(BATCH, SEQ, *shape)


def _diff_mixer(layer, p, x, sc, sh, cache, tables):
  w = p['w_qkv']
  d = D_MODEL
  wq, wk, wv = w[:, :d], w[:, d:2 * d], w[:, 2 * d:]
  rot = lambda m: _rotated_tiles(m, DA_QK // 2)
  w5 = jnp.concatenate([wq, rot(wq), wk, rot(wk), wv], axis=1).astype(BF16)
  cos, sin = tables['da']
  q, k, v = _qkv_rope(x, p['norm1_g'], sc, sh, w5, cos, sin, DA_QK ** -0.5)
  lam_rows = jnp.zeros((8, LANES), F32)
  for i, nme in enumerate(('lam_q1', 'lam_k1', 'lam_q2', 'lam_k2')):
    lam_rows = lam_rows.at[i, :DA_QK].set(p[nme])
  g = p['subln_g'].reshape(1, DA_V)
  oc = _diff_attention(q, k, v, None, lam_rows, g, layer, lat=False)
  ol = _diff_attention(q, k, v, cache, lam_rows, g, layer, lat=True)
  state = (_ctx_rows(k, DA_HEADS, 2 * DA_QK), _ctx_rows(v, DA_HEADS, DA_V))
  return oc, ol, state


def _na_mixer(p, x, sc, sh, cache):
  d = D_MODEL
  q, k, v = _qkv_plain(x, p['norm1_g'], sc, sh, p['w_qkv'].astype(BF16), NA_HD ** -0.5)
  oc = _pair_attention(q, k, v, lat=False)
  bias = _na_bias_table(p['rpb'])
  ol = _na_lat_attention(q, k, v, cache[0].reshape(DEC_BATCH, PAST_LEN, d),
                         cache[1].reshape(DEC_BATCH, PAST_LEN, d), bias)
  state = (_ctx_rows(k, NA_HEADS, NA_HD), _ctx_rows(v, NA_HEADS, NA_HD))
  return oc, ol, state


def _mla_mixer(p, x, sc, sh, cache, tables):
  a, b = MLA_Q_RANK, MLA_Q_RANK + MLA_KV_RANK
  w_in = p['w_in']
  kpe_w = w_in[:, b:]
  w_in_x = jnp.concatenate([w_in[:, :b], jnp.tile(kpe_w, (1, 4)),
                            jnp.tile(_rotated_tiles(kpe_w, MLA_ROPE // 2), (1, 4))],
                           axis=1).astype(BF16)
  cos, sin = tables['mla']
  cq, ckv, kpe = _mla_in(x, p['norm1_g'], sc, sh, w_in_x, p['q_norm_g'].reshape(1, a),
                         p['kv_norm_g'].reshape(1, MLA_KV_RANK), cos, sin)
  w_uq = p['w_uq'].reshape(a, MLA_HEADS, MLA_NOPE + MLA_ROPE)
  w_qn = w_uq[:, :, :MLA_NOPE].reshape(a, -1)
  w_qp = w_uq[:, :, MLA_NOPE:].reshape(a, -1)
  w_uq_x = jnp.concatenate([w_qn, w_qp, _rotated_tiles(w_qp, MLA_ROPE // 2)], axis=1).astype(BF16)
  qn, qp = _mla_q(cq, w_uq_x, cos, sin)
  w_ukv = p['w_ukv'].reshape(MLA_KV_RANK, MLA_HEADS, MLA_NOPE + MLA_V)
  w_ukv_x = jnp.concatenate([w_ukv[:, :, :MLA_NOPE].reshape(MLA_KV_RANK, -1),
                             w_ukv[:, :, MLA_NOPE:].reshape(MLA_KV_RANK, -1)], axis=1).astype(BF16)
  kn, v = _mla_kv(ckv, w_ukv_x)
  knc, vc = _mla_kv(cache[0].reshape(DEC_BATCH * PAST_LEN, MLA_KV_RANK), w_ukv_x)
  cached = (knc.reshape(DEC_BATCH, PAST_LEN, D_MODEL), jnp.tile(cache[1], (1, 1, LANES // MLA_ROPE)),
            vc.reshape(DEC_BATCH, PAST_LEN, D_MODEL))
  scale = (MLA_NOPE + MLA_ROPE) ** -0.5
  oc = _pair_attention(qn, kn, v, lat=False, scale=scale, q_pe=qp, k_pe=kpe)
  ol = _pair_attention(qn, kn, v, lat=True, cache=cached, scale=scale, q_pe=qp, k_pe=kpe)
  state = (_ctx_rows(ckv, MLA_KV_RANK), _ctx_rows(kpe[:, :MLA_ROPE], MLA_ROPE))
  return oc, ol, state


def _layer(layer, p, x, cond8, cache, tables):
  sh1, sc1, g1, sh2, sc2, g2 = _modulation(cond8, p['ada_w'], p['ada_b'])
  kind = layer % 3
  if kind == 0:
    oc, ol, state = _diff_mixer(layer, p, x, sc1, sh1, cache, tables)
  elif kind == 1:
    oc, ol, state = _na_mixer(p, x, sc1, sh1, cache)
  else:
    oc, ol, state = _mla_mixer(p, x, sc1, sh1, cache, tables)
  x = _out_proj(oc, ol, p['w_o'].astype(BF16), x, g1)
  x = _peer_layer(x, p, sc2, sh2, g2)
  return x, state


def kernel(x_prompt, x_sample, cache_l0_k, cache_l0_v, cache_l1_k, cache_l1_v, cache_l2_ckv, cache_l2_kpe, cache_l3_k, cache_l3_v, c, c_ctx, l0_norm1_g, l0_norm2_g, l0_ada_w, l0_ada_b, l0_w_qkv, l0_w_o, l0_lam_q1, l0_lam_k1, l0_lam_q2, l0_lam_k2, l0_subln_g, l0_peer_wq, l0_peer_k1, l0_peer_k2, l0_peer_u, l0_peer_v, l1_norm1_g, l1_norm2_g, l1_ada_w, l1_ada_b, l1_w_qkv, l1_w_o, l1_rpb, l1_peer_wq, l1_peer_k1, l1_peer_k2, l1_peer_u, l1_peer_v, l2_norm1_g, l2_norm2_g, l2_ada_w, l2_ada_b, l2_w_in, l2_q_norm_g, l2_w_uq, l2_kv_norm_g, l2_w_ukv, l2_w_o, l2_peer_wq, l2_peer_k1, l2_peer_k2, l2_peer_u, l2_peer_v, l3_norm1_g, l3_norm2_g, l3_ada_w, l3_ada_b, l3_w_qkv, l3_w_o, l3_lam_q1, l3_lam_k1, l3_lam_q2, l3_lam_k2, l3_subln_g, l3_peer_wq, l3_peer_k1, l3_peer_k2, l3_peer_u, l3_peer_v, final_norm_g):
  common = lambda n1, n2, aw, ab, wq, k1, k2, u, v: dict(
      norm1_g=n1.reshape(1, D_MODEL), norm2_g=n2.reshape(1, D_MODEL), ada_w=aw, ada_b=ab,
      peer_wq=wq.astype(BF16), peer_k1=k1.astype(BF16), peer_k2=k2.astype(BF16),
      peer_u=u.astype(BF16), peer_vt=v.T.astype(BF16))
  p0 = dict(common(l0_norm1_g, l0_norm2_g, l0_ada_w, l0_ada_b, l0_peer_wq, l0_peer_k1, l0_peer_k2,
                   l0_peer_u, l0_peer_v),
            w_qkv=l0_w_qkv, w_o=l0_w_o, lam_q1=l0_lam_q1, lam_k1=l0_lam_k1, lam_q2=l0_lam_q2,
            lam_k2=l0_lam_k2, subln_g=l0_subln_g)
  p1 = dict(common(l1_norm1_g, l1_norm2_g, l1_ada_w, l1_ada_b, l1_peer_wq, l1_peer_k1, l1_peer_k2,
                   l1_peer_u, l1_peer_v),
            w_qkv=l1_w_qkv, w_o=l1_w_o, rpb=l1_rpb)
  p2 = dict(common(l2_norm1_g, l2_norm2_g, l2_ada_w, l2_ada_b, l2_peer_wq, l2_peer_k1, l2_peer_k2,
                   l2_peer_u, l2_peer_v),
            w_in=l2_w_in, q_norm_g=l2_q_norm_g, w_uq=l2_w_uq, kv_norm_g=l2_kv_norm_g,
            w_ukv=l2_w_ukv, w_o=l2_w_o)
  p3 = dict(common(l3_norm1_g, l3_norm2_g, l3_ada_w, l3_ada_b, l3_peer_wq, l3_peer_k1, l3_peer_k2,
                   l3_peer_u, l3_peer_v),
            w_qkv=l3_w_qkv, w_o=l3_w_o, lam_q1=l3_lam_q1, lam_k1=l3_lam_k1, lam_q2=l3_lam_q2,
            lam_k2=l3_lam_k2, subln_g=l3_subln_g)
  params = (p0, p1, p2, p3)
  caches = ((cache_l0_k, cache_l0_v), (cache_l1_k, cache_l1_v),
            (cache_l2_ckv, cache_l2_kpe), (cache_l3_k, cache_l3_v))
  tables = dict(da=_rope_table(DA_QK // 2, LANES // DA_QK),
                mla=_rope_table(MLA_ROPE // 2, LANES // MLA_ROPE))

  cond8 = jnp.zeros((8, D_MODEL), F32).at[0].set(c_ctx).at[1:1 + DEC_BATCH].set(c)
  x = jnp.concatenate([x_prompt.reshape(N_CTX, D_MODEL), x_sample.reshape(N_LAT, D_MODEL)], axis=0)
  states = []
  for layer in range(DEPTH):
    x, st = _layer(layer, params[layer], x, cond8, caches[layer], tables)
    states.extend(st)
  y = _final_norm(x, final_norm_g.reshape(1, D_MODEL))
  return (y[:N_CTX].reshape(BATCH, SEQ, D_MODEL), y[N_CTX:].reshape(DEC_BATCH, DEC_SEQ, D_MODEL),
          *states)
```

```python
import functools
import math

import numpy as np
import jax
import jax.numpy as jnp
from jax import lax
from jax.experimental import pallas as pl
from jax.experimental.pallas import tpu as pltpu

F32 = jnp.float32
BF16 = jnp.bfloat16

D_MODEL = 1024
BATCH = 32
SEQ = 256
DEPTH = 4
DEC_BATCH = 2
DEC_SEQ = 4096
PAST_LEN = 512
GRID_W = 64
GRID_R = DEC_SEQ // GRID_W
EPS = 1e-6
ROPE_BASE = 10000.0
NEG_INF = -1e30

DA_HEADS = 8
DA_QK = 64
DA_V = 128
NA_HEADS = 16
NA_HD = 64
NA_KH = 8
NA_KW = 16
MLA_HEADS = 16
MLA_NOPE = 64
MLA_ROPE = 32
MLA_V = 64
MLA_Q_RANK = 384
MLA_KV_RANK = 256
PEER_HEADS = 8
PEER_NKEYS = 128
PEER_EXPERTS = PEER_NKEYS * PEER_NKEYS
PEER_DK = 256
PEER_TOPK = 16

N_CTX = BATCH * SEQ
N_LAT = DEC_BATCH * DEC_SEQ
N_TOK = N_CTX + N_LAT
LAT_KEYS = DEC_SEQ + PAST_LEN

LANES = 128
VMEM_LIMIT = 48 << 20
VMEM_LIMIT_PEER = 56 << 20

TM = 256
TQ = 512
TK = 512
ATTN_UNROLL = 9
NA_RB = 8
PEER_T = 512
PEER_EC = 2048
PEER_SUBC = 256

_NT = (((1,), (1,)), ((), ()))


def _cparams(sem, vmem=VMEM_LIMIT):
  return pltpu.CompilerParams(dimension_semantics=sem, vmem_limit_bytes=vmem)


def _mod_row(i, tm):
  nb_ctx = N_CTX // tm
  nb_bat = DEC_SEQ // tm
  return jnp.where(i < nb_ctx, 0, 1 + (i - nb_ctx) // nb_bat)


def _rope_blk(i, tm):
  nb_ctx = N_CTX // tm
  nb_bat = DEC_SEQ // tm
  return jnp.where(i < nb_ctx, nb_bat, (i - nb_ctx) % nb_bat)


def _rms_mod(x, g, sc, sh):
  y = x * lax.rsqrt(jnp.mean(x * x, axis=-1, keepdims=True) + EPS)
  return (y * g) * (1.0 + sc) + sh


def _rms(x, g):
  return x * lax.rsqrt(jnp.mean(x * x, axis=-1, keepdims=True) + EPS) * g


def _row_spec(width, tm=TM):
  return pl.BlockSpec((tm, width), lambda i: (i, 0))


def _full_spec(shape):
  return pl.BlockSpec(shape, lambda i: (0,) * len(shape))


def _mod_spec(tm=TM):
  return pl.BlockSpec((None, 1, D_MODEL), lambda i: (_mod_row(i, tm), 0, 0))


def _mod_body(c_ref, w_ref, b_ref, o_ref):
  c = c_ref[...]
  s = c / (1.0 + jnp.exp(-c))
  o_ref[...] = jnp.dot(s, w_ref[...], precision=lax.Precision.HIGHEST,
                       preferred_element_type=F32) + b_ref[...]


def _modulation(cond8, ada_w, ada_b):
  n = ada_w.shape[1]
  tn = 1536
  out = pl.pallas_call(
      _mod_body,
      grid=(n // tn,),
      in_specs=[pl.BlockSpec((8, D_MODEL), lambda j: (0, 0)),
                pl.BlockSpec((D_MODEL, tn), lambda j: (0, j)),
                pl.BlockSpec((1, tn), lambda j: (0, j))],
      out_specs=pl.BlockSpec((8, tn), lambda j: (0, j)),
      out_shape=jax.ShapeDtypeStruct((8, n), F32),
      compiler_params=_cparams(("parallel",)),
      name="modulation",
  )(cond8, ada_w, ada_b.reshape(1, n))
  return [out[:, k * D_MODEL:(k + 1) * D_MODEL].reshape(8, 1, D_MODEL) for k in range(6)]


def _tile_lanes(t, reps):
  return jnp.concatenate([t] * reps, axis=1)


def _qkv_rope_body(x_ref, g_ref, sc_ref, sh_ref, w_ref, cos_ref, sin_ref,
                   q_ref, k_ref, v_ref, *, q_scale):
  h = _rms_mod(x_ref[...], g_ref[...], sc_ref[...], sh_ref[...]).astype(BF16)
  reps = D_MODEL // LANES
  cos = _tile_lanes(cos_ref[...], reps)
  sin = _tile_lanes(sin_ref[...], reps)
  d = D_MODEL
  dot = lambda a, b: jnp.dot(h, w_ref[:, a:b], preferred_element_type=F32)
  q_ref[...] = (dot(0, d) * cos + dot(d, 2 * d) * sin) * q_scale
  k_ref[...] = dot(2 * d, 3 * d) * cos + dot(3 * d, 4 * d) * sin
  v_ref[...] = dot(4 * d, 5 * d)


def _qkv_rope(x, g, sc, sh, w5, cos, sin, q_scale):
  n = x.shape[0]
  rope_spec = pl.BlockSpec((TM, LANES), lambda i: (_rope_blk(i, TM), 0))
  return pl.pallas_call(
      functools.partial(_qkv_rope_body, q_scale=q_scale),
      grid=(n // TM,),
      in_specs=[_row_spec(D_MODEL), _full_spec((1, D_MODEL)), _mod_spec(), _mod_spec(),
                _full_spec(w5.shape), rope_spec, rope_spec],
      out_specs=[_row_spec(D_MODEL)] * 3,
      out_shape=[jax.ShapeDtypeStruct((n, D_MODEL), F32)] * 3,
      compiler_params=_cparams(("parallel",)),
      name="qkv_rope_proj",
  )(x, g, sc, sh, w5, cos, sin)


def _qkv_plain_body(x_ref, g_ref, sc_ref, sh_ref, w_ref, q_ref, k_ref, v_ref, *, q_scale):
  h = _rms_mod(x_ref[...], g_ref[...], sc_ref[...], sh_ref[...]).astype(BF16)
  d = D_MODEL
  dot = lambda a, b: jnp.dot(h, w_ref[:, a:b], preferred_element_type=F32)
  q_ref[...] = dot(0, d) * q_scale
  k_ref[...] = dot(d, 2 * d)
  v_ref[...] = dot(2 * d, 3 * d)


def _qkv_plain(x, g, sc, sh, w3, q_scale):
  n = x.shape[0]
  return pl.pallas_call(
      functools.partial(_qkv_plain_body, q_scale=q_scale),
      grid=(n // TM,),
      in_specs=[_row_spec(D_MODEL), _full_spec((1, D_MODEL)), _mod_spec(), _mod_spec(),
                _full_spec(w3.shape)],
      out_specs=[_row_spec(D_MODEL)] * 3,
      out_shape=[jax.ShapeDtypeStruct((n, D_MODEL), F32)] * 3,
      compiler_params=_cparams(("parallel",)),
      name="qkv_proj",
  )(x, g, sc, sh, w3)


def _mla_in_body(x_ref, g_ref, sc_ref, sh_ref, w_ref, qg_ref, kvg_ref, cos_ref, sin_ref,
                 cq_ref, ckv_ref, kpe_ref):
  h = _rms_mod(x_ref[...], g_ref[...], sc_ref[...], sh_ref[...]).astype(BF16)
  z = jnp.dot(h, w_ref[...], preferred_element_type=F32)
  a, b = MLA_Q_RANK, MLA_Q_RANK + MLA_KV_RANK
  cq_ref[...] = _rms(z[:, :a], qg_ref[...]).astype(BF16)
  ckv_ref[...] = _rms(z[:, a:b], kvg_ref[...])
  kpe_ref[...] = z[:, b:b + LANES] * cos_ref[...] + z[:, b + LANES:] * sin_ref[...]


def _mla_in(x, g, sc, sh, w_in, qg, kvg, cos, sin):
  n = x.shape[0]
  rope_spec = pl.BlockSpec((TM, LANES), lambda i: (_rope_blk(i, TM), 0))
  return pl.pallas_call(
      _mla_in_body,
      grid=(n // TM,),
      in_specs=[_row_spec(D_MODEL), _full_spec((1, D_MODEL)), _mod_spec(), _mod_spec(),
                _full_spec(w_in.shape), _full_spec((1, MLA_Q_RANK)),
                _full_spec((1, MLA_KV_RANK)), rope_spec, rope_spec],
      out_specs=[_row_spec(MLA_Q_RANK), _row_spec(MLA_KV_RANK), _row_spec(LANES)],
      out_shape=[jax.ShapeDtypeStruct((n, MLA_Q_RANK), BF16),
                 jax.ShapeDtypeStruct((n, MLA_KV_RANK), F32),
                 jax.ShapeDtypeStruct((n, LANES), F32)],
      compiler_params=_cparams(("parallel",)),
      name="mla_in_proj",
  )(x, g, sc, sh, w_in, qg, kvg, cos, sin)


def _mla_q_body(cq_ref, w_ref, cos_ref, sin_ref, qn_ref, qp_ref):
  z = jnp.dot(cq_ref[...], w_ref[...], preferred_element_type=F32)
  pe = MLA_HEADS * MLA_ROPE
  reps = pe // LANES
  cos = _tile_lanes(cos_ref[...], reps)
  sin = _tile_lanes(sin_ref[...], reps)
  qn_ref[...] = z[:, :D_MODEL]
  qp_ref[...] = z[:, D_MODEL:D_MODEL + pe] * cos + z[:, D_MODEL + pe:] * sin


def _mla_q(cq, w_uq, cos, sin):
  n = cq.shape[0]
  pe = MLA_HEADS * MLA_ROPE
  rope_spec = pl.BlockSpec((TM, LANES), lambda i: (_rope_blk(i, TM), 0))
  return pl.pallas_call(
      _mla_q_body,
      grid=(n // TM,),
      in_specs=[_row_spec(MLA_Q_RANK), _full_spec(w_uq.shape), rope_spec, rope_spec],
      out_specs=[_row_spec(D_MODEL), _row_spec(pe)],
      out_shape=[jax.ShapeDtypeStruct((n, D_MODEL), F32), jax.ShapeDtypeStruct((n, pe), F32)],
      compiler_params=_cparams(("parallel",)),
      name="mla_q_proj",
  )(cq, w_uq, cos, sin)


def _mla_kv_body(c_ref, w_ref, kn_ref, v_ref):
  z = jnp.dot(c_ref[...].astype(BF16), w_ref[...], preferred_element_type=F32)
  kn_ref[...] = z[:, :D_MODEL]
  v_ref[...] = z[:, D_MODEL:]


def _mla_kv(ckv, w_ukv):
  n = ckv.shape[0]
  return pl.pallas_call(
      _mla_kv_body,
      grid=(n // TM,),
      in_specs=[_row_spec(MLA_KV_RANK), _full_spec(w_ukv.shape)],
      out_specs=[_row_spec(D_MODEL)] * 2,
      out_shape=[jax.ShapeDtypeStruct((n, D_MODEL), F32)] * 2,
      compiler_params=_cparams(("parallel",)),
      name="mla_kv_proj",
  )(ckv, w_ukv)


def _out_proj_body(oc_ref, ol_ref, w_ref, x_ref, gate_ref, y_ref):
  def run(o_ref):
    y_ref[...] = x_ref[...] + gate_ref[...] * jnp.dot(
        o_ref[...].astype(BF16), w_ref[...], preferred_element_type=F32)

  is_ctx = pl.program_id(0) < N_CTX // TM
  pl.when(is_ctx)(lambda: run(oc_ref))
  pl.when(jnp.logical_not(is_ctx))(lambda: run(ol_ref))


def _out_proj(oc, ol, w_o, x, gate):
  n = x.shape[0]
  nb_ctx = N_CTX // TM
  return pl.pallas_call(
      _out_proj_body,
      grid=(n // TM,),
      in_specs=[pl.BlockSpec((TM, D_MODEL), lambda i: (jnp.minimum(i, nb_ctx - 1), 0)),
                pl.BlockSpec((TM, D_MODEL), lambda i: (jnp.maximum(i - nb_ctx, 0), 0)),
                _full_spec(w_o.shape), _row_spec(D_MODEL), _mod_spec()],
      out_specs=_row_spec(D_MODEL),
      out_shape=jax.ShapeDtypeStruct((n, D_MODEL), F32),
      compiler_params=_cparams(("parallel",)),
      name="out_proj",
  )(oc.reshape(N_CTX, D_MODEL), ol.reshape(N_LAT, D_MODEL), w_o, x, gate)


def _final_norm_body(x_ref, g_ref, y_ref):
  y_ref[...] = _rms(x_ref[...], g_ref[...])


def _final_norm(x, g):
  n = x.shape[0]
  return pl.pallas_call(
      _final_norm_body,
      grid=(n // TM,),
      in_specs=[_row_spec(D_MODEL), _full_spec((1, D_MODEL))],
      out_specs=_row_spec(D_MODEL),
      out_shape=jax.ShapeDtypeStruct((n, D_MODEL), F32),
      compiler_params=_cparams(("parallel",)),
      name="final_norm",
  )(x, g)


def _online_update(s, m, l, acc, vb):
  m_new = jnp.maximum(m, jnp.max(s, axis=-1, keepdims=True))
  alpha = jnp.exp(m - m_new)
  p = jnp.exp(s - m_new)
  l_new = alpha * l + jnp.sum(p, axis=-1, keepdims=True)
  acc_new = alpha * acc + jnp.dot(p.astype(BF16), vb, preferred_element_type=F32)
  return m_new, l_new, acc_new


def _softmax_state(tq):
  return (jnp.full((tq, 1), NEG_INF, F32), jnp.zeros((tq, 1), F32), jnp.zeros((tq, LANES), F32))


def _stacked_attention(q_pair, kb_s, vb_s):
  tq = q_pair[0].shape[0]
  qq = jnp.concatenate(q_pair, axis=0)
  tk, n_chunks, unroll = _key_chunks(kb_s.shape[0])

  def chunk(c, carry):
    off = pl.multiple_of(c * tk, tk)
    s = lax.dot_general(qq, kb_s[pl.ds(off, tk), :], _NT, preferred_element_type=F32)
    return _online_update(s, *carry, vb_s[pl.ds(off, tk), :])

  _, l, acc = lax.fori_loop(0, n_chunks, chunk, _softmax_state(2 * tq), unroll=unroll)
  o = acc / l
  return o[:tq], o[tq:]


def _pass_view(a, lat):
  w = a.shape[-1]
  if lat:
    return a.reshape(N_TOK // DEC_SEQ, DEC_SEQ, w), N_CTX // DEC_SEQ, DEC_BATCH
  return a.reshape(N_TOK // SEQ, SEQ, w), 0, BATCH


def _stage_keys(dst, lanes, own_ref, cache_ref):
  n_own = own_ref.shape[0]
  dst[:n_own, lanes] = own_ref[...].astype(BF16)
  if cache_ref is not None:
    dst[n_own:, lanes] = cache_ref[...].astype(BF16)


def _key_chunks(n_keys):
  tk = min(TK, n_keys)
  n = n_keys // tk
  return tk, n, (ATTN_UNROLL if n % ATTN_UNROLL == 0 else 1)


def _attn_specs(q, lat):
  qv, b0, nb = _pass_view(q, lat)
  sq = qv.shape[1]
  tq = min(TQ, sq)
  grid = (nb, D_MODEL // LANES, sq // tq)
  qspec = pl.BlockSpec((None, tq, LANES), lambda bi, h, qi: (bi + b0, qi, h))
  kspec = pl.BlockSpec((None, sq, LANES), lambda bi, h, qi: (bi + b0, 0, h))
  cspec = pl.BlockSpec((None, PAST_LEN, LANES), lambda bi, h, qi: (bi, 0, h))
  ospec = pl.BlockSpec((None, tq, LANES), lambda bi, h, qi: (bi, qi, h))
  n_keys = sq + (PAST_LEN if lat else 0)
  return grid, qspec, kspec, cspec, ospec, n_keys, (nb, sq, D_MODEL)


def _diff_attn_body(*refs, lam_init, cached):
  if cached:
    lam_ref, g_ref, q_ref, k_ref, v_ref, kc_ref, vc_ref, o_ref, kb_s, vb_s = refs
  else:
    lam_ref, g_ref, q_ref, k_ref, v_ref, o_ref, kb_s, vb_s = refs
    kc_ref = vc_ref = None

  @pl.when(pl.program_id(2) == 0)
  def _():
    _stage_keys(kb_s, slice(None), k_ref, kc_ref)
    _stage_keys(vb_s, slice(None), v_ref, vc_ref)

  q = q_ref[...]
  tq = q.shape[0]
  lane = lax.broadcasted_iota(jnp.int32, q.shape, 1)
  q1 = jnp.where(lane < DA_QK, q, 0.0).astype(BF16)
  q2 = jnp.where(lane >= DA_QK, q, 0.0).astype(BF16)
  lv = lam_ref[...]
  lam = (jnp.exp(jnp.sum(lv[0:1] * lv[1:2], axis=-1, keepdims=True))
         - jnp.exp(jnp.sum(lv[2:3] * lv[3:4], axis=-1, keepdims=True)) + lam_init)
  o1, o2 = _stacked_attention((q1, q2), kb_s, vb_s)
  o_ref[...] = _rms(o1 - lam * o2, g_ref[...]) * (1.0 - lam_init)


def _diff_attention(q, k, v, cache, lam_rows, subln_g, layer, lat):
  grid, qspec, kspec, cspec, ospec, n_keys, oshape = _attn_specs(q, lat)
  view = lambda a: _pass_view(a, lat)[0]
  lam_init = 0.8 - 0.6 * math.exp(-0.3 * layer)
  const = lambda shape: pl.BlockSpec(shape, lambda bi, h, qi: (0, 0))
  in_specs = [const((8, LANES)), const((1, LANES)), qspec, kspec, kspec]
  args = [lam_rows, subln_g, view(q), view(k), view(v)]
  if lat:
    in_specs += [cspec, cspec]
    args += [cache[0].reshape(DEC_BATCH, PAST_LEN, D_MODEL),
             cache[1].reshape(DEC_BATCH, PAST_LEN, D_MODEL)]
  return pl.pallas_call(
      functools.partial(_diff_attn_body, lam_init=lam_init, cached=lat),
      grid=grid, in_specs=in_specs, out_specs=ospec,
      out_shape=jax.ShapeDtypeStruct(oshape, F32),
      scratch_shapes=[pltpu.VMEM((n_keys, LANES), BF16)] * 2,
      compiler_params=_cparams(("parallel", "parallel", "arbitrary")),
      name="diff_attention",
  )(*args)


def _pair_attn_body(*refs, scale, with_pe, cached):
  refs = list(refs)
  q_ref = refs.pop(0)
  qp_ref = refs.pop(0) if with_pe else None
  k_ref = refs.pop(0)
  kp_ref = refs.pop(0) if with_pe else None
  v_ref = refs.pop(0)
  kc_ref = refs.pop(0) if cached else None
  kpc_ref = refs.pop(0) if (cached and with_pe) else None
  vc_ref = refs.pop(0) if cached else None
  o_ref, kb_s, vb_s = refs

  @pl.when(pl.program_id(2) == 0)
  def _():
    _stage_keys(kb_s, slice(0, LANES), k_ref, kc_ref)
    if with_pe:
      _stage_keys(kb_s, slice(LANES, 2 * LANES), kp_ref, kpc_ref)
    _stage_keys(vb_s, slice(None), v_ref, vc_ref)

  q = q_ref[...] * scale
  tq = q.shape[0]
  lane = lax.broadcasted_iota(jnp.int32, q.shape, 1)
  halves = (lane < NA_HD, lane >= NA_HD)
  qs = [jnp.where(hm, q, 0.0).astype(BF16) for hm in halves]
  if with_pe:
    qp = qp_ref[...] * scale
    base = (pl.program_id(1) % 2) * (2 * MLA_ROPE)
    qs = [jnp.concatenate(
        [qs[a], jnp.where((lane >= base + a * MLA_ROPE) & (lane < base + (a + 1) * MLA_ROPE),
                          qp, 0.0).astype(BF16)], axis=1) for a in range(2)]
  o_ref[...] = jnp.where(halves[0], *_stacked_attention(qs, kb_s, vb_s))


def _pair_attention(q, k, v, lat, cache=None, scale=1.0, q_pe=None, k_pe=None):
  grid, qspec, kspec, cspec, ospec, n_keys, oshape = _attn_specs(q, lat)
  view = lambda a: _pass_view(a, lat)[0]
  b0 = _pass_view(q, lat)[1]
  with_pe = q_pe is not None
  tq, sq = qspec.block_shape[1], kspec.block_shape[1]
  in_specs, args = [qspec], [view(q)]
  if with_pe:
    in_specs.append(pl.BlockSpec((None, tq, LANES), lambda bi, h, qi: (bi + b0, qi, h // 2)))
    args.append(view(q_pe))
  in_specs.append(kspec)
  args.append(view(k))
  if with_pe:
    in_specs.append(pl.BlockSpec((None, sq, LANES), lambda bi, h, qi: (bi + b0, 0, 0)))
    args.append(view(k_pe))
  in_specs.append(kspec)
  args.append(view(v))
  if lat:
    in_specs.append(cspec)
    args.append(cache[0])
    if with_pe:
      in_specs.append(pl.BlockSpec((None, PAST_LEN, LANES), lambda bi, h, qi: (bi, 0, 0)))
      args.append(cache[1])
    in_specs.append(cspec)
    args.append(cache[-1])
  return pl.pallas_call(
      functools.partial(_pair_attn_body, scale=scale, with_pe=with_pe, cached=lat),
      grid=grid, in_specs=in_specs, out_specs=ospec,
      out_shape=jax.ShapeDtypeStruct(oshape, F32),
      scratch_shapes=[pltpu.VMEM((n_keys, 2 * LANES if with_pe else LANES), BF16),
                      pltpu.VMEM((n_keys, LANES), BF16)],
      compiler_params=_cparams(("parallel", "parallel", "arbitrary")),
      name="pair_attention_pe" if with_pe else "pair_attention",
  )(*args)


def _na_lat_body(q_ref, k_ref, v_ref, kc_ref, vc_ref, bias_ref, o_ref, kb_s, vb_s, kcb_s, vcb_s):
  rb = pl.program_id(2)

  @pl.when(rb == 0)
  def _():
    kb_s[...] = k_ref[...].astype(BF16)
    vb_s[...] = v_ref[...].astype(BF16)
    kcb_s[...] = kc_ref[...].astype(BF16)
    vcb_s[...] = vc_ref[...].astype(BF16)

  n_loc = NA_KH * GRID_W
  lane = lax.broadcasted_iota(jnp.int32, (GRID_W, LANES), 1)
  halves = (lane < NA_HD, lane >= NA_HD)
  kc = kcb_s[...]
  vc = vcb_s[...]
  for rr in range(NA_RB):
    r = rb * NA_RB + rr
    start = jnp.clip(r - NA_KH // 2, 0, GRID_R - NA_KH)
    pat = jnp.where(r < NA_KH // 2, 1 + r,
                    jnp.where(r > GRID_R - NA_KH // 2, r - (GRID_R - NA_KH), 0))
    off = pl.multiple_of(start * GRID_W, GRID_W)
    kw = kb_s[pl.ds(off, n_loc), :]
    vw = vb_s[pl.ds(off, n_loc), :]
    q = q_ref[rr * GRID_W:(rr + 1) * GRID_W, :]
    qq = jnp.concatenate([jnp.where(hm, q, 0.0).astype(BF16) for hm in halves], axis=0)
    bias = bias_ref[pat].reshape(2 * GRID_W, n_loc)
    s_loc = lax.dot_general(qq, kw, _NT, preferred_element_type=F32) + bias
    s_ctx = lax.dot_general(qq, kc, _NT, preferred_element_type=F32)
    m = jnp.maximum(jnp.max(s_loc, axis=-1, keepdims=True),
                    jnp.max(s_ctx, axis=-1, keepdims=True))
    p_loc = jnp.exp(s_loc - m)
    p_ctx = jnp.exp(s_ctx - m)
    l = jnp.sum(p_loc, axis=-1, keepdims=True) + jnp.sum(p_ctx, axis=-1, keepdims=True)
    o = (jnp.dot(p_loc.astype(BF16), vw, preferred_element_type=F32)
         + jnp.dot(p_ctx.astype(BF16), vc, preferred_element_type=F32)) / l
    o_ref[rr * GRID_W:(rr + 1) * GRID_W, :] = jnp.where(halves[0], o[:GRID_W], o[GRID_W:])


def _na_lat_attention(q, k, v, kc, vc, bias):
  qv, b0, b = _pass_view(q, True)
  kv, vv = _pass_view(k, True)[0], _pass_view(v, True)[0]
  n_pat = bias.shape[0]
  blk = NA_RB * GRID_W
  qspec = pl.BlockSpec((None, blk, LANES), lambda bi, h, r: (bi + b0, r, h))
  kspec = pl.BlockSpec((None, DEC_SEQ, LANES), lambda bi, h, r: (bi + b0, 0, h))
  cspec = pl.BlockSpec((None, PAST_LEN, LANES), lambda bi, h, r: (bi, 0, h))
  bspec = pl.BlockSpec((n_pat, 2, GRID_W, NA_KH * GRID_W), lambda bi, h, r: (0, h, 0, 0))
  return pl.pallas_call(
      _na_lat_body,
      grid=(b, D_MODEL // LANES, GRID_R // NA_RB),
      in_specs=[qspec, kspec, kspec, cspec, cspec, bspec],
      out_specs=pl.BlockSpec((None, blk, LANES), lambda bi, h, r: (bi, r, h)),
      out_shape=jax.ShapeDtypeStruct((b, DEC_SEQ, D_MODEL), F32),
      scratch_shapes=[pltpu.VMEM((DEC_SEQ, LANES), BF16)] * 2
      + [pltpu.VMEM((PAST_LEN, LANES), BF16)] * 2,
      compiler_params=_cparams(("parallel", "parallel", "arbitrary")),
      name="na_lat_attention",
  )(qv, kv, vv, kc, vc, bias)


SUBLANES = 8


def _sort_network(n):
  pairs = []
  p = 1
  while p < n:
    k = p
    while k >= 1:
      for j in range(k % p, n - k, 2 * k):
        for i in range(min(k, n - j - k)):
          if (i + j) // (2 * p) == (i + j + k) // (2 * p):
            pairs.append((i + j, i + j + k))
      k //= 2
    p *= 2
  return pairs


def _pop_heads(lists, hit, depth):
  for k in range(depth):
    lists[k] = jnp.where(hit, lists[k + 1], lists[k])


def _top16_sorted(s):
  t = s.shape[1]
  n = s.shape[0] // SUBLANES
  xs = [s[SUBLANES * k:SUBLANES * (k + 1), :] for k in range(n)]
  for i, j in _sort_network(n):
    xs[i], xs[j] = jnp.maximum(xs[i], xs[j]), jnp.minimum(xs[i], xs[j])
  rows = lax.broadcasted_iota(jnp.int32, (PEER_TOPK, t), 0)
  v = jnp.zeros((PEER_TOPK, t), F32)
  for r in range(PEER_TOPK):
    m = jnp.max(xs[0], axis=0, keepdims=True)
    v = jnp.where(rows == r, m, v)
    _pop_heads(xs, xs[0] == m, PEER_TOPK - 1 - r)
  return v


def _router_body(x_ref, g_ref, sc_ref, sh_ref, wq_ref, k1_ref, k2_ref,
                 hb_ref, a1_ref, n1_ref, b2_ref, r2_ref, h_s):
  @pl.when(pl.program_id(1) == 0)
  def _():
    h = _rms_mod(x_ref[...], g_ref[...], sc_ref[...], sh_ref[...]).astype(BF16)
    h_s[...] = h
    hb_ref[...] = h

  qh = jnp.dot(h_s[...], wq_ref[...], preferred_element_type=F32)
  half = PEER_DK // 2
  s1 = lax.dot_general(k1_ref[...], qh[:, :half].astype(BF16), _NT, preferred_element_type=F32)
  s2 = lax.dot_general(k2_ref[...], qh[:, half:].astype(BF16), _NT, preferred_element_type=F32)
  s1 = s1 - jnp.max(s1, axis=0, keepdims=True)
  s2 = s2 - jnp.max(s2, axis=0, keepdims=True)
  v1 = _top16_sorted(s1)
  v2 = _top16_sorted(s2)
  cand = [v1[:SUBLANES] + v2[k:k + 1] for k in range(PEER_TOPK)]
  tail = v1[SUBLANES:] + v2[0:1]
  z = jnp.zeros_like(v2[0:1])
  for r in range(PEER_TOPK):
    th = jnp.max(jnp.maximum(cand[0], tail), axis=0, keepdims=True)
    z = z + jnp.exp(th)
    tail = jnp.where(tail == th, NEG_INF, tail)
    _pop_heads(cand, cand[0] == th, PEER_TOPK - 1 - r)
  n1 = jnp.zeros_like(s1)
  r2 = jnp.zeros_like(s2)
  for r in range(PEER_TOPK):
    n1 = n1 + jnp.where(s1 + v2[r:r + 1] >= th, 1.0, 0.0)
    r2 = r2 + jnp.where(v2[r:r + 1] > s2, 1.0, 0.0)
  a1_ref[...] = jnp.exp(s1)
  n1_ref[...] = n1
  b2_ref[...] = pltpu.bitcast((jnp.exp(s2) * (1.0 / z)).astype(BF16), jnp.uint32)
  r2_ref[...] = pltpu.bitcast(r2.astype(BF16), jnp.uint32)


def _peer_router(x, g, sc, sh, wq, k1, k2):
  n = x.shape[0]
  tm = PEER_T
  row = lambda w: pl.BlockSpec((tm, w), lambda i, h: (i, 0))
  full = lambda shape: pl.BlockSpec(shape, lambda i, h: (0,) * len(shape))
  modspec = pl.BlockSpec((None, 1, D_MODEL), lambda i, h: (_mod_row(i, tm), 0, 0))
  keyspec = pl.BlockSpec((None, PEER_NKEYS, PEER_DK // 2), lambda i, h: (h, 0, 0))
  tspec = pl.BlockSpec((None, PEER_NKEYS, tm), lambda i, h: (h, 0, i))
  pspec = pl.BlockSpec((None, PEER_NKEYS // 2, tm), lambda i, h: (h, 0, i))
  return pl.pallas_call(
      _router_body,
      grid=(n // tm, PEER_HEADS),
      in_specs=[row(D_MODEL), full((1, D_MODEL)), modspec, modspec,
                pl.BlockSpec((D_MODEL, PEER_DK), lambda i, h: (0, h)), keyspec, keyspec],
      out_specs=[row(D_MODEL), tspec, tspec, pspec, pspec],
      out_shape=[jax.ShapeDtypeStruct((n, D_MODEL), BF16)]
      + [jax.ShapeDtypeStruct((PEER_HEADS, PEER_NKEYS, n), F32)] * 2
      + [jax.ShapeDtypeStruct((PEER_HEADS, PEER_NKEYS // 2, n), jnp.uint32)] * 2,
      scratch_shapes=[pltpu.VMEM((tm, D_MODEL), BF16)],
      compiler_params=_cparams(("parallel", "arbitrary")),
      name="peer_router",
  )(x, g, sc, sh, wq, k1, k2)


def _gelu(x):
  return 0.5 * x * (1.0 + lax.erf(x * np.float32(math.sqrt(0.5))))


def _peer_mix_body(hb_ref, u_ref, vt_ref, a1_ref, n1_ref, b2_ref, r2_ref, x_ref, gate_ref,
                   y_ref, acc_s, *piece_s):
  c = pl.program_id(1)
  n_pieces = PEER_EC // PEER_SUBC

  @pl.when(c == 0)
  def _():
    acc_s[...] = jnp.zeros_like(acc_s)

  hb = hb_ref[...]
  t = hb.shape[0]
  pk = 16
  for j in range(n_pieces):
    rows = slice(j * PEER_SUBC, (j + 1) * PEER_SUBC)
    piece_s[j][...] = lax.dot_general(u_ref[rows, :], hb, _NT, preferred_element_type=F32)
  for j in range(n_pieces):
    s_s, p_s = piece_s[j], piece_s[n_pieces + j]
    for ii in range(PEER_SUBC // PEER_NKEYS):
      i1 = j * (PEER_SUBC // PEER_NKEYS) + ii
      w = [jnp.zeros((pk, t), BF16) for _ in range(PEER_NKEYS // pk)]
      for h in range(PEER_HEADS):
        a_row = jnp.broadcast_to(a1_ref[h, i1:i1 + 1, :], (pk, t)).astype(BF16)
        n_row = jnp.broadcast_to(n1_ref[h, i1:i1 + 1, :], (pk, t)).astype(BF16)
        for sub in range(PEER_NKEYS // pk):
          words = slice(sub * pk // 2, (sub + 1) * pk // 2)
          prod = a_row * pltpu.bitcast(b2_ref[h, words, :], BF16)
          rank = pltpu.bitcast(r2_ref[h, words, :], BF16)
          w[sub] = w[sub] + jnp.where(rank < n_row, prod, jnp.zeros_like(prod))
      for sub in range(PEER_NKEYS // pk):
        row0 = ii * PEER_NKEYS + sub * pk
        p_s[row0:row0 + pk, :] = w[sub] * _gelu(s_s[row0:row0 + pk, :]).astype(BF16)
    rows = slice(j * PEER_SUBC, (j + 1) * PEER_SUBC)
    acc_s[...] += jnp.dot(vt_ref[:, rows], p_s[...], preferred_element_type=F32)

  @pl.when(c == pl.num_programs(1) - 1)
  def _():
    y_ref[...] = x_ref[...] + gate_ref[...] * acc_s[...].T


def _peer_mix(hb, u, vt, a1, n1, b2, r2, x, gate):
  n = x.shape[0]
  t = PEER_T
  n_i1 = PEER_EC // PEER_NKEYS
  row = lambda w: pl.BlockSpec((t, w), lambda i, c: (i, 0))
  i1spec = pl.BlockSpec((PEER_HEADS, n_i1, t), lambda i, c: (0, c, i))
  i2spec = pl.BlockSpec((PEER_HEADS, PEER_NKEYS // 2, t), lambda i, c: (0, 0, i))
  return pl.pallas_call(
      _peer_mix_body,
      grid=(n // t, PEER_EXPERTS // PEER_EC),
      in_specs=[row(D_MODEL),
                pl.BlockSpec((PEER_EC, D_MODEL), lambda i, c: (c, 0)),
                pl.BlockSpec((D_MODEL, PEER_EC), lambda i, c: (0, c)),
                i1spec, i1spec, i2spec, i2spec,
                row(D_MODEL),
                pl.BlockSpec((None, 1, D_MODEL), lambda i, c: (_mod_row(i, t), 0, 0))],
      out_specs=row(D_MODEL),
      out_shape=jax.ShapeDtypeStruct((n, D_MODEL), F32),
      scratch_shapes=[pltpu.VMEM((D_MODEL, t), F32)]
      + [pltpu.VMEM((PEER_SUBC, t), F32)] * (PEER_EC // PEER_SUBC)
      + [pltpu.VMEM((PEER_SUBC, t), BF16)] * (PEER_EC // PEER_SUBC),
      compiler_params=_cparams(("parallel", "arbitrary"), VMEM_LIMIT_PEER),
      name="peer_mix",
  )(hb, u, vt, a1, n1, b2, r2, x, gate)


def _peer_layer(x, p, sc, sh, gate):
  hb, a1, n1, b2, r2 = _peer_router(x, p['norm2_g'], sc, sh, p['peer_wq'], p['peer_k1'],
                                    p['peer_k2'])
  return _peer_mix(hb, p['peer_u'], p['peer_vt'], a1, n1, b2, r2, x, gate)


def _rotated_tiles(w, group):
  k, n = w.shape
  g = w.reshape(k, n // group, 2, group // 2)
  return jnp.concatenate([-g[:, :, 1:], g[:, :, :1]], axis=2).reshape(k, n)


def _rope_table(rot_dims, reps):
  half = rot_dims // 2
  t = jnp.arange(DEC_SEQ)
  inv = ROPE_BASE ** (-jnp.arange(half, dtype=F32) / half)
  parts_c, parts_s = [], []
  for pos in (t // GRID_W, t % GRID_W):
    ang = pos.astype(F32)[:, None] * inv[None, :]
    parts_c += [jnp.cos(ang), jnp.cos(ang)]
    parts_s += [jnp.sin(ang), jnp.sin(ang)]
  cos = jnp.tile(jnp.concatenate(parts_c, axis=1), (1, reps))
  sin = jnp.tile(jnp.concatenate(parts_s, axis=1), (1, reps))
  cos = jnp.concatenate([cos, jnp.ones((TM, LANES), F32)], axis=0)
  sin = jnp.concatenate([sin, jnp.zeros((TM, LANES), F32)], axis=0)
  return cos, sin


def _na_bias_table(rpb):
  reps = [NA_KH // 2] + list(range(NA_KH // 2)) + list(range(GRID_R - NA_KH // 2 + 1, GRID_R))
  cols = np.arange(GRID_W)
  col_start = np.clip(cols - NA_KW // 2, 0, GRID_W - NA_KW)
  col_mask = (cols[None, :] >= col_start[:, None]) & (cols[None, :] < col_start[:, None] + NA_KW)
  dc = np.clip(cols[None, :] - cols[:, None], -(NA_KW - 1), NA_KW - 1) + (NA_KW - 1)
  dr = np.stack([np.clip(r - NA_KH // 2, 0, GRID_R - NA_KH) + np.arange(NA_KH) - r + (NA_KH - 1)
                 for r in reps])
  onehot = (dc[:, :, None] == np.arange(2 * NA_KW - 1)).astype(np.float32)
  bias = jnp.einsum('hpjc,qwc->phqjw', rpb[:, dr, :], onehot, precision=lax.Precision.HIGHEST)
  bias = jnp.where(col_mask[None, None, :, None, :], bias, NEG_INF)
  return bias.reshape(len(reps), NA_HEADS, GRID_W, NA_KH * GRID_W)


def _ctx_rows(a, *shape):
  return a[:N_CTX].reshape(BATCH, SEQ, *shape)


def _diff_mixer(layer, p, x, sc, sh, cache, tables):
  w = p['w_qkv']
  d = D_MODEL
  wq, wk, wv = w[:, :d], w[:, d:2 * d], w[:, 2 * d:]
  rot = lambda m: _rotated_tiles(m, DA_QK // 2)
  w5 = jnp.concatenate([wq, rot(wq), wk, rot(wk), wv], axis=1).astype(BF16)
  cos, sin = tables['da']
  q, k, v = _qkv_rope(x, p['norm1_g'], sc, sh, w5, cos, sin, DA_QK ** -0.5)
  lam_rows = jnp.zeros((8, LANES), F32)
  for i, nme in enumerate(('lam_q1', 'lam_k1', 'lam_q2', 'lam_k2')):
    lam_rows = lam_rows.at[i, :DA_QK].set(p[nme])
  g = p['subln_g'].reshape(1, DA_V)
  oc = _diff_attention(q, k, v, None, lam_rows, g, layer, lat=False)
  ol = _diff_attention(q, k, v, cache, lam_rows, g, layer, lat=True)
  state = (_ctx_rows(k, DA_HEADS, 2 * DA_QK), _ctx_rows(v, DA_HEADS, DA_V))
  return oc, ol, state


def _na_mixer(p, x, sc, sh, cache):
  d = D_MODEL
  q, k, v = _qkv_plain(x, p['norm1_g'], sc, sh, p['w_qkv'].astype(BF16), NA_HD ** -0.5)
  oc = _pair_attention(q, k, v, lat=False)
  bias = _na_bias_table(p['rpb'])
  ol = _na_lat_attention(q, k, v, cache[0].reshape(DEC_BATCH, PAST_LEN, d),
                         cache[1].reshape(DEC_BATCH, PAST_LEN, d), bias)
  state = (_ctx_rows(k, NA_HEADS, NA_HD), _ctx_rows(v, NA_HEADS, NA_HD))
  return oc, ol, state


def _mla_mixer(p, x, sc, sh, cache, tables):
  a, b = MLA_Q_RANK, MLA_Q_RANK + MLA_KV_RANK
  w_in = p['w_in']
  kpe_w = w_in[:, b:]
  w_in_x = jnp.concatenate([w_in[:, :b], jnp.tile(kpe_w, (1, 4)),
                            jnp.tile(_rotated_tiles(kpe_w, MLA_ROPE // 2), (1, 4))],
                           axis=1).astype(BF16)
  cos, sin = tables['mla']
  cq, ckv, kpe = _mla_in(x, p['norm1_g'], sc, sh, w_in_x, p['q_norm_g'].reshape(1, a),
                         p['kv_norm_g'].reshape(1, MLA_KV_RANK), cos, sin)
  w_uq = p['w_uq'].reshape(a, MLA_HEADS, MLA_NOPE + MLA_ROPE)
  w_qn = w_uq[:, :, :MLA_NOPE].reshape(a, -1)
  w_qp = w_uq[:, :, MLA_NOPE:].reshape(a, -1)
  w_uq_x = jnp.concatenate([w_qn, w_qp, _rotated_tiles(w_qp, MLA_ROPE // 2)], axis=1).astype(BF16)
  qn, qp = _mla_q(cq, w_uq_x, cos, sin)
  w_ukv = p['w_ukv'].reshape(MLA_KV_RANK, MLA_HEADS, MLA_NOPE + MLA_V)
  w_ukv_x = jnp.concatenate([w_ukv[:, :, :MLA_NOPE].reshape(MLA_KV_RANK, -1),
                             w_ukv[:, :, MLA_NOPE:].reshape(MLA_KV_RANK, -1)], axis=1).astype(BF16)
  kn, v = _mla_kv(ckv, w_ukv_x)
  knc, vc = _mla_kv(cache[0].reshape(DEC_BATCH * PAST_LEN, MLA_KV_RANK), w_ukv_x)
  cached = (knc.reshape(DEC_BATCH, PAST_LEN, D_MODEL), jnp.tile(cache[1], (1, 1, LANES // MLA_ROPE)),
            vc.reshape(DEC_BATCH, PAST_LEN, D_MODEL))
  scale = (MLA_NOPE + MLA_ROPE) ** -0.5
  oc = _pair_attention(qn, kn, v, lat=False, scale=scale, q_pe=qp, k_pe=kpe)
  ol = _pair_attention(qn, kn, v, lat=True, cache=cached, scale=scale, q_pe=qp, k_pe=kpe)
  state = (_ctx_rows(ckv, MLA_KV_RANK), _ctx_rows(kpe[:, :MLA_ROPE], MLA_ROPE))
  return oc, ol, state


def _layer(layer, p, x, cond8, cache, tables):
  sh1, sc1, g1, sh2, sc2, g2 = _modulation(cond8, p['ada_w'], p['ada_b'])
  kind = layer % 3
  if kind == 0:
    oc, ol, state = _diff_mixer(layer, p, x, sc1, sh1, cache, tables)
  elif kind == 1:
    oc, ol, state = _na_mixer(p, x, sc1, sh1, cache)
  else:
    oc, ol, state = _mla_mixer(p, x, sc1, sh1, cache, tables)
  x = _out_proj(oc, ol, p['w_o'].astype(BF16), x, g1)
  x = _peer_layer(x, p, sc2, sh2, g2)
  return x, state


def kernel(x_prompt, x_sample, cache_l0_k, cache_l0_v, cache_l1_k, cache_l1_v, cache_l2_ckv, cache_l2_kpe, cache_l3_k, cache_l3_v, c, c_ctx, l0_norm1_g, l0_norm2_g, l0_ada_w, l0_ada_b, l0_w_qkv, l0_w_o, l0_lam_q1, l0_lam_k1, l0_lam_q2, l0_lam_k2, l0_subln_g, l0_peer_wq, l0_peer_k1, l0_peer_k2, l0_peer_u, l0_peer_v, l1_norm1_g, l1_norm2_g, l1_ada_w, l1_ada_b, l1_w_qkv, l1_w_o, l1_rpb, l1_peer_wq, l1_peer_k1, l1_peer_k2, l1_peer_u, l1_peer_v, l2_norm1_g, l2_norm2_g, l2_ada_w, l2_ada_b, l2_w_in, l2_q_norm_g, l2_w_uq, l2_kv_norm_g, l2_w_ukv, l2_w_o, l2_peer_wq, l2_peer_k1, l2_peer_k2, l2_peer_u, l2_peer_v, l3_norm1_g, l3_norm2_g, l3_ada_w, l3_ada_b, l3_w_qkv, l3_w_o, l3_lam_q1, l3_lam_k1, l3_lam_q2, l3_lam_k2, l3_subln_g, l3_peer_wq, l3_peer_k1, l3_peer_k2, l3_peer_u, l3_peer_v, final_norm_g):
  common = lambda n1, n2, aw, ab, wq, k1, k2, u, v: dict(
      norm1_g=n1.reshape(1, D_MODEL), norm2_g=n2.reshape(1, D_MODEL), ada_w=aw, ada_b=ab,
      peer_wq=wq.astype(BF16), peer_k1=k1.astype(BF16), peer_k2=k2.astype(BF16),
      peer_u=u.astype(BF16), peer_vt=v.T.astype(BF16))
  p0 = dict(common(l0_norm1_g, l0_norm2_g, l0_ada_w, l0_ada_b, l0_peer_wq, l0_peer_k1, l0_peer_k2,
                   l0_peer_u, l0_peer_v),
            w_qkv=l0_w_qkv, w_o=l0_w_o, lam_q1=l0_lam_q1, lam_k1=l0_lam_k1, lam_q2=l0_lam_q2,
            lam_k2=l0_lam_k2, subln_g=l0_subln_g)
  p1 = dict(common(l1_norm1_g, l1_norm2_g, l1_ada_w, l1_ada_b, l1_peer_wq, l1_peer_k1, l1_peer_k2,
                   l1_peer_u, l1_peer_v),
            w_qkv=l1_w_qkv, w_o=l1_w_o, rpb=l1_rpb)
  p2 = dict(common(l2_norm1_g, l2_norm2_g, l2_ada_w, l2_ada_b, l2_peer_wq, l2_peer_k1, l2_peer_k2,
                   l2_peer_u, l2_peer_v),
            w_in=l2_w_in, q_norm_g=l2_q_norm_g, w_uq=l2_w_uq, kv_norm_g=l2_kv_norm_g,
            w_ukv=l2_w_ukv, w_o=l2_w_o)
  p3 = dict(common(l3_norm1_g, l3_norm2_g, l3_ada_w, l3_ada_b, l3_peer_wq, l3_peer_k1, l3_peer_k2,
                   l3_peer_u, l3_peer_v),
            w_qkv=l3_w_qkv, w_o=l3_w_o, lam_q1=l3_lam_q1, lam_k1=l3_lam_k1, lam_q2=l3_lam_q2,
            lam_k2=l3_lam_k2, subln_g=l3_subln_g)
  params = (p0, p1, p2, p3)
  caches = ((cache_l0_k, cache_l0_v), (cache_l1_k, cache_l1_v),
            (cache_l2_ckv, cache_l2_kpe), (cache_l3_k, cache_l3_v))
  tables = dict(da=_rope_table(DA_QK // 2, LANES // DA_QK),
                mla=_rope_table(MLA_ROPE // 2, LANES // MLA_ROPE))

  cond8 = jnp.zeros((8, D_MODEL), F32).at[0].set(c_ctx).at[1:1 + DEC_BATCH].set(c)
  x = jnp.concatenate([x_prompt.reshape(N_CTX, D_MODEL), x_sample.reshape(N_LAT, D_MODEL)], axis=0)
  states = []
  for layer in range(DEPTH):
    x, st = _layer(layer, params[layer], x, cond8, caches[layer], tables)
    states.extend(st)
  y = _final_norm(x, final_norm_g.reshape(1, D_MODEL))
  return (y[:N_CTX].reshape(BATCH, SEQ, D_MODEL), y[N_CTX:].reshape(DEC_BATCH, DEC_SEQ, D_MODEL),
          *states)
```

```python
import functools
import math

import numpy as np
import jax
import jax.numpy as jnp
from jax import lax
from jax.experimental import pallas as pl
from jax.experimental.pallas import tpu as pltpu

F32 = jnp.float32
BF16 = jnp.bfloat16

D_MODEL = 1024
BATCH = 32
SEQ = 256
DEPTH = 4
DEC_BATCH = 2
DEC_SEQ = 4096
PAST_LEN = 512
GRID_W = 64
GRID_R = DEC_SEQ // GRID_W
EPS = 1e-6
ROPE_BASE = 10000.0
NEG_INF = -1e30

DA_HEADS = 8
DA_QK = 64
DA_V = 128
NA_HEADS = 16
NA_HD = 64
NA_KH = 8
NA_KW = 16
MLA_HEADS = 16
MLA_NOPE = 64
MLA_ROPE = 32
MLA_V = 64
MLA_Q_RANK = 384
MLA_KV_RANK = 256
PEER_HEADS = 8
PEER_NKEYS = 128
PEER_EXPERTS = PEER_NKEYS * PEER_NKEYS
PEER_DK = 256
PEER_TOPK = 16

N_CTX = BATCH * SEQ
N_LAT = DEC_BATCH * DEC_SEQ
N_TOK = N_CTX + N_LAT
LAT_KEYS = DEC_SEQ + PAST_LEN

LANES = 128
VMEM_LIMIT = 48 << 20
VMEM_LIMIT_PEER = 56 << 20

TM = 256
TQ = 512
TK = 512
ATTN_UNROLL = 9
NA_RB = 8
PEER_T = 512
PEER_EC = 2048
PEER_SUBC = 256

_NT = (((1,), (1,)), ((), ()))


def _cparams(sem, vmem=VMEM_LIMIT):
  return pltpu.CompilerParams(dimension_semantics=sem, vmem_limit_bytes=vmem)


def _mod_row(i, tm):
  nb_ctx = N_CTX // tm
  nb_bat = DEC_SEQ // tm
  return jnp.where(i < nb_ctx, 0, 1 + (i - nb_ctx) // nb_bat)


def _rope_blk(i, tm):
  nb_ctx = N_CTX // tm
  nb_bat = DEC_SEQ // tm
  return jnp.where(i < nb_ctx, nb_bat, (i - nb_ctx) % nb_bat)


def _rms_mod(x, g, sc, sh):
  y = x * lax.rsqrt(jnp.mean(x * x, axis=-1, keepdims=True) + EPS)
  return (y * g) * (1.0 + sc) + sh


def _rms(x, g):
  return x * lax.rsqrt(jnp.mean(x * x, axis=-1, keepdims=True) + EPS) * g


def _row_spec(width, tm=TM):
  return pl.BlockSpec((tm, width), lambda i: (i, 0))


def _full_spec(shape):
  return pl.BlockSpec(shape, lambda i: (0,) * len(shape))


def _mod_spec(tm=TM):
  return pl.BlockSpec((None, 1, D_MODEL), lambda i: (_mod_row(i, tm), 0, 0))


def _mod_body(c_ref, w_ref, b_ref, o_ref):
  c = c_ref[...]
  s = c / (1.0 + jnp.exp(-c))
  o_ref[...] = jnp.dot(s, w_ref[...], precision=lax.Precision.HIGHEST,
                       preferred_element_type=F32) + b_ref[...]


def _modulation(cond8, ada_w, ada_b):
  n = ada_w.shape[1]
  tn = 1536
  out = pl.pallas_call(
      _mod_body,
      grid=(n // tn,),
      in_specs=[pl.BlockSpec((8, D_MODEL), lambda j: (0, 0)),
                pl.BlockSpec((D_MODEL, tn), lambda j: (0, j)),
                pl.BlockSpec((1, tn), lambda j: (0, j))],
      out_specs=pl.BlockSpec((8, tn), lambda j: (0, j)),
      out_shape=jax.ShapeDtypeStruct((8, n), F32),
      compiler_params=_cparams(("parallel",)),
      name="modulation",
  )(cond8, ada_w, ada_b.reshape(1, n))
  return [out[:, k * D_MODEL:(k + 1) * D_MODEL].reshape(8, 1, D_MODEL) for k in range(6)]


def _tile_lanes(t, reps):
  return jnp.concatenate([t] * reps, axis=1)


def _qkv_rope_body(x_ref, g_ref, sc_ref, sh_ref, w_ref, cos_ref, sin_ref,
                   q_ref, k_ref, v_ref, *, q_scale):
  h = _rms_mod(x_ref[...], g_ref[...], sc_ref[...], sh_ref[...]).astype(BF16)
  reps = D_MODEL // LANES
  cos = _tile_lanes(cos_ref[...], reps)
  sin = _tile_lanes(sin_ref[...], reps)
  d = D_MODEL
  dot = lambda a, b: jnp.dot(h, w_ref[:, a:b], preferred_element_type=F32)
  q_ref[...] = (dot(0, d) * cos + dot(d, 2 * d) * sin) * q_scale
  k_ref[...] = dot(2 * d, 3 * d) * cos + dot(3 * d, 4 * d) * sin
  v_ref[...] = dot(4 * d, 5 * d)


def _qkv_rope(x, g, sc, sh, w5, cos, sin, q_scale):
  n = x.shape[0]
  rope_spec = pl.BlockSpec((TM, LANES), lambda i: (_rope_blk(i, TM), 0))
  return pl.pallas_call(
      functools.partial(_qkv_rope_body, q_scale=q_scale),
      grid=(n // TM,),
      in_specs=[_row_spec(D_MODEL), _full_spec((1, D_MODEL)), _mod_spec(), _mod_spec(),
                _full_spec(w5.shape), rope_spec, rope_spec],
      out_specs=[_row_spec(D_MODEL)] * 3,
      out_shape=[jax.ShapeDtypeStruct((n, D_MODEL), F32)] * 3,
      compiler_params=_cparams(("parallel",)),
      name="qkv_rope_proj",
  )(x, g, sc, sh, w5, cos, sin)


def _qkv_plain_body(x_ref, g_ref, sc_ref, sh_ref, w_ref, q_ref, k_ref, v_ref, *, q_scale):
  h = _rms_mod(x_ref[...], g_ref[...], sc_ref[...], sh_ref[...]).astype(BF16)
  d = D_MODEL
  dot = lambda a, b: jnp.dot(h, w_ref[:, a:b], preferred_element_type=F32)
  q_ref[...] = dot(0, d) * q_scale
  k_ref[...] = dot(d, 2 * d)
  v_ref[...] = dot(2 * d, 3 * d)


def _qkv_plain(x, g, sc, sh, w3, q_scale):
  n = x.shape[0]
  return pl.pallas_call(
      functools.partial(_qkv_plain_body, q_scale=q_scale),
      grid=(n // TM,),
      in_specs=[_row_spec(D_MODEL), _full_spec((1, D_MODEL)), _mod_spec(), _mod_spec(),
                _full_spec(w3.shape)],
      out_specs=[_row_spec(D_MODEL)] * 3,
      out_shape=[jax.ShapeDtypeStruct((n, D_MODEL), F32)] * 3,
      compiler_params=_cparams(("parallel",)),
      name="qkv_proj",
  )(x, g, sc, sh, w3)


def _mla_in_body(x_ref, g_ref, sc_ref, sh_ref, w_ref, qg_ref, kvg_ref, cos_ref, sin_ref,
                 cq_ref, ckv_ref, kpe_ref):
  h = _rms_mod(x_ref[...], g_ref[...], sc_ref[...], sh_ref[...]).astype(BF16)
  z = jnp.dot(h, w_ref[...], preferred_element_type=F32)
  a, b = MLA_Q_RANK, MLA_Q_RANK + MLA_KV_RANK
  cq_ref[...] = _rms(z[:, :a], qg_ref[...]).astype(BF16)
  ckv_ref[...] = _rms(z[:, a:b], kvg_ref[...])
  kpe_ref[...] = z[:, b:b + LANES] * cos_ref[...] + z[:, b + LANES:] * sin_ref[...]


def _mla_in(x, g, sc, sh, w_in, qg, kvg, cos, sin):
  n = x.shape[0]
  rope_spec = pl.BlockSpec((TM, LANES), lambda i: (_rope_blk(i, TM), 0))
  return pl.pallas_call(
      _mla_in_body,
      grid=(n // TM,),
      in_specs=[_row_spec(D_MODEL), _full_spec((1, D_MODEL)), _mod_spec(), _mod_spec(),
                _full_spec(w_in.shape), _full_spec((1, MLA_Q_RANK)),
                _full_spec((1, MLA_KV_RANK)), rope_spec, rope_spec],
      out_specs=[_row_spec(MLA_Q_RANK), _row_spec(MLA_KV_RANK), _row_spec(LANES)],
      out_shape=[jax.ShapeDtypeStruct((n, MLA_Q_RANK), BF16),
                 jax.ShapeDtypeStruct((n, MLA_KV_RANK), F32),
                 jax.ShapeDtypeStruct((n, LANES), F32)],
      compiler_params=_cparams(("parallel",)),
      name="mla_in_proj",
  )(x, g, sc, sh, w_in, qg, kvg, cos, sin)


def _mla_q_body(cq_ref, w_ref, cos_ref, sin_ref, qn_ref, qp_ref):
  z = jnp.dot(cq_ref[...], w_ref[...], preferred_element_type=F32)
  pe = MLA_HEADS * MLA_ROPE
  reps = pe // LANES
  cos = _tile_lanes(cos_ref[...], reps)
  sin = _tile_lanes(sin_ref[...], reps)
  qn_ref[...] = z[:, :D_MODEL]
  qp_ref[...] = z[:, D_MODEL:D_MODEL + pe] * cos + z[:, D_MODEL + pe:] * sin


def _mla_q(cq, w_uq, cos, sin):
  n = cq.shape[0]
  pe = MLA_HEADS * MLA_ROPE
  rope_spec = pl.BlockSpec((TM, LANES), lambda i: (_rope_blk(i, TM), 0))
  return pl.pallas_call(
      _mla_q_body,
      grid=(n // TM,),
      in_specs=[_row_spec(MLA_Q_RANK), _full_spec(w_uq.shape), rope_spec, rope_spec],
      out_specs=[_row_spec(D_MODEL), _row_spec(pe)],
      out_shape=[jax.ShapeDtypeStruct((n, D_MODEL), F32), jax.ShapeDtypeStruct((n, pe), F32)],
      compiler_params=_cparams(("parallel",)),
      name="mla_q_proj",
  )(cq, w_uq, cos, sin)


def _mla_kv_body(c_ref, w_ref, kn_ref, v_ref):
  z = jnp.dot(c_ref[...].astype(BF16), w_ref[...], preferred_element_type=F32)
  kn_ref[...] = z[:, :D_MODEL]
  v_ref[...] = z[:, D_MODEL:]


def _mla_kv(ckv, w_ukv):
  n = ckv.shape[0]
  return pl.pallas_call(
      _mla_kv_body,
      grid=(n // TM,),
      in_specs=[_row_spec(MLA_KV_RANK), _full_spec(w_ukv.shape)],
      out_specs=[_row_spec(D_MODEL)] * 2,
      out_shape=[jax.ShapeDtypeStruct((n, D_MODEL), F32)] * 2,
      compiler_params=_cparams(("parallel",)),
      name="mla_kv_proj",
  )(ckv, w_ukv)


def _online_update(s, m, l, acc, vb):
  m_new = jnp.maximum(m, jnp.max(s, axis=-1, keepdims=True))
  alpha = jnp.exp(m - m_new)
  p = jnp.exp(s - m_new)
  l_new = alpha * l + jnp.sum(p, axis=-1, keepdims=True)
  acc_new = alpha * acc + jnp.dot(p.astype(BF16), vb, preferred_element_type=F32)
  return m_new, l_new, acc_new


def _softmax_state(tq):
  return (jnp.full((tq, 1), NEG_INF, F32), jnp.zeros((tq, 1), F32), jnp.zeros((tq, LANES), F32))


def _stacked_attention(q_pair, kb_s, vb_s):
  tq = q_pair[0].shape[0]
  qq = jnp.concatenate(q_pair, axis=0)
  tk, n_chunks, unroll = _key_chunks(kb_s.shape[0])

  def chunk(c, carry):
    off = pl.multiple_of(c * tk, tk)
    s = lax.dot_general(qq, kb_s[pl.ds(off, tk), :], _NT, preferred_element_type=F32)
    return _online_update(s, *carry, vb_s[pl.ds(off, tk), :])

  _, l, acc = lax.fori_loop(0, n_chunks, chunk, _softmax_state(2 * tq), unroll=unroll)
  o = acc / l
  return o[:tq], o[tq:]


def _pass_view(a, lat):
  w = a.shape[-1]
  if lat:
    return a.reshape(N_TOK // DEC_SEQ, DEC_SEQ, w), N_CTX // DEC_SEQ, DEC_BATCH
  return a.reshape(N_TOK // SEQ, SEQ, w), 0, BATCH


def _stage_keys(dst, lanes, own_ref, cache_ref):
  n_own = own_ref.shape[0]
  dst[:n_own, lanes] = own_ref[...].astype(BF16)
  if cache_ref is not None:
    dst[n_own:, lanes] = cache_ref[...].astype(BF16)


def _key_chunks(n_keys):
  tk = min(TK, n_keys)
  n = n_keys // tk
  return tk, n, (ATTN_UNROLL if n % ATTN_UNROLL == 0 else 1)


def _attn_specs(q, lat):
  qv, b0, nb = _pass_view(q, lat)
  sq = qv.shape[1]
  tq = min(TQ, sq)
  grid = (nb, D_MODEL // LANES, sq // tq)
  qspec = pl.BlockSpec((None, tq, LANES), lambda bi, h, qi: (bi + b0, qi, h))
  kspec = pl.BlockSpec((None, sq, LANES), lambda bi, h, qi: (bi + b0, 0, h))
  cspec = pl.BlockSpec((None, PAST_LEN, LANES), lambda bi, h, qi: (bi, 0, h))
  ospec = pl.BlockSpec((None, tq, LANES), lambda bi, h, qi: (bi, qi, h))
  n_keys = sq + (PAST_LEN if lat else 0)
  return grid, qspec, kspec, cspec, ospec, n_keys, (nb, sq, D_MODEL)


def _diff_attn_body(*refs, lam_init, cached):
  if cached:
    lam_ref, g_ref, q_ref, k_ref, v_ref, kc_ref, vc_ref, o_ref, kb_s, vb_s = refs
  else:
    lam_ref, g_ref, q_ref, k_ref, v_ref, o_ref, kb_s, vb_s = refs
    kc_ref = vc_ref = None

  @pl.when(pl.program_id(2) == 0)
  def _():
    _stage_keys(kb_s, slice(None), k_ref, kc_ref)
    _stage_keys(vb_s, slice(None), v_ref, vc_ref)

  q = q_ref[...]
  tq = q.shape[0]
  lane = lax.broadcasted_iota(jnp.int32, q.shape, 1)
  q1 = jnp.where(lane < DA_QK, q, 0.0).astype(BF16)
  q2 = jnp.where(lane >= DA_QK, q, 0.0).astype(BF16)
  lv = lam_ref[...]
  lam = (jnp.exp(jnp.sum(lv[0:1] * lv[1:2], axis=-1, keepdims=True))
         - jnp.exp(jnp.sum(lv[2:3] * lv[3:4], axis=-1, keepdims=True)) + lam_init)
  o1, o2 = _stacked_attention((q1, q2), kb_s, vb_s)
  o_ref[...] = _rms(o1 - lam * o2, g_ref[...]) * (1.0 - lam_init)


def _diff_attention(q, k, v, cache, lam_rows, subln_g, layer, lat):
  grid, qspec, kspec, cspec, ospec, n_keys, oshape = _attn_specs(q, lat)
  view = lambda a: _pass_view(a, lat)[0]
  lam_init = 0.8 - 0.6 * math.exp(-0.3 * layer)
  const = lambda shape: pl.BlockSpec(shape, lambda bi, h, qi: (0, 0))
  in_specs = [const((8, LANES)), const((1, LANES)), qspec, kspec, kspec]
  args = [lam_rows, subln_g, view(q), view(k), view(v)]
  if lat:
    in_specs += [cspec, cspec]
    args += [cache[0].reshape(DEC_BATCH, PAST_LEN, D_MODEL),
             cache[1].reshape(DEC_BATCH, PAST_LEN, D_MODEL)]
  return pl.pallas_call(
      functools.partial(_diff_attn_body, lam_init=lam_init, cached=lat),
      grid=grid, in_specs=in_specs, out_specs=ospec,
      out_shape=jax.ShapeDtypeStruct(oshape, F32),
      scratch_shapes=[pltpu.VMEM((n_keys, LANES), BF16)] * 2,
      compiler_params=_cparams(("parallel", "parallel", "arbitrary")),
      name="diff_attention",
  )(*args)


def _pair_attn_body(*refs, scale, with_pe, cached):
  refs = list(refs)
  q_ref = refs.pop(0)
  qp_ref = refs.pop(0) if with_pe else None
  k_ref = refs.pop(0)
  kp_ref = refs.pop(0) if with_pe else None
  v_ref = refs.pop(0)
  kc_ref = refs.pop(0) if cached else None
  kpc_ref = refs.pop(0) if (cached and with_pe) else None
  vc_ref = refs.pop(0) if cached else None
  o_ref, kb_s, vb_s = refs

  @pl.when(pl.program_id(2) == 0)
  def _():
    _stage_keys(kb_s, slice(0, LANES), k_ref, kc_ref)
    if with_pe:
      _stage_keys(kb_s, slice(LANES, 2 * LANES), kp_ref, kpc_ref)
    _stage_keys(vb_s, slice(None), v_ref, vc_ref)

  q = q_ref[...] * scale
  tq = q.shape[0]
  lane = lax.broadcasted_iota(jnp.int32, q.shape, 1)
  halves = (lane < NA_HD, lane >= NA_HD)
  qs = [jnp.where(hm, q, 0.0).astype(BF16) for hm in halves]
  if with_pe:
    qp = qp_ref[...] * scale
    base = (pl.program_id(1) % 2) * (2 * MLA_ROPE)
    qs = [jnp.concatenate(
        [qs[a], jnp.where((lane >= base + a * MLA_ROPE) & (lane < base + (a + 1) * MLA_ROPE),
                          qp, 0.0).astype(BF16)], axis=1) for a in range(2)]
  o_ref[...] = jnp.where(halves[0], *_stacked_attention(qs, kb_s, vb_s))


def _pair_attention(q, k, v, lat, cache=None, scale=1.0, q_pe=None, k_pe=None):
  grid, qspec, kspec, cspec, ospec, n_keys, oshape = _attn_specs(q, lat)
  view = lambda a: _pass_view(a, lat)[0]
  b0 = _pass_view(q, lat)[1]
  with_pe = q_pe is not None
  tq, sq = qspec.block_shape[1], kspec.block_shape[1]
  in_specs, args = [qspec], [view(q)]
  if with_pe:
    in_specs.append(pl.BlockSpec((None, tq, LANES), lambda bi, h, qi: (bi + b0, qi, h // 2)))
    args.append(view(q_pe))
  in_specs.append(kspec)
  args.append(view(k))
  if with_pe:
    in_specs.append(pl.BlockSpec((None, sq, LANES), lambda bi, h, qi: (bi + b0, 0, 0)))
    args.append(view(k_pe))
  in_specs.append(kspec)
  args.append(view(v))
  if lat:
    in_specs.append(cspec)
    args.append(cache[0])
    if with_pe:
      in_specs.append(pl.BlockSpec((None, PAST_LEN, LANES), lambda bi, h, qi: (bi, 0, 0)))
      args.append(cache[1])
    in_specs.append(cspec)
    args.append(cache[-1])
  return pl.pallas_call(
      functools.partial(_pair_attn_body, scale=scale, with_pe=with_pe, cached=lat),
      grid=grid, in_specs=in_specs, out_specs=ospec,
      out_shape=jax.ShapeDtypeStruct(oshape, F32),
      scratch_shapes=[pltpu.VMEM((n_keys, 2 * LANES if with_pe else LANES), BF16),
                      pltpu.VMEM((n_keys, LANES), BF16)],
      compiler_params=_cparams(("parallel", "parallel", "arbitrary")),
      name="pair_attention_pe" if with_pe else "pair_attention",
  )(*args)


def _na_lat_body(q_ref, k_ref, v_ref, kc_ref, vc_ref, bias_ref, o_ref, kb_s, vb_s, kcb_s, vcb_s):
  rb = pl.program_id(2)

  @pl.when(rb == 0)
  def _():
    kb_s[...] = k_ref[...].astype(BF16)
    vb_s[...] = v_ref[...].astype(BF16)
    kcb_s[...] = kc_ref[...].astype(BF16)
    vcb_s[...] = vc_ref[...].astype(BF16)

  n_loc = NA_KH * GRID_W
  lane = lax.broadcasted_iota(jnp.int32, (GRID_W, LANES), 1)
  halves = (lane < NA_HD, lane >= NA_HD)
  kc = kcb_s[...]
  vc = vcb_s[...]
  for rr in range(NA_RB):
    r = rb * NA_RB + rr
    start = jnp.clip(r - NA_KH // 2, 0, GRID_R - NA_KH)
    pat = jnp.where(r < NA_KH // 2, 1 + r,
                    jnp.where(r > GRID_R - NA_KH // 2, r - (GRID_R - NA_KH), 0))
    off = pl.multiple_of(start * GRID_W, GRID_W)
    kw = kb_s[pl.ds(off, n_loc), :]
    vw = vb_s[pl.ds(off, n_loc), :]
    q = q_ref[rr * GRID_W:(rr + 1) * GRID_W, :]
    qq = jnp.concatenate([jnp.where(hm, q, 0.0).astype(BF16) for hm in halves], axis=0)
    bias = bias_ref[pat].reshape(2 * GRID_W, n_loc)
    s_loc = lax.dot_general(qq, kw, _NT, preferred_element_type=F32) + bias
    s_ctx = lax.dot_general(qq, kc, _NT, preferred_element_type=F32)
    m = jnp.maximum(jnp.max(s_loc, axis=-1, keepdims=True),
                    jnp.max(s_ctx, axis=-1, keepdims=True))
    p_loc = jnp.exp(s_loc - m)
    p_ctx = jnp.exp(s_ctx - m)
    l = jnp.sum(p_loc, axis=-1, keepdims=True) + jnp.sum(p_ctx, axis=-1, keepdims=True)
    o = (jnp.dot(p_loc.astype(BF16), vw, preferred_element_type=F32)
         + jnp.dot(p_ctx.astype(BF16), vc, preferred_element_type=F32)) / l
    o_ref[rr * GRID_W:(rr + 1) * GRID_W, :] = jnp.where(halves[0], o[:GRID_W], o[GRID_W:])


def _na_lat_attention(q, k, v, kc, vc, bias):
  qv, b0, b = _pass_view(q, True)
  kv, vv = _pass_view(k, True)[0], _pass_view(v, True)[0]
  n_pat = bias.shape[0]
  blk = NA_RB * GRID_W
  qspec = pl.BlockSpec((None, blk, LANES), lambda bi, h, r: (bi + b0, r, h))
  kspec = pl.BlockSpec((None, DEC_SEQ, LANES), lambda bi, h, r: (bi + b0, 0, h))
  cspec = pl.BlockSpec((None, PAST_LEN, LANES), lambda bi, h, r: (bi, 0, h))
  bspec = pl.BlockSpec((n_pat, 2, GRID_W, NA_KH * GRID_W), lambda bi, h, r: (0, h, 0, 0))
  return pl.pallas_call(
      _na_lat_body,
      grid=(b, D_MODEL // LANES, GRID_R // NA_RB),
      in_specs=[qspec, kspec, kspec, cspec, cspec, bspec],
      out_specs=pl.BlockSpec((None, blk, LANES), lambda bi, h, r: (bi, r, h)),
      out_shape=jax.ShapeDtypeStruct((b, DEC_SEQ, D_MODEL), F32),
      scratch_shapes=[pltpu.VMEM((DEC_SEQ, LANES), BF16)] * 2
      + [pltpu.VMEM((PAST_LEN, LANES), BF16)] * 2,
      compiler_params=_cparams(("parallel", "parallel", "arbitrary")),
      name="na_lat_attention",
  )(qv, kv, vv, kc, vc, bias)


SUBLANES = 8


def _sort_network(n):
  pairs = []
  p = 1
  while p < n:
    k = p
    while k >= 1:
      for j in range(k % p, n - k, 2 * k):
        for i in range(min(k, n - j - k)):
          if (i + j) // (2 * p) == (i + j + k) // (2 * p):
            pairs.append((i + j, i + j + k))
      k //= 2
    p *= 2
  return pairs


def _pop_heads(lists, hit, depth):
  for k in range(depth):
    lists[k] = jnp.where(hit, lists[k + 1], lists[k])


def _top16_sorted(s):
  t = s.shape[1]
  n = s.shape[0] // SUBLANES
  xs = [s[SUBLANES * k:SUBLANES * (k + 1), :] for k in range(n)]
  for i, j in _sort_network(n):
    xs[i], xs[j] = jnp.maximum(xs[i], xs[j]), jnp.minimum(xs[i], xs[j])
  rows = lax.broadcasted_iota(jnp.int32, (PEER_TOPK, t), 0)
  v = jnp.zeros((PEER_TOPK, t), F32)
  for r in range(PEER_TOPK):
    m = jnp.max(xs[0], axis=0, keepdims=True)
    v = jnp.where(rows == r, m, v)
    _pop_heads(xs, xs[0] == m, PEER_TOPK - 1 - r)
  return v


def _router_body(oc_ref, ol_ref, wo_ref, x_ref, g1_ref, g_ref, sc_ref, sh_ref,
                 wq_ref, k1_ref, k2_ref,
                 xn_ref, hb_ref, a1_ref, n1_ref, b2_ref, r2_ref, h_s):
  def first_head(o_ref):
    xn = x_ref[...] + g1_ref[...] * jnp.dot(o_ref[...].astype(BF16), wo_ref[...],
                                            preferred_element_type=F32)
    xn_ref[...] = xn
    h = _rms_mod(xn, g_ref[...], sc_ref[...], sh_ref[...]).astype(BF16)
    h_s[...] = h
    hb_ref[...] = h

  is_first = pl.program_id(1) == 0
  is_ctx = pl.program_id(0) < N_CTX // PEER_T
  pl.when(is_first & is_ctx)(lambda: first_head(oc_ref))
  pl.when(is_first & jnp.logical_not(is_ctx))(lambda: first_head(ol_ref))

  qh = jnp.dot(h_s[...], wq_ref[...], preferred_element_type=F32)
  half = PEER_DK // 2
  s1 = lax.dot_general(k1_ref[...], qh[:, :half].astype(BF16), _NT, preferred_element_type=F32)
  s2 = lax.dot_general(k2_ref[...], qh[:, half:].astype(BF16), _NT, preferred_element_type=F32)
  s1 = s1 - jnp.max(s1, axis=0, keepdims=True)
  s2 = s2 - jnp.max(s2, axis=0, keepdims=True)
  v1 = _top16_sorted(s1)
  v2 = _top16_sorted(s2)
  cand = [v1[:SUBLANES] + v2[k:k + 1] for k in range(PEER_TOPK)]
  tail = v1[SUBLANES:] + v2[0:1]
  z = jnp.zeros_like(v2[0:1])
  taken = jnp.zeros_like(cand[0])
  for r in range(PEER_TOPK):
    th = jnp.max(jnp.maximum(cand[0], tail), axis=0, keepdims=True)
    z = z + jnp.exp(th)
    tail = jnp.where(tail == th, NEG_INF, tail)
    hit = cand[0] == th
    taken = taken + jnp.where(hit, 1.0, 0.0)
    _pop_heads(cand, hit, PEER_TOPK - 1 - r)
  inf = jnp.float32(jnp.inf)
  first = jnp.minimum(jnp.min(jnp.where(taken >= 1.0, v1[:SUBLANES], inf), axis=0, keepdims=True),
                      jnp.min(jnp.where(tail == NEG_INF, v1[SUBLANES:], inf), axis=0,
                              keepdims=True))
  n1 = jnp.where(s1 >= first, 1.0, 0.0)
  k_max = PEER_TOPK // 2
  for k in range(2, k_max + 1):
    t_k = jnp.min(jnp.where(taken >= float(k), v1[:SUBLANES], inf), axis=0, keepdims=True)
    n1 = n1 + jnp.where(s1 >= t_k, 1.0, 0.0)
  n1 = n1 + jnp.where(s1 >= v1[0:1], jnp.maximum(taken[0:1] - float(k_max), 0.0), 0.0)
  r2 = jnp.zeros_like(s2)
  for r in range(PEER_TOPK):
    r2 = r2 + jnp.where(v2[r:r + 1] > s2, 1.0, 0.0)
  a1_ref[...] = jnp.exp(s1)
  n1_ref[...] = n1
  b2_ref[...] = pltpu.bitcast((jnp.exp(s2) * (1.0 / z)).astype(BF16), jnp.uint32)
  r2_ref[...] = pltpu.bitcast(r2.astype(BF16), jnp.uint32)


def _peer_router(oc, ol, w_o, x, gate1, g, sc, sh, wq, k1, k2):
  n = x.shape[0]
  tm = PEER_T
  nb_ctx = N_CTX // tm
  row = lambda w: pl.BlockSpec((tm, w), lambda i, h: (i, 0))
  full = lambda shape: pl.BlockSpec(shape, lambda i, h: (0,) * len(shape))
  modspec = pl.BlockSpec((None, 1, D_MODEL), lambda i, h: (_mod_row(i, tm), 0, 0))
  keyspec = pl.BlockSpec((None, PEER_NKEYS, PEER_DK // 2), lambda i, h: (h, 0, 0))
  tspec = pl.BlockSpec((None, PEER_NKEYS, tm), lambda i, h: (h, 0, i))
  pspec = pl.BlockSpec((None, PEER_NKEYS // 2, tm), lambda i, h: (h, 0, i))
  return pl.pallas_call(
      _router_body,
      grid=(n // tm, PEER_HEADS),
      in_specs=[pl.BlockSpec((tm, D_MODEL), lambda i, h: (jnp.minimum(i, nb_ctx - 1), 0)),
                pl.BlockSpec((tm, D_MODEL), lambda i, h: (jnp.maximum(i - nb_ctx, 0), 0)),
                full(w_o.shape), row(D_MODEL), modspec,
                full((1, D_MODEL)), modspec, modspec,
                pl.BlockSpec((D_MODEL, PEER_DK), lambda i, h: (0, h)), keyspec, keyspec],
      out_specs=[row(D_MODEL), row(D_MODEL), tspec, tspec, pspec, pspec],
      out_shape=[jax.ShapeDtypeStruct((n, D_MODEL), F32), jax.ShapeDtypeStruct((n, D_MODEL), BF16)]
      + [jax.ShapeDtypeStruct((PEER_HEADS, PEER_NKEYS, n), F32)] * 2
      + [jax.ShapeDtypeStruct((PEER_HEADS, PEER_NKEYS // 2, n), jnp.uint32)] * 2,
      scratch_shapes=[pltpu.VMEM((tm, D_MODEL), BF16)],
      compiler_params=_cparams(("parallel", "arbitrary")),
      name="peer_router",
  )(oc.reshape(N_CTX, D_MODEL), ol.reshape(N_LAT, D_MODEL), w_o, x, gate1, g, sc, sh, wq, k1, k2)


def _gelu(x):
  return 0.5 * x * (1.0 + lax.erf(x * np.float32(math.sqrt(0.5))))


def _peer_mix_body(hb_ref, u_ref, vt_ref, a1_ref, n1_ref, b2_ref, r2_ref, x_ref, gate_ref,
                   fg_ref, y_ref, acc_s, *piece_s, final_norm):
  c = pl.program_id(1)
  n_pieces = PEER_EC // PEER_SUBC

  @pl.when(c == 0)
  def _():
    acc_s[...] = jnp.zeros_like(acc_s)

  hb = hb_ref[...]
  t = hb.shape[0]
  pk = 16
  for j in range(n_pieces):
    rows = slice(j * PEER_SUBC, (j + 1) * PEER_SUBC)
    piece_s[j][...] = lax.dot_general(u_ref[rows, :], hb, _NT, preferred_element_type=F32)
  for j in range(n_pieces):
    s_s, p_s = piece_s[j], piece_s[n_pieces + j]
    for ii in range(PEER_SUBC // PEER_NKEYS):
      i1 = j * (PEER_SUBC // PEER_NKEYS) + ii
      w = [jnp.zeros((pk, t), BF16) for _ in range(PEER_NKEYS // pk)]
      for h in range(PEER_HEADS):
        a_row = jnp.broadcast_to(a1_ref[h, i1:i1 + 1, :], (pk, t)).astype(BF16)
        n_row = jnp.broadcast_to(n1_ref[h, i1:i1 + 1, :], (pk, t)).astype(BF16)
        for sub in range(PEER_NKEYS // pk):
          words = slice(sub * pk // 2, (sub + 1) * pk // 2)
          prod = a_row * pltpu.bitcast(b2_ref[h, words, :], BF16)
          rank = pltpu.bitcast(r2_ref[h, words, :], BF16)
          w[sub] = w[sub] + jnp.where(rank < n_row, prod, jnp.zeros_like(prod))
      for sub in range(PEER_NKEYS // pk):
        row0 = ii * PEER_NKEYS + sub * pk
        p_s[row0:row0 + pk, :] = w[sub] * _gelu(s_s[row0:row0 + pk, :]).astype(BF16)
    rows = slice(j * PEER_SUBC, (j + 1) * PEER_SUBC)
    acc_s[...] += jnp.dot(vt_ref[:, rows], p_s[...], preferred_element_type=F32)

  @pl.when(c == pl.num_programs(1) - 1)
  def _():
    y = x_ref[...] + gate_ref[...] * acc_s[...].T
    y_ref[...] = _rms(y, fg_ref[...]) if final_norm else y


def _peer_mix(hb, u, vt, a1, n1, b2, r2, x, gate, final_g, final_norm):
  n = x.shape[0]
  t = PEER_T
  n_i1 = PEER_EC // PEER_NKEYS
  row = lambda w: pl.BlockSpec((t, w), lambda i, c: (i, 0))
  i1spec = pl.BlockSpec((PEER_HEADS, n_i1, t), lambda i, c: (0, c, i))
  i2spec = pl.BlockSpec((PEER_HEADS, PEER_NKEYS // 2, t), lambda i, c: (0, 0, i))
  return pl.pallas_call(
      functools.partial(_peer_mix_body, final_norm=final_norm),
      grid=(n // t, PEER_EXPERTS // PEER_EC),
      in_specs=[row(D_MODEL),
                pl.BlockSpec((PEER_EC, D_MODEL), lambda i, c: (c, 0)),
                pl.BlockSpec((None, D_MODEL, PEER_EC), lambda i, c: (c, 0, 0)),
                i1spec, i1spec, i2spec, i2spec,
                row(D_MODEL),
                pl.BlockSpec((None, 1, D_MODEL), lambda i, c: (_mod_row(i, t), 0, 0)),
                pl.BlockSpec((1, D_MODEL), lambda i, c: (0, 0))],
      out_specs=row(D_MODEL),
      out_shape=jax.ShapeDtypeStruct((n, D_MODEL), F32),
      scratch_shapes=[pltpu.VMEM((D_MODEL, t), F32)]
      + [pltpu.VMEM((PEER_SUBC, t), F32)] * (PEER_EC // PEER_SUBC)
      + [pltpu.VMEM((PEER_SUBC, t), BF16)] * (PEER_EC // PEER_SUBC),
      compiler_params=_cparams(("parallel", "arbitrary"), VMEM_LIMIT_PEER),
      name="peer_mix",
  )(hb, u, vt, a1, n1, b2, r2, x, gate, final_g)


def _rotated_tiles(w, group):
  k, n = w.shape
  g = w.reshape(k, n // group, 2, group // 2)
  return jnp.concatenate([-g[:, :, 1:], g[:, :, :1]], axis=2).reshape(k, n)


def _rope_table(rot_dims, reps):
  half = rot_dims // 2
  t = jnp.arange(DEC_SEQ)
  inv = ROPE_BASE ** (-jnp.arange(half, dtype=F32) / half)
  parts_c, parts_s = [], []
  for pos in (t // GRID_W, t % GRID_W):
    ang = pos.astype(F32)[:, None] * inv[None, :]
    parts_c += [jnp.cos(ang), jnp.cos(ang)]
    parts_s += [jnp.sin(ang), jnp.sin(ang)]
  cos = jnp.tile(jnp.concatenate(parts_c, axis=1), (1, reps))
  sin = jnp.tile(jnp.concatenate(parts_s, axis=1), (1, reps))
  cos = jnp.concatenate([cos, jnp.ones((TM, LANES), F32)], axis=0)
  sin = jnp.concatenate([sin, jnp.zeros((TM, LANES), F32)], axis=0)
  return cos, sin


def _na_bias_table(rpb):
  reps = [NA_KH // 2] + list(range(NA_KH // 2)) + list(range(GRID_R - NA_KH // 2 + 1, GRID_R))
  cols = np.arange(GRID_W)
  col_start = np.clip(cols - NA_KW // 2, 0, GRID_W - NA_KW)
  col_mask = (cols[None, :] >= col_start[:, None]) & (cols[None, :] < col_start[:, None] + NA_KW)
  dc = np.clip(cols[None, :] - cols[:, None], -(NA_KW - 1), NA_KW - 1) + (NA_KW - 1)
  dr = np.stack([np.clip(r - NA_KH // 2, 0, GRID_R - NA_KH) + np.arange(NA_KH) - r + (NA_KH - 1)
                 for r in reps])
  onehot = (dc[:, :, None] == np.arange(2 * NA_KW - 1)).astype(np.float32)
  bias = jnp.einsum('hpjc,qwc->phqjw', rpb[:, dr, :], onehot, precision=lax.Precision.HIGHEST)
  bias = jnp.where(col_mask[None, None, :, None, :], bias, NEG_INF)
  return bias.reshape(len(reps), NA_HEADS, GRID_W, NA_KH * GRID_W)


def _ctx_rows(a, *shape):
  return a[:N_CTX].reshape(BATCH, SEQ, *shape)


def _diff_mixer(layer, p, x, sc, sh, cache, tables):
  w = p['w_qkv']
  d = D_MODEL
  wq, wk, wv = w[:, :d], w[:, d:2 * d], w[:, 2 * d:]
  rot = lambda m: _rotated_tiles(m, DA_QK // 2)
  w5 = jnp.concatenate([wq, rot(wq), wk, rot(wk), wv], axis=1).astype(BF16)
  cos, sin = tables['da']
  q, k, v = _qkv_rope(x, p['norm1_g'], sc, sh, w5, cos, sin, DA_QK ** -0.5)
  lam_rows = jnp.zeros((8, LANES), F32)
  for i, nme in enumerate(('lam_q1', 'lam_k1', 'lam_q2', 'lam_k2')):
    lam_rows = lam_rows.at[i, :DA_QK].set(p[nme])
  g = p['subln_g'].reshape(1, DA_V)
  oc = _diff_attention(q, k, v, None, lam_rows, g, layer, lat=False)
  ol = _diff_attention(q, k, v, cache, lam_rows, g, layer, lat=True)
  state = (_ctx_rows(k, DA_HEADS, 2 * DA_QK), _ctx_rows(v, DA_HEADS, DA_V))
  return oc, ol, state


def _na_mixer(p, x, sc, sh, cache):
  d = D_MODEL
  q, k, v = _qkv_plain(x, p['norm1_g'], sc, sh, p['w_qkv'].astype(BF16), NA_HD ** -0.5)
  oc = _pair_attention(q, k, v, lat=False)
  bias = _na_bias_table(p['rpb'])
  ol = _na_lat_attention(q, k, v, cache[0].reshape(DEC_BATCH, PAST_LEN, d),
                         cache[1].reshape(DEC_BATCH, PAST_LEN, d), bias)
  state = (_ctx_rows(k, NA_HEADS, NA_HD), _ctx_rows(v, NA_HEADS, NA_HD))
  return oc, ol, state


def _mla_mixer(p, x, sc, sh, cache, tables):
  a, b = MLA_Q_RANK, MLA_Q_RANK + MLA_KV_RANK
  w_in = p['w_in']
  kpe_w = w_in[:, b:]
  w_in_x = jnp.concatenate([w_in[:, :b], jnp.tile(kpe_w, (1, 4)),
                            jnp.tile(_rotated_tiles(kpe_w, MLA_ROPE // 2), (1, 4))],
                           axis=1).astype(BF16)
  cos, sin = tables['mla']
  cq, ckv, kpe = _mla_in(x, p['norm1_g'], sc, sh, w_in_x, p['q_norm_g'].reshape(1, a),
                         p['kv_norm_g'].reshape(1, MLA_KV_RANK), cos, sin)
  w_uq = p['w_uq'].reshape(a, MLA_HEADS, MLA_NOPE + MLA_ROPE)
  w_qn = w_uq[:, :, :MLA_NOPE].reshape(a, -1)
  w_qp = w_uq[:, :, MLA_NOPE:].reshape(a, -1)
  w_uq_x = jnp.concatenate([w_qn, w_qp, _rotated_tiles(w_qp, MLA_ROPE // 2)], axis=1).astype(BF16)
  qn, qp = _mla_q(cq, w_uq_x, cos, sin)
  w_ukv = p['w_ukv'].reshape(MLA_KV_RANK, MLA_HEADS, MLA_NOPE + MLA_V)
  w_ukv_x = jnp.concatenate([w_ukv[:, :, :MLA_NOPE].reshape(MLA_KV_RANK, -1),
                             w_ukv[:, :, MLA_NOPE:].reshape(MLA_KV_RANK, -1)], axis=1).astype(BF16)
  kn, v = _mla_kv(ckv, w_ukv_x)
  knc, vc = _mla_kv(cache[0].reshape(DEC_BATCH * PAST_LEN, MLA_KV_RANK), w_ukv_x)
  cached = (knc.reshape(DEC_BATCH, PAST_LEN, D_MODEL), jnp.tile(cache[1], (1, 1, LANES // MLA_ROPE)),
            vc.reshape(DEC_BATCH, PAST_LEN, D_MODEL))
  scale = (MLA_NOPE + MLA_ROPE) ** -0.5
  oc = _pair_attention(qn, kn, v, lat=False, scale=scale, q_pe=qp, k_pe=kpe)
  ol = _pair_attention(qn, kn, v, lat=True, cache=cached, scale=scale, q_pe=qp, k_pe=kpe)
  state = (_ctx_rows(ckv, MLA_KV_RANK), _ctx_rows(kpe[:, :MLA_ROPE], MLA_ROPE))
  return oc, ol, state


def _layer(layer, p, x, cond8, cache, tables, final_g):
  sh1, sc1, g1, sh2, sc2, g2 = _modulation(cond8, p['ada_w'], p['ada_b'])
  kind = layer % 3
  if kind == 0:
    oc, ol, state = _diff_mixer(layer, p, x, sc1, sh1, cache, tables)
  elif kind == 1:
    oc, ol, state = _na_mixer(p, x, sc1, sh1, cache)
  else:
    oc, ol, state = _mla_mixer(p, x, sc1, sh1, cache, tables)
  x, hb, a1, n1, b2, r2 = _peer_router(oc, ol, p['w_o'].astype(BF16), x, g1, p['norm2_g'],
                                       sc2, sh2, p['peer_wq'], p['peer_k1'], p['peer_k2'])
  x = _peer_mix(hb, p['peer_u'], p['peer_vt'], a1, n1, b2, r2, x, g2, final_g,
                final_norm=layer == DEPTH - 1)
  return x, state


def kernel(x_prompt, x_sample, cache_l0_k, cache_l0_v, cache_l1_k, cache_l1_v, cache_l2_ckv, cache_l2_kpe, cache_l3_k, cache_l3_v, c, c_ctx, l0_norm1_g, l0_norm2_g, l0_ada_w, l0_ada_b, l0_w_qkv, l0_w_o, l0_lam_q1, l0_lam_k1, l0_lam_q2, l0_lam_k2, l0_subln_g, l0_peer_wq, l0_peer_k1, l0_peer_k2, l0_peer_u, l0_peer_v, l1_norm1_g, l1_norm2_g, l1_ada_w, l1_ada_b, l1_w_qkv, l1_w_o, l1_rpb, l1_peer_wq, l1_peer_k1, l1_peer_k2, l1_peer_u, l1_peer_v, l2_norm1_g, l2_norm2_g, l2_ada_w, l2_ada_b, l2_w_in, l2_q_norm_g, l2_w_uq, l2_kv_norm_g, l2_w_ukv, l2_w_o, l2_peer_wq, l2_peer_k1, l2_peer_k2, l2_peer_u, l2_peer_v, l3_norm1_g, l3_norm2_g, l3_ada_w, l3_ada_b, l3_w_qkv, l3_w_o, l3_lam_q1, l3_lam_k1, l3_lam_q2, l3_lam_k2, l3_subln_g, l3_peer_wq, l3_peer_k1, l3_peer_k2, l3_peer_u, l3_peer_v, final_norm_g):
  common = lambda n1, n2, aw, ab, wq, k1, k2, u, v: dict(
      norm1_g=n1.reshape(1, D_MODEL), norm2_g=n2.reshape(1, D_MODEL), ada_w=aw, ada_b=ab,
      peer_wq=wq.astype(BF16), peer_k1=k1.astype(BF16), peer_k2=k2.astype(BF16),
      peer_u=u.astype(BF16),
      peer_vt=v.reshape(PEER_EXPERTS // PEER_EC, PEER_EC, D_MODEL).transpose(0, 2, 1).astype(BF16))
  p0 = dict(common(l0_norm1_g, l0_norm2_g, l0_ada_w, l0_ada_b, l0_peer_wq, l0_peer_k1, l0_peer_k2,
                   l0_peer_u, l0_peer_v),
            w_qkv=l0_w_qkv, w_o=l0_w_o, lam_q1=l0_lam_q1, lam_k1=l0_lam_k1, lam_q2=l0_lam_q2,
            lam_k2=l0_lam_k2, subln_g=l0_subln_g)
  p1 = dict(common(l1_norm1_g, l1_norm2_g, l1_ada_w, l1_ada_b, l1_peer_wq, l1_peer_k1, l1_peer_k2,
                   l1_peer_u, l1_peer_v),
            w_qkv=l1_w_qkv, w_o=l1_w_o, rpb=l1_rpb)
  p2 = dict(common(l2_norm1_g, l2_norm2_g, l2_ada_w, l2_ada_b, l2_peer_wq, l2_peer_k1, l2_peer_k2,
                   l2_peer_u, l2_peer_v),
            w_in=l2_w_in, q_norm_g=l2_q_norm_g, w_uq=l2_w_uq, kv_norm_g=l2_kv_norm_g,
            w_ukv=l2_w_ukv, w_o=l2_w_o)
  p3 = dict(common(l3_norm1_g, l3_norm2_g, l3_ada_w, l3_ada_b, l3_peer_wq, l3_peer_k1, l3_peer_k2,
                   l3_peer_u, l3_peer_v),
            w_qkv=l3_w_qkv, w_o=l3_w_o, lam_q1=l3_lam_q1, lam_k1=l3_lam_k1, lam_q2=l3_lam_q2,
            lam_k2=l3_lam_k2, subln_g=l3_subln_g)
  params = (p0, p1, p2, p3)
  caches = ((cache_l0_k, cache_l0_v), (cache_l1_k, cache_l1_v),
            (cache_l2_ckv, cache_l2_kpe), (cache_l3_k, cache_l3_v))
  tables = dict(da=_rope_table(DA_QK // 2, LANES // DA_QK),
                mla=_rope_table(MLA_ROPE // 2, LANES // MLA_ROPE))

  cond8 = jnp.zeros((8, D_MODEL), F32).at[0].set(c_ctx).at[1:1 + DEC_BATCH].set(c)
  x = jnp.concatenate([x_prompt.reshape(N_CTX, D_MODEL), x_sample.reshape(N_LAT, D_MODEL)], axis=0)
  states = []
  for layer in range(DEPTH):
    x, st = _layer(layer, params[layer], x, cond8, caches[layer], tables,
                   final_norm_g.reshape(1, D_MODEL))
    states.extend(st)
  y = x
  return (y[:N_CTX].reshape(BATCH, SEQ, D_MODEL), y[N_CTX:].reshape(DEC_BATCH, DEC_SEQ, D_MODEL),
          *states)
```

```python
import functools
import math

import numpy as np
import jax
import jax.numpy as jnp
from jax import lax
from jax.experimental import pallas as pl
from jax.experimental.pallas import tpu as pltpu

F32 = jnp.float32
BF16 = jnp.bfloat16

D_MODEL = 1024
BATCH = 32
SEQ = 256
DEPTH = 4
DEC_BATCH = 2
DEC_SEQ = 4096
PAST_LEN = 512
GRID_W = 64
GRID_R = DEC_SEQ // GRID_W
EPS = 1e-6
ROPE_BASE = 10000.0
NEG_INF = -1e30

DA_HEADS = 8
DA_QK = 64
DA_V = 128
NA_HEADS = 16
NA_HD = 64
NA_KH = 8
NA_KW = 16
MLA_HEADS = 16
MLA_NOPE = 64
MLA_ROPE = 32
MLA_V = 64
MLA_Q_RANK = 384
MLA_KV_RANK = 256
PEER_HEADS = 8
PEER_NKEYS = 128
PEER_EXPERTS = PEER_NKEYS * PEER_NKEYS
PEER_DK = 256
PEER_TOPK = 16

N_CTX = BATCH * SEQ
N_LAT = DEC_BATCH * DEC_SEQ
N_TOK = N_CTX + N_LAT
LAT_KEYS = DEC_SEQ + PAST_LEN

LANES = 128
VMEM_LIMIT = 48 << 20
VMEM_LIMIT_PEER = 56 << 20

TM = 256
TQ = 512
TK = 512
ATTN_UNROLL = 9
NA_RB = 8
PEER_T = 512
PEER_EC = 2048
PEER_SUBC = 256

_NT = (((1,), (1,)), ((), ()))


def _cparams(sem, vmem=VMEM_LIMIT):
  return pltpu.CompilerParams(dimension_semantics=sem, vmem_limit_bytes=vmem)


def _mod_row(i, tm):
  nb_ctx = N_CTX // tm
  nb_bat = DEC_SEQ // tm
  return jnp.where(i < nb_ctx, 0, 1 + (i - nb_ctx) // nb_bat)


def _rope_blk(i, tm):
  nb_ctx = N_CTX // tm
  nb_bat = DEC_SEQ // tm
  return jnp.where(i < nb_ctx, nb_bat, (i - nb_ctx) % nb_bat)


def _rms_mod(x, g, sc, sh):
  y = x * lax.rsqrt(jnp.mean(x * x, axis=-1, keepdims=True) + EPS)
  return (y * g) * (1.0 + sc) + sh


def _rms(x, g):
  return x * lax.rsqrt(jnp.mean(x * x, axis=-1, keepdims=True) + EPS) * g


def _row_spec(width, tm=TM):
  return pl.BlockSpec((tm, width), lambda i: (i, 0))


def _full_spec(shape):
  return pl.BlockSpec(shape, lambda i: (0,) * len(shape))


def _mod_spec(tm=TM):
  return pl.BlockSpec((None, 1, D_MODEL), lambda i: (_mod_row(i, tm), 0, 0))


def _mod_body(c_ref, w_ref, b_ref, o_ref):
  c = c_ref[...]
  s = c / (1.0 + jnp.exp(-c))
  o_ref[...] = jnp.dot(s, w_ref[...], precision=lax.Precision.HIGHEST,
                       preferred_element_type=F32) + b_ref[...]


def _modulation(cond8, ada_w, ada_b):
  n = ada_w.shape[1]
  tn = 1536
  out = pl.pallas_call(
      _mod_body,
      grid=(n // tn,),
      in_specs=[pl.BlockSpec((8, D_MODEL), lambda j: (0, 0)),
                pl.BlockSpec((D_MODEL, tn), lambda j: (0, j)),
                pl.BlockSpec((1, tn), lambda j: (0, j))],
      out_specs=pl.BlockSpec((8, tn), lambda j: (0, j)),
      out_shape=jax.ShapeDtypeStruct((8, n), F32),
      compiler_params=_cparams(("parallel",)),
      name="modulation",
  )(cond8, ada_w, ada_b.reshape(1, n))
  return [out[:, k * D_MODEL:(k + 1) * D_MODEL].reshape(8, 1, D_MODEL) for k in range(6)]


def _tile_lanes(t, reps):
  return jnp.concatenate([t] * reps, axis=1)


def _qkv_rope_body(x_ref, g_ref, sc_ref, sh_ref, w_ref, cos_ref, sin_ref,
                   q_ref, k_ref, v_ref, *, q_scale):
  h = _rms_mod(x_ref[...], g_ref[...], sc_ref[...], sh_ref[...]).astype(BF16)
  reps = D_MODEL // LANES
  cos = _tile_lanes(cos_ref[...], reps)
  sin = _tile_lanes(sin_ref[...], reps)
  d = D_MODEL
  dot = lambda a, b: jnp.dot(h, w_ref[:, a:b], preferred_element_type=F32)
  q_ref[...] = (dot(0, d) * cos + dot(d, 2 * d) * sin) * q_scale
  k_ref[...] = dot(2 * d, 3 * d) * cos + dot(3 * d, 4 * d) * sin
  v_ref[...] = dot(4 * d, 5 * d)


def _qkv_rope(x, g, sc, sh, w5, cos, sin, q_scale):
  n = x.shape[0]
  rope_spec = pl.BlockSpec((TM, LANES), lambda i: (_rope_blk(i, TM), 0))
  return pl.pallas_call(
      functools.partial(_qkv_rope_body, q_scale=q_scale),
      grid=(n // TM,),
      in_specs=[_row_spec(D_MODEL), _full_spec((1, D_MODEL)), _mod_spec(), _mod_spec(),
                _full_spec(w5.shape), rope_spec, rope_spec],
      out_specs=[_row_spec(D_MODEL)] * 3,
      out_shape=[jax.ShapeDtypeStruct((n, D_MODEL), F32)] * 3,
      compiler_params=_cparams(("parallel",)),
      name="qkv_rope_proj",
  )(x, g, sc, sh, w5, cos, sin)


def _qkv_plain_body(x_ref, g_ref, sc_ref, sh_ref, w_ref, q_ref, k_ref, v_ref, *, q_scale):
  h = _rms_mod(x_ref[...], g_ref[...], sc_ref[...], sh_ref[...]).astype(BF16)
  d = D_MODEL
  dot = lambda a, b: jnp.dot(h, w_ref[:, a:b], preferred_element_type=F32)
  q_ref[...] = dot(0, d) * q_scale
  k_ref[...] = dot(d, 2 * d)
  v_ref[...] = dot(2 * d, 3 * d)


def _qkv_plain(x, g, sc, sh, w3, q_scale):
  n = x.shape[0]
  return pl.pallas_call(
      functools.partial(_qkv_plain_body, q_scale=q_scale),
      grid=(n // TM,),
      in_specs=[_row_spec(D_MODEL), _full_spec((1, D_MODEL)), _mod_spec(), _mod_spec(),
                _full_spec(w3.shape)],
      out_specs=[_row_spec(D_MODEL)] * 3,
      out_shape=[jax.ShapeDtypeStruct((n, D_MODEL), F32)] * 3,
      compiler_params=_cparams(("parallel",)),
      name="qkv_proj",
  )(x, g, sc, sh, w3)


def _mla_in_body(x_ref, g_ref, sc_ref, sh_ref, w_ref, qg_ref, kvg_ref, cos_ref, sin_ref,
                 cq_ref, ckv_ref, kpe_ref):
  h = _rms_mod(x_ref[...], g_ref[...], sc_ref[...], sh_ref[...]).astype(BF16)
  z = jnp.dot(h, w_ref[...], preferred_element_type=F32)
  a, b = MLA_Q_RANK, MLA_Q_RANK + MLA_KV_RANK
  cq_ref[...] = _rms(z[:, :a], qg_ref[...]).astype(BF16)
  ckv_ref[...] = _rms(z[:, a:b], kvg_ref[...])
  kpe_ref[...] = z[:, b:b + LANES] * cos_ref[...] + z[:, b + LANES:] * sin_ref[...]


def _mla_in(x, g, sc, sh, w_in, qg, kvg, cos, sin):
  n = x.shape[0]
  rope_spec = pl.BlockSpec((TM, LANES), lambda i: (_rope_blk(i, TM), 0))
  return pl.pallas_call(
      _mla_in_body,
      grid=(n // TM,),
      in_specs=[_row_spec(D_MODEL), _full_spec((1, D_MODEL)), _mod_spec(), _mod_spec(),
                _full_spec(w_in.shape), _full_spec((1, MLA_Q_RANK)),
                _full_spec((1, MLA_KV_RANK)), rope_spec, rope_spec],
      out_specs=[_row_spec(MLA_Q_RANK), _row_spec(MLA_KV_RANK), _row_spec(LANES)],
      out_shape=[jax.ShapeDtypeStruct((n, MLA_Q_RANK), BF16),
                 jax.ShapeDtypeStruct((n, MLA_KV_RANK), F32),
                 jax.ShapeDtypeStruct((n, LANES), F32)],
      compiler_params=_cparams(("parallel",)),
      name="mla_in_proj",
  )(x, g, sc, sh, w_in, qg, kvg, cos, sin)


def _mla_q_body(cq_ref, w_ref, cos_ref, sin_ref, qn_ref, qp_ref):
  z = jnp.dot(cq_ref[...], w_ref[...], preferred_element_type=F32)
  pe = MLA_HEADS * MLA_ROPE
  reps = pe // LANES
  cos = _tile_lanes(cos_ref[...], reps)
  sin = _tile_lanes(sin_ref[...], reps)
  qn_ref[...] = z[:, :D_MODEL]
  qp_ref[...] = z[:, D_MODEL:D_MODEL + pe] * cos + z[:, D_MODEL + pe:] * sin


def _mla_q(cq, w_uq, cos, sin):
  n = cq.shape[0]
  pe = MLA_HEADS * MLA_ROPE
  rope_spec = pl.BlockSpec((TM, LANES), lambda i: (_rope_blk(i, TM), 0))
  return pl.pallas_call(
      _mla_q_body,
      grid=(n // TM,),
      in_specs=[_row_spec(MLA_Q_RANK), _full_spec(w_uq.shape), rope_spec, rope_spec],
      out_specs=[_row_spec(D_MODEL), _row_spec(pe)],
      out_shape=[jax.ShapeDtypeStruct((n, D_MODEL), F32), jax.ShapeDtypeStruct((n, pe), F32)],
      compiler_params=_cparams(("parallel",)),
      name="mla_q_proj",
  )(cq, w_uq, cos, sin)


def _mla_kv_body(c_ref, w_ref, kn_ref, v_ref):
  z = jnp.dot(c_ref[...].astype(BF16), w_ref[...], preferred_element_type=F32)
  kn_ref[...] = z[:, :D_MODEL]
  v_ref[...] = z[:, D_MODEL:]


def _mla_kv(ckv, w_ukv):
  n = ckv.shape[0]
  return pl.pallas_call(
      _mla_kv_body,
      grid=(n // TM,),
      in_specs=[_row_spec(MLA_KV_RANK), _full_spec(w_ukv.shape)],
      out_specs=[_row_spec(D_MODEL)] * 2,
      out_shape=[jax.ShapeDtypeStruct((n, D_MODEL), F32)] * 2,
      compiler_params=_cparams(("parallel",)),
      name="mla_kv_proj",
  )(ckv, w_ukv)


def _online_update(s, m, l, acc, vb):
  m_new = jnp.maximum(m, jnp.max(s, axis=-1, keepdims=True))
  alpha = jnp.exp(m - m_new)
  p = jnp.exp(s - m_new)
  l_new = alpha * l + jnp.sum(p, axis=-1, keepdims=True)
  acc_new = alpha * acc + jnp.dot(p.astype(BF16), vb, preferred_element_type=F32)
  return m_new, l_new, acc_new


def _softmax_state(tq):
  return (jnp.full((tq, 1), NEG_INF, F32), jnp.zeros((tq, 1), F32), jnp.zeros((tq, LANES), F32))


def _stacked_attention(q_pair, kb_s, vb_s):
  tq = q_pair[0].shape[0]
  qq = jnp.concatenate(q_pair, axis=0)
  tk, n_chunks, unroll = _key_chunks(kb_s.shape[0])

  def chunk(c, carry):
    off = pl.multiple_of(c * tk, tk)
    s = lax.dot_general(qq, kb_s[pl.ds(off, tk), :], _NT, preferred_element_type=F32)
    return _online_update(s, *carry, vb_s[pl.ds(off, tk), :])

  _, l, acc = lax.fori_loop(0, n_chunks, chunk, _softmax_state(2 * tq), unroll=unroll)
  o = acc / l
  return o[:tq], o[tq:]


def _pass_view(a, lat):
  w = a.shape[-1]
  if lat:
    return a.reshape(N_TOK // DEC_SEQ, DEC_SEQ, w), N_CTX // DEC_SEQ, DEC_BATCH
  return a.reshape(N_TOK // SEQ, SEQ, w), 0, BATCH


def _stage_keys(dst, lanes, own_ref, cache_ref):
  n_own = own_ref.shape[0]
  dst[:n_own, lanes] = own_ref[...].astype(BF16)
  if cache_ref is not None:
    dst[n_own:, lanes] = cache_ref[...].astype(BF16)


def _key_chunks(n_keys):
  tk = min(TK, n_keys)
  n = n_keys // tk
  return tk, n, (ATTN_UNROLL if n % ATTN_UNROLL == 0 else 1)


def _attn_specs(q, lat):
  qv, b0, nb = _pass_view(q, lat)
  sq = qv.shape[1]
  tq = min(TQ, sq)
  tiles = 1 if lat else D_MODEL // LANES
  width = tiles * LANES
  grid = (nb, D_MODEL // width, sq // tq)
  qspec = pl.BlockSpec((None, tq, width), lambda bi, h, qi: (bi + b0, qi, h))
  kspec = pl.BlockSpec((None, sq, width), lambda bi, h, qi: (bi + b0, 0, h))
  cspec = pl.BlockSpec((None, PAST_LEN, width), lambda bi, h, qi: (bi, 0, h))
  ospec = pl.BlockSpec((None, tq, width), lambda bi, h, qi: (bi, qi, h))
  n_keys = sq + (PAST_LEN if lat else 0)
  return grid, qspec, kspec, cspec, ospec, n_keys, (nb, sq, D_MODEL), tiles


def _tile(ref, tl):
  return None if ref is None else ref.at[:, tl * LANES:(tl + 1) * LANES]


def _diff_attn_body(*refs, lam_init, cached):
  if cached:
    lam_ref, g_ref, q_ref, k_ref, v_ref, kc_ref, vc_ref, o_ref, kb_s, vb_s = refs
  else:
    lam_ref, g_ref, q_ref, k_ref, v_ref, o_ref, kb_s, vb_s = refs
    kc_ref = vc_ref = None

  @pl.when(pl.program_id(2) == 0)
  def _():
    _stage_keys(kb_s, slice(None), k_ref, kc_ref)
    _stage_keys(vb_s, slice(None), v_ref, vc_ref)

  lv = lam_ref[...]
  lam = (jnp.exp(jnp.sum(lv[0:1] * lv[1:2], axis=-1, keepdims=True))
         - jnp.exp(jnp.sum(lv[2:3] * lv[3:4], axis=-1, keepdims=True)) + lam_init)
  lane = lax.broadcasted_iota(jnp.int32, (q_ref.shape[0], LANES), 1)
  for tl in range(q_ref.shape[1] // LANES):
    q = _tile(q_ref, tl)[...]
    q1 = jnp.where(lane < DA_QK, q, 0.0).astype(BF16)
    q2 = jnp.where(lane >= DA_QK, q, 0.0).astype(BF16)
    o1, o2 = _stacked_attention((q1, q2), _tile(kb_s, tl), _tile(vb_s, tl))
    _tile(o_ref, tl)[...] = _rms(o1 - lam * o2, g_ref[...]) * (1.0 - lam_init)


def _diff_attention(q, k, v, cache, lam_rows, subln_g, layer, lat):
  grid, qspec, kspec, cspec, ospec, n_keys, oshape, tiles = _attn_specs(q, lat)
  view = lambda a: _pass_view(a, lat)[0]
  lam_init = 0.8 - 0.6 * math.exp(-0.3 * layer)
  const = lambda shape: pl.BlockSpec(shape, lambda bi, h, qi: (0, 0))
  in_specs = [const((8, LANES)), const((1, LANES)), qspec, kspec, kspec]
  args = [lam_rows, subln_g, view(q), view(k), view(v)]
  if lat:
    in_specs += [cspec, cspec]
    args += [cache[0].reshape(DEC_BATCH, PAST_LEN, D_MODEL),
             cache[1].reshape(DEC_BATCH, PAST_LEN, D_MODEL)]
  return pl.pallas_call(
      functools.partial(_diff_attn_body, lam_init=lam_init, cached=lat),
      grid=grid, in_specs=in_specs, out_specs=ospec,
      out_shape=jax.ShapeDtypeStruct(oshape, F32),
      scratch_shapes=[pltpu.VMEM((n_keys, tiles * LANES), BF16)] * 2,
      compiler_params=_cparams(("parallel", "parallel", "arbitrary")),
      name="diff_attention",
  )(*args)


def _pair_attn_body(*refs, scale, with_pe, cached):
  refs = list(refs)
  q_ref = refs.pop(0)
  qp_ref = refs.pop(0) if with_pe else None
  k_ref = refs.pop(0)
  kp_ref = refs.pop(0) if with_pe else None
  v_ref = refs.pop(0)
  kc_ref = refs.pop(0) if cached else None
  kpc_ref = refs.pop(0) if (cached and with_pe) else None
  vc_ref = refs.pop(0) if cached else None
  o_ref, kb_s, vb_s = refs

  tiles = q_ref.shape[1] // LANES
  kw = kb_s.shape[1] // tiles
  key_view = lambda tl: kb_s.at[:, tl * kw:(tl + 1) * kw]

  @pl.when(pl.program_id(2) == 0)
  def _():
    for tl in range(tiles):
      _stage_keys(key_view(tl), slice(0, LANES), _tile(k_ref, tl), _tile(kc_ref, tl))
      if with_pe:
        _stage_keys(key_view(tl), slice(LANES, 2 * LANES), kp_ref, kpc_ref)
    _stage_keys(vb_s, slice(None), v_ref, vc_ref)

  lane = lax.broadcasted_iota(jnp.int32, (q_ref.shape[0], LANES), 1)
  halves = (lane < NA_HD, lane >= NA_HD)
  for tl in range(tiles):
    q = _tile(q_ref, tl)[...] * scale
    qs = [jnp.where(hm, q, 0.0).astype(BF16) for hm in halves]
    if with_pe:
      tile_id = tl if tiles > 1 else pl.program_id(1)
      qp = (_tile(qp_ref, tl // 2) if tiles > 1 else qp_ref)[...] * scale
      base = (tile_id % 2) * (2 * MLA_ROPE)
      qs = [jnp.concatenate(
          [qs[a], jnp.where((lane >= base + a * MLA_ROPE) & (lane < base + (a + 1) * MLA_ROPE),
                            qp, 0.0).astype(BF16)], axis=1) for a in range(2)]
    _tile(o_ref, tl)[...] = jnp.where(
        halves[0], *_stacked_attention(qs, key_view(tl), _tile(vb_s, tl)))


def _pair_attention(q, k, v, lat, cache=None, scale=1.0, q_pe=None, k_pe=None):
  grid, qspec, kspec, cspec, ospec, n_keys, oshape, tiles = _attn_specs(q, lat)
  view = lambda a: _pass_view(a, lat)[0]
  b0 = _pass_view(q, lat)[1]
  with_pe = q_pe is not None
  tq, sq = qspec.block_shape[1], kspec.block_shape[1]
  in_specs, args = [qspec], [view(q)]
  if with_pe:
    if tiles == 1:
      in_specs.append(pl.BlockSpec((None, tq, LANES), lambda bi, h, qi: (bi + b0, qi, h // 2)))
    else:
      in_specs.append(pl.BlockSpec((None, tq, q_pe.shape[-1]), lambda bi, h, qi: (bi + b0, qi, 0)))
    args.append(view(q_pe))
  in_specs.append(kspec)
  args.append(view(k))
  if with_pe:
    in_specs.append(pl.BlockSpec((None, sq, LANES), lambda bi, h, qi: (bi + b0, 0, 0)))
    args.append(view(k_pe))
  in_specs.append(kspec)
  args.append(view(v))
  if lat:
    in_specs.append(cspec)
    args.append(cache[0])
    if with_pe:
      in_specs.append(pl.BlockSpec((None, PAST_LEN, LANES), lambda bi, h, qi: (bi, 0, 0)))
      args.append(cache[1])
    in_specs.append(cspec)
    args.append(cache[-1])
  return pl.pallas_call(
      functools.partial(_pair_attn_body, scale=scale, with_pe=with_pe, cached=lat),
      grid=grid, in_specs=in_specs, out_specs=ospec,
      out_shape=jax.ShapeDtypeStruct(oshape, F32),
      scratch_shapes=[pltpu.VMEM((n_keys, tiles * (2 * LANES if with_pe else LANES)), BF16),
                      pltpu.VMEM((n_keys, tiles * LANES), BF16)],
      compiler_params=_cparams(("parallel", "parallel", "arbitrary")),
      name="pair_attention_pe" if with_pe else "pair_attention",
  )(*args)


def _na_lat_body(q_ref, k_ref, v_ref, kc_ref, vc_ref, bias_ref, o_ref, kb_s, vb_s, kcb_s, vcb_s):
  rb = pl.program_id(2)

  @pl.when(rb == 0)
  def _():
    kb_s[...] = k_ref[...].astype(BF16)
    vb_s[...] = v_ref[...].astype(BF16)
    kcb_s[...] = kc_ref[...].astype(BF16)
    vcb_s[...] = vc_ref[...].astype(BF16)

  n_loc = NA_KH * GRID_W
  lane = lax.broadcasted_iota(jnp.int32, (GRID_W, LANES), 1)
  halves = (lane < NA_HD, lane >= NA_HD)
  kc = kcb_s[...]
  vc = vcb_s[...]
  for rr in range(NA_RB):
    r = rb * NA_RB + rr
    start = jnp.clip(r - NA_KH // 2, 0, GRID_R - NA_KH)
    pat = jnp.where(r < NA_KH // 2, 1 + r,
                    jnp.where(r > GRID_R - NA_KH // 2, r - (GRID_R - NA_KH), 0))
    off = pl.multiple_of(start * GRID_W, GRID_W)
    kw = kb_s[pl.ds(off, n_loc), :]
    vw = vb_s[pl.ds(off, n_loc), :]
    q = q_ref[rr * GRID_W:(rr + 1) * GRID_W, :]
    qq = jnp.concatenate([jnp.where(hm, q, 0.0).astype(BF16) for hm in halves], axis=0)
    bias = bias_ref[pat].reshape(2 * GRID_W, n_loc)
    s_loc = lax.dot_general(qq, kw, _NT, preferred_element_type=F32) + bias
    s_ctx = lax.dot_general(qq, kc, _NT, preferred_element_type=F32)
    m = jnp.maximum(jnp.max(s_loc, axis=-1, keepdims=True),
                    jnp.max(s_ctx, axis=-1, keepdims=True))
    p_loc = jnp.exp(s_loc - m)
    p_ctx = jnp.exp(s_ctx - m)
    l = jnp.sum(p_loc, axis=-1, keepdims=True) + jnp.sum(p_ctx, axis=-1, keepdims=True)
    o = (jnp.dot(p_loc.astype(BF16), vw, preferred_element_type=F32)
         + jnp.dot(p_ctx.astype(BF16), vc, preferred_element_type=F32)) / l
    o_ref[rr * GRID_W:(rr + 1) * GRID_W, :] = jnp.where(halves[0], o[:GRID_W], o[GRID_W:])


def _na_lat_attention(q, k, v, kc, vc, bias):
  qv, b0, b = _pass_view(q, True)
  kv, vv = _pass_view(k, True)[0], _pass_view(v, True)[0]
  n_pat = bias.shape[0]
  blk = NA_RB * GRID_W
  qspec = pl.BlockSpec((None, blk, LANES), lambda bi, h, r: (bi + b0, r, h))
  kspec = pl.BlockSpec((None, DEC_SEQ, LANES), lambda bi, h, r: (bi + b0, 0, h))
  cspec = pl.BlockSpec((None, PAST_LEN, LANES), lambda bi, h, r: (bi, 0, h))
  bspec = pl.BlockSpec((n_pat, 2, GRID_W, NA_KH * GRID_W), lambda bi, h, r: (0, h, 0, 0))
  return pl.pallas_call(
      _na_lat_body,
      grid=(b, D_MODEL // LANES, GRID_R // NA_RB),
      in_specs=[qspec, kspec, kspec, cspec, cspec, bspec],
      out_specs=pl.BlockSpec((None, blk, LANES), lambda bi, h, r: (bi, r, h)),
      out_shape=jax.ShapeDtypeStruct((b, DEC_SEQ, D_MODEL), F32),
      scratch_shapes=[pltpu.VMEM((DEC_SEQ, LANES), BF16)] * 2
      + [pltpu.VMEM((PAST_LEN, LANES), BF16)] * 2,
      compiler_params=_cparams(("parallel", "parallel", "arbitrary")),
      name="na_lat_attention",
  )(qv, kv, vv, kc, vc, bias)


SUBLANES = 8


def _sort_network(n):
  pairs = []
  p = 1
  while p < n:
    k = p
    while k >= 1:
      for j in range(k % p, n - k, 2 * k):
        for i in range(min(k, n - j - k)):
          if (i + j) // (2 * p) == (i + j + k) // (2 * p):
            pairs.append((i + j, i + j + k))
      k //= 2
    p *= 2
  return pairs


def _pop_heads(lists, hit, depth):
  for k in range(depth):
    lists[k] = jnp.where(hit, lists[k + 1], lists[k])


def _top16_sorted(s):
  t = s.shape[1]
  n = s.shape[0] // SUBLANES
  xs = [s[SUBLANES * k:SUBLANES * (k + 1), :] for k in range(n)]
  for i, j in _sort_network(n):
    xs[i], xs[j] = jnp.maximum(xs[i], xs[j]), jnp.minimum(xs[i], xs[j])
  rows = lax.broadcasted_iota(jnp.int32, (PEER_TOPK, t), 0)
  v = jnp.zeros((PEER_TOPK, t), F32)
  for r in range(PEER_TOPK):
    m = jnp.max(xs[0], axis=0, keepdims=True)
    v = jnp.where(rows == r, m, v)
    _pop_heads(xs, xs[0] == m, PEER_TOPK - 1 - r)
  return v


def _router_body(oc_ref, ol_ref, wo_ref, x_ref, g1_ref, g_ref, sc_ref, sh_ref,
                 wq_ref, k1_ref, k2_ref,
                 xn_ref, hb_ref, a1_ref, n1_ref, b2_ref, r2_ref, h_s):
  def first_head(o_ref):
    xn = x_ref[...] + g1_ref[...] * jnp.dot(o_ref[...].astype(BF16), wo_ref[...],
                                            preferred_element_type=F32)
    xn_ref[...] = xn
    h = _rms_mod(xn, g_ref[...], sc_ref[...], sh_ref[...]).astype(BF16)
    h_s[...] = h
    hb_ref[...] = h

  is_first = pl.program_id(1) == 0
  is_ctx = pl.program_id(0) < N_CTX // PEER_T
  pl.when(is_first & is_ctx)(lambda: first_head(oc_ref))
  pl.when(is_first & jnp.logical_not(is_ctx))(lambda: first_head(ol_ref))

  qh = jnp.dot(h_s[...], wq_ref[...], preferred_element_type=F32)
  half = PEER_DK // 2
  s1 = lax.dot_general(k1_ref[...], qh[:, :half].astype(BF16), _NT, preferred_element_type=F32)
  s2 = lax.dot_general(k2_ref[...], qh[:, half:].astype(BF16), _NT, preferred_element_type=F32)
  s1 = s1 - jnp.max(s1, axis=0, keepdims=True)
  s2 = s2 - jnp.max(s2, axis=0, keepdims=True)
  v1 = _top16_sorted(s1)
  v2 = _top16_sorted(s2)
  cand = [v1[:SUBLANES] + v2[k:k + 1] for k in range(PEER_TOPK)]
  tail = v1[SUBLANES:] + v2[0:1]
  z = jnp.zeros_like(v2[0:1])
  taken = jnp.zeros_like(cand[0])
  for r in range(PEER_TOPK):
    th = jnp.max(jnp.maximum(cand[0], tail), axis=0, keepdims=True)
    z = z + jnp.exp(th)
    tail = jnp.where(tail == th, NEG_INF, tail)
    hit = cand[0] == th
    taken = taken + jnp.where(hit, 1.0, 0.0)
    _pop_heads(cand, hit, PEER_TOPK - 1 - r)
  inf = jnp.float32(jnp.inf)
  first = jnp.minimum(jnp.min(jnp.where(taken >= 1.0, v1[:SUBLANES], inf), axis=0, keepdims=True),
                      jnp.min(jnp.where(tail == NEG_INF, v1[SUBLANES:], inf), axis=0,
                              keepdims=True))
  n1 = jnp.where(s1 >= first, 1.0, 0.0)
  k_max = PEER_TOPK // 2
  for k in range(2, k_max + 1):
    t_k = jnp.min(jnp.where(taken >= float(k), v1[:SUBLANES], inf), axis=0, keepdims=True)
    n1 = n1 + jnp.where(s1 >= t_k, 1.0, 0.0)
  n1 = n1 + jnp.where(s1 >= v1[0:1], jnp.maximum(taken[0:1] - float(k_max), 0.0), 0.0)
  r2 = jnp.zeros_like(s2)
  for r in range(PEER_TOPK):
    r2 = r2 + jnp.where(v2[r:r + 1] > s2, 1.0, 0.0)
  a1_ref[...] = jnp.exp(s1)
  n1_ref[...] = n1
  b2_ref[...] = pltpu.bitcast((jnp.exp(s2) * (1.0 / z)).astype(BF16), jnp.uint32)
  r2_ref[...] = pltpu.bitcast(r2.astype(BF16), jnp.uint32)


def _peer_router(oc, ol, w_o, x, gate1, g, sc, sh, wq, k1, k2):
  n = x.shape[0]
  tm = PEER_T
  nb_ctx = N_CTX // tm
  row = lambda w: pl.BlockSpec((tm, w), lambda i, h: (i, 0))
  full = lambda shape: pl.BlockSpec(shape, lambda i, h: (0,) * len(shape))
  modspec = pl.BlockSpec((None, 1, D_MODEL), lambda i, h: (_mod_row(i, tm), 0, 0))
  keyspec = pl.BlockSpec((None, PEER_NKEYS, PEER_DK // 2), lambda i, h: (h, 0, 0))
  tspec = pl.BlockSpec((None, PEER_NKEYS, tm), lambda i, h: (h, 0, i))
  pspec = pl.BlockSpec((None, PEER_NKEYS // 2, tm), lambda i, h: (h, 0, i))
  return pl.pallas_call(
      _router_body,
      grid=(n // tm, PEER_HEADS),
      in_specs=[pl.BlockSpec((tm, D_MODEL), lambda i, h: (jnp.minimum(i, nb_ctx - 1), 0)),
                pl.BlockSpec((tm, D_MODEL), lambda i, h: (jnp.maximum(i - nb_ctx, 0), 0)),
                full(w_o.shape), row(D_MODEL), modspec,
                full((1, D_MODEL)), modspec, modspec,
                pl.BlockSpec((D_MODEL, PEER_DK), lambda i, h: (0, h)), keyspec, keyspec],
      out_specs=[row(D_MODEL), row(D_MODEL), tspec, tspec, pspec, pspec],
      out_shape=[jax.ShapeDtypeStruct((n, D_MODEL), F32), jax.ShapeDtypeStruct((n, D_MODEL), BF16)]
      + [jax.ShapeDtypeStruct((PEER_HEADS, PEER_NKEYS, n), F32)] * 2
      + [jax.ShapeDtypeStruct((PEER_HEADS, PEER_NKEYS // 2, n), jnp.uint32)] * 2,
      scratch_shapes=[pltpu.VMEM((tm, D_MODEL), BF16)],
      compiler_params=_cparams(("parallel", "arbitrary")),
      name="peer_router",
  )(oc.reshape(N_CTX, D_MODEL), ol.reshape(N_LAT, D_MODEL), w_o, x, gate1, g, sc, sh, wq, k1, k2)


def _gelu(x):
  return 0.5 * x * (1.0 + lax.erf(x * np.float32(math.sqrt(0.5))))


def _peer_mix_body(hb_ref, u_ref, vt_ref, a1_ref, n1_ref, b2_ref, r2_ref, x_ref, gate_ref,
                   fg_ref, y_ref, acc_s, *piece_s, final_norm):
  c = pl.program_id(1)
  n_pieces = PEER_EC // PEER_SUBC

  @pl.when(c == 0)
  def _():
    acc_s[...] = jnp.zeros_like(acc_s)

  hb = hb_ref[...]
  t = hb.shape[0]
  pk = 16
  for j in range(n_pieces):
    rows = slice(j * PEER_SUBC, (j + 1) * PEER_SUBC)
    piece_s[j][...] = lax.dot_general(u_ref[rows, :], hb, _NT, preferred_element_type=F32)
  for j in range(n_pieces):
    s_s, p_s = piece_s[j], piece_s[n_pieces + j]
    for ii in range(PEER_SUBC // PEER_NKEYS):
      i1 = j * (PEER_SUBC // PEER_NKEYS) + ii
      w = [jnp.zeros((pk, t), BF16) for _ in range(PEER_NKEYS // pk)]
      for h in range(PEER_HEADS):
        a_row = jnp.broadcast_to(a1_ref[h, i1:i1 + 1, :], (pk, t)).astype(BF16)
        n_row = jnp.broadcast_to(n1_ref[h, i1:i1 + 1, :], (pk, t)).astype(BF16)
        for sub in range(PEER_NKEYS // pk):
          words = slice(sub * pk // 2, (sub + 1) * pk // 2)
          prod = a_row * pltpu.bitcast(b2_ref[h, words, :], BF16)
          rank = pltpu.bitcast(r2_ref[h, words, :], BF16)
          w[sub] = w[sub] + jnp.where(rank < n_row, prod, jnp.zeros_like(prod))
      for sub in range(PEER_NKEYS // pk):
        row0 = ii * PEER_NKEYS + sub * pk
        p_s[row0:row0 + pk, :] = w[sub] * _gelu(s_s[row0:row0 + pk, :]).astype(BF16)
    rows = slice(j * PEER_SUBC, (j + 1) * PEER_SUBC)
    acc_s[...] += jnp.dot(vt_ref[:, rows], p_s[...], preferred_element_type=F32)

  @pl.when(c == pl.num_programs(1) - 1)
  def _():
    y = x_ref[...] + gate_ref[...] * acc_s[...].T
    y_ref[...] = _rms(y, fg_ref[...]) if final_norm else y


def _peer_mix(hb, u, vt, a1, n1, b2, r2, x, gate, final_g, final_norm):
  n = x.shape[0]
  t = PEER_T
  n_i1 = PEER_EC // PEER_NKEYS
  row = lambda w: pl.BlockSpec((t, w), lambda i, c: (i, 0))
  i1spec = pl.BlockSpec((PEER_HEADS, n_i1, t), lambda i, c: (0, c, i))
  i2spec = pl.BlockSpec((PEER_HEADS, PEER_NKEYS // 2, t), lambda i, c: (0, 0, i))
  return pl.pallas_call(
      functools.partial(_peer_mix_body, final_norm=final_norm),
      grid=(n // t, PEER_EXPERTS // PEER_EC),
      in_specs=[row(D_MODEL),
                pl.BlockSpec((PEER_EC, D_MODEL), lambda i, c: (c, 0)),
                pl.BlockSpec((None, D_MODEL, PEER_EC), lambda i, c: (c, 0, 0)),
                i1spec, i1spec, i2spec, i2spec,
                row(D_MODEL),
                pl.BlockSpec((None, 1, D_MODEL), lambda i, c: (_mod_row(i, t), 0, 0)),
                pl.BlockSpec((1, D_MODEL), lambda i, c: (0, 0))],
      out_specs=row(D_MODEL),
      out_shape=jax.ShapeDtypeStruct((n, D_MODEL), F32),
      scratch_shapes=[pltpu.VMEM((D_MODEL, t), F32)]
      + [pltpu.VMEM((PEER_SUBC, t), F32)] * (PEER_EC // PEER_SUBC)
      + [pltpu.VMEM((PEER_SUBC, t), BF16)] * (PEER_EC // PEER_SUBC),
      compiler_params=_cparams(("parallel", "arbitrary"), VMEM_LIMIT_PEER),
      name="peer_mix",
  )(hb, u, vt, a1, n1, b2, r2, x, gate, final_g)


def _rotated_tiles(w, group):
  k, n = w.shape
  g = w.reshape(k, n // group, 2, group // 2)
  return jnp.concatenate([-g[:, :, 1:], g[:, :, :1]], axis=2).reshape(k, n)


def _rope_table(rot_dims, reps):
  half = rot_dims // 2
  t = jnp.arange(DEC_SEQ)
  inv = ROPE_BASE ** (-jnp.arange(half, dtype=F32) / half)
  parts_c, parts_s = [], []
  for pos in (t // GRID_W, t % GRID_W):
    ang = pos.astype(F32)[:, None] * inv[None, :]
    parts_c += [jnp.cos(ang), jnp.cos(ang)]
    parts_s += [jnp.sin(ang), jnp.sin(ang)]
  cos = jnp.tile(jnp.concatenate(parts_c, axis=1), (1, reps))
  sin = jnp.tile(jnp.concatenate(parts_s, axis=1), (1, reps))
  cos = jnp.concatenate([cos, jnp.ones((TM, LANES), F32)], axis=0)
  sin = jnp.concatenate([sin, jnp.zeros((TM, LANES), F32)], axis=0)
  return cos, sin


def _na_bias_table(rpb):
  reps = [NA_KH // 2] + list(range(NA_KH // 2)) + list(range(GRID_R - NA_KH // 2 + 1, GRID_R))
  cols = np.arange(GRID_W)
  col_start = np.clip(cols - NA_KW // 2, 0, GRID_W - NA_KW)
  col_mask = (cols[None, :] >= col_start[:, None]) & (cols[None, :] < col_start[:, None] + NA_KW)
  dc = np.clip(cols[None, :] - cols[:, None], -(NA_KW - 1), NA_KW - 1) + (NA_KW - 1)
  dr = np.stack([np.clip(r - NA_KH // 2, 0, GRID_R - NA_KH) + np.arange(NA_KH) - r + (NA_KH - 1)
                 for r in reps])
  onehot = (dc[:, :, None] == np.arange(2 * NA_KW - 1)).astype(np.float32)
  bias = jnp.einsum('hpjc,qwc->phqjw', rpb[:, dr, :], onehot, precision=lax.Precision.HIGHEST)
  bias = jnp.where(col_mask[None, None, :, None, :], bias, NEG_INF)
  return bias.reshape(len(reps), NA_HEADS, GRID_W, NA_KH * GRID_W)


def _ctx_rows(a, *shape):
  return a[:N_CTX].reshape(BATCH, SEQ, *shape)


def _diff_mixer(layer, p, x, sc, sh, cache, tables):
  w = p['w_qkv']
  d = D_MODEL
  wq, wk, wv = w[:, :d], w[:, d:2 * d], w[:, 2 * d:]
  rot = lambda m: _rotated_tiles(m, DA_QK // 2)
  w5 = jnp.concatenate([wq, rot(wq), wk, rot(wk), wv], axis=1).astype(BF16)
  cos, sin = tables['da']
  q, k, v = _qkv_rope(x, p['norm1_g'], sc, sh, w5, cos, sin, DA_QK ** -0.5)
  lam_rows = jnp.zeros((8, LANES), F32)
  for i, nme in enumerate(('lam_q1', 'lam_k1', 'lam_q2', 'lam_k2')):
    lam_rows = lam_rows.at[i, :DA_QK].set(p[nme])
  g = p['subln_g'].reshape(1, DA_V)
  oc = _diff_attention(q, k, v, None, lam_rows, g, layer, lat=False)
  ol = _diff_attention(q, k, v, cache, lam_rows, g, layer, lat=True)
  state = (_ctx_rows(k, DA_HEADS, 2 * DA_QK), _ctx_rows(v, DA_HEADS, DA_V))
  return oc, ol, state


def _na_mixer(p, x, sc, sh, cache):
  d = D_MODEL
  q, k, v = _qkv_plain(x, p['norm1_g'], sc, sh, p['w_qkv'].astype(BF16), NA_HD ** -0.5)
  oc = _pair_attention(q, k, v, lat=False)
  bias = _na_bias_table(p['rpb'])
  ol = _na_lat_attention(q, k, v, cache[0].reshape(DEC_BATCH, PAST_LEN, d),
                         cache[1].reshape(DEC_BATCH, PAST_LEN, d), bias)
  state = (_ctx_rows(k, NA_HEADS, NA_HD), _ctx_rows(v, NA_HEADS, NA_HD))
  return oc, ol, state


def _mla_mixer(p, x, sc, sh, cache, tables):
  a, b = MLA_Q_RANK, MLA_Q_RANK + MLA_KV_RANK
  w_in = p['w_in']
  kpe_w = w_in[:, b:]
  w_in_x = jnp.concatenate([w_in[:, :b], jnp.tile(kpe_w, (1, 4)),
                            jnp.tile(_rotated_tiles(kpe_w, MLA_ROPE // 2), (1, 4))],
                           axis=1).astype(BF16)
  cos, sin = tables['mla']
  cq, ckv, kpe = _mla_in(x, p['norm1_g'], sc, sh, w_in_x, p['q_norm_g'].reshape(1, a),
                         p['kv_norm_g'].reshape(1, MLA_KV_RANK), cos, sin)
  w_uq = p['w_uq'].reshape(a, MLA_HEADS, MLA_NOPE + MLA_ROPE)
  w_qn = w_uq[:, :, :MLA_NOPE].reshape(a, -1)
  w_qp = w_uq[:, :, MLA_NOPE:].reshape(a, -1)
  w_uq_x = jnp.concatenate([w_qn, w_qp, _rotated_tiles(w_qp, MLA_ROPE // 2)], axis=1).astype(BF16)
  qn, qp = _mla_q(cq, w_uq_x, cos, sin)
  w_ukv = p['w_ukv'].reshape(MLA_KV_RANK, MLA_HEADS, MLA_NOPE + MLA_V)
  w_ukv_x = jnp.concatenate([w_ukv[:, :, :MLA_NOPE].reshape(MLA_KV_RANK, -1),
                             w_ukv[:, :, MLA_NOPE:].reshape(MLA_KV_RANK, -1)], axis=1).astype(BF16)
  kn, v = _mla_kv(ckv, w_ukv_x)
  knc, vc = _mla_kv(cache[0].reshape(DEC_BATCH * PAST_LEN, MLA_KV_RANK), w_ukv_x)
  cached = (knc.reshape(DEC_BATCH, PAST_LEN, D_MODEL), jnp.tile(cache[1], (1, 1, LANES // MLA_ROPE)),
            vc.reshape(DEC_BATCH, PAST_LEN, D_MODEL))
  scale = (MLA_NOPE + MLA_ROPE) ** -0.5
  oc = _pair_attention(qn, kn, v, lat=False, scale=scale, q_pe=qp, k_pe=kpe)
  ol = _pair_attention(qn, kn, v, lat=True, cache=cached, scale=scale, q_pe=qp, k_pe=kpe)
  state = (_ctx_rows(ckv, MLA_KV_RANK), _ctx_rows(kpe[:, :MLA_ROPE], MLA_ROPE))
  return oc, ol, state


def _layer(layer, p, x, cond8, cache, tables, final_g):
  sh1, sc1, g1, sh2, sc2, g2 = _modulation(cond8, p['ada_w'], p['ada_b'])
  kind = layer % 3
  if kind == 0:
    oc, ol, state = _diff_mixer(layer, p, x, sc1, sh1, cache, tables)
  elif kind == 1:
    oc, ol, state = _na_mixer(p, x, sc1, sh1, cache)
  else:
    oc, ol, state = _mla_mixer(p, x, sc1, sh1, cache, tables)
  x, hb, a1, n1, b2, r2 = _peer_router(oc, ol, p['w_o'].astype(BF16), x, g1, p['norm2_g'],
                                       sc2, sh2, p['peer_wq'], p['peer_k1'], p['peer_k2'])
  x = _peer_mix(hb, p['peer_u'], p['peer_vt'], a1, n1, b2, r2, x, g2, final_g,
                final_norm=layer == DEPTH - 1)
  return x, state


def kernel(x_prompt, x_sample, cache_l0_k, cache_l0_v, cache_l1_k, cache_l1_v, cache_l2_ckv, cache_l2_kpe, cache_l3_k, cache_l3_v, c, c_ctx, l0_norm1_g, l0_norm2_g, l0_ada_w, l0_ada_b, l0_w_qkv, l0_w_o, l0_lam_q1, l0_lam_k1, l0_lam_q2, l0_lam_k2, l0_subln_g, l0_peer_wq, l0_peer_k1, l0_peer_k2, l0_peer_u, l0_peer_v, l1_norm1_g, l1_norm2_g, l1_ada_w, l1_ada_b, l1_w_qkv, l1_w_o, l1_rpb, l1_peer_wq, l1_peer_k1, l1_peer_k2, l1_peer_u, l1_peer_v, l2_norm1_g, l2_norm2_g, l2_ada_w, l2_ada_b, l2_w_in, l2_q_norm_g, l2_w_uq, l2_kv_norm_g, l2_w_ukv, l2_w_o, l2_peer_wq, l2_peer_k1, l2_peer_k2, l2_peer_u, l2_peer_v, l3_norm1_g, l3_norm2_g, l3_ada_w, l3_ada_b, l3_w_qkv, l3_w_o, l3_lam_q1, l3_lam_k1, l3_lam_q2, l3_lam_k2, l3_subln_g, l3_peer_wq, l3_peer_k1, l3_peer_k2, l3_peer_u, l3_peer_v, final_norm_g):
  common = lambda n1, n2, aw, ab, wq, k1, k2, u, v: dict(
      norm1_g=n1.reshape(1, D_MODEL), norm2_g=n2.reshape(1, D_MODEL), ada_w=aw, ada_b=ab,
      peer_wq=wq.astype(BF16), peer_k1=k1.astype(BF16), peer_k2=k2.astype(BF16),
      peer_u=u.astype(BF16),
      peer_vt=v.reshape(PEER_EXPERTS // PEER_EC, PEER_EC, D_MODEL).transpose(0, 2, 1).astype(BF16))
  p0 = dict(common(l0_norm1_g, l0_norm2_g, l0_ada_w, l0_ada_b, l0_peer_wq, l0_peer_k1, l0_peer_k2,
                   l0_peer_u, l0_peer_v),
            w_qkv=l0_w_qkv, w_o=l0_w_o, lam_q1=l0_lam_q1, lam_k1=l0_lam_k1, lam_q2=l0_lam_q2,
            lam_k2=l0_lam_k2, subln_g=l0_subln_g)
  p1 = dict(common(l1_norm1_g, l1_norm2_g, l1_ada_w, l1_ada_b, l1_peer_wq, l1_peer_k1, l1_peer_k2,
                   l1_peer_u, l1_peer_v),
            w_qkv=l1_w_qkv, w_o=l1_w_o, rpb=l1_rpb)
  p2 = dict(common(l2_norm1_g, l2_norm2_g, l2_ada_w, l2_ada_b, l2_peer_wq, l2_peer_k1, l2_peer_k2,
                   l2_peer_u, l2_peer_v),
            w_in=l2_w_in, q_norm_g=l2_q_norm_g, w_uq=l2_w_uq, kv_norm_g=l2_kv_norm_g,
            w_ukv=l2_w_ukv, w_o=l2_w_o)
  p3 = dict(common(l3_norm1_g, l3_norm2_g, l3_ada_w, l3_ada_b, l3_peer_wq, l3_peer_k1, l3_peer_k2,
                   l3_peer_u, l3_peer_v),
            w_qkv=l3_w_qkv, w_o=l3_w_o, lam_q1=l3_lam_q1, lam_k1=l3_lam_k1, lam_q2=l3_lam_q2,
            lam_k2=l3_lam_k2, subln_g=l3_subln_g)
  params = (p0, p1, p2, p3)
  caches = ((cache_l0_k, cache_l0_v), (cache_l1_k, cache_l1_v),
            (cache_l2_ckv, cache_l2_kpe), (cache_l3_k, cache_l3_v))
  tables = dict(da=_rope_table(DA_QK // 2, LANES // DA_QK),
                mla=_rope_table(MLA_ROPE // 2, LANES // MLA_ROPE))

  cond8 = jnp.zeros((8, D_MODEL), F32).at[0].set(c_ctx).at[1:1 + DEC_BATCH].set(c)
  x = jnp.concatenate([x_prompt.reshape(N_CTX, D_MODEL), x_sample.reshape(N_LAT, D_MODEL)], axis=0)
  states = []
  for layer in range(DEPTH):
    x, st = _layer(layer, params[layer], x, cond8, caches[layer], tables,
                   final_norm_g.reshape(1, D_MODEL))
    states.extend(st)
  y = x
  return (y[:N_CTX].reshape(BATCH, SEQ, D_MODEL), y[N_CTX:].reshape(DEC_BATCH, DEC_SEQ, D_MODEL),
          *states)
```

```python
import functools
import math

import numpy as np
import jax
import jax.numpy as jnp
from jax import lax
from jax.experimental import pallas as pl
from jax.experimental.pallas import tpu as pltpu

F32 = jnp.float32
BF16 = jnp.bfloat16

D_MODEL = 1024
BATCH = 32
SEQ = 256
DEPTH = 4
DEC_BATCH = 2
DEC_SEQ = 4096
PAST_LEN = 512
GRID_W = 64
GRID_R = DEC_SEQ // GRID_W
EPS = 1e-6
ROPE_BASE = 10000.0
NEG_INF = -1e30

DA_HEADS = 8
DA_QK = 64
DA_V = 128
NA_HEADS = 16
NA_HD = 64
NA_KH = 8
NA_KW = 16
MLA_HEADS = 16
MLA_NOPE = 64
MLA_ROPE = 32
MLA_V = 64
MLA_Q_RANK = 384
MLA_KV_RANK = 256
PEER_HEADS = 8
PEER_NKEYS = 128
PEER_EXPERTS = PEER_NKEYS * PEER_NKEYS
PEER_DK = 256
PEER_TOPK = 16

N_CTX = BATCH * SEQ
N_LAT = DEC_BATCH * DEC_SEQ
N_TOK = N_CTX + N_LAT
LAT_KEYS = DEC_SEQ + PAST_LEN

LANES = 128
VMEM_LIMIT = 48 << 20
VMEM_LIMIT_PEER = 56 << 20

TM = 256
TQ = 512
TK = 1536
ATTN_UNROLL = 3
NA_RB = 8
PEER_T = 512
PEER_EC = 2048
PEER_SUBC = 256

_NT = (((1,), (1,)), ((), ()))


def _cparams(sem, vmem=VMEM_LIMIT):
  return pltpu.CompilerParams(dimension_semantics=sem, vmem_limit_bytes=vmem)


def _mod_row(i, tm):
  nb_ctx = N_CTX // tm
  nb_bat = DEC_SEQ // tm
  return jnp.where(i < nb_ctx, 0, 1 + (i - nb_ctx) // nb_bat)


def _rope_blk(i, tm):
  nb_ctx = N_CTX // tm
  nb_bat = DEC_SEQ // tm
  return jnp.where(i < nb_ctx, nb_bat, (i - nb_ctx) % nb_bat)


def _rms_mod(x, g, sc, sh):
  y = x * lax.rsqrt(jnp.mean(x * x, axis=-1, keepdims=True) + EPS)
  return (y * g) * (1.0 + sc) + sh


def _rms(x, g):
  return x * lax.rsqrt(jnp.mean(x * x, axis=-1, keepdims=True) + EPS) * g


def _row_spec(width, tm=TM):
  return pl.BlockSpec((tm, width), lambda i: (i, 0))


def _full_spec(shape):
  return pl.BlockSpec(shape, lambda i: (0,) * len(shape))


def _mod_spec(tm=TM):
  return pl.BlockSpec((None, 1, D_MODEL), lambda i: (_mod_row(i, tm), 0, 0))


def _pass_specs(width, tm, index=lambda i, *_: i):
  nb = N_CTX // tm
  return [pl.BlockSpec((tm, width), lambda *g: (jnp.minimum(index(*g), nb - 1), 0)),
          pl.BlockSpec((tm, width), lambda *g: (jnp.maximum(index(*g) - nb, 0), 0))]


def _pass_shapes(width, dtype=F32):
  return [jax.ShapeDtypeStruct((N_CTX, width), dtype), jax.ShapeDtypeStruct((N_LAT, width), dtype)]


def _store_by_pass(i, tm, val, ctx_ref, lat_ref):
  is_ctx = i < N_CTX // tm

  @pl.when(is_ctx)
  def _():
    ctx_ref[...] = val

  @pl.when(jnp.logical_not(is_ctx))
  def _():
    lat_ref[...] = val


def _mod_body(c_ref, w_ref, b_ref, o_ref):
  c = c_ref[...]
  s = c / (1.0 + jnp.exp(-c))
  o_ref[...] = jnp.dot(s, w_ref[...], precision=lax.Precision.HIGHEST,
                       preferred_element_type=F32) + b_ref[...]


def _modulation(cond8, ada_w, ada_b):
  n = ada_w.shape[1]
  tn = 1536
  out = pl.pallas_call(
      _mod_body,
      grid=(n // tn,),
      in_specs=[pl.BlockSpec((8, D_MODEL), lambda j: (0, 0)),
                pl.BlockSpec((D_MODEL, tn), lambda j: (0, j)),
                pl.BlockSpec((1, tn), lambda j: (0, j))],
      out_specs=pl.BlockSpec((8, tn), lambda j: (0, j)),
      out_shape=jax.ShapeDtypeStruct((8, n), F32),
      compiler_params=_cparams(("parallel",)),
      name="modulation",
  )(cond8, ada_w, ada_b.reshape(1, n))
  return [out[:, k * D_MODEL:(k + 1) * D_MODEL].reshape(8, 1, D_MODEL) for k in range(6)]


def _tile_lanes(t, reps):
  return jnp.concatenate([t] * reps, axis=1)


def _qkv_rope_body(x_ref, g_ref, sc_ref, sh_ref, w_ref, cos_ref, sin_ref,
                   q_ref, kc_ref, kl_ref, vc_ref, vl_ref, *, q_scale):
  h = _rms_mod(x_ref[...], g_ref[...], sc_ref[...], sh_ref[...]).astype(BF16)
  reps = D_MODEL // LANES
  cos = _tile_lanes(cos_ref[...], reps)
  sin = _tile_lanes(sin_ref[...], reps)
  d = D_MODEL
  dot = lambda a, b: jnp.dot(h, w_ref[:, a:b], preferred_element_type=F32)
  q_ref[...] = (dot(0, d) * cos + dot(d, 2 * d) * sin) * q_scale
  i = pl.program_id(0)
  _store_by_pass(i, TM, dot(2 * d, 3 * d) * cos + dot(3 * d, 4 * d) * sin, kc_ref, kl_ref)
  _store_by_pass(i, TM, dot(4 * d, 5 * d), vc_ref, vl_ref)


def _qkv_rope(x, g, sc, sh, w5, cos, sin, q_scale):
  n = x.shape[0]
  rope_spec = pl.BlockSpec((TM, LANES), lambda i: (_rope_blk(i, TM), 0))
  q, kc, kl, vc, vl = pl.pallas_call(
      functools.partial(_qkv_rope_body, q_scale=q_scale),
      grid=(n // TM,),
      in_specs=[_row_spec(D_MODEL), _full_spec((1, D_MODEL)), _mod_spec(), _mod_spec(),
                _full_spec(w5.shape), rope_spec, rope_spec],
      out_specs=[_row_spec(D_MODEL)] + _pass_specs(D_MODEL, TM) * 2,
      out_shape=[jax.ShapeDtypeStruct((n, D_MODEL), F32)] + _pass_shapes(D_MODEL) * 2,
      compiler_params=_cparams(("arbitrary",)),
      name="qkv_rope_proj",
  )(x, g, sc, sh, w5, cos, sin)
  return q, (kc, kl), (vc, vl)


def _qkv_plain_body(x_ref, g_ref, sc_ref, sh_ref, w_ref, q_ref, kc_ref, kl_ref, vc_ref, vl_ref,
                    *, q_scale):
  h = _rms_mod(x_ref[...], g_ref[...], sc_ref[...], sh_ref[...]).astype(BF16)
  d = D_MODEL
  dot = lambda a, b: jnp.dot(h, w_ref[:, a:b], preferred_element_type=F32)
  q_ref[...] = dot(0, d) * q_scale
  i = pl.program_id(0)
  _store_by_pass(i, TM, dot(d, 2 * d), kc_ref, kl_ref)
  _store_by_pass(i, TM, dot(2 * d, 3 * d), vc_ref, vl_ref)


def _qkv_plain(x, g, sc, sh, w3, q_scale):
  n = x.shape[0]
  q, kc, kl, vc, vl = pl.pallas_call(
      functools.partial(_qkv_plain_body, q_scale=q_scale),
      grid=(n // TM,),
      in_specs=[_row_spec(D_MODEL), _full_spec((1, D_MODEL)), _mod_spec(), _mod_spec(),
                _full_spec(w3.shape)],
      out_specs=[_row_spec(D_MODEL)] + _pass_specs(D_MODEL, TM) * 2,
      out_shape=[jax.ShapeDtypeStruct((n, D_MODEL), F32)] + _pass_shapes(D_MODEL) * 2,
      compiler_params=_cparams(("arbitrary",)),
      name="qkv_proj",
  )(x, g, sc, sh, w3)
  return q, (kc, kl), (vc, vl)


def _mla_in_body(x_ref, g_ref, sc_ref, sh_ref, w_ref, qg_ref, kvg_ref, cos_ref, sin_ref,
                 cq_ref, ckv_ref, kpe_ref):
  h = _rms_mod(x_ref[...], g_ref[...], sc_ref[...], sh_ref[...]).astype(BF16)
  z = jnp.dot(h, w_ref[...], preferred_element_type=F32)
  a, b = MLA_Q_RANK, MLA_Q_RANK + MLA_KV_RANK
  cq_ref[...] = _rms(z[:, :a], qg_ref[...]).astype(BF16)
  ckv_ref[...] = _rms(z[:, a:b], kvg_ref[...])
  kpe_ref[...] = z[:, b:b + LANES] * cos_ref[...] + z[:, b + LANES:] * sin_ref[...]


def _mla_in(x, g, sc, sh, w_in, qg, kvg, cos, sin):
  n = x.shape[0]
  rope_spec = pl.BlockSpec((TM, LANES), lambda i: (_rope_blk(i, TM), 0))
  return pl.pallas_call(
      _mla_in_body,
      grid=(n // TM,),
      in_specs=[_row_spec(D_MODEL), _full_spec((1, D_MODEL)), _mod_spec(), _mod_spec(),
                _full_spec(w_in.shape), _full_spec((1, MLA_Q_RANK)),
                _full_spec((1, MLA_KV_RANK)), rope_spec, rope_spec],
      out_specs=[_row_spec(MLA_Q_RANK), _row_spec(MLA_KV_RANK), _row_spec(LANES)],
      out_shape=[jax.ShapeDtypeStruct((n, MLA_Q_RANK), BF16),
                 jax.ShapeDtypeStruct((n, MLA_KV_RANK), F32),
                 jax.ShapeDtypeStruct((n, LANES), F32)],
      compiler_params=_cparams(("parallel",)),
      name="mla_in_proj",
  )(x, g, sc, sh, w_in, qg, kvg, cos, sin)


def _mla_q_body(cq_ref, w_ref, cos_ref, sin_ref, qn_ref, qp_ref):
  z = jnp.dot(cq_ref[...], w_ref[...], preferred_element_type=F32)
  pe = MLA_HEADS * MLA_ROPE
  reps = pe // LANES
  cos = _tile_lanes(cos_ref[...], reps)
  sin = _tile_lanes(sin_ref[...], reps)
  qn_ref[...] = z[:, :D_MODEL]
  qp_ref[...] = z[:, D_MODEL:D_MODEL + pe] * cos + z[:, D_MODEL + pe:] * sin


def _mla_q(cq, w_uq, cos, sin):
  n = cq.shape[0]
  pe = MLA_HEADS * MLA_ROPE
  rope_spec = pl.BlockSpec((TM, LANES), lambda i: (_rope_blk(i, TM), 0))
  return pl.pallas_call(
      _mla_q_body,
      grid=(n // TM,),
      in_specs=[_row_spec(MLA_Q_RANK), _full_spec(w_uq.shape), rope_spec, rope_spec],
      out_specs=[_row_spec(D_MODEL), _row_spec(pe)],
      out_shape=[jax.ShapeDtypeStruct((n, D_MODEL), F32), jax.ShapeDtypeStruct((n, pe), F32)],
      compiler_params=_cparams(("parallel",)),
      name="mla_q_proj",
  )(cq, w_uq, cos, sin)


def _mla_kv_body(c_ref, w_ref, kn_ref, v_ref):
  z = jnp.dot(c_ref[...].astype(BF16), w_ref[...], preferred_element_type=F32)
  kn_ref[...] = z[:, :D_MODEL]
  v_ref[...] = z[:, D_MODEL:]


def _mla_kv(ckv, w_ukv):
  n = ckv.shape[0]
  return pl.pallas_call(
      _mla_kv_body,
      grid=(n // TM,),
      in_specs=[_row_spec(MLA_KV_RANK), _full_spec(w_ukv.shape)],
      out_specs=[_row_spec(D_MODEL)] * 2,
      out_shape=[jax.ShapeDtypeStruct((n, D_MODEL), F32)] * 2,
      compiler_params=_cparams(("parallel",)),
      name="mla_kv_proj",
  )(ckv, w_ukv)


def _online_update(s, m, l, acc, vb):
  m_new = jnp.maximum(m, jnp.max(s, axis=-1, keepdims=True))
  alpha = jnp.exp(m - m_new)
  p = jnp.exp(s - m_new)
  l_new = alpha * l + jnp.sum(p, axis=-1, keepdims=True)
  acc_new = alpha * acc + jnp.dot(p.astype(BF16), vb, preferred_element_type=F32)
  return m_new, l_new, acc_new


def _softmax_state(tq):
  return (jnp.full((tq, 1), NEG_INF, F32), jnp.zeros((tq, 1), F32), jnp.zeros((tq, LANES), F32))


def _stacked_attention(q_pair, kb_s, vb_s):
  tq = q_pair[0].shape[0]
  qq = jnp.concatenate(q_pair, axis=0)
  tk, n_chunks, unroll = _key_chunks(kb_s.shape[0])

  def chunk(c, carry):
    off = pl.multiple_of(c * tk, tk)
    s = lax.dot_general(qq, kb_s[pl.ds(off, tk), :], _NT, preferred_element_type=F32)
    return _online_update(s, *carry, vb_s[pl.ds(off, tk), :])

  _, l, acc = lax.fori_loop(0, n_chunks, chunk, _softmax_state(2 * tq), unroll=unroll)
  o = acc / l
  return o[:tq], o[tq:]


def _pass_view(a, lat):
  w = a.shape[-1]
  if lat:
    return a.reshape(N_TOK // DEC_SEQ, DEC_SEQ, w), N_CTX // DEC_SEQ, DEC_BATCH
  return a.reshape(N_TOK // SEQ, SEQ, w), 0, BATCH


def _stage_keys(dst, lanes, own_ref, cache_ref):
  n_own = own_ref.shape[0]
  dst[:n_own, lanes] = own_ref[...].astype(BF16)
  if cache_ref is not None:
    dst[n_own:, lanes] = cache_ref[...].astype(BF16)


def _key_chunks(n_keys):
  tk = min(TK, n_keys)
  n = n_keys // tk
  return tk, n, (ATTN_UNROLL if n % ATTN_UNROLL == 0 else 1)


def _attn_specs(q, lat, kv_joint=True):
  qv, b0, nb = _pass_view(q, lat)
  kb0 = b0 if kv_joint else 0
  sq = qv.shape[1]
  tq = min(TQ, sq)
  tiles = 1 if lat else D_MODEL // LANES
  width = tiles * LANES
  grid = (nb, D_MODEL // width, sq // tq)
  qspec = pl.BlockSpec((None, tq, width), lambda bi, h, qi: (bi + b0, qi, h))
  kspec = pl.BlockSpec((None, sq, width), lambda bi, h, qi: (bi + kb0, 0, h))
  cspec = pl.BlockSpec((None, PAST_LEN, width), lambda bi, h, qi: (bi, 0, h))
  ospec = pl.BlockSpec((None, tq, width), lambda bi, h, qi: (bi, qi, h))
  n_keys = sq + (PAST_LEN if lat else 0)
  return grid, qspec, kspec, cspec, ospec, n_keys, (nb, sq, D_MODEL), tiles


def _tile(ref, tl):
  return None if ref is None else ref.at[:, tl * LANES:(tl + 1) * LANES]


def _diff_attn_body(*refs, lam_init, cached):
  if cached:
    lam_ref, g_ref, q_ref, k_ref, v_ref, kc_ref, vc_ref, o_ref, kb_s, vb_s = refs
  else:
    lam_ref, g_ref, q_ref, k_ref, v_ref, o_ref, kb_s, vb_s = refs
    kc_ref = vc_ref = None

  @pl.when(pl.program_id(2) == 0)
  def _():
    _stage_keys(kb_s, slice(None), k_ref, kc_ref)
    _stage_keys(vb_s, slice(None), v_ref, vc_ref)

  lv = lam_ref[...]
  lam = (jnp.exp(jnp.sum(lv[0:1] * lv[1:2], axis=-1, keepdims=True))
         - jnp.exp(jnp.sum(lv[2:3] * lv[3:4], axis=-1, keepdims=True)) + lam_init)
  lane = lax.broadcasted_iota(jnp.int32, (q_ref.shape[0], LANES), 1)
  for tl in range(q_ref.shape[1] // LANES):
    q = _tile(q_ref, tl)[...]
    q1 = jnp.where(lane < DA_QK, q, 0.0).astype(BF16)
    q2 = jnp.where(lane >= DA_QK, q, 0.0).astype(BF16)
    o1, o2 = _stacked_attention((q1, q2), _tile(kb_s, tl), _tile(vb_s, tl))
    _tile(o_ref, tl)[...] = _rms(o1 - lam * o2, g_ref[...]) * (1.0 - lam_init)


def _diff_attention(q, k, v, cache, lam_rows, subln_g, layer, lat):
  grid, qspec, kspec, cspec, ospec, n_keys, oshape, tiles = _attn_specs(q, lat, kv_joint=False)
  nb, sq, _ = oshape
  lam_init = 0.8 - 0.6 * math.exp(-0.3 * layer)
  const = lambda shape: pl.BlockSpec(shape, lambda bi, h, qi: (0, 0))
  in_specs = [const((8, LANES)), const((1, LANES)), qspec, kspec, kspec]
  args = [lam_rows, subln_g, _pass_view(q, lat)[0], k.reshape(nb, sq, D_MODEL),
          v.reshape(nb, sq, D_MODEL)]
  if lat:
    in_specs += [cspec, cspec]
    args += [cache[0].reshape(DEC_BATCH, PAST_LEN, D_MODEL),
             cache[1].reshape(DEC_BATCH, PAST_LEN, D_MODEL)]
  return pl.pallas_call(
      functools.partial(_diff_attn_body, lam_init=lam_init, cached=lat),
      grid=grid, in_specs=in_specs, out_specs=ospec,
      out_shape=jax.ShapeDtypeStruct(oshape, F32),
      scratch_shapes=[pltpu.VMEM((n_keys, tiles * LANES), BF16)] * 2,
      compiler_params=_cparams(("parallel", "parallel", "arbitrary")),
      name="diff_attention",
  )(*args)


def _pair_attn_body(*refs, scale, with_pe, cached):
  refs = list(refs)
  q_ref = refs.pop(0)
  qp_ref = refs.pop(0) if with_pe else None
  k_ref = refs.pop(0)
  kp_ref = refs.pop(0) if with_pe else None
  v_ref = refs.pop(0)
  kc_ref = refs.pop(0) if cached else None
  kpc_ref = refs.pop(0) if (cached and with_pe) else None
  vc_ref = refs.pop(0) if cached else None
  o_ref, kb_s, vb_s = refs

  tiles = q_ref.shape[1] // LANES
  kw = kb_s.shape[1] // tiles
  key_view = lambda tl: kb_s.at[:, tl * kw:(tl + 1) * kw]

  @pl.when(pl.program_id(2) == 0)
  def _():
    for tl in range(tiles):
      _stage_keys(key_view(tl), slice(0, LANES), _tile(k_ref, tl), _tile(kc_ref, tl))
      if with_pe:
        _stage_keys(key_view(tl), slice(LANES, 2 * LANES), kp_ref, kpc_ref)
    _stage_keys(vb_s, slice(None), v_ref, vc_ref)

  lane = lax.broadcasted_iota(jnp.int32, (q_ref.shape[0], LANES), 1)
  halves = (lane < NA_HD, lane >= NA_HD)
  for tl in range(tiles):
    q = _tile(q_ref, tl)[...] * scale
    qs = [jnp.where(hm, q, 0.0).astype(BF16) for hm in halves]
    if with_pe:
      tile_id = tl if tiles > 1 else pl.program_id(1)
      qp = (_tile(qp_ref, tl // 2) if tiles > 1 else qp_ref)[...] * scale
      base = (tile_id % 2) * (2 * MLA_ROPE)
      qs = [jnp.concatenate(
          [qs[a], jnp.where((lane >= base + a * MLA_ROPE) & (lane < base + (a + 1) * MLA_ROPE),
                            qp, 0.0).astype(BF16)], axis=1) for a in range(2)]
    _tile(o_ref, tl)[...] = jnp.where(
        halves[0], *_stacked_attention(qs, key_view(tl), _tile(vb_s, tl)))


def _pair_attention(q, k, v, lat, cache=None, scale=1.0, q_pe=None, k_pe=None, kv_joint=True):
  grid, qspec, kspec, cspec, ospec, n_keys, oshape, tiles = _attn_specs(q, lat, kv_joint)
  view = lambda a: _pass_view(a, lat)[0]
  b0 = _pass_view(q, lat)[1]
  if not kv_joint:
    k, v = (a.reshape(oshape) for a in (k, v))
  kv_view = view if kv_joint else (lambda a: a)
  with_pe = q_pe is not None
  tq, sq = qspec.block_shape[1], kspec.block_shape[1]
  in_specs, args = [qspec], [view(q)]
  if with_pe:
    if tiles == 1:
      in_specs.append(pl.BlockSpec((None, tq, LANES), lambda bi, h, qi: (bi + b0, qi, h // 2)))
    else:
      in_specs.append(pl.BlockSpec((None, tq, q_pe.shape[-1]), lambda bi, h, qi: (bi + b0, qi, 0)))
    args.append(view(q_pe))
  in_specs.append(kspec)
  args.append(kv_view(k))
  if with_pe:
    in_specs.append(pl.BlockSpec((None, sq, LANES), lambda bi, h, qi: (bi + b0, 0, 0)))
    args.append(view(k_pe))
  in_specs.append(kspec)
  args.append(kv_view(v))
  if lat:
    in_specs.append(cspec)
    args.append(cache[0])
    if with_pe:
      in_specs.append(pl.BlockSpec((None, PAST_LEN, LANES), lambda bi, h, qi: (bi, 0, 0)))
      args.append(cache[1])
    in_specs.append(cspec)
    args.append(cache[-1])
  return pl.pallas_call(
      functools.partial(_pair_attn_body, scale=scale, with_pe=with_pe, cached=lat),
      grid=grid, in_specs=in_specs, out_specs=ospec,
      out_shape=jax.ShapeDtypeStruct(oshape, F32),
      scratch_shapes=[pltpu.VMEM((n_keys, tiles * (2 * LANES if with_pe else LANES)), BF16),
                      pltpu.VMEM((n_keys, tiles * LANES), BF16)],
      compiler_params=_cparams(("parallel", "parallel", "arbitrary")),
      name="pair_attention_pe" if with_pe else "pair_attention",
  )(*args)


def _na_lat_body(q_ref, k_ref, v_ref, kc_ref, vc_ref, bias_ref, o_ref, kb_s, vb_s, kcb_s, vcb_s):
  rb = pl.program_id(2)

  @pl.when(rb == 0)
  def _():
    kb_s[...] = k_ref[...].astype(BF16)
    vb_s[...] = v_ref[...].astype(BF16)
    kcb_s[...] = kc_ref[...].astype(BF16)
    vcb_s[...] = vc_ref[...].astype(BF16)

  n_loc = NA_KH * GRID_W
  lane = lax.broadcasted_iota(jnp.int32, (GRID_W, LANES), 1)
  halves = (lane < NA_HD, lane >= NA_HD)
  kc = kcb_s[...]
  vc = vcb_s[...]
  for rr in range(NA_RB):
    r = rb * NA_RB + rr
    start = jnp.clip(r - NA_KH // 2, 0, GRID_R - NA_KH)
    pat = jnp.where(r < NA_KH // 2, 1 + r,
                    jnp.where(r > GRID_R - NA_KH // 2, r - (GRID_R - NA_KH), 0))
    off = pl.multiple_of(start * GRID_W, GRID_W)
    kw = kb_s[pl.ds(off, n_loc), :]
    vw = vb_s[pl.ds(off, n_loc), :]
    q = q_ref[rr * GRID_W:(rr + 1) * GRID_W, :]
    qq = jnp.concatenate([jnp.where(hm, q, 0.0).astype(BF16) for hm in halves], axis=0)
    bias = bias_ref[pat].reshape(2 * GRID_W, n_loc)
    s_loc = lax.dot_general(qq, kw, _NT, preferred_element_type=F32) + bias
    s_ctx = lax.dot_general(qq, kc, _NT, preferred_element_type=F32)
    m = jnp.maximum(jnp.max(s_loc, axis=-1, keepdims=True),
                    jnp.max(s_ctx, axis=-1, keepdims=True))
    p_loc = jnp.exp(s_loc - m)
    p_ctx = jnp.exp(s_ctx - m)
    l = jnp.sum(p_loc, axis=-1, keepdims=True) + jnp.sum(p_ctx, axis=-1, keepdims=True)
    o = (jnp.dot(p_loc.astype(BF16), vw, preferred_element_type=F32)
         + jnp.dot(p_ctx.astype(BF16), vc, preferred_element_type=F32)) / l
    o_ref[rr * GRID_W:(rr + 1) * GRID_W, :] = jnp.where(halves[0], o[:GRID_W], o[GRID_W:])


def _na_lat_attention(q, k, v, kc, vc, bias):
  qv, b0, b = _pass_view(q, True)
  kv, vv = (a.reshape(b, DEC_SEQ, D_MODEL) for a in (k, v))
  n_pat = bias.shape[0]
  blk = NA_RB * GRID_W
  qspec = pl.BlockSpec((None, blk, LANES), lambda bi, h, r: (bi + b0, r, h))
  kspec = pl.BlockSpec((None, DEC_SEQ, LANES), lambda bi, h, r: (bi, 0, h))
  cspec = pl.BlockSpec((None, PAST_LEN, LANES), lambda bi, h, r: (bi, 0, h))
  bspec = pl.BlockSpec((n_pat, 2, GRID_W, NA_KH * GRID_W), lambda bi, h, r: (0, h, 0, 0))
  return pl.pallas_call(
      _na_lat_body,
      grid=(b, D_MODEL // LANES, GRID_R // NA_RB),
      in_specs=[qspec, kspec, kspec, cspec, cspec, bspec],
      out_specs=pl.BlockSpec((None, blk, LANES), lambda bi, h, r: (bi, r, h)),
      out_shape=jax.ShapeDtypeStruct((b, DEC_SEQ, D_MODEL), F32),
      scratch_shapes=[pltpu.VMEM((DEC_SEQ, LANES), BF16)] * 2
      + [pltpu.VMEM((PAST_LEN, LANES), BF16)] * 2,
      compiler_params=_cparams(("parallel", "parallel", "arbitrary")),
      name="na_lat_attention",
  )(qv, kv, vv, kc, vc, bias)


SUBLANES = 8


def _sort_network(n):
  pairs = []
  p = 1
  while p < n:
    k = p
    while k >= 1:
      for j in range(k % p, n - k, 2 * k):
        for i in range(min(k, n - j - k)):
          if (i + j) // (2 * p) == (i + j + k) // (2 * p):
            pairs.append((i + j, i + j + k))
      k //= 2
    p *= 2
  return pairs


def _pop_heads(lists, hit, depth):
  for k in range(depth):
    lists[k] = jnp.where(hit, lists[k + 1], lists[k])


def _top16_sorted(s):
  t = s.shape[1]
  n = s.shape[0] // SUBLANES
  xs = [s[SUBLANES * k:SUBLANES * (k + 1), :] for k in range(n)]
  for i, j in _sort_network(n):
    xs[i], xs[j] = jnp.maximum(xs[i], xs[j]), jnp.minimum(xs[i], xs[j])
  rows = lax.broadcasted_iota(jnp.int32, (PEER_TOPK, t), 0)
  v = jnp.zeros((PEER_TOPK, t), F32)
  for r in range(PEER_TOPK):
    m = jnp.max(xs[0], axis=0, keepdims=True)
    v = jnp.where(rows == r, m, v)
    _pop_heads(xs, xs[0] == m, PEER_TOPK - 1 - r)
  return v


def _router_body(oc_ref, ol_ref, wo_ref, x_ref, g1_ref, g_ref, sc_ref, sh_ref,
                 wq_ref, k1_ref, k2_ref,
                 xn_ref, hb_ref, a1_ref, n1_ref, b2_ref, r2_ref, h_s):
  def first_head(o_ref):
    xn = x_ref[...] + g1_ref[...] * jnp.dot(o_ref[...].astype(BF16), wo_ref[...],
                                            preferred_element_type=F32)
    xn_ref[...] = xn
    h = _rms_mod(xn, g_ref[...], sc_ref[...], sh_ref[...]).astype(BF16)
    h_s[...] = h
    hb_ref[...] = h

  is_first = pl.program_id(1) == 0
  is_ctx = pl.program_id(0) < N_CTX // PEER_T
  pl.when(is_first & is_ctx)(lambda: first_head(oc_ref))
  pl.when(is_first & jnp.logical_not(is_ctx))(lambda: first_head(ol_ref))

  qh = jnp.dot(h_s[...], wq_ref[...], preferred_element_type=F32)
  half = PEER_DK // 2
  s1 = lax.dot_general(k1_ref[...], qh[:, :half].astype(BF16), _NT, preferred_element_type=F32)
  s2 = lax.dot_general(k2_ref[...], qh[:, half:].astype(BF16), _NT, preferred_element_type=F32)
  s1 = s1 - jnp.max(s1, axis=0, keepdims=True)
  s2 = s2 - jnp.max(s2, axis=0, keepdims=True)
  v1 = _top16_sorted(s1)
  v2 = _top16_sorted(s2)
  cand = [v1[:SUBLANES] + v2[k:k + 1] for k in range(PEER_TOPK)]
  tail = v1[SUBLANES:] + v2[0:1]
  z = jnp.zeros_like(v2[0:1])
  taken = jnp.zeros_like(cand[0])
  for r in range(PEER_TOPK):
    th = jnp.max(jnp.maximum(cand[0], tail), axis=0, keepdims=True)
    z = z + jnp.exp(th)
    tail = jnp.where(tail == th, NEG_INF, tail)
    hit = cand[0] == th
    taken = taken + jnp.where(hit, 1.0, 0.0)
    _pop_heads(cand, hit, PEER_TOPK - 1 - r)
  inf = jnp.float32(jnp.inf)
  first = jnp.minimum(jnp.min(jnp.where(taken >= 1.0, v1[:SUBLANES], inf), axis=0, keepdims=True),
                      jnp.min(jnp.where(tail == NEG_INF, v1[SUBLANES:], inf), axis=0,
                              keepdims=True))
  n1 = jnp.where(s1 >= first, 1.0, 0.0)
  k_max = PEER_TOPK // 2
  for k in range(2, k_max + 1):
    t_k = jnp.min(jnp.where(taken >= float(k), v1[:SUBLANES], inf), axis=0, keepdims=True)
    n1 = n1 + jnp.where(s1 >= t_k, 1.0, 0.0)
  n1 = n1 + jnp.where(s1 >= v1[0:1], jnp.maximum(taken[0:1] - float(k_max), 0.0), 0.0)
  r2 = jnp.zeros_like(s2)
  for r in range(PEER_TOPK):
    r2 = r2 + jnp.where(v2[r:r + 1] > s2, 1.0, 0.0)
  a1_ref[...] = jnp.exp(s1)
  n1_ref[...] = n1
  b2_ref[...] = pltpu.bitcast((jnp.exp(s2) * (1.0 / z)).astype(BF16), jnp.uint32)
  r2_ref[...] = pltpu.bitcast(r2.astype(BF16), jnp.uint32)


def _peer_router(oc, ol, w_o, x, gate1, g, sc, sh, wq, k1, k2):
  n = x.shape[0]
  tm = PEER_T
  nb_ctx = N_CTX // tm
  row = lambda w: pl.BlockSpec((tm, w), lambda i, h: (i, 0))
  full = lambda shape: pl.BlockSpec(shape, lambda i, h: (0,) * len(shape))
  modspec = pl.BlockSpec((None, 1, D_MODEL), lambda i, h: (_mod_row(i, tm), 0, 0))
  keyspec = pl.BlockSpec((None, PEER_NKEYS, PEER_DK // 2), lambda i, h: (h, 0, 0))
  tspec = pl.BlockSpec((None, PEER_NKEYS, tm), lambda i, h: (h, 0, i))
  pspec = pl.BlockSpec((None, PEER_NKEYS // 2, tm), lambda i, h: (h, 0, i))
  return pl.pallas_call(
      _router_body,
      grid=(n // tm, PEER_HEADS),
      in_specs=[pl.BlockSpec((tm, D_MODEL), lambda i, h: (jnp.minimum(i, nb_ctx - 1), 0)),
                pl.BlockSpec((tm, D_MODEL), lambda i, h: (jnp.maximum(i - nb_ctx, 0), 0)),
                full(w_o.shape), row(D_MODEL), modspec,
                full((1, D_MODEL)), modspec, modspec,
                pl.BlockSpec((D_MODEL, PEER_DK), lambda i, h: (0, h)), keyspec, keyspec],
      out_specs=[row(D_MODEL), row(D_MODEL), tspec, tspec, pspec, pspec],
      out_shape=[jax.ShapeDtypeStruct((n, D_MODEL), F32), jax.ShapeDtypeStruct((n, D_MODEL), BF16)]
      + [jax.ShapeDtypeStruct((PEER_HEADS, PEER_NKEYS, n), F32)] * 2
      + [jax.ShapeDtypeStruct((PEER_HEADS, PEER_NKEYS // 2, n), jnp.uint32)] * 2,
      scratch_shapes=[pltpu.VMEM((tm, D_MODEL), BF16)],
      compiler_params=_cparams(("parallel", "arbitrary")),
      name="peer_router",
  )(oc.reshape(N_CTX, D_MODEL), ol.reshape(N_LAT, D_MODEL), w_o, x, gate1, g, sc, sh, wq, k1, k2)


def _gelu(x):
  return 0.5 * x * (1.0 + lax.erf(x * np.float32(math.sqrt(0.5))))


def _peer_mix_body(hb_ref, u_ref, vt_ref, a1_ref, n1_ref, b2_ref, r2_ref, x_ref, gate_ref,
                   fg_ref, y_ref, *scratch, final_norm):
  c = pl.program_id(1)
  n_pieces = PEER_EC // PEER_SUBC
  if final_norm:
    yl_ref, acc_s, *piece_s = scratch
  else:
    acc_s, *piece_s = scratch

  @pl.when(c == 0)
  def _():
    acc_s[...] = jnp.zeros_like(acc_s)

  hb = hb_ref[...]
  t = hb.shape[0]
  pk = 16
  for j in range(n_pieces):
    rows = slice(j * PEER_SUBC, (j + 1) * PEER_SUBC)
    piece_s[j][...] = lax.dot_general(u_ref[rows, :], hb, _NT, preferred_element_type=F32)
  for j in range(n_pieces):
    s_s, p_s = piece_s[j], piece_s[n_pieces + j]
    for ii in range(PEER_SUBC // PEER_NKEYS):
      i1 = j * (PEER_SUBC // PEER_NKEYS) + ii
      w = [jnp.zeros((pk, t), BF16) for _ in range(PEER_NKEYS // pk)]
      for h in range(PEER_HEADS):
        a_row = jnp.broadcast_to(a1_ref[h, i1:i1 + 1, :], (pk, t)).astype(BF16)
        n_row = jnp.broadcast_to(n1_ref[h, i1:i1 + 1, :], (pk, t)).astype(BF16)
        for sub in range(PEER_NKEYS // pk):
          words = slice(sub * pk // 2, (sub + 1) * pk // 2)
          prod = a_row * pltpu.bitcast(b2_ref[h, words, :], BF16)
          rank = pltpu.bitcast(r2_ref[h, words, :], BF16)
          w[sub] = w[sub] + jnp.where(rank < n_row, prod, jnp.zeros_like(prod))
      for sub in range(PEER_NKEYS // pk):
        row0 = ii * PEER_NKEYS + sub * pk
        p_s[row0:row0 + pk, :] = w[sub] * _gelu(s_s[row0:row0 + pk, :]).astype(BF16)
    rows = slice(j * PEER_SUBC, (j + 1) * PEER_SUBC)
    acc_s[...] += jnp.dot(vt_ref[:, rows], p_s[...], preferred_element_type=F32)

  @pl.when(c == pl.num_programs(1) - 1)
  def _():
    y = x_ref[...] + gate_ref[...] * acc_s[...].T
    if final_norm:
      _store_by_pass(pl.program_id(0), t, _rms(y, fg_ref[...]), y_ref, yl_ref)
    else:
      y_ref[...] = y


def _peer_mix(hb, u, vt, a1, n1, b2, r2, x, gate, final_g, final_norm):
  n = x.shape[0]
  t = PEER_T
  n_i1 = PEER_EC // PEER_NKEYS
  row = lambda w: pl.BlockSpec((t, w), lambda i, c: (i, 0))
  i1spec = pl.BlockSpec((PEER_HEADS, n_i1, t), lambda i, c: (0, c, i))
  i2spec = pl.BlockSpec((PEER_HEADS, PEER_NKEYS // 2, t), lambda i, c: (0, 0, i))
  return pl.pallas_call(
      functools.partial(_peer_mix_body, final_norm=final_norm),
      grid=(n // t, PEER_EXPERTS // PEER_EC),
      in_specs=[row(D_MODEL),
                pl.BlockSpec((PEER_EC, D_MODEL), lambda i, c: (c, 0)),
                pl.BlockSpec((None, D_MODEL, PEER_EC), lambda i, c: (c, 0, 0)),
                i1spec, i1spec, i2spec, i2spec,
                row(D_MODEL),
                pl.BlockSpec((None, 1, D_MODEL), lambda i, c: (_mod_row(i, t), 0, 0)),
                pl.BlockSpec((1, D_MODEL), lambda i, c: (0, 0))],
      out_specs=_pass_specs(D_MODEL, t) if final_norm else row(D_MODEL),
      out_shape=_pass_shapes(D_MODEL) if final_norm else jax.ShapeDtypeStruct((n, D_MODEL), F32),
      scratch_shapes=[pltpu.VMEM((D_MODEL, t), F32)]
      + [pltpu.VMEM((PEER_SUBC, t), F32)] * (PEER_EC // PEER_SUBC)
      + [pltpu.VMEM((PEER_SUBC, t), BF16)] * (PEER_EC // PEER_SUBC),
      compiler_params=_cparams(("arbitrary", "arbitrary"), VMEM_LIMIT_PEER),
      name="peer_mix",
  )(hb, u, vt, a1, n1, b2, r2, x, gate, final_g)


def _rotated_tiles(w, group):
  k, n = w.shape
  g = w.reshape(k, n // group, 2, group // 2)
  return jnp.concatenate([-g[:, :, 1:], g[:, :, :1]], axis=2).reshape(k, n)


def _rope_table(rot_dims, reps):
  half = rot_dims // 2
  t = jnp.arange(DEC_SEQ)
  inv = ROPE_BASE ** (-jnp.arange(half, dtype=F32) / half)
  parts_c, parts_s = [], []
  for pos in (t // GRID_W, t % GRID_W):
    ang = pos.astype(F32)[:, None] * inv[None, :]
    parts_c += [jnp.cos(ang), jnp.cos(ang)]
    parts_s += [jnp.sin(ang), jnp.sin(ang)]
  cos = jnp.tile(jnp.concatenate(parts_c, axis=1), (1, reps))
  sin = jnp.tile(jnp.concatenate(parts_s, axis=1), (1, reps))
  cos = jnp.concatenate([cos, jnp.ones((TM, LANES), F32)], axis=0)
  sin = jnp.concatenate([sin, jnp.zeros((TM, LANES), F32)], axis=0)
  return cos, sin


def _na_bias_table(rpb):
  reps = [NA_KH // 2] + list(range(NA_KH // 2)) + list(range(GRID_R - NA_KH // 2 + 1, GRID_R))
  cols = np.arange(GRID_W)
  col_start = np.clip(cols - NA_KW // 2, 0, GRID_W - NA_KW)
  col_mask = (cols[None, :] >= col_start[:, None]) & (cols[None, :] < col_start[:, None] + NA_KW)
  dc = np.clip(cols[None, :] - cols[:, None], -(NA_KW - 1), NA_KW - 1) + (NA_KW - 1)
  dr = np.stack([np.clip(r - NA_KH // 2, 0, GRID_R - NA_KH) + np.arange(NA_KH) - r + (NA_KH - 1)
                 for r in reps])
  onehot = (dc[:, :, None] == np.arange(2 * NA_KW - 1)).astype(np.float32)
  bias = jnp.einsum('hpjc,qwc->phqjw', rpb[:, dr, :], onehot, precision=lax.Precision.HIGHEST)
  bias = jnp.where(col_mask[None, None, :, None, :], bias, NEG_INF)
  return bias.reshape(len(reps), NA_HEADS, GRID_W, NA_KH * GRID_W)


def _ctx_rows(a, *shape):
  return a[:N_CTX].reshape(BATCH, SEQ, *shape)


def _diff_mixer(layer, p, x, sc, sh, cache, tables):
  w = p['w_qkv']
  d = D_MODEL
  wq, wk, wv = w[:, :d], w[:, d:2 * d], w[:, 2 * d:]
  rot = lambda m: _rotated_tiles(m, DA_QK // 2)
  w5 = jnp.concatenate([wq, rot(wq), wk, rot(wk), wv], axis=1).astype(BF16)
  cos, sin = tables['da']
  q, (kc, kl), (vc, vl) = _qkv_rope(x, p['norm1_g'], sc, sh, w5, cos, sin, DA_QK ** -0.5)
  lam_rows = jnp.zeros((8, LANES), F32)
  for i, nme in enumerate(('lam_q1', 'lam_k1', 'lam_q2', 'lam_k2')):
    lam_rows = lam_rows.at[i, :DA_QK].set(p[nme])
  g = p['subln_g'].reshape(1, DA_V)
  oc = _diff_attention(q, kc, vc, None, lam_rows, g, layer, lat=False)
  ol = _diff_attention(q, kl, vl, cache, lam_rows, g, layer, lat=True)
  state = (kc.reshape(BATCH, SEQ, DA_HEADS, 2 * DA_QK), vc.reshape(BATCH, SEQ, DA_HEADS, DA_V))
  return oc, ol, state


def _na_mixer(p, x, sc, sh, cache):
  d = D_MODEL
  q, (kc, kl), (vc, vl) = _qkv_plain(x, p['norm1_g'], sc, sh, p['w_qkv'].astype(BF16),
                                     NA_HD ** -0.5)
  oc = _pair_attention(q, kc, vc, lat=False, kv_joint=False)
  bias = _na_bias_table(p['rpb'])
  ol = _na_lat_attention(q, kl, vl, cache[0].reshape(DEC_BATCH, PAST_LEN, d),
                         cache[1].reshape(DEC_BATCH, PAST_LEN, d), bias)
  state = (kc.reshape(BATCH, SEQ, NA_HEADS, NA_HD), vc.reshape(BATCH, SEQ, NA_HEADS, NA_HD))
  return oc, ol, state


def _mla_mixer(p, x, sc, sh, cache, tables):
  a, b = MLA_Q_RANK, MLA_Q_RANK + MLA_KV_RANK
  w_in = p['w_in']
  kpe_w = w_in[:, b:]
  w_in_x = jnp.concatenate([w_in[:, :b], jnp.tile(kpe_w, (1, 4)),
                            jnp.tile(_rotated_tiles(kpe_w, MLA_ROPE // 2), (1, 4))],
                           axis=1).astype(BF16)
  cos, sin = tables['mla']
  cq, ckv, kpe = _mla_in(x, p['norm1_g'], sc, sh, w_in_x, p['q_norm_g'].reshape(1, a),
                         p['kv_norm_g'].reshape(1, MLA_KV_RANK), cos, sin)
  w_uq = p['w_uq'].reshape(a, MLA_HEADS, MLA_NOPE + MLA_ROPE)
  w_qn = w_uq[:, :, :MLA_NOPE].reshape(a, -1)
  w_qp = w_uq[:, :, MLA_NOPE:].reshape(a, -1)
  w_uq_x = jnp.concatenate([w_qn, w_qp, _rotated_tiles(w_qp, MLA_ROPE // 2)], axis=1).astype(BF16)
  qn, qp = _mla_q(cq, w_uq_x, cos, sin)
  w_ukv = p['w_ukv'].reshape(MLA_KV_RANK, MLA_HEADS, MLA_NOPE + MLA_V)
  w_ukv_x = jnp.concatenate([w_ukv[:, :, :MLA_NOPE].reshape(MLA_KV_RANK, -1),
                             w_ukv[:, :, MLA_NOPE:].reshape(MLA_KV_RANK, -1)], axis=1).astype(BF16)
  kn, v = _mla_kv(ckv, w_ukv_x)
  knc, vc = _mla_kv(cache[0].reshape(DEC_BATCH * PAST_LEN, MLA_KV_RANK), w_ukv_x)
  cached = (knc.reshape(DEC_BATCH, PAST_LEN, D_MODEL), jnp.tile(cache[1], (1, 1, LANES // MLA_ROPE)),
            vc.reshape(DEC_BATCH, PAST_LEN, D_MODEL))
  scale = (MLA_NOPE + MLA_ROPE) ** -0.5
  oc = _pair_attention(qn, kn, v, lat=False, scale=scale, q_pe=qp, k_pe=kpe)
  ol = _pair_attention(qn, kn, v, lat=True, cache=cached, scale=scale, q_pe=qp, k_pe=kpe)
  state = (_ctx_rows(ckv, MLA_KV_RANK), _ctx_rows(kpe[:, :MLA_ROPE], MLA_ROPE))
  return oc, ol, state


def _layer(layer, p, x, cond8, cache, tables, final_g):
  sh1, sc1, g1, sh2, sc2, g2 = _modulation(cond8, p['ada_w'], p['ada_b'])
  kind = layer % 3
  if kind == 0:
    oc, ol, state = _diff_mixer(layer, p, x, sc1, sh1, cache, tables)
  elif kind == 1:
    oc, ol, state = _na_mixer(p, x, sc1, sh1, cache)
  else:
    oc, ol, state = _mla_mixer(p, x, sc1, sh1, cache, tables)
  x, hb, a1, n1, b2, r2 = _peer_router(oc, ol, p['w_o'].astype(BF16), x, g1, p['norm2_g'],
                                       sc2, sh2, p['peer_wq'], p['peer_k1'], p['peer_k2'])
  x = _peer_mix(hb, p['peer_u'], p['peer_vt'], a1, n1, b2, r2, x, g2, final_g,
                final_norm=layer == DEPTH - 1)
  return x, state


def kernel(x_prompt, x_sample, cache_l0_k, cache_l0_v, cache_l1_k, cache_l1_v, cache_l2_ckv, cache_l2_kpe, cache_l3_k, cache_l3_v, c, c_ctx, l0_norm1_g, l0_norm2_g, l0_ada_w, l0_ada_b, l0_w_qkv, l0_w_o, l0_lam_q1, l0_lam_k1, l0_lam_q2, l0_lam_k2, l0_subln_g, l0_peer_wq, l0_peer_k1, l0_peer_k2, l0_peer_u, l0_peer_v, l1_norm1_g, l1_norm2_g, l1_ada_w, l1_ada_b, l1_w_qkv, l1_w_o, l1_rpb, l1_peer_wq, l1_peer_k1, l1_peer_k2, l1_peer_u, l1_peer_v, l2_norm1_g, l2_norm2_g, l2_ada_w, l2_ada_b, l2_w_in, l2_q_norm_g, l2_w_uq, l2_kv_norm_g, l2_w_ukv, l2_w_o, l2_peer_wq, l2_peer_k1, l2_peer_k2, l2_peer_u, l2_peer_v, l3_norm1_g, l3_norm2_g, l3_ada_w, l3_ada_b, l3_w_qkv, l3_w_o, l3_lam_q1, l3_lam_k1, l3_lam_q2, l3_lam_k2, l3_subln_g, l3_peer_wq, l3_peer_k1, l3_peer_k2, l3_peer_u, l3_peer_v, final_norm_g):
  common = lambda n1, n2, aw, ab, wq, k1, k2, u, v: dict(
      norm1_g=n1.reshape(1, D_MODEL), norm2_g=n2.reshape(1, D_MODEL), ada_w=aw, ada_b=ab,
      peer_wq=wq.astype(BF16), peer_k1=k1.astype(BF16), peer_k2=k2.astype(BF16),
      peer_u=u.astype(BF16),
      peer_vt=v.reshape(PEER_EXPERTS // PEER_EC, PEER_EC, D_MODEL).transpose(0, 2, 1).astype(BF16))
  p0 = dict(common(l0_norm1_g, l0_norm2_g, l0_ada_w, l0_ada_b, l0_peer_wq, l0_peer_k1, l0_peer_k2,
                   l0_peer_u, l0_peer_v),
            w_qkv=l0_w_qkv, w_o=l0_w_o, lam_q1=l0_lam_q1, lam_k1=l0_lam_k1, lam_q2=l0_lam_q2,
            lam_k2=l0_lam_k2, subln_g=l0_subln_g)
  p1 = dict(common(l1_norm1_g, l1_norm2_g, l1_ada_w, l1_ada_b, l1_peer_wq, l1_peer_k1, l1_peer_k2,
                   l1_peer_u, l1_peer_v),
            w_qkv=l1_w_qkv, w_o=l1_w_o, rpb=l1_rpb)
  p2 = dict(common(l2_norm1_g, l2_norm2_g, l2_ada_w, l2_ada_b, l2_peer_wq, l2_peer_k1, l2_peer_k2,
                   l2_peer_u, l2_peer_v),
            w_in=l2_w_in, q_norm_g=l2_q_norm_g, w_uq=l2_w_uq, kv_norm_g=l2_kv_norm_g,
            w_ukv=l2_w_ukv, w_o=l2_w_o)
  p3 = dict(common(l3_norm1_g, l3_norm2_g, l3_ada_w, l3_ada_b, l3_peer_wq, l3_peer_k1, l3_peer_k2,
                   l3_peer_u, l3_peer_v),
            w_qkv=l3_w_qkv, w_o=l3_w_o, lam_q1=l3_lam_q1, lam_k1=l3_lam_k1, lam_q2=l3_lam_q2,
            lam_k2=l3_lam_k2, subln_g=l3_subln_g)
  params = (p0, p1, p2, p3)
  caches = ((cache_l0_k, cache_l0_v), (cache_l1_k, cache_l1_v),
            (cache_l2_ckv, cache_l2_kpe), (cache_l3_k, cache_l3_v))
  tables = dict(da=_rope_table(DA_QK // 2, LANES // DA_QK),
                mla=_rope_table(MLA_ROPE // 2, LANES // MLA_ROPE))

  cond8 = jnp.zeros((8, D_MODEL), F32).at[0].set(c_ctx).at[1:1 + DEC_BATCH].set(c)
  x = jnp.concatenate([x_prompt.reshape(N_CTX, D_MODEL), x_sample.reshape(N_LAT, D_MODEL)], axis=0)
  states = []
  for layer in range(DEPTH):
    x, st = _layer(layer, params[layer], x, cond8, caches[layer], tables,
                   final_norm_g.reshape(1, D_MODEL))
    states.extend(st)
  y_ctx, y_lat = x
  return (y_ctx.reshape(BATCH, SEQ, D_MODEL), y_lat.reshape(DEC_BATCH, DEC_SEQ, D_MODEL), *states)
```

```python
import functools
import math

import numpy as np
import jax
import jax.numpy as jnp
from jax import lax
from jax.experimental import pallas as pl
from jax.experimental.pallas import tpu as pltpu

F32 = jnp.float32
BF16 = jnp.bfloat16

D_MODEL = 1024
BATCH = 32
SEQ = 256
DEPTH = 4
DEC_BATCH = 2
DEC_SEQ = 4096
PAST_LEN = 512
GRID_W = 64
GRID_R = DEC_SEQ // GRID_W
EPS = 1e-6
ROPE_BASE = 10000.0
NEG_INF = -1e30

DA_HEADS = 8
DA_QK = 64
DA_V = 128
NA_HEADS = 16
NA_HD = 64
NA_KH = 8
NA_KW = 16
MLA_HEADS = 16
MLA_NOPE = 64
MLA_ROPE = 32
MLA_V = 64
MLA_Q_RANK = 384
MLA_KV_RANK = 256
PEER_HEADS = 8
PEER_NKEYS = 128
PEER_EXPERTS = PEER_NKEYS * PEER_NKEYS
PEER_DK = 256
PEER_TOPK = 16

N_CTX = BATCH * SEQ
N_LAT = DEC_BATCH * DEC_SEQ
N_TOK = N_CTX + N_LAT
N_COND = 8

LANES = 128
SUBLANES = 8
BF16_ROWS = 16
VMEM_LIMIT = 48 << 20
VMEM_LIMIT_PEER = 56 << 20

TM = 256
MOD_TN = 1536
TQ = 512
TK = 1536
ATTN_UNROLL = 3
NA_RB = 8
ROUTER_T = 1024
PEER_T = 512
PEER_EC = 2048
PEER_SUBC = 256

_NT = (((1,), (1,)), ((), ()))


def _cparams(sem, vmem=VMEM_LIMIT):
  return pltpu.CompilerParams(dimension_semantics=sem, vmem_limit_bytes=vmem)


def _mod_row(i, tm):
  nb_ctx = N_CTX // tm
  nb_bat = DEC_SEQ // tm
  return jnp.where(i < nb_ctx, 0, 1 + (i - nb_ctx) // nb_bat)


def _rope_blk(i, tm):
  nb_ctx = N_CTX // tm
  nb_bat = DEC_SEQ // tm
  return jnp.where(i < nb_ctx, nb_bat, (i - nb_ctx) % nb_bat)


def _rms_mod(x, g, sc, sh):
  y = x * lax.rsqrt(jnp.mean(x * x, axis=-1, keepdims=True) + EPS)
  return (y * g) * (1.0 + sc) + sh


def _rms(x, g):
  return x * lax.rsqrt(jnp.mean(x * x, axis=-1, keepdims=True) + EPS) * g


def _row_spec(width, tm=TM):
  return pl.BlockSpec((tm, width), lambda i: (i, 0))


def _full_spec(shape):
  return pl.BlockSpec(shape, lambda i: (0,) * len(shape))


def _mod_spec(tm=TM):
  return pl.BlockSpec((None, 1, D_MODEL), lambda i: (_mod_row(i, tm), 0, 0))


def _pass_specs(width, tm, index=lambda i, *_: i):
  nb = N_CTX // tm
  return [pl.BlockSpec((tm, width), lambda *g: (jnp.minimum(index(*g), nb - 1), 0)),
          pl.BlockSpec((tm, width), lambda *g: (jnp.maximum(index(*g) - nb, 0), 0))]


def _pass_shapes(width, dtype=F32):
  return [jax.ShapeDtypeStruct((N_CTX, width), dtype), jax.ShapeDtypeStruct((N_LAT, width), dtype)]


def _store_by_pass(i, tm, val, ctx_ref, lat_ref):
  is_ctx = i < N_CTX // tm

  @pl.when(is_ctx)
  def _():
    ctx_ref[...] = val

  @pl.when(jnp.logical_not(is_ctx))
  def _():
    lat_ref[...] = val


def _mod_body(c_ref, w_ref, b_ref, o_ref):
  c = c_ref[...]
  s = c / (1.0 + jnp.exp(-c))
  o_ref[...] = jnp.dot(s, w_ref[...], precision=lax.Precision.HIGHEST,
                       preferred_element_type=F32) + b_ref[...]


def _modulation(cond8, ada_w, ada_b):
  n = ada_w.shape[1]
  tn = MOD_TN
  out = pl.pallas_call(
      _mod_body,
      grid=(n // tn,),
      in_specs=[pl.BlockSpec((N_COND, D_MODEL), lambda j: (0, 0)),
                pl.BlockSpec((D_MODEL, tn), lambda j: (0, j)),
                pl.BlockSpec((1, tn), lambda j: (0, j))],
      out_specs=pl.BlockSpec((N_COND, tn), lambda j: (0, j)),
      out_shape=jax.ShapeDtypeStruct((N_COND, n), F32),
      compiler_params=_cparams(("parallel",)),
      name="modulation",
  )(cond8, ada_w, ada_b.reshape(1, n))
  return [out[:, k * D_MODEL:(k + 1) * D_MODEL].reshape(N_COND, 1, D_MODEL)
          for k in range(n // D_MODEL)]


def _tile_lanes(t, reps):
  return jnp.concatenate([t] * reps, axis=1)


def _qkv_rope_body(x_ref, g_ref, sc_ref, sh_ref, w_ref, cos_ref, sin_ref,
                   q_ref, kc_ref, kl_ref, vc_ref, vl_ref, *, q_scale):
  h = _rms_mod(x_ref[...], g_ref[...], sc_ref[...], sh_ref[...]).astype(BF16)
  reps = D_MODEL // LANES
  cos = _tile_lanes(cos_ref[...], reps)
  sin = _tile_lanes(sin_ref[...], reps)
  d = D_MODEL
  dot = lambda a, b: jnp.dot(h, w_ref[:, a:b], preferred_element_type=F32)
  q_ref[...] = (dot(0, d) * cos + dot(d, 2 * d) * sin) * q_scale
  i = pl.program_id(0)
  _store_by_pass(i, TM, dot(2 * d, 3 * d) * cos + dot(3 * d, 4 * d) * sin, kc_ref, kl_ref)
  _store_by_pass(i, TM, dot(4 * d, 5 * d), vc_ref, vl_ref)


def _qkv_rope(x, g, sc, sh, w5, cos, sin, q_scale):
  n = x.shape[0]
  rope_spec = pl.BlockSpec((TM, LANES), lambda i: (_rope_blk(i, TM), 0))
  q, kc, kl, vc, vl = pl.pallas_call(
      functools.partial(_qkv_rope_body, q_scale=q_scale),
      grid=(n // TM,),
      in_specs=[_row_spec(D_MODEL), _full_spec((1, D_MODEL)), _mod_spec(), _mod_spec(),
                _full_spec(w5.shape), rope_spec, rope_spec],
      out_specs=[_row_spec(D_MODEL)] + _pass_specs(D_MODEL, TM) * 2,
      out_shape=[jax.ShapeDtypeStruct((n, D_MODEL), F32)] + _pass_shapes(D_MODEL) * 2,
      compiler_params=_cparams(("arbitrary",)),
      name="qkv_rope_proj",
  )(x, g, sc, sh, w5, cos, sin)
  return q, (kc, kl), (vc, vl)


def _qkv_plain_body(x_ref, g_ref, sc_ref, sh_ref, w_ref, q_ref, kc_ref, kl_ref, vc_ref, vl_ref,
                    *, q_scale):
  h = _rms_mod(x_ref[...], g_ref[...], sc_ref[...], sh_ref[...]).astype(BF16)
  d = D_MODEL
  dot = lambda a, b: jnp.dot(h, w_ref[:, a:b], preferred_element_type=F32)
  q_ref[...] = dot(0, d) * q_scale
  i = pl.program_id(0)
  _store_by_pass(i, TM, dot(d, 2 * d), kc_ref, kl_ref)
  _store_by_pass(i, TM, dot(2 * d, 3 * d), vc_ref, vl_ref)


def _qkv_plain(x, g, sc, sh, w3, q_scale):
  n = x.shape[0]
  q, kc, kl, vc, vl = pl.pallas_call(
      functools.partial(_qkv_plain_body, q_scale=q_scale),
      grid=(n // TM,),
      in_specs=[_row_spec(D_MODEL), _full_spec((1, D_MODEL)), _mod_spec(), _mod_spec(),
                _full_spec(w3.shape)],
      out_specs=[_row_spec(D_MODEL)] + _pass_specs(D_MODEL, TM) * 2,
      out_shape=[jax.ShapeDtypeStruct((n, D_MODEL), F32)] + _pass_shapes(D_MODEL) * 2,
      compiler_params=_cparams(("arbitrary",)),
      name="qkv_proj",
  )(x, g, sc, sh, w3)
  return q, (kc, kl), (vc, vl)


def _mla_in_body(x_ref, g_ref, sc_ref, sh_ref, w_ref, qg_ref, kvg_ref, cos_ref, sin_ref,
                 cq_ref, ckv_ref, kpe_ref):
  h = _rms_mod(x_ref[...], g_ref[...], sc_ref[...], sh_ref[...]).astype(BF16)
  z = jnp.dot(h, w_ref[...], preferred_element_type=F32)
  a, b = MLA_Q_RANK, MLA_Q_RANK + MLA_KV_RANK
  cq_ref[...] = _rms(z[:, :a], qg_ref[...]).astype(BF16)
  ckv_ref[...] = _rms(z[:, a:b], kvg_ref[...])
  kpe_ref[...] = z[:, b:b + LANES] * cos_ref[...] + z[:, b + LANES:] * sin_ref[...]


def _mla_in(x, g, sc, sh, w_in, qg, kvg, cos, sin):
  n = x.shape[0]
  rope_spec = pl.BlockSpec((TM, LANES), lambda i: (_rope_blk(i, TM), 0))
  return pl.pallas_call(
      _mla_in_body,
      grid=(n // TM,),
      in_specs=[_row_spec(D_MODEL), _full_spec((1, D_MODEL)), _mod_spec(), _mod_spec(),
                _full_spec(w_in.shape), _full_spec((1, MLA_Q_RANK)),
                _full_spec((1, MLA_KV_RANK)), rope_spec, rope_spec],
      out_specs=[_row_spec(MLA_Q_RANK), _row_spec(MLA_KV_RANK), _row_spec(LANES)],
      out_shape=[jax.ShapeDtypeStruct((n, MLA_Q_RANK), BF16),
                 jax.ShapeDtypeStruct((n, MLA_KV_RANK), F32),
                 jax.ShapeDtypeStruct((n, LANES), F32)],
      compiler_params=_cparams(("parallel",)),
      name="mla_in_proj",
  )(x, g, sc, sh, w_in, qg, kvg, cos, sin)


def _mla_q_body(cq_ref, w_ref, cos_ref, sin_ref, qn_ref, qp_ref):
  z = jnp.dot(cq_ref[...], w_ref[...], preferred_element_type=F32)
  pe = MLA_HEADS * MLA_ROPE
  reps = pe // LANES
  cos = _tile_lanes(cos_ref[...], reps)
  sin = _tile_lanes(sin_ref[...], reps)
  qn_ref[...] = z[:, :D_MODEL]
  qp_ref[...] = z[:, D_MODEL:D_MODEL + pe] * cos + z[:, D_MODEL + pe:] * sin


def _mla_q(cq, w_uq, cos, sin):
  n = cq.shape[0]
  pe = MLA_HEADS * MLA_ROPE
  rope_spec = pl.BlockSpec((TM, LANES), lambda i: (_rope_blk(i, TM), 0))
  return pl.pallas_call(
      _mla_q_body,
      grid=(n // TM,),
      in_specs=[_row_spec(MLA_Q_RANK), _full_spec(w_uq.shape), rope_spec, rope_spec],
      out_specs=[_row_spec(D_MODEL), _row_spec(pe)],
      out_shape=[jax.ShapeDtypeStruct((n, D_MODEL), F32), jax.ShapeDtypeStruct((n, pe), F32)],
      compiler_params=_cparams(("parallel",)),
      name="mla_q_proj",
  )(cq, w_uq, cos, sin)


def _mla_kv_body(c_ref, w_ref, kn_ref, v_ref):
  z = jnp.dot(c_ref[...].astype(BF16), w_ref[...], preferred_element_type=F32)
  kn_ref[...] = z[:, :D_MODEL]
  v_ref[...] = z[:, D_MODEL:]


def _mla_kv(ckv, w_ukv):
  n = ckv.shape[0]
  return pl.pallas_call(
      _mla_kv_body,
      grid=(n // TM,),
      in_specs=[_row_spec(MLA_KV_RANK), _full_spec(w_ukv.shape)],
      out_specs=[_row_spec(D_MODEL)] * 2,
      out_shape=[jax.ShapeDtypeStruct((n, D_MODEL), F32)] * 2,
      compiler_params=_cparams(("parallel",)),
      name="mla_kv_proj",
  )(ckv, w_ukv)


def _online_update(s, m, l, acc, vb):
  m_new = jnp.maximum(m, jnp.max(s, axis=-1, keepdims=True))
  alpha = jnp.exp(m - m_new)
  p = jnp.exp(s - m_new)
  l_new = alpha * l + jnp.sum(p, axis=-1, keepdims=True)
  acc_new = alpha * acc + jnp.dot(p.astype(BF16), vb, preferred_element_type=F32)
  return m_new, l_new, acc_new


def _softmax_state(tq):
  return (jnp.full((tq, 1), NEG_INF, F32), jnp.zeros((tq, 1), F32), jnp.zeros((tq, LANES), F32))


def _stacked_attention(q_pair, kb_s, vb_s):
  tq = q_pair[0].shape[0]
  qq = jnp.concatenate(q_pair, axis=0)
  tk, n_chunks, unroll = _key_chunks(kb_s.shape[0])

  def chunk(c, carry):
    off = pl.multiple_of(c * tk, tk)
    s = lax.dot_general(qq, kb_s[pl.ds(off, tk), :], _NT, preferred_element_type=F32)
    return _online_update(s, *carry, vb_s[pl.ds(off, tk), :])

  _, l, acc = lax.fori_loop(0, n_chunks, chunk, _softmax_state(2 * tq), unroll=unroll)
  o = acc / l
  return o[:tq], o[tq:]


def _pass_view(a, lat):
  w = a.shape[-1]
  if lat:
    return a.reshape(N_TOK // DEC_SEQ, DEC_SEQ, w), N_CTX // DEC_SEQ, DEC_BATCH
  return a.reshape(N_TOK // SEQ, SEQ, w), 0, BATCH


def _stage_keys(dst, lanes, own_ref, cache_ref):
  n_own = own_ref.shape[0]
  dst[:n_own, lanes] = own_ref[...].astype(BF16)
  if cache_ref is not None:
    dst[n_own:, lanes] = cache_ref[...].astype(BF16)


def _key_chunks(n_keys):
  tk = min(TK, n_keys)
  n = n_keys // tk
  return tk, n, (ATTN_UNROLL if n % ATTN_UNROLL == 0 else 1)


def _attn_specs(q, lat, kv_joint=True):
  qv, b0, nb = _pass_view(q, lat)
  kb0 = b0 if kv_joint else 0
  sq = qv.shape[1]
  tq = min(TQ, sq)
  tiles = 1 if lat else D_MODEL // LANES
  width = tiles * LANES
  grid = (nb, D_MODEL // width, sq // tq)
  qspec = pl.BlockSpec((None, tq, width), lambda bi, h, qi: (bi + b0, qi, h))
  kspec = pl.BlockSpec((None, sq, width), lambda bi, h, qi: (bi + kb0, 0, h))
  cspec = pl.BlockSpec((None, PAST_LEN, width), lambda bi, h, qi: (bi, 0, h))
  ospec = pl.BlockSpec((None, tq, width), lambda bi, h, qi: (bi, qi, h))
  n_keys = sq + (PAST_LEN if lat else 0)
  return grid, qspec, kspec, cspec, ospec, n_keys, (nb, sq, D_MODEL), tiles


def _tile(ref, tl):
  return None if ref is None else ref.at[:, tl * LANES:(tl + 1) * LANES]


def _diff_attn_body(*refs, lam_init, cached):
  if cached:
    lam_ref, g_ref, q_ref, k_ref, v_ref, kc_ref, vc_ref, o_ref, kb_s, vb_s = refs
  else:
    lam_ref, g_ref, q_ref, k_ref, v_ref, o_ref, kb_s, vb_s = refs
    kc_ref = vc_ref = None

  @pl.when(pl.program_id(2) == 0)
  def _():
    _stage_keys(kb_s, slice(None), k_ref, kc_ref)
    _stage_keys(vb_s, slice(None), v_ref, vc_ref)

  lv = lam_ref[...]
  lam = (jnp.exp(jnp.sum(lv[0:1] * lv[1:2], axis=-1, keepdims=True))
         - jnp.exp(jnp.sum(lv[2:3] * lv[3:4], axis=-1, keepdims=True)) + lam_init)
  lane = lax.broadcasted_iota(jnp.int32, (q_ref.shape[0], LANES), 1)
  for tl in range(q_ref.shape[1] // LANES):
    q = _tile(q_ref, tl)[...]
    q1 = jnp.where(lane < DA_QK, q, 0.0).astype(BF16)
    q2 = jnp.where(lane >= DA_QK, q, 0.0).astype(BF16)
    o1, o2 = _stacked_attention((q1, q2), _tile(kb_s, tl), _tile(vb_s, tl))
    _tile(o_ref, tl)[...] = _rms(o1 - lam * o2, g_ref[...]) * (1.0 - lam_init)


def _diff_attention(q, k, v, cache, lam_rows, subln_g, layer, lat):
  grid, qspec, kspec, cspec, ospec, n_keys, oshape, tiles = _attn_specs(q, lat, kv_joint=False)
  nb, sq, _ = oshape
  lam_init = 0.8 - 0.6 * math.exp(-0.3 * layer)
  const = lambda shape: pl.BlockSpec(shape, lambda bi, h, qi: (0, 0))
  in_specs = [const((SUBLANES, LANES)), const((1, LANES)), qspec, kspec, kspec]
  args = [lam_rows, subln_g, _pass_view(q, lat)[0], k.reshape(nb, sq, D_MODEL),
          v.reshape(nb, sq, D_MODEL)]
  if lat:
    in_specs += [cspec, cspec]
    args += [cache[0].reshape(DEC_BATCH, PAST_LEN, D_MODEL),
             cache[1].reshape(DEC_BATCH, PAST_LEN, D_MODEL)]
  return pl.pallas_call(
      functools.partial(_diff_attn_body, lam_init=lam_init, cached=lat),
      grid=grid, in_specs=in_specs, out_specs=ospec,
      out_shape=jax.ShapeDtypeStruct(oshape, F32),
      scratch_shapes=[pltpu.VMEM((n_keys, tiles * LANES), BF16)] * 2,
      compiler_params=_cparams(("parallel", "parallel", "arbitrary")),
      name="diff_attention",
  )(*args)


def _pair_attn_body(*refs, scale, with_pe, cached):
  refs = list(refs)
  q_ref = refs.pop(0)
  qp_ref = refs.pop(0) if with_pe else None
  k_ref = refs.pop(0)
  kp_ref = refs.pop(0) if with_pe else None
  v_ref = refs.pop(0)
  kc_ref = refs.pop(0) if cached else None
  kpc_ref = refs.pop(0) if (cached and with_pe) else None
  vc_ref = refs.pop(0) if cached else None
  o_ref, kb_s, vb_s = refs

  tiles = q_ref.shape[1] // LANES
  kw = kb_s.shape[1] // tiles
  key_view = lambda tl: kb_s.at[:, tl * kw:(tl + 1) * kw]

  @pl.when(pl.program_id(2) == 0)
  def _():
    for tl in range(tiles):
      _stage_keys(key_view(tl), slice(0, LANES), _tile(k_ref, tl), _tile(kc_ref, tl))
      if with_pe:
        _stage_keys(key_view(tl), slice(LANES, 2 * LANES), kp_ref, kpc_ref)
    _stage_keys(vb_s, slice(None), v_ref, vc_ref)

  lane = lax.broadcasted_iota(jnp.int32, (q_ref.shape[0], LANES), 1)
  halves = (lane < NA_HD, lane >= NA_HD)
  for tl in range(tiles):
    q = _tile(q_ref, tl)[...] * scale
    qs = [jnp.where(hm, q, 0.0).astype(BF16) for hm in halves]
    if with_pe:
      tile_id = tl if tiles > 1 else pl.program_id(1)
      qp = (_tile(qp_ref, tl // 2) if tiles > 1 else qp_ref)[...] * scale
      base = (tile_id % 2) * (2 * MLA_ROPE)
      qs = [jnp.concatenate(
          [qs[a], jnp.where((lane >= base + a * MLA_ROPE) & (lane < base + (a + 1) * MLA_ROPE),
                            qp, 0.0).astype(BF16)], axis=1) for a in range(2)]
    _tile(o_ref, tl)[...] = jnp.where(
        halves[0], *_stacked_attention(qs, key_view(tl), _tile(vb_s, tl)))


def _pair_attention(q, k, v, lat, cache=None, scale=1.0, q_pe=None, k_pe=None, kv_joint=True):
  grid, qspec, kspec, cspec, ospec, n_keys, oshape, tiles = _attn_specs(q, lat, kv_joint)
  view = lambda a: _pass_view(a, lat)[0]
  b0 = _pass_view(q, lat)[1]
  if not kv_joint:
    k, v = (a.reshape(oshape) for a in (k, v))
  kv_view = view if kv_joint else (lambda a: a)
  with_pe = q_pe is not None
  tq, sq = qspec.block_shape[1], kspec.block_shape[1]
  in_specs, args = [qspec], [view(q)]
  if with_pe:
    if tiles == 1:
      in_specs.append(pl.BlockSpec((None, tq, LANES), lambda bi, h, qi: (bi + b0, qi, h // 2)))
    else:
      in_specs.append(pl.BlockSpec((None, tq, q_pe.shape[-1]), lambda bi, h, qi: (bi + b0, qi, 0)))
    args.append(view(q_pe))
  in_specs.append(kspec)
  args.append(kv_view(k))
  if with_pe:
    in_specs.append(pl.BlockSpec((None, sq, LANES), lambda bi, h, qi: (bi + b0, 0, 0)))
    args.append(view(k_pe))
  in_specs.append(kspec)
  args.append(kv_view(v))
  if lat:
    in_specs.append(cspec)
    args.append(cache[0])
    if with_pe:
      in_specs.append(pl.BlockSpec((None, PAST_LEN, LANES), lambda bi, h, qi: (bi, 0, 0)))
      args.append(cache[1])
    in_specs.append(cspec)
    args.append(cache[-1])
  return pl.pallas_call(
      functools.partial(_pair_attn_body, scale=scale, with_pe=with_pe, cached=lat),
      grid=grid, in_specs=in_specs, out_specs=ospec,
      out_shape=jax.ShapeDtypeStruct(oshape, F32),
      scratch_shapes=[pltpu.VMEM((n_keys, tiles * (2 * LANES if with_pe else LANES)), BF16),
                      pltpu.VMEM((n_keys, tiles * LANES), BF16)],
      compiler_params=_cparams(("parallel", "parallel", "arbitrary")),
      name="pair_attention_pe" if with_pe else "pair_attention",
  )(*args)


def _na_lat_body(q_ref, k_ref, v_ref, kc_ref, vc_ref, bias_ref, o_ref, kb_s, vb_s, kcb_s, vcb_s):
  rb = pl.program_id(2)

  @pl.when(rb == 0)
  def _():
    kb_s[...] = k_ref[...].astype(BF16)
    vb_s[...] = v_ref[...].astype(BF16)
    kcb_s[...] = kc_ref[...].astype(BF16)
    vcb_s[...] = vc_ref[...].astype(BF16)

  n_loc = NA_KH * GRID_W
  lane = lax.broadcasted_iota(jnp.int32, (GRID_W, LANES), 1)
  halves = (lane < NA_HD, lane >= NA_HD)
  kc = kcb_s[...]
  vc = vcb_s[...]
  for rr in range(NA_RB):
    r = rb * NA_RB + rr
    start = jnp.clip(r - NA_KH // 2, 0, GRID_R - NA_KH)
    pat = jnp.where(r < NA_KH // 2, 1 + r,
                    jnp.where(r > GRID_R - NA_KH // 2, r - (GRID_R - NA_KH), 0))
    off = pl.multiple_of(start * GRID_W, GRID_W)
    kw = kb_s[pl.ds(off, n_loc), :]
    vw = vb_s[pl.ds(off, n_loc), :]
    q = q_ref[rr * GRID_W:(rr + 1) * GRID_W, :]
    qq = jnp.concatenate([jnp.where(hm, q, 0.0).astype(BF16) for hm in halves], axis=0)
    bias = bias_ref[pat].reshape(2 * GRID_W, n_loc)
    s_loc = lax.dot_general(qq, kw, _NT, preferred_element_type=F32) + bias
    s_ctx = lax.dot_general(qq, kc, _NT, preferred_element_type=F32)
    m = jnp.maximum(jnp.max(s_loc, axis=-1, keepdims=True),
                    jnp.max(s_ctx, axis=-1, keepdims=True))
    p_loc = jnp.exp(s_loc - m)
    p_ctx = jnp.exp(s_ctx - m)
    l = jnp.sum(p_loc, axis=-1, keepdims=True) + jnp.sum(p_ctx, axis=-1, keepdims=True)
    o = (jnp.dot(p_loc.astype(BF16), vw, preferred_element_type=F32)
         + jnp.dot(p_ctx.astype(BF16), vc, preferred_element_type=F32)) / l
    o_ref[rr * GRID_W:(rr + 1) * GRID_W, :] = jnp.where(halves[0], o[:GRID_W], o[GRID_W:])


def _na_lat_attention(q, k, v, kc, vc, bias):
  qv, b0, b = _pass_view(q, True)
  kv, vv = (a.reshape(b, DEC_SEQ, D_MODEL) for a in (k, v))
  n_pat = bias.shape[0]
  blk = NA_RB * GRID_W
  qspec = pl.BlockSpec((None, blk, LANES), lambda bi, h, r: (bi + b0, r, h))
  kspec = pl.BlockSpec((None, DEC_SEQ, LANES), lambda bi, h, r: (bi, 0, h))
  cspec = pl.BlockSpec((None, PAST_LEN, LANES), lambda bi, h, r: (bi, 0, h))
  bspec = pl.BlockSpec((n_pat, 2, GRID_W, NA_KH * GRID_W), lambda bi, h, r: (0, h, 0, 0))
  return pl.pallas_call(
      _na_lat_body,
      grid=(b, D_MODEL // LANES, GRID_R // NA_RB),
      in_specs=[qspec, kspec, kspec, cspec, cspec, bspec],
      out_specs=pl.BlockSpec((None, blk, LANES), lambda bi, h, r: (bi, r, h)),
      out_shape=jax.ShapeDtypeStruct((b, DEC_SEQ, D_MODEL), F32),
      scratch_shapes=[pltpu.VMEM((DEC_SEQ, LANES), BF16)] * 2
      + [pltpu.VMEM((PAST_LEN, LANES), BF16)] * 2,
      compiler_params=_cparams(("parallel", "parallel", "arbitrary")),
      name="na_lat_attention",
  )(qv, kv, vv, kc, vc, bias)


def _sort_network(n):
  pairs = []
  p = 1
  while p < n:
    k = p
    while k >= 1:
      for j in range(k % p, n - k, 2 * k):
        for i in range(min(k, n - j - k)):
          if (i + j) // (2 * p) == (i + j + k) // (2 * p):
            pairs.append((i + j, i + j + k))
      k //= 2
    p *= 2
  return pairs


def _pop_heads(lists, hit, depth):
  for k in range(depth):
    lists[k] = jnp.where(hit, lists[k + 1], lists[k])


def _top16_sorted(s):
  t = s.shape[1]
  n = s.shape[0] // SUBLANES
  xs = [s[SUBLANES * k:SUBLANES * (k + 1), :] for k in range(n)]
  for i, j in _sort_network(n):
    xs[i], xs[j] = jnp.maximum(xs[i], xs[j]), jnp.minimum(xs[i], xs[j])
  rows = lax.broadcasted_iota(jnp.int32, (PEER_TOPK, t), 0)
  v = jnp.zeros((PEER_TOPK, t), F32)
  for r in range(PEER_TOPK):
    m = jnp.max(xs[0], axis=0, keepdims=True)
    v = jnp.where(rows == r, m, v)
    _pop_heads(xs, xs[0] == m, PEER_TOPK - 1 - r)
  return v


def _router_body(oc_ref, ol_ref, wo_ref, x_ref, g1_ref, g_ref, sc_ref, sh_ref,
                 wq_ref, k1_ref, k2_ref,
                 xn_ref, hb_ref, a1_ref, n1_ref, b2_ref, r2_ref, h_s):
  def first_head(o_ref):
    xn = x_ref[...] + g1_ref[...] * jnp.dot(o_ref[...].astype(BF16), wo_ref[...],
                                            preferred_element_type=F32)
    xn_ref[...] = xn
    h = _rms_mod(xn, g_ref[...], sc_ref[...], sh_ref[...]).astype(BF16)
    h_s[...] = h
    hb_ref[...] = h

  is_first = pl.program_id(1) == 0
  is_ctx = pl.program_id(0) < N_CTX // ROUTER_T
  pl.when(is_first & is_ctx)(lambda: first_head(oc_ref))
  pl.when(is_first & jnp.logical_not(is_ctx))(lambda: first_head(ol_ref))

  qh = jnp.dot(h_s[...], wq_ref[...], preferred_element_type=F32)
  half = PEER_DK // 2
  s1 = lax.dot_general(k1_ref[...], qh[:, :half].astype(BF16), _NT, preferred_element_type=F32)
  s2 = lax.dot_general(k2_ref[...], qh[:, half:].astype(BF16), _NT, preferred_element_type=F32)
  s1 = s1 - jnp.max(s1, axis=0, keepdims=True)
  s2 = s2 - jnp.max(s2, axis=0, keepdims=True)
  v1 = _top16_sorted(s1)
  v2 = _top16_sorted(s2)
  cand = [v1[:SUBLANES] + v2[k:k + 1] for k in range(PEER_TOPK)]
  tail = v1[SUBLANES:] + v2[0:1]
  z = jnp.zeros_like(v2[0:1])
  taken = jnp.zeros_like(cand[0])
  for r in range(PEER_TOPK):
    th = jnp.max(jnp.maximum(cand[0], tail), axis=0, keepdims=True)
    z = z + jnp.exp(th)
    tail = jnp.where(tail == th, NEG_INF, tail)
    hit = cand[0] == th
    taken = taken + jnp.where(hit, 1.0, 0.0)
    _pop_heads(cand, hit, PEER_TOPK - 1 - r)
  inf = jnp.float32(jnp.inf)
  first = jnp.minimum(jnp.min(jnp.where(taken >= 1.0, v1[:SUBLANES], inf), axis=0, keepdims=True),
                      jnp.min(jnp.where(tail == NEG_INF, v1[SUBLANES:], inf), axis=0,
                              keepdims=True))
  n1 = jnp.where(s1 >= first, 1.0, 0.0)
  k_max = PEER_TOPK // 2
  for k in range(2, k_max + 1):
    t_k = jnp.min(jnp.where(taken >= float(k), v1[:SUBLANES], inf), axis=0, keepdims=True)
    n1 = n1 + jnp.where(s1 >= t_k, 1.0, 0.0)
  n1 = n1 + jnp.where(s1 >= v1[0:1], jnp.maximum(taken[0:1] - float(k_max), 0.0), 0.0)
  r2 = jnp.zeros_like(s2)
  for r in range(PEER_TOPK):
    r2 = r2 + jnp.where(v2[r:r + 1] > s2, 1.0, 0.0)
  a1_ref[...] = jnp.exp(s1)
  n1_ref[...] = n1
  b2_ref[...] = pltpu.bitcast((jnp.exp(s2) * (1.0 / z)).astype(BF16), jnp.uint32)
  r2_ref[...] = pltpu.bitcast(r2.astype(BF16), jnp.uint32)


def _peer_router(oc, ol, w_o, x, gate1, g, sc, sh, wq, k1, k2):
  n = x.shape[0]
  tm = ROUTER_T
  nb_ctx = N_CTX // tm
  row = lambda w: pl.BlockSpec((tm, w), lambda i, h: (i, 0))
  full = lambda shape: pl.BlockSpec(shape, lambda i, h: (0,) * len(shape))
  modspec = pl.BlockSpec((None, 1, D_MODEL), lambda i, h: (_mod_row(i, tm), 0, 0))
  keyspec = pl.BlockSpec((None, PEER_NKEYS, PEER_DK // 2), lambda i, h: (h, 0, 0))
  tspec = pl.BlockSpec((None, PEER_NKEYS, tm), lambda i, h: (h, 0, i))
  pspec = pl.BlockSpec((None, PEER_NKEYS // 2, tm), lambda i, h: (h, 0, i))
  return pl.pallas_call(
      _router_body,
      grid=(n // tm, PEER_HEADS),
      in_specs=[pl.BlockSpec((tm, D_MODEL), lambda i, h: (jnp.minimum(i, nb_ctx - 1), 0)),
                pl.BlockSpec((tm, D_MODEL), lambda i, h: (jnp.maximum(i - nb_ctx, 0), 0)),
                full(w_o.shape), row(D_MODEL), modspec,
                full((1, D_MODEL)), modspec, modspec,
                pl.BlockSpec((D_MODEL, PEER_DK), lambda i, h: (0, h)), keyspec, keyspec],
      out_specs=[row(D_MODEL), row(D_MODEL), tspec, tspec, pspec, pspec],
      out_shape=[jax.ShapeDtypeStruct((n, D_MODEL), F32), jax.ShapeDtypeStruct((n, D_MODEL), BF16)]
      + [jax.ShapeDtypeStruct((PEER_HEADS, PEER_NKEYS, n), F32)] * 2
      + [jax.ShapeDtypeStruct((PEER_HEADS, PEER_NKEYS // 2, n), jnp.uint32)] * 2,
      scratch_shapes=[pltpu.VMEM((tm, D_MODEL), BF16)],
      compiler_params=_cparams(("parallel", "arbitrary")),
      name="peer_router",
  )(oc.reshape(N_CTX, D_MODEL), ol.reshape(N_LAT, D_MODEL), w_o, x, gate1, g, sc, sh, wq, k1, k2)


def _gelu(x):
  return 0.5 * x * (1.0 + lax.erf(x * np.float32(math.sqrt(0.5))))


def _peer_mix_body(hb_ref, u_ref, vt_ref, a1_ref, n1_ref, b2_ref, r2_ref, x_ref, gate_ref,
                   fg_ref, y_ref, *scratch, final_norm):
  c = pl.program_id(1)
  n_pieces = PEER_EC // PEER_SUBC
  if final_norm:
    yl_ref, acc_s, *piece_s = scratch
  else:
    acc_s, *piece_s = scratch

  @pl.when(c == 0)
  def _():
    acc_s[...] = jnp.zeros_like(acc_s)

  hb = hb_ref[...]
  t = hb.shape[0]
  pk = BF16_ROWS
  for j in range(n_pieces):
    rows = slice(j * PEER_SUBC, (j + 1) * PEER_SUBC)
    piece_s[j][...] = lax.dot_general(u_ref[rows, :], hb, _NT, preferred_element_type=F32)
  for j in range(n_pieces):
    s_s, p_s = piece_s[j], piece_s[n_pieces + j]
    for ii in range(PEER_SUBC // PEER_NKEYS):
      i1 = j * (PEER_SUBC // PEER_NKEYS) + ii
      w = [jnp.zeros((pk, t), BF16) for _ in range(PEER_NKEYS // pk)]
      for h in range(PEER_HEADS):
        a_row = jnp.broadcast_to(a1_ref[h, i1:i1 + 1, :], (pk, t)).astype(BF16)
        n_row = jnp.broadcast_to(n1_ref[h, i1:i1 + 1, :], (pk, t)).astype(BF16)
        for sub in range(PEER_NKEYS // pk):
          words = slice(sub * pk // 2, (sub + 1) * pk // 2)
          prod = a_row * pltpu.bitcast(b2_ref[h, words, :], BF16)
          rank = pltpu.bitcast(r2_ref[h, words, :], BF16)
          w[sub] = w[sub] + jnp.where(rank < n_row, prod, jnp.zeros_like(prod))
      for sub in range(PEER_NKEYS // pk):
        row0 = ii * PEER_NKEYS + sub * pk
        p_s[row0:row0 + pk, :] = w[sub] * _gelu(s_s[row0:row0 + pk, :]).astype(BF16)
    rows = slice(j * PEER_SUBC, (j + 1) * PEER_SUBC)
    acc_s[...] += jnp.dot(vt_ref[:, rows], p_s[...], preferred_element_type=F32)

  @pl.when(c == pl.num_programs(1) - 1)
  def _():
    y = x_ref[...] + gate_ref[...] * acc_s[...].T
    if final_norm:
      _store_by_pass(pl.program_id(0), t, _rms(y, fg_ref[...]), y_ref, yl_ref)
    else:
      y_ref[...] = y


def _peer_mix(hb, u, vt, a1, n1, b2, r2, x, gate, final_g, final_norm):
  n = x.shape[0]
  t = PEER_T
  n_i1 = PEER_EC // PEER_NKEYS
  row = lambda w: pl.BlockSpec((t, w), lambda i, c: (i, 0))
  i1spec = pl.BlockSpec((PEER_HEADS, n_i1, t), lambda i, c: (0, c, i))
  i2spec = pl.BlockSpec((PEER_HEADS, PEER_NKEYS // 2, t), lambda i, c: (0, 0, i))
  return pl.pallas_call(
      functools.partial(_peer_mix_body, final_norm=final_norm),
      grid=(n // t, PEER_EXPERTS // PEER_EC),
      in_specs=[row(D_MODEL),
                pl.BlockSpec((PEER_EC, D_MODEL), lambda i, c: (c, 0)),
                pl.BlockSpec((None, D_MODEL, PEER_EC), lambda i, c: (c, 0, 0)),
                i1spec, i1spec, i2spec, i2spec,
                row(D_MODEL),
                pl.BlockSpec((None, 1, D_MODEL), lambda i, c: (_mod_row(i, t), 0, 0)),
                pl.BlockSpec((1, D_MODEL), lambda i, c: (0, 0))],
      out_specs=_pass_specs(D_MODEL, t) if final_norm else row(D_MODEL),
      out_shape=_pass_shapes(D_MODEL) if final_norm else jax.ShapeDtypeStruct((n, D_MODEL), F32),
      scratch_shapes=[pltpu.VMEM((D_MODEL, t), F32)]
      + [pltpu.VMEM((PEER_SUBC, t), F32)] * (PEER_EC // PEER_SUBC)
      + [pltpu.VMEM((PEER_SUBC, t), BF16)] * (PEER_EC // PEER_SUBC),
      compiler_params=_cparams(("arbitrary", "arbitrary"), VMEM_LIMIT_PEER),
      name="peer_mix",
  )(hb, u, vt, a1, n1, b2, r2, x, gate, final_g)


def _rotated_tiles(w, group):
  k, n = w.shape
  g = w.reshape(k, n // group, 2, group // 2)
  return jnp.concatenate([-g[:, :, 1:], g[:, :, :1]], axis=2).reshape(k, n)


def _rope_table(rot_dims, reps):
  half = rot_dims // 2
  t = jnp.arange(DEC_SEQ)
  inv = ROPE_BASE ** (-jnp.arange(half, dtype=F32) / half)
  parts_c, parts_s = [], []
  for pos in (t // GRID_W, t % GRID_W):
    ang = pos.astype(F32)[:, None] * inv[None, :]
    parts_c += [jnp.cos(ang), jnp.cos(ang)]
    parts_s += [jnp.sin(ang), jnp.sin(ang)]
  cos = jnp.tile(jnp.concatenate(parts_c, axis=1), (1, reps))
  sin = jnp.tile(jnp.concatenate(parts_s, axis=1), (1, reps))
  cos = jnp.concatenate([cos, jnp.ones((TM, LANES), F32)], axis=0)
  sin = jnp.concatenate([sin, jnp.zeros((TM, LANES), F32)], axis=0)
  return cos, sin


def _na_bias_table(rpb):
  reps = [NA_KH // 2] + list(range(NA_KH // 2)) + list(range(GRID_R - NA_KH // 2 + 1, GRID_R))
  cols = np.arange(GRID_W)
  col_start = np.clip(cols - NA_KW // 2, 0, GRID_W - NA_KW)
  col_mask = (cols[None, :] >= col_start[:, None]) & (cols[None, :] < col_start[:, None] + NA_KW)
  dc = np.clip(cols[None, :] - cols[:, None], -(NA_KW - 1), NA_KW - 1) + (NA_KW - 1)
  dr = np.stack([np.clip(r - NA_KH // 2, 0, GRID_R - NA_KH) + np.arange(NA_KH) - r + (NA_KH - 1)
                 for r in reps])
  onehot = (dc[:, :, None] == np.arange(2 * NA_KW - 1)).astype(np.float32)
  bias = jnp.einsum('hpjc,qwc->phqjw', rpb[:, dr, :], onehot, precision=lax.Precision.HIGHEST)
  bias = jnp.where(col_mask[None, None, :, None, :], bias, NEG_INF)
  return bias.reshape(len(reps), NA_HEADS, GRID_W, NA_KH * GRID_W)


def _ctx_rows(a, *shape):
  return a[:N_CTX].reshape(BATCH, SEQ, *shape)


def _diff_mixer(layer, p, x, sc, sh, cache, tables):
  w = p['w_qkv']
  d = D_MODEL
  wq, wk, wv = w[:, :d], w[:, d:2 * d], w[:, 2 * d:]
  rot = lambda m: _rotated_tiles(m, DA_QK // 2)
  w5 = jnp.concatenate([wq, rot(wq), wk, rot(wk), wv], axis=1).astype(BF16)
  cos, sin = tables['da']
  q, (kc, kl), (vc, vl) = _qkv_rope(x, p['norm1_g'], sc, sh, w5, cos, sin, DA_QK ** -0.5)
  lam_rows = jnp.zeros((SUBLANES, LANES), F32)
  for i, nme in enumerate(('lam_q1', 'lam_k1', 'lam_q2', 'lam_k2')):
    lam_rows = lam_rows.at[i, :DA_QK].set(p[nme])
  g = p['subln_g'].reshape(1, DA_V)
  oc = _diff_attention(q, kc, vc, None, lam_rows, g, layer, lat=False)
  ol = _diff_attention(q, kl, vl, cache, lam_rows, g, layer, lat=True)
  state = (kc.reshape(BATCH, SEQ, DA_HEADS, 2 * DA_QK), vc.reshape(BATCH, SEQ, DA_HEADS, DA_V))
  return oc, ol, state


def _na_mixer(p, x, sc, sh, cache):
  d = D_MODEL
  q, (kc, kl), (vc, vl) = _qkv_plain(x, p['norm1_g'], sc, sh, p['w_qkv'].astype(BF16),
                                     NA_HD ** -0.5)
  oc = _pair_attention(q, kc, vc, lat=False, kv_joint=False)
  bias = _na_bias_table(p['rpb'])
  ol = _na_lat_attention(q, kl, vl, cache[0].reshape(DEC_BATCH, PAST_LEN, d),
                         cache[1].reshape(DEC_BATCH, PAST_LEN, d), bias)
  state = (kc.reshape(BATCH, SEQ, NA_HEADS, NA_HD), vc.reshape(BATCH, SEQ, NA_HEADS, NA_HD))
  return oc, ol, state


def _mla_mixer(p, x, sc, sh, cache, tables):
  a, b = MLA_Q_RANK, MLA_Q_RANK + MLA_KV_RANK
  w_in = p['w_in']
  kpe_w = w_in[:, b:]
  w_in_x = jnp.concatenate([w_in[:, :b], jnp.tile(kpe_w, (1, 4)),
                            jnp.tile(_rotated_tiles(kpe_w, MLA_ROPE // 2), (1, 4))],
                           axis=1).astype(BF16)
  cos, sin = tables['mla']
  cq, ckv, kpe = _mla_in(x, p['norm1_g'], sc, sh, w_in_x, p['q_norm_g'].reshape(1, a),
                         p['kv_norm_g'].reshape(1, MLA_KV_RANK), cos, sin)
  w_uq = p['w_uq'].reshape(a, MLA_HEADS, MLA_NOPE + MLA_ROPE)
  w_qn = w_uq[:, :, :MLA_NOPE].reshape(a, -1)
  w_qp = w_uq[:, :, MLA_NOPE:].reshape(a, -1)
  w_uq_x = jnp.concatenate([w_qn, w_qp, _rotated_tiles(w_qp, MLA_ROPE // 2)], axis=1).astype(BF16)
  qn, qp = _mla_q(cq, w_uq_x, cos, sin)
  w_ukv = p['w_ukv'].reshape(MLA_KV_RANK, MLA_HEADS, MLA_NOPE + MLA_V)
  w_ukv_x = jnp.concatenate([w_ukv[:, :, :MLA_NOPE].reshape(MLA_KV_RANK, -1),
                             w_ukv[:, :, MLA_NOPE:].reshape(MLA_KV_RANK, -1)], axis=1).astype(BF16)
  kn, v = _mla_kv(ckv, w_ukv_x)
  knc, vc = _mla_kv(cache[0].reshape(DEC_BATCH * PAST_LEN, MLA_KV_RANK), w_ukv_x)
  cached = (knc.reshape(DEC_BATCH, PAST_LEN, D_MODEL), jnp.tile(cache[1], (1, 1, LANES // MLA_ROPE)),
            vc.reshape(DEC_BATCH, PAST_LEN, D_MODEL))
  scale = (MLA_NOPE + MLA_ROPE) ** -0.5
  oc = _pair_attention(qn, kn, v, lat=False, scale=scale, q_pe=qp, k_pe=kpe)
  ol = _pair_attention(qn, kn, v, lat=True, cache=cached, scale=scale, q_pe=qp, k_pe=kpe)
  state = (_ctx_rows(ckv, MLA_KV_RANK), _ctx_rows(kpe[:, :MLA_ROPE], MLA_ROPE))
  return oc, ol, state


def _layer(layer, p, x, cond8, cache, tables, final_g):
  sh1, sc1, g1, sh2, sc2, g2 = _modulation(cond8, p['ada_w'], p['ada_b'])
  kind = layer % 3
  if kind == 0:
    oc, ol, state = _diff_mixer(layer, p, x, sc1, sh1, cache, tables)
  elif kind == 1:
    oc, ol, state = _na_mixer(p, x, sc1, sh1, cache)
  else:
    oc, ol, state = _mla_mixer(p, x, sc1, sh1, cache, tables)
  x, hb, a1, n1, b2, r2 = _peer_router(oc, ol, p['w_o'].astype(BF16), x, g1, p['norm2_g'],
                                       sc2, sh2, p['peer_wq'], p['peer_k1'], p['peer_k2'])
  x = _peer_mix(hb, p['peer_u'], p['peer_vt'], a1, n1, b2, r2, x, g2, final_g,
                final_norm=layer == DEPTH - 1)
  return x, state


def kernel(x_prompt, x_sample, cache_l0_k, cache_l0_v, cache_l1_k, cache_l1_v, cache_l2_ckv, cache_l2_kpe, cache_l3_k, cache_l3_v, c, c_ctx, l0_norm1_g, l0_norm2_g, l0_ada_w, l0_ada_b, l0_w_qkv, l0_w_o, l0_lam_q1, l0_lam_k1, l0_lam_q2, l0_lam_k2, l0_subln_g, l0_peer_wq, l0_peer_k1, l0_peer_k2, l0_peer_u, l0_peer_v, l1_norm1_g, l1_norm2_g, l1_ada_w, l1_ada_b, l1_w_qkv, l1_w_o, l1_rpb, l1_peer_wq, l1_peer_k1, l1_peer_k2, l1_peer_u, l1_peer_v, l2_norm1_g, l2_norm2_g, l2_ada_w, l2_ada_b, l2_w_in, l2_q_norm_g, l2_w_uq, l2_kv_norm_g, l2_w_ukv, l2_w_o, l2_peer_wq, l2_peer_k1, l2_peer_k2, l2_peer_u, l2_peer_v, l3_norm1_g, l3_norm2_g, l3_ada_w, l3_ada_b, l3_w_qkv, l3_w_o, l3_lam_q1, l3_lam_k1, l3_lam_q2, l3_lam_k2, l3_subln_g, l3_peer_wq, l3_peer_k1, l3_peer_k2, l3_peer_u, l3_peer_v, final_norm_g):
  common = lambda n1, n2, aw, ab, wq, k1, k2, u, v: dict(
      norm1_g=n1.reshape(1, D_MODEL), norm2_g=n2.reshape(1, D_MODEL), ada_w=aw, ada_b=ab,
      peer_wq=wq.astype(BF16), peer_k1=k1.astype(BF16), peer_k2=k2.astype(BF16),
      peer_u=u.astype(BF16),
      peer_vt=v.reshape(PEER_EXPERTS // PEER_EC, PEER_EC, D_MODEL).transpose(0, 2, 1).astype(BF16))
  p0 = dict(common(l0_norm1_g, l0_norm2_g, l0_ada_w, l0_ada_b, l0_peer_wq, l0_peer_k1, l0_peer_k2,
                   l0_peer_u, l0_peer_v),
            w_qkv=l0_w_qkv, w_o=l0_w_o, lam_q1=l0_lam_q1, lam_k1=l0_lam_k1, lam_q2=l0_lam_q2,
            lam_k2=l0_lam_k2, subln_g=l0_subln_g)
  p1 = dict(common(l1_norm1_g, l1_norm2_g, l1_ada_w, l1_ada_b, l1_peer_wq, l1_peer_k1, l1_peer_k2,
                   l1_peer_u, l1_peer_v),
            w_qkv=l1_w_qkv, w_o=l1_w_o, rpb=l1_rpb)
  p2 = dict(common(l2_norm1_g, l2_norm2_g, l2_ada_w, l2_ada_b, l2_peer_wq, l2_peer_k1, l2_peer_k2,
                   l2_peer_u, l2_peer_v),
            w_in=l2_w_in, q_norm_g=l2_q_norm_g, w_uq=l2_w_uq, kv_norm_g=l2_kv_norm_g,
            w_ukv=l2_w_ukv, w_o=l2_w_o)
  p3 = dict(common(l3_norm1_g, l3_norm2_g, l3_ada_w, l3_ada_b, l3_peer_wq, l3_peer_k1, l3_peer_k2,
                   l3_peer_u, l3_peer_v),
            w_qkv=l3_w_qkv, w_o=l3_w_o, lam_q1=l3_lam_q1, lam_k1=l3_lam_k1, lam_q2=l3_lam_q2,
            lam_k2=l3_lam_k2, subln_g=l3_subln_g)
  params = (p0, p1, p2, p3)
  caches = ((cache_l0_k, cache_l0_v), (cache_l1_k, cache_l1_v),
            (cache_l2_ckv, cache_l2_kpe), (cache_l3_k, cache_l3_v))
  tables = dict(da=_rope_table(DA_QK // 2, LANES // DA_QK),
                mla=_rope_table(MLA_ROPE // 2, LANES // MLA_ROPE))

  cond8 = jnp.zeros((N_COND, D_MODEL), F32).at[0].set(c_ctx).at[1:1 + DEC_BATCH].set(c)
  x = jnp.concatenate([x_prompt.reshape(N_CTX, D_MODEL), x_sample.reshape(N_LAT, D_MODEL)], axis=0)
  states = []
  for layer in range(DEPTH):
    x, st = _layer(layer, params[layer], x, cond8, caches[layer], tables,
                   final_norm_g.reshape(1, D_MODEL))
    states.extend(st)
  y_ctx, y_lat = x
  return (y_ctx.reshape(BATCH, SEQ, D_MODEL), y_lat.reshape(DEC_BATCH, DEC_SEQ, D_MODEL), *states)
```

```python
import functools
import math

import numpy as np
import jax
import jax.numpy as jnp
from jax import lax
from jax.experimental import pallas as pl
from jax.experimental.pallas import tpu as pltpu

F32 = jnp.float32
BF16 = jnp.bfloat16

D_MODEL = 1024
BATCH = 32
SEQ = 256
DEPTH = 4
DEC_BATCH = 2
DEC_SEQ = 4096
PAST_LEN = 512
GRID_W = 64
GRID_R = DEC_SEQ // GRID_W
EPS = 1e-6
ROPE_BASE = 10000.0
NEG_INF = -1e30

DA_HEADS = 8
DA_QK = 64
DA_V = 128
NA_HEADS = 16
NA_HD = 64
NA_KH = 8
NA_KW = 16
MLA_HEADS = 16
MLA_NOPE = 64
MLA_ROPE = 32
MLA_V = 64
MLA_Q_RANK = 384
MLA_KV_RANK = 256
PEER_HEADS = 8
PEER_NKEYS = 128
PEER_EXPERTS = PEER_NKEYS * PEER_NKEYS
PEER_DK = 256
PEER_TOPK = 16

N_CTX = BATCH * SEQ
N_LAT = DEC_BATCH * DEC_SEQ
N_TOK = N_CTX + N_LAT
N_COND = 8

LANES = 128
SUBLANES = 8
BF16_ROWS = 16
VMEM_LIMIT = 48 << 20
VMEM_LIMIT_PEER = 56 << 20

TM = 256
MOD_TN = 1536
TQ = 512
TK = 1536
ATTN_UNROLL = 3
NA_RB = 8
ROUTER_T = 1024
PEER_T = 512
PEER_EC = 2048
PEER_SUBC = 256

_NT = (((1,), (1,)), ((), ()))


def _cparams(sem, vmem=VMEM_LIMIT):
  return pltpu.CompilerParams(dimension_semantics=sem, vmem_limit_bytes=vmem)


def _mod_row(i, tm):
  nb_ctx = N_CTX // tm
  nb_bat = DEC_SEQ // tm
  return jnp.where(i < nb_ctx, 0, 1 + (i - nb_ctx) // nb_bat)


def _rope_blk(i, tm):
  nb_ctx = N_CTX // tm
  nb_bat = DEC_SEQ // tm
  return jnp.where(i < nb_ctx, nb_bat, (i - nb_ctx) % nb_bat)


def _rms_mod(x, g, sc, sh):
  y = x * lax.rsqrt(jnp.mean(x * x, axis=-1, keepdims=True) + EPS)
  return (y * g) * (1.0 + sc) + sh


def _rms(x, g):
  return x * lax.rsqrt(jnp.mean(x * x, axis=-1, keepdims=True) + EPS) * g


def _row_spec(width, tm=TM):
  return pl.BlockSpec((tm, width), lambda i: (i, 0))


def _full_spec(shape):
  return pl.BlockSpec(shape, lambda i: (0,) * len(shape))


def _mod_spec(tm=TM):
  return pl.BlockSpec((None, 1, D_MODEL), lambda i: (_mod_row(i, tm), 0, 0))


def _pass_specs(width, tm, index=lambda i, *_: i):
  nb = N_CTX // tm
  return [pl.BlockSpec((tm, width), lambda *g: (jnp.minimum(index(*g), nb - 1), 0)),
          pl.BlockSpec((tm, width), lambda *g: (jnp.maximum(index(*g) - nb, 0), 0))]


def _pass_shapes(width, dtype=F32):
  return [jax.ShapeDtypeStruct((N_CTX, width), dtype), jax.ShapeDtypeStruct((N_LAT, width), dtype)]


def _store_by_pass(i, tm, val, ctx_ref, lat_ref):
  is_ctx = i < N_CTX // tm

  @pl.when(is_ctx)
  def _():
    ctx_ref[...] = val

  @pl.when(jnp.logical_not(is_ctx))
  def _():
    lat_ref[...] = val


def _mod_body(c_ref, w_ref, b_ref, o_ref):
  c = c_ref[...]
  s = c / (1.0 + jnp.exp(-c))
  o_ref[...] = jnp.dot(s, w_ref[...], precision=lax.Precision.HIGHEST,
                       preferred_element_type=F32) + b_ref[...]


def _modulation(cond8, ada_w, ada_b):
  n = ada_w.shape[1]
  tn = MOD_TN
  out = pl.pallas_call(
      _mod_body,
      grid=(n // tn,),
      in_specs=[pl.BlockSpec((N_COND, D_MODEL), lambda j: (0, 0)),
                pl.BlockSpec((D_MODEL, tn), lambda j: (0, j)),
                pl.BlockSpec((1, tn), lambda j: (0, j))],
      out_specs=pl.BlockSpec((N_COND, tn), lambda j: (0, j)),
      out_shape=jax.ShapeDtypeStruct((N_COND, n), F32),
      compiler_params=_cparams(("parallel",)),
      name="modulation",
  )(cond8, ada_w, ada_b.reshape(1, n))
  return [out[:, k * D_MODEL:(k + 1) * D_MODEL].reshape(N_COND, 1, D_MODEL)
          for k in range(n // D_MODEL)]


def _tile_lanes(t, reps):
  return jnp.concatenate([t] * reps, axis=1)


def _qkv_rope_body(x_ref, g_ref, sc_ref, sh_ref, w_ref, cos_ref, sin_ref,
                   q_ref, kc_ref, kl_ref, vc_ref, vl_ref, *, q_scale):
  h = _rms_mod(x_ref[...], g_ref[...], sc_ref[...], sh_ref[...]).astype(BF16)
  reps = D_MODEL // LANES
  cos = _tile_lanes(cos_ref[...], reps)
  sin = _tile_lanes(sin_ref[...], reps)
  d = D_MODEL
  dot = lambda a, b: jnp.dot(h, w_ref[:, a:b], preferred_element_type=F32)
  q_ref[...] = (dot(0, d) * cos + dot(d, 2 * d) * sin) * q_scale
  i = pl.program_id(0)
  _store_by_pass(i, TM, dot(2 * d, 3 * d) * cos + dot(3 * d, 4 * d) * sin, kc_ref, kl_ref)
  _store_by_pass(i, TM, dot(4 * d, 5 * d), vc_ref, vl_ref)


def _qkv_rope(x, g, sc, sh, w5, cos, sin, q_scale):
  n = x.shape[0]
  rope_spec = pl.BlockSpec((TM, LANES), lambda i: (_rope_blk(i, TM), 0))
  q, kc, kl, vc, vl = pl.pallas_call(
      functools.partial(_qkv_rope_body, q_scale=q_scale),
      grid=(n // TM,),
      in_specs=[_row_spec(D_MODEL), _full_spec((1, D_MODEL)), _mod_spec(), _mod_spec(),
                _full_spec(w5.shape), rope_spec, rope_spec],
      out_specs=[_row_spec(D_MODEL)] + _pass_specs(D_MODEL, TM) * 2,
      out_shape=[jax.ShapeDtypeStruct((n, D_MODEL), F32)] + _pass_shapes(D_MODEL) * 2,
      compiler_params=_cparams(("arbitrary",)),
      name="qkv_rope_proj",
  )(x, g, sc, sh, w5, cos, sin)
  return q, (kc, kl), (vc, vl)


def _qkv_plain_body(x_ref, g_ref, sc_ref, sh_ref, w_ref, q_ref, kc_ref, kl_ref, vc_ref, vl_ref,
                    *, q_scale):
  h = _rms_mod(x_ref[...], g_ref[...], sc_ref[...], sh_ref[...]).astype(BF16)
  d = D_MODEL
  dot = lambda a, b: jnp.dot(h, w_ref[:, a:b], preferred_element_type=F32)
  q_ref[...] = dot(0, d) * q_scale
  i = pl.program_id(0)
  _store_by_pass(i, TM, dot(d, 2 * d), kc_ref, kl_ref)
  _store_by_pass(i, TM, dot(2 * d, 3 * d), vc_ref, vl_ref)


def _qkv_plain(x, g, sc, sh, w3, q_scale):
  n = x.shape[0]
  q, kc, kl, vc, vl = pl.pallas_call(
      functools.partial(_qkv_plain_body, q_scale=q_scale),
      grid=(n // TM,),
      in_specs=[_row_spec(D_MODEL), _full_spec((1, D_MODEL)), _mod_spec(), _mod_spec(),
                _full_spec(w3.shape)],
      out_specs=[_row_spec(D_MODEL)] + _pass_specs(D_MODEL, TM) * 2,
      out_shape=[jax.ShapeDtypeStruct((n, D_MODEL), F32)] + _pass_shapes(D_MODEL) * 2,
      compiler_params=_cparams(("arbitrary",)),
      name="qkv_proj",
  )(x, g, sc, sh, w3)
  return q, (kc, kl), (vc, vl)


def _mla_in_body(x_ref, g_ref, sc_ref, sh_ref, w_ref, qg_ref, kvg_ref, cos_ref, sin_ref,
                 cq_ref, ckv_ref, kpe_ref):
  h = _rms_mod(x_ref[...], g_ref[...], sc_ref[...], sh_ref[...]).astype(BF16)
  z = jnp.dot(h, w_ref[...], preferred_element_type=F32)
  a, b = MLA_Q_RANK, MLA_Q_RANK + MLA_KV_RANK
  cq_ref[...] = _rms(z[:, :a], qg_ref[...]).astype(BF16)
  ckv_ref[...] = _rms(z[:, a:b], kvg_ref[...])
  kpe_ref[...] = z[:, b:b + LANES] * cos_ref[...] + z[:, b + LANES:] * sin_ref[...]


def _mla_in(x, g, sc, sh, w_in, qg, kvg, cos, sin):
  n = x.shape[0]
  rope_spec = pl.BlockSpec((TM, LANES), lambda i: (_rope_blk(i, TM), 0))
  return pl.pallas_call(
      _mla_in_body,
      grid=(n // TM,),
      in_specs=[_row_spec(D_MODEL), _full_spec((1, D_MODEL)), _mod_spec(), _mod_spec(),
                _full_spec(w_in.shape), _full_spec((1, MLA_Q_RANK)),
                _full_spec((1, MLA_KV_RANK)), rope_spec, rope_spec],
      out_specs=[_row_spec(MLA_Q_RANK), _row_spec(MLA_KV_RANK), _row_spec(LANES)],
      out_shape=[jax.ShapeDtypeStruct((n, MLA_Q_RANK), BF16),
                 jax.ShapeDtypeStruct((n, MLA_KV_RANK), F32),
                 jax.ShapeDtypeStruct((n, LANES), F32)],
      compiler_params=_cparams(("parallel",)),
      name="mla_in_proj",
  )(x, g, sc, sh, w_in, qg, kvg, cos, sin)


def _mla_q_body(cq_ref, w_ref, cos_ref, sin_ref, qn_ref, qp_ref):
  z = jnp.dot(cq_ref[...], w_ref[...], preferred_element_type=F32)
  pe = MLA_HEADS * MLA_ROPE
  reps = pe // LANES
  cos = _tile_lanes(cos_ref[...], reps)
  sin = _tile_lanes(sin_ref[...], reps)
  qn_ref[...] = z[:, :D_MODEL]
  qp_ref[...] = z[:, D_MODEL:D_MODEL + pe] * cos + z[:, D_MODEL + pe:] * sin


def _mla_q(cq, w_uq, cos, sin):
  n = cq.shape[0]
  pe = MLA_HEADS * MLA_ROPE
  rope_spec = pl.BlockSpec((TM, LANES), lambda i: (_rope_blk(i, TM), 0))
  return pl.pallas_call(
      _mla_q_body,
      grid=(n // TM,),
      in_specs=[_row_spec(MLA_Q_RANK), _full_spec(w_uq.shape), rope_spec, rope_spec],
      out_specs=[_row_spec(D_MODEL), _row_spec(pe)],
      out_shape=[jax.ShapeDtypeStruct((n, D_MODEL), F32), jax.ShapeDtypeStruct((n, pe), F32)],
      compiler_params=_cparams(("parallel",)),
      name="mla_q_proj",
  )(cq, w_uq, cos, sin)


def _mla_kv_body(c_ref, w_ref, kn_ref, v_ref):
  z = jnp.dot(c_ref[...].astype(BF16), w_ref[...], preferred_element_type=F32)
  kn_ref[...] = z[:, :D_MODEL]
  v_ref[...] = z[:, D_MODEL:]


def _mla_kv(ckv, w_ukv):
  n = ckv.shape[0]
  return pl.pallas_call(
      _mla_kv_body,
      grid=(n // TM,),
      in_specs=[_row_spec(MLA_KV_RANK), _full_spec(w_ukv.shape)],
      out_specs=[_row_spec(D_MODEL)] * 2,
      out_shape=[jax.ShapeDtypeStruct((n, D_MODEL), F32)] * 2,
      compiler_params=_cparams(("parallel",)),
      name="mla_kv_proj",
  )(ckv, w_ukv)


def _online_update(s, m, l, acc, vb):
  m_new = jnp.maximum(m, jnp.max(s, axis=-1, keepdims=True))
  alpha = jnp.exp(m - m_new)
  p = jnp.exp(s - m_new)
  l_new = alpha * l + jnp.sum(p, axis=-1, keepdims=True)
  acc_new = alpha * acc + jnp.dot(p.astype(BF16), vb, preferred_element_type=F32)
  return m_new, l_new, acc_new


def _softmax_state(tq):
  return (jnp.full((tq, 1), NEG_INF, F32), jnp.zeros((tq, 1), F32), jnp.zeros((tq, LANES), F32))


def _stacked_attention(q_pair, kb_s, vb_s):
  tq = q_pair[0].shape[0]
  qq = jnp.concatenate(q_pair, axis=0)
  tk, n_chunks, unroll = _key_chunks(kb_s.shape[0])

  def chunk(c, carry):
    off = pl.multiple_of(c * tk, tk)
    s = lax.dot_general(qq, kb_s[pl.ds(off, tk), :], _NT, preferred_element_type=F32)
    return _online_update(s, *carry, vb_s[pl.ds(off, tk), :])

  _, l, acc = lax.fori_loop(0, n_chunks, chunk, _softmax_state(2 * tq), unroll=unroll)
  o = acc / l
  return o[:tq], o[tq:]


def _pass_view(a, lat):
  w = a.shape[-1]
  if lat:
    return a.reshape(N_TOK // DEC_SEQ, DEC_SEQ, w), N_CTX // DEC_SEQ, DEC_BATCH
  return a.reshape(N_TOK // SEQ, SEQ, w), 0, BATCH


def _stage_keys(dst, lanes, own_ref, cache_ref):
  n_own = own_ref.shape[0]
  dst[:n_own, lanes] = own_ref[...].astype(BF16)
  if cache_ref is not None:
    dst[n_own:, lanes] = cache_ref[...].astype(BF16)


def _key_chunks(n_keys):
  tk = min(TK, n_keys)
  n = n_keys // tk
  return tk, n, (ATTN_UNROLL if n % ATTN_UNROLL == 0 else 1)


def _attn_specs(q, lat, kv_joint=True):
  qv, b0, nb = _pass_view(q, lat)
  kb0 = b0 if kv_joint else 0
  sq = qv.shape[1]
  tq = min(TQ, sq)
  tiles = 1 if lat else D_MODEL // LANES
  width = tiles * LANES
  grid = (nb, D_MODEL // width, sq // tq)
  qspec = pl.BlockSpec((None, tq, width), lambda bi, h, qi: (bi + b0, qi, h))
  kspec = pl.BlockSpec((None, sq, width), lambda bi, h, qi: (bi + kb0, 0, h))
  cspec = pl.BlockSpec((None, PAST_LEN, width), lambda bi, h, qi: (bi, 0, h))
  ospec = pl.BlockSpec((None, tq, width), lambda bi, h, qi: (bi, qi, h))
  n_keys = sq + (PAST_LEN if lat else 0)
  return grid, qspec, kspec, cspec, ospec, n_keys, (nb, sq, D_MODEL), tiles


def _tile(ref, tl):
  return None if ref is None else ref.at[:, tl * LANES:(tl + 1) * LANES]


def _diff_attn_body(*refs, lam_init, cached):
  if cached:
    lam_ref, g_ref, q_ref, k_ref, v_ref, kc_ref, vc_ref, o_ref, kb_s, vb_s = refs
  else:
    lam_ref, g_ref, q_ref, k_ref, v_ref, o_ref, kb_s, vb_s = refs
    kc_ref = vc_ref = None

  @pl.when(pl.program_id(2) == 0)
  def _():
    _stage_keys(kb_s, slice(None), k_ref, kc_ref)
    _stage_keys(vb_s, slice(None), v_ref, vc_ref)

  lv = lam_ref[...]
  lam = (jnp.exp(jnp.sum(lv[0:1] * lv[1:2], axis=-1, keepdims=True))
         - jnp.exp(jnp.sum(lv[2:3] * lv[3:4], axis=-1, keepdims=True)) + lam_init)
  lane = lax.broadcasted_iota(jnp.int32, (q_ref.shape[0], LANES), 1)
  for tl in range(q_ref.shape[1] // LANES):
    q = _tile(q_ref, tl)[...]
    q1 = jnp.where(lane < DA_QK, q, 0.0).astype(BF16)
    q2 = jnp.where(lane >= DA_QK, q, 0.0).astype(BF16)
    o1, o2 = _stacked_attention((q1, q2), _tile(kb_s, tl), _tile(vb_s, tl))
    _tile(o_ref, tl)[...] = _rms(o1 - lam * o2, g_ref[...]) * (1.0 - lam_init)


def _diff_attention(q, k, v, cache, lam_rows, subln_g, layer, lat):
  grid, qspec, kspec, cspec, ospec, n_keys, oshape, tiles = _attn_specs(q, lat, kv_joint=False)
  nb, sq, _ = oshape
  lam_init = 0.8 - 0.6 * math.exp(-0.3 * layer)
  const = lambda shape: pl.BlockSpec(shape, lambda bi, h, qi: (0, 0))
  in_specs = [const((SUBLANES, LANES)), const((1, LANES)), qspec, kspec, kspec]
  args = [lam_rows, subln_g, _pass_view(q, lat)[0], k.reshape(nb, sq, D_MODEL),
          v.reshape(nb, sq, D_MODEL)]
  if lat:
    in_specs += [cspec, cspec]
    args += [cache[0].reshape(DEC_BATCH, PAST_LEN, D_MODEL),
             cache[1].reshape(DEC_BATCH, PAST_LEN, D_MODEL)]
  return pl.pallas_call(
      functools.partial(_diff_attn_body, lam_init=lam_init, cached=lat),
      grid=grid, in_specs=in_specs, out_specs=ospec,
      out_shape=jax.ShapeDtypeStruct(oshape, F32),
      scratch_shapes=[pltpu.VMEM((n_keys, tiles * LANES), BF16)] * 2,
      compiler_params=_cparams(("parallel", "parallel", "arbitrary")),
      name="diff_attention",
  )(*args)


def _pair_attn_body(*refs, scale, with_pe, cached):
  refs = list(refs)
  q_ref = refs.pop(0)
  qp_ref = refs.pop(0) if with_pe else None
  k_ref = refs.pop(0)
  kp_ref = refs.pop(0) if with_pe else None
  v_ref = refs.pop(0)
  kc_ref = refs.pop(0) if cached else None
  kpc_ref = refs.pop(0) if (cached and with_pe) else None
  vc_ref = refs.pop(0) if cached else None
  o_ref, kb_s, vb_s = refs

  tiles = q_ref.shape[1] // LANES
  kw = kb_s.shape[1] // tiles
  key_view = lambda tl: kb_s.at[:, tl * kw:(tl + 1) * kw]

  @pl.when(pl.program_id(2) == 0)
  def _():
    for tl in range(tiles):
      _stage_keys(key_view(tl), slice(0, LANES), _tile(k_ref, tl), _tile(kc_ref, tl))
      if with_pe:
        _stage_keys(key_view(tl), slice(LANES, 2 * LANES), kp_ref, kpc_ref)
    _stage_keys(vb_s, slice(None), v_ref, vc_ref)

  lane = lax.broadcasted_iota(jnp.int32, (q_ref.shape[0], LANES), 1)
  halves = (lane < NA_HD, lane >= NA_HD)
  for tl in range(tiles):
    q = _tile(q_ref, tl)[...] * scale
    qs = [jnp.where(hm, q, 0.0).astype(BF16) for hm in halves]
    if with_pe:
      tile_id = tl if tiles > 1 else pl.program_id(1)
      qp = (_tile(qp_ref, tl // 2) if tiles > 1 else qp_ref)[...] * scale
      base = (tile_id % 2) * (2 * MLA_ROPE)
      qs = [jnp.concatenate(
          [qs[a], jnp.where((lane >= base + a * MLA_ROPE) & (lane < base + (a + 1) * MLA_ROPE),
                            qp, 0.0).astype(BF16)], axis=1) for a in range(2)]
    _tile(o_ref, tl)[...] = jnp.where(
        halves[0], *_stacked_attention(qs, key_view(tl), _tile(vb_s, tl)))


def _pair_attention(q, k, v, lat, cache=None, scale=1.0, q_pe=None, k_pe=None, kv_joint=True):
  grid, qspec, kspec, cspec, ospec, n_keys, oshape, tiles = _attn_specs(q, lat, kv_joint)
  view = lambda a: _pass_view(a, lat)[0]
  b0 = _pass_view(q, lat)[1]
  if not kv_joint:
    k, v = (a.reshape(oshape) for a in (k, v))
  kv_view = view if kv_joint else (lambda a: a)
  with_pe = q_pe is not None
  tq, sq = qspec.block_shape[1], kspec.block_shape[1]
  in_specs, args = [qspec], [view(q)]
  if with_pe:
    if tiles == 1:
      in_specs.append(pl.BlockSpec((None, tq, LANES), lambda bi, h, qi: (bi + b0, qi, h // 2)))
    else:
      in_specs.append(pl.BlockSpec((None, tq, q_pe.shape[-1]), lambda bi, h, qi: (bi + b0, qi, 0)))
    args.append(view(q_pe))
  in_specs.append(kspec)
  args.append(kv_view(k))
  if with_pe:
    in_specs.append(pl.BlockSpec((None, sq, LANES), lambda bi, h, qi: (bi + b0, 0, 0)))
    args.append(view(k_pe))
  in_specs.append(kspec)
  args.append(kv_view(v))
  if lat:
    in_specs.append(cspec)
    args.append(cache[0])
    if with_pe:
      in_specs.append(pl.BlockSpec((None, PAST_LEN, LANES), lambda bi, h, qi: (bi, 0, 0)))
      args.append(cache[1])
    in_specs.append(cspec)
    args.append(cache[-1])
  return pl.pallas_call(
      functools.partial(_pair_attn_body, scale=scale, with_pe=with_pe, cached=lat),
      grid=grid, in_specs=in_specs, out_specs=ospec,
      out_shape=jax.ShapeDtypeStruct(oshape, F32),
      scratch_shapes=[pltpu.VMEM((n_keys, tiles * (2 * LANES if with_pe else LANES)), BF16),
                      pltpu.VMEM((n_keys, tiles * LANES), BF16)],
      compiler_params=_cparams(("parallel", "parallel", "arbitrary")),
      name="pair_attention_pe" if with_pe else "pair_attention",
  )(*args)


def _na_lat_body(q_ref, k_ref, v_ref, kc_ref, vc_ref, bias_ref, o_ref, kb_s, vb_s, kcb_s, vcb_s):
  rb = pl.program_id(2)

  @pl.when(rb == 0)
  def _():
    kb_s[...] = k_ref[...].astype(BF16)
    vb_s[...] = v_ref[...].astype(BF16)
    kcb_s[...] = kc_ref[...].astype(BF16)
    vcb_s[...] = vc_ref[...].astype(BF16)

  n_loc = NA_KH * GRID_W
  lane = lax.broadcasted_iota(jnp.int32, (GRID_W, LANES), 1)
  halves = (lane < NA_HD, lane >= NA_HD)
  kc = kcb_s[...]
  vc = vcb_s[...]
  for rr in range(NA_RB):
    r = rb * NA_RB + rr
    start = jnp.clip(r - NA_KH // 2, 0, GRID_R - NA_KH)
    pat = jnp.where(r < NA_KH // 2, 1 + r,
                    jnp.where(r > GRID_R - NA_KH // 2, r - (GRID_R - NA_KH), 0))
    off = pl.multiple_of(start * GRID_W, GRID_W)
    kw = kb_s[pl.ds(off, n_loc), :]
    vw = vb_s[pl.ds(off, n_loc), :]
    q = q_ref[rr * GRID_W:(rr + 1) * GRID_W, :]
    qq = jnp.concatenate([jnp.where(hm, q, 0.0).astype(BF16) for hm in halves], axis=0)
    bias = bias_ref[pat].reshape(2 * GRID_W, n_loc)
    s_loc = lax.dot_general(qq, kw, _NT, preferred_element_type=F32) + bias
    s_ctx = lax.dot_general(qq, kc, _NT, preferred_element_type=F32)
    m = jnp.maximum(jnp.max(s_loc, axis=-1, keepdims=True),
                    jnp.max(s_ctx, axis=-1, keepdims=True))
    p_loc = jnp.exp(s_loc - m)
    p_ctx = jnp.exp(s_ctx - m)
    l = jnp.sum(p_loc, axis=-1, keepdims=True) + jnp.sum(p_ctx, axis=-1, keepdims=True)
    o = (jnp.dot(p_loc.astype(BF16), vw, preferred_element_type=F32)
         + jnp.dot(p_ctx.astype(BF16), vc, preferred_element_type=F32)) / l
    o_ref[rr * GRID_W:(rr + 1) * GRID_W, :] = jnp.where(halves[0], o[:GRID_W], o[GRID_W:])


def _na_lat_attention(q, k, v, kc, vc, bias):
  qv, b0, b = _pass_view(q, True)
  kv, vv = (a.reshape(b, DEC_SEQ, D_MODEL) for a in (k, v))
  n_pat = bias.shape[0]
  blk = NA_RB * GRID_W
  qspec = pl.BlockSpec((None, blk, LANES), lambda bi, h, r: (bi + b0, r, h))
  kspec = pl.BlockSpec((None, DEC_SEQ, LANES), lambda bi, h, r: (bi, 0, h))
  cspec = pl.BlockSpec((None, PAST_LEN, LANES), lambda bi, h, r: (bi, 0, h))
  bspec = pl.BlockSpec((n_pat, 2, GRID_W, NA_KH * GRID_W), lambda bi, h, r: (0, h, 0, 0))
  return pl.pallas_call(
      _na_lat_body,
      grid=(b, D_MODEL // LANES, GRID_R // NA_RB),
      in_specs=[qspec, kspec, kspec, cspec, cspec, bspec],
      out_specs=pl.BlockSpec((None, blk, LANES), lambda bi, h, r: (bi, r, h)),
      out_shape=jax.ShapeDtypeStruct((b, DEC_SEQ, D_MODEL), F32),
      scratch_shapes=[pltpu.VMEM((DEC_SEQ, LANES), BF16)] * 2
      + [pltpu.VMEM((PAST_LEN, LANES), BF16)] * 2,
      compiler_params=_cparams(("parallel", "parallel", "arbitrary")),
      name="na_lat_attention",
  )(qv, kv, vv, kc, vc, bias)


def _sort_network(n):
  pairs = []
  p = 1
  while p < n:
    k = p
    while k >= 1:
      for j in range(k % p, n - k, 2 * k):
        for i in range(min(k, n - j - k)):
          if (i + j) // (2 * p) == (i + j + k) // (2 * p):
            pairs.append((i + j, i + j + k))
      k //= 2
    p *= 2
  return pairs


def _pop_heads(lists, hit, depth):
  for k in range(depth):
    lists[k] = jnp.where(hit, lists[k + 1], lists[k])


def _top16_sorted(s):
  t = s.shape[1]
  n = s.shape[0] // SUBLANES
  xs = [s[SUBLANES * k:SUBLANES * (k + 1), :] for k in range(n)]
  for i, j in _sort_network(n):
    xs[i], xs[j] = jnp.maximum(xs[i], xs[j]), jnp.minimum(xs[i], xs[j])
  rows = lax.broadcasted_iota(jnp.int32, (PEER_TOPK, t), 0)
  v = jnp.zeros((PEER_TOPK, t), F32)
  for r in range(PEER_TOPK):
    m = jnp.max(xs[0], axis=0, keepdims=True)
    v = jnp.where(rows == r, m, v)
    _pop_heads(xs, xs[0] == m, PEER_TOPK - 1 - r)
  return v


def _router_body(oc_ref, ol_ref, wo_ref, x_ref, g1_ref, g_ref, sc_ref, sh_ref,
                 wq_ref, k1_ref, k2_ref,
                 xn_ref, hb_ref, a1_ref, n1_ref, b2_ref, r2_ref, h_s):
  def first_head(o_ref):
    xn = x_ref[...] + g1_ref[...] * jnp.dot(o_ref[...].astype(BF16), wo_ref[...],
                                            preferred_element_type=F32)
    xn_ref[...] = xn
    h = _rms_mod(xn, g_ref[...], sc_ref[...], sh_ref[...]).astype(BF16)
    h_s[...] = h
    hb_ref[...] = pltpu.bitcast(h, jnp.uint32)

  is_first = pl.program_id(1) == 0
  is_ctx = pl.program_id(0) < N_CTX // ROUTER_T
  pl.when(is_first & is_ctx)(lambda: first_head(oc_ref))
  pl.when(is_first & jnp.logical_not(is_ctx))(lambda: first_head(ol_ref))

  qh = jnp.dot(h_s[...], wq_ref[...], preferred_element_type=F32)
  half = PEER_DK // 2
  s1 = lax.dot_general(k1_ref[...], qh[:, :half].astype(BF16), _NT, preferred_element_type=F32)
  s2 = lax.dot_general(k2_ref[...], qh[:, half:].astype(BF16), _NT, preferred_element_type=F32)
  s1 = s1 - jnp.max(s1, axis=0, keepdims=True)
  s2 = s2 - jnp.max(s2, axis=0, keepdims=True)
  v1 = _top16_sorted(s1)
  v2 = _top16_sorted(s2)
  cand = [v1[:SUBLANES] + v2[k:k + 1] for k in range(PEER_TOPK)]
  tail = v1[SUBLANES:] + v2[0:1]
  z = jnp.zeros_like(v2[0:1])
  taken = jnp.zeros_like(cand[0])
  for r in range(PEER_TOPK):
    th = jnp.max(jnp.maximum(cand[0], tail), axis=0, keepdims=True)
    z = z + jnp.exp(th)
    tail = jnp.where(tail == th, NEG_INF, tail)
    hit = cand[0] == th
    taken = taken + jnp.where(hit, 1.0, 0.0)
    _pop_heads(cand, hit, PEER_TOPK - 1 - r)
  inf = jnp.float32(jnp.inf)
  first = jnp.minimum(jnp.min(jnp.where(taken >= 1.0, v1[:SUBLANES], inf), axis=0, keepdims=True),
                      jnp.min(jnp.where(tail == NEG_INF, v1[SUBLANES:], inf), axis=0,
                              keepdims=True))
  n1 = jnp.where(s1 >= first, 1.0, 0.0)
  k_max = PEER_TOPK // 2
  for k in range(2, k_max + 1):
    t_k = jnp.min(jnp.where(taken >= float(k), v1[:SUBLANES], inf), axis=0, keepdims=True)
    n1 = n1 + jnp.where(s1 >= t_k, 1.0, 0.0)
  n1 = n1 + jnp.where(s1 >= v1[0:1], jnp.maximum(taken[0:1] - float(k_max), 0.0), 0.0)
  r2 = jnp.zeros_like(s2)
  for r in range(PEER_TOPK):
    r2 = r2 + jnp.where(v2[r:r + 1] > s2, 1.0, 0.0)
  a1_ref[...] = jnp.exp(s1)
  n1_ref[...] = n1
  b2_ref[...] = pltpu.bitcast((jnp.exp(s2) * (1.0 / z)).astype(BF16), jnp.uint32)
  r2_ref[...] = pltpu.bitcast(r2.astype(BF16), jnp.uint32)


def _peer_router(oc, ol, w_o, x, gate1, g, sc, sh, wq, k1, k2):
  n = x.shape[0]
  tm = ROUTER_T
  nb_ctx = N_CTX // tm
  row = lambda w: pl.BlockSpec((tm, w), lambda i, h: (i, 0))
  full = lambda shape: pl.BlockSpec(shape, lambda i, h: (0,) * len(shape))
  modspec = pl.BlockSpec((None, 1, D_MODEL), lambda i, h: (_mod_row(i, tm), 0, 0))
  keyspec = pl.BlockSpec((None, PEER_NKEYS, PEER_DK // 2), lambda i, h: (h, 0, 0))
  tspec = pl.BlockSpec((None, PEER_NKEYS, tm), lambda i, h: (h, 0, i))
  pspec = pl.BlockSpec((None, PEER_NKEYS // 2, tm), lambda i, h: (h, 0, i))
  return pl.pallas_call(
      _router_body,
      grid=(n // tm, PEER_HEADS),
      in_specs=[pl.BlockSpec((tm, D_MODEL), lambda i, h: (jnp.minimum(i, nb_ctx - 1), 0)),
                pl.BlockSpec((tm, D_MODEL), lambda i, h: (jnp.maximum(i - nb_ctx, 0), 0)),
                full(w_o.shape), row(D_MODEL), modspec,
                full((1, D_MODEL)), modspec, modspec,
                pl.BlockSpec((D_MODEL, PEER_DK), lambda i, h: (0, h)), keyspec, keyspec],
      out_specs=[row(D_MODEL), pl.BlockSpec((tm // 2, D_MODEL), lambda i, h: (i, 0)),
                 tspec, tspec, pspec, pspec],
      out_shape=[jax.ShapeDtypeStruct((n, D_MODEL), F32),
                 jax.ShapeDtypeStruct((n // 2, D_MODEL), jnp.uint32)]
      + [jax.ShapeDtypeStruct((PEER_HEADS, PEER_NKEYS, n), F32)] * 2
      + [jax.ShapeDtypeStruct((PEER_HEADS, PEER_NKEYS // 2, n), jnp.uint32)] * 2,
      scratch_shapes=[pltpu.VMEM((tm, D_MODEL), BF16)],
      compiler_params=_cparams(("parallel", "arbitrary")),
      name="peer_router",
  )(oc.reshape(N_CTX, D_MODEL), ol.reshape(N_LAT, D_MODEL), w_o, x, gate1, g, sc, sh, wq, k1, k2)


def _gelu(x):
  return 0.5 * x * (1.0 + lax.erf(x * np.float32(math.sqrt(0.5))))


def _peer_mix_body(hb_ref, u_ref, vt_ref, a1_ref, n1_ref, b2_ref, r2_ref, x_ref, gate_ref,
                   fg_ref, y_ref, *scratch, final_norm):
  c = pl.program_id(1)
  n_pieces = PEER_EC // PEER_SUBC
  if final_norm:
    yl_ref, acc_s, *piece_s = scratch
  else:
    acc_s, *piece_s = scratch

  @pl.when(c == 0)
  def _():
    acc_s[...] = jnp.zeros_like(acc_s)

  hb = pltpu.bitcast(hb_ref[...], BF16)
  t = hb.shape[0]
  pk = BF16_ROWS
  for j in range(n_pieces):
    rows = slice(j * PEER_SUBC, (j + 1) * PEER_SUBC)
    piece_s[j][...] = lax.dot_general(u_ref[rows, :], hb, _NT, preferred_element_type=F32)
  for j in range(n_pieces):
    s_s, p_s = piece_s[j], piece_s[n_pieces + j]
    for ii in range(PEER_SUBC // PEER_NKEYS):
      i1 = j * (PEER_SUBC // PEER_NKEYS) + ii
      w = [jnp.zeros((pk, t), BF16) for _ in range(PEER_NKEYS // pk)]
      for h in range(PEER_HEADS):
        a_row = jnp.broadcast_to(a1_ref[h, i1:i1 + 1, :], (pk, t)).astype(BF16)
        n_row = jnp.broadcast_to(n1_ref[h, i1:i1 + 1, :], (pk, t)).astype(BF16)
        for sub in range(PEER_NKEYS // pk):
          words = slice(sub * pk // 2, (sub + 1) * pk // 2)
          prod = a_row * pltpu.bitcast(b2_ref[h, words, :], BF16)
          rank = pltpu.bitcast(r2_ref[h, words, :], BF16)
          w[sub] = w[sub] + jnp.where(rank < n_row, prod, jnp.zeros_like(prod))
      for sub in range(PEER_NKEYS // pk):
        row0 = ii * PEER_NKEYS + sub * pk
        p_s[row0:row0 + pk, :] = w[sub] * _gelu(s_s[row0:row0 + pk, :]).astype(BF16)
    rows = slice(j * PEER_SUBC, (j + 1) * PEER_SUBC)
    acc_s[...] += jnp.dot(vt_ref[:, rows], p_s[...], preferred_element_type=F32)

  @pl.when(c == pl.num_programs(1) - 1)
  def _():
    y = x_ref[...] + gate_ref[...] * acc_s[...].T
    if final_norm:
      _store_by_pass(pl.program_id(0), t, _rms(y, fg_ref[...]), y_ref, yl_ref)
    else:
      y_ref[...] = y


def _peer_mix(hb, u, vt, a1, n1, b2, r2, x, gate, final_g, final_norm):
  n = x.shape[0]
  t = PEER_T
  n_i1 = PEER_EC // PEER_NKEYS
  row = lambda w: pl.BlockSpec((t, w), lambda i, c: (i, 0))
  i1spec = pl.BlockSpec((PEER_HEADS, n_i1, t), lambda i, c: (0, c, i))
  i2spec = pl.BlockSpec((PEER_HEADS, PEER_NKEYS // 2, t), lambda i, c: (0, 0, i))
  return pl.pallas_call(
      functools.partial(_peer_mix_body, final_norm=final_norm),
      grid=(n // t, PEER_EXPERTS // PEER_EC),
      in_specs=[pl.BlockSpec((t // 2, D_MODEL), lambda i, c: (i, 0)),
                pl.BlockSpec((PEER_EC, D_MODEL), lambda i, c: (c, 0)),
                pl.BlockSpec((None, D_MODEL, PEER_EC), lambda i, c: (c, 0, 0)),
                i1spec, i1spec, i2spec, i2spec,
                row(D_MODEL),
                pl.BlockSpec((None, 1, D_MODEL), lambda i, c: (_mod_row(i, t), 0, 0)),
                pl.BlockSpec((1, D_MODEL), lambda i, c: (0, 0))],
      out_specs=_pass_specs(D_MODEL, t) if final_norm else row(D_MODEL),
      out_shape=_pass_shapes(D_MODEL) if final_norm else jax.ShapeDtypeStruct((n, D_MODEL), F32),
      scratch_shapes=[pltpu.VMEM((D_MODEL, t), F32)]
      + [pltpu.VMEM((PEER_SUBC, t), F32)] * (PEER_EC // PEER_SUBC)
      + [pltpu.VMEM((PEER_SUBC, t), BF16)] * (PEER_EC // PEER_SUBC),
      compiler_params=_cparams(("arbitrary", "arbitrary"), VMEM_LIMIT_PEER),
      name="peer_mix",
  )(hb, u, vt, a1, n1, b2, r2, x, gate, final_g)


def _rotated_tiles(w, group):
  k, n = w.shape
  g = w.reshape(k, n // group, 2, group // 2)
  return jnp.concatenate([-g[:, :, 1:], g[:, :, :1]], axis=2).reshape(k, n)


def _rope_table(rot_dims, reps):
  half = rot_dims // 2
  t = jnp.arange(DEC_SEQ)
  inv = ROPE_BASE ** (-jnp.arange(half, dtype=F32) / half)
  parts_c, parts_s = [], []
  for pos in (t // GRID_W, t % GRID_W):
    ang = pos.astype(F32)[:, None] * inv[None, :]
    parts_c += [jnp.cos(ang), jnp.cos(ang)]
    parts_s += [jnp.sin(ang), jnp.sin(ang)]
  cos = jnp.tile(jnp.concatenate(parts_c, axis=1), (1, reps))
  sin = jnp.tile(jnp.concatenate(parts_s, axis=1), (1, reps))
  cos = jnp.concatenate([cos, jnp.ones((TM, LANES), F32)], axis=0)
  sin = jnp.concatenate([sin, jnp.zeros((TM, LANES), F32)], axis=0)
  return cos, sin


def _na_bias_table(rpb):
  reps = [NA_KH // 2] + list(range(NA_KH // 2)) + list(range(GRID_R - NA_KH // 2 + 1, GRID_R))
  cols = np.arange(GRID_W)
  col_start = np.clip(cols - NA_KW // 2, 0, GRID_W - NA_KW)
  col_mask = (cols[None, :] >= col_start[:, None]) & (cols[None, :] < col_start[:, None] + NA_KW)
  dc = np.clip(cols[None, :] - cols[:, None], -(NA_KW - 1), NA_KW - 1) + (NA_KW - 1)
  dr = np.stack([np.clip(r - NA_KH // 2, 0, GRID_R - NA_KH) + np.arange(NA_KH) - r + (NA_KH - 1)
                 for r in reps])
  onehot = (dc[:, :, None] == np.arange(2 * NA_KW - 1)).astype(np.float32)
  bias = jnp.einsum('hpjc,qwc->phqjw', rpb[:, dr, :], onehot, precision=lax.Precision.HIGHEST)
  bias = jnp.where(col_mask[None, None, :, None, :], bias, NEG_INF)
  return bias.reshape(len(reps), NA_HEADS, GRID_W, NA_KH * GRID_W)


def _ctx_rows(a, *shape):
  return a[:N_CTX].reshape(BATCH, SEQ, *shape)


def _diff_mixer(layer, p, x, sc, sh, cache, tables):
  w = p['w_qkv']
  d = D_MODEL
  wq, wk, wv = w[:, :d], w[:, d:2 * d], w[:, 2 * d:]
  rot = lambda m: _rotated_tiles(m, DA_QK // 2)
  w5 = jnp.concatenate([wq, rot(wq), wk, rot(wk), wv], axis=1).astype(BF16)
  cos, sin = tables['da']
  q, (kc, kl), (vc, vl) = _qkv_rope(x, p['norm1_g'], sc, sh, w5, cos, sin, DA_QK ** -0.5)
  lam_rows = jnp.zeros((SUBLANES, LANES), F32)
  for i, nme in enumerate(('lam_q1', 'lam_k1', 'lam_q2', 'lam_k2')):
    lam_rows = lam_rows.at[i, :DA_QK].set(p[nme])
  g = p['subln_g'].reshape(1, DA_V)
  oc = _diff_attention(q, kc, vc, None, lam_rows, g, layer, lat=False)
  ol = _diff_attention(q, kl, vl, cache, lam_rows, g, layer, lat=True)
  state = (kc.reshape(BATCH, SEQ, DA_HEADS, 2 * DA_QK), vc.reshape(BATCH, SEQ, DA_HEADS, DA_V))
  return oc, ol, state


def _na_mixer(p, x, sc, sh, cache):
  d = D_MODEL
  q, (kc, kl), (vc, vl) = _qkv_plain(x, p['norm1_g'], sc, sh, p['w_qkv'].astype(BF16),
                                     NA_HD ** -0.5)
  oc = _pair_attention(q, kc, vc, lat=False, kv_joint=False)
  bias = _na_bias_table(p['rpb'])
  ol = _na_lat_attention(q, kl, vl, cache[0].reshape(DEC_BATCH, PAST_LEN, d),
                         cache[1].reshape(DEC_BATCH, PAST_LEN, d), bias)
  state = (kc.reshape(BATCH, SEQ, NA_HEADS, NA_HD), vc.reshape(BATCH, SEQ, NA_HEADS, NA_HD))
  return oc, ol, state


def _mla_mixer(p, x, sc, sh, cache, tables):
  a, b = MLA_Q_RANK, MLA_Q_RANK + MLA_KV_RANK
  w_in = p['w_in']
  kpe_w = w_in[:, b:]
  w_in_x = jnp.concatenate([w_in[:, :b], jnp.tile(kpe_w, (1, 4)),
                            jnp.tile(_rotated_tiles(kpe_w, MLA_ROPE // 2), (1, 4))],
                           axis=1).astype(BF16)
  cos, sin = tables['mla']
  cq, ckv, kpe = _mla_in(x, p['norm1_g'], sc, sh, w_in_x, p['q_norm_g'].reshape(1, a),
                         p['kv_norm_g'].reshape(1, MLA_KV_RANK), cos, sin)
  w_uq = p['w_uq'].reshape(a, MLA_HEADS, MLA_NOPE + MLA_ROPE)
  w_qn = w_uq[:, :, :MLA_NOPE].reshape(a, -1)
  w_qp = w_uq[:, :, MLA_NOPE:].reshape(a, -1)
  w_uq_x = jnp.concatenate([w_qn, w_qp, _rotated_tiles(w_qp, MLA_ROPE // 2)], axis=1).astype(BF16)
  qn, qp = _mla_q(cq, w_uq_x, cos, sin)
  w_ukv = p['w_ukv'].reshape(MLA_KV_RANK, MLA_HEADS, MLA_NOPE + MLA_V)
  w_ukv_x = jnp.concatenate([w_ukv[:, :, :MLA_NOPE].reshape(MLA_KV_RANK, -1),
                             w_ukv[:, :, MLA_NOPE:].reshape(MLA_KV_RANK, -1)], axis=1).astype(BF16)
  kn, v = _mla_kv(ckv, w_ukv_x)
  knc, vc = _mla_kv(cache[0].reshape(DEC_BATCH * PAST_LEN, MLA_KV_RANK), w_ukv_x)
  cached = (knc.reshape(DEC_BATCH, PAST_LEN, D_MODEL), jnp.tile(cache[1], (1, 1, LANES // MLA_ROPE)),
            vc.reshape(DEC_BATCH, PAST_LEN, D_MODEL))
  scale = (MLA_NOPE + MLA_ROPE) ** -0.5
  oc = _pair_attention(qn, kn, v, lat=False, scale=scale, q_pe=qp, k_pe=kpe)
  ol = _pair_attention(qn, kn, v, lat=True, cache=cached, scale=scale, q_pe=qp, k_pe=kpe)
  state = (_ctx_rows(ckv, MLA_KV_RANK), _ctx_rows(kpe[:, :MLA_ROPE], MLA_ROPE))
  return oc, ol, state


def _layer(layer, p, x, cond8, cache, tables, final_g):
  sh1, sc1, g1, sh2, sc2, g2 = _modulation(cond8, p['ada_w'], p['ada_b'])
  kind = layer % 3
  if kind == 0:
    oc, ol, state = _diff_mixer(layer, p, x, sc1, sh1, cache, tables)
  elif kind == 1:
    oc, ol, state = _na_mixer(p, x, sc1, sh1, cache)
  else:
    oc, ol, state = _mla_mixer(p, x, sc1, sh1, cache, tables)
  x, hb, a1, n1, b2, r2 = _peer_router(oc, ol, p['w_o'].astype(BF16), x, g1, p['norm2_g'],
                                       sc2, sh2, p['peer_wq'], p['peer_k1'], p['peer_k2'])
  x = _peer_mix(hb, p['peer_u'], p['peer_vt'], a1, n1, b2, r2, x, g2, final_g,
                final_norm=layer == DEPTH - 1)
  return x, state


def kernel(x_prompt, x_sample, cache_l0_k, cache_l0_v, cache_l1_k, cache_l1_v, cache_l2_ckv, cache_l2_kpe, cache_l3_k, cache_l3_v, c, c_ctx, l0_norm1_g, l0_norm2_g, l0_ada_w, l0_ada_b, l0_w_qkv, l0_w_o, l0_lam_q1, l0_lam_k1, l0_lam_q2, l0_lam_k2, l0_subln_g, l0_peer_wq, l0_peer_k1, l0_peer_k2, l0_peer_u, l0_peer_v, l1_norm1_g, l1_norm2_g, l1_ada_w, l1_ada_b, l1_w_qkv, l1_w_o, l1_rpb, l1_peer_wq, l1_peer_k1, l1_peer_k2, l1_peer_u, l1_peer_v, l2_norm1_g, l2_norm2_g, l2_ada_w, l2_ada_b, l2_w_in, l2_q_norm_g, l2_w_uq, l2_kv_norm_g, l2_w_ukv, l2_w_o, l2_peer_wq, l2_peer_k1, l2_peer_k2, l2_peer_u, l2_peer_v, l3_norm1_g, l3_norm2_g, l3_ada_w, l3_ada_b, l3_w_qkv, l3_w_o, l3_lam_q1, l3_lam_k1, l3_lam_q2, l3_lam_k2, l3_subln_g, l3_peer_wq, l3_peer_k1, l3_peer_k2, l3_peer_u, l3_peer_v, final_norm_g):
  common = lambda n1, n2, aw, ab, wq, k1, k2, u, v: dict(
      norm1_g=n1.reshape(1, D_MODEL), norm2_g=n2.reshape(1, D_MODEL), ada_w=aw, ada_b=ab,
      peer_wq=wq.astype(BF16), peer_k1=k1.astype(BF16), peer_k2=k2.astype(BF16),
      peer_u=u.astype(BF16),
      peer_vt=v.reshape(PEER_EXPERTS // PEER_EC, PEER_EC, D_MODEL).transpose(0, 2, 1).astype(BF16))
  p0 = dict(common(l0_norm1_g, l0_norm2_g, l0_ada_w, l0_ada_b, l0_peer_wq, l0_peer_k1, l0_peer_k2,
                   l0_peer_u, l0_peer_v),
            w_qkv=l0_w_qkv, w_o=l0_w_o, lam_q1=l0_lam_q1, lam_k1=l0_lam_k1, lam_q2=l0_lam_q2,
            lam_k2=l0_lam_k2, subln_g=l0_subln_g)
  p1 = dict(common(l1_norm1_g, l1_norm2_g, l1_ada_w, l1_ada_b, l1_peer_wq, l1_peer_k1, l1_peer_k2,
                   l1_peer_u, l1_peer_v),
            w_qkv=l1_w_qkv, w_o=l1_w_o, rpb=l1_rpb)
  p2 = dict(common(l2_norm1_g, l2_norm2_g, l2_ada_w, l2_ada_b, l2_peer_wq, l2_peer_k1, l2_peer_k2,
                   l2_peer_u, l2_peer_v),
            w_in=l2_w_in, q_norm_g=l2_q_norm_g, w_uq=l2_w_uq, kv_norm_g=l2_kv_norm_g,
            w_ukv=l2_w_ukv, w_o=l2_w_o)
  p3 = dict(common(l3_norm1_g, l3_norm2_g, l3_ada_w, l3_ada_b, l3_peer_wq, l3_peer_k1, l3_peer_k2,
                   l3_peer_u, l3_peer_v),
            w_qkv=l3_w_qkv, w_o=l3_w_o, lam_q1=l3_lam_q1, lam_k1=l3_lam_k1, lam_q2=l3_lam_q2,
            lam_k2=l3_lam_k2, subln_g=l3_subln_g)
  params = (p0, p1, p2, p3)
  caches = ((cache_l0_k, cache_l0_v), (cache_l1_k, cache_l1_v),
            (cache_l2_ckv, cache_l2_kpe), (cache_l3_k, cache_l3_v))
  tables = dict(da=_rope_table(DA_QK // 2, LANES // DA_QK),
                mla=_rope_table(MLA_ROPE // 2, LANES // MLA_ROPE))

  cond8 = jnp.zeros((N_COND, D_MODEL), F32).at[0].set(c_ctx).at[1:1 + DEC_BATCH].set(c)
  x = jnp.concatenate([x_prompt.reshape(N_CTX, D_MODEL), x_sample.reshape(N_LAT, D_MODEL)], axis=0)
  states = []
  for layer in range(DEPTH):
    x, st = _layer(layer, params[layer], x, cond8, caches[layer], tables,
                   final_norm_g.reshape(1, D_MODEL))
    states.extend(st)
  y_ctx, y_lat = x
  return (y_ctx.reshape(BATCH, SEQ, D_MODEL), y_lat.reshape(DEC_BATCH, DEC_SEQ, D_MODEL), *states)
```

```python
import functools
import math

import numpy as np
import jax
import jax.numpy as jnp
from jax import lax
from jax.experimental import pallas as pl
from jax.experimental.pallas import tpu as pltpu

F32 = jnp.float32
BF16 = jnp.bfloat16

D_MODEL = 1024
BATCH = 32
SEQ = 256
DEPTH = 4
DEC_BATCH = 2
DEC_SEQ = 4096
PAST_LEN = 512
GRID_W = 64
GRID_R = DEC_SEQ // GRID_W
EPS = 1e-6
ROPE_BASE = 10000.0
NEG_INF = -1e30

DA_HEADS = 8
DA_QK = 64
DA_V = 128
NA_HEADS = 16
NA_HD = 64
NA_KH = 8
NA_KW = 16
MLA_HEADS = 16
MLA_NOPE = 64
MLA_ROPE = 32
MLA_V = 64
MLA_Q_RANK = 384
MLA_KV_RANK = 256
PEER_HEADS = 8
PEER_NKEYS = 128
PEER_EXPERTS = PEER_NKEYS * PEER_NKEYS
PEER_DK = 256
PEER_TOPK = 16

N_CTX = BATCH * SEQ
N_LAT = DEC_BATCH * DEC_SEQ
N_TOK = N_CTX + N_LAT
N_COND = 8

LANES = 128
SUBLANES = 8
BF16_ROWS = 16
VMEM_LIMIT = 48 << 20
VMEM_LIMIT_PEER = 56 << 20

TM = 256
MOD_TN = 1536
TQ = 512
TK = 1536
ATTN_UNROLL = 3
NA_RB = 8
ROUTER_T = 1024
ROUTER_HEADS = 2
PEER_T = 512
PEER_EC = 2048
PEER_SUBC = 256

_NT = (((1,), (1,)), ((), ()))


def _cparams(sem, vmem=VMEM_LIMIT):
  return pltpu.CompilerParams(dimension_semantics=sem, vmem_limit_bytes=vmem)


def _mod_row(i, tm):
  nb_ctx = N_CTX // tm
  nb_bat = DEC_SEQ // tm
  return jnp.where(i < nb_ctx, 0, 1 + (i - nb_ctx) // nb_bat)


def _rope_blk(i, tm):
  nb_ctx = N_CTX // tm
  nb_bat = DEC_SEQ // tm
  return jnp.where(i < nb_ctx, nb_bat, (i - nb_ctx) % nb_bat)


def _rms_mod(x, g, sc, sh):
  y = x * lax.rsqrt(jnp.mean(x * x, axis=-1, keepdims=True) + EPS)
  return (y * g) * (1.0 + sc) + sh


def _rms(x, g):
  return x * lax.rsqrt(jnp.mean(x * x, axis=-1, keepdims=True) + EPS) * g


def _row_spec(width, tm=TM):
  return pl.BlockSpec((tm, width), lambda i: (i, 0))


def _full_spec(shape):
  return pl.BlockSpec(shape, lambda i: (0,) * len(shape))


def _mod_spec(tm=TM):
  return pl.BlockSpec((None, 1, D_MODEL), lambda i: (_mod_row(i, tm), 0, 0))


def _pass_specs(width, tm, index=lambda i, *_: i):
  nb = N_CTX // tm
  return [pl.BlockSpec((tm, width), lambda *g: (jnp.minimum(index(*g), nb - 1), 0)),
          pl.BlockSpec((tm, width), lambda *g: (jnp.maximum(index(*g) - nb, 0), 0))]


def _pass_shapes(width, dtype=F32):
  return [jax.ShapeDtypeStruct((N_CTX, width), dtype), jax.ShapeDtypeStruct((N_LAT, width), dtype)]


def _store_by_pass(i, tm, val, ctx_ref, lat_ref):
  is_ctx = i < N_CTX // tm

  @pl.when(is_ctx)
  def _():
    ctx_ref[...] = val

  @pl.when(jnp.logical_not(is_ctx))
  def _():
    lat_ref[...] = val


def _mod_body(c_ref, w_ref, b_ref, o_ref):
  c = c_ref[...]
  s = c / (1.0 + jnp.exp(-c))
  o_ref[...] = jnp.dot(s, w_ref[...], precision=lax.Precision.HIGHEST,
                       preferred_element_type=F32) + b_ref[...]


def _modulation(cond8, ada_w, ada_b):
  n = ada_w.shape[1]
  tn = MOD_TN
  out = pl.pallas_call(
      _mod_body,
      grid=(n // tn,),
      in_specs=[pl.BlockSpec((N_COND, D_MODEL), lambda j: (0, 0)),
                pl.BlockSpec((D_MODEL, tn), lambda j: (0, j)),
                pl.BlockSpec((1, tn), lambda j: (0, j))],
      out_specs=pl.BlockSpec((N_COND, tn), lambda j: (0, j)),
      out_shape=jax.ShapeDtypeStruct((N_COND, n), F32),
      compiler_params=_cparams(("parallel",)),
      name="modulation",
  )(cond8, ada_w, ada_b.reshape(1, n))
  return [out[:, k * D_MODEL:(k + 1) * D_MODEL].reshape(N_COND, 1, D_MODEL)
          for k in range(n // D_MODEL)]


def _tile_lanes(t, reps):
  return jnp.concatenate([t] * reps, axis=1)


def _qkv_rope_body(x_ref, g_ref, sc_ref, sh_ref, w_ref, cos_ref, sin_ref,
                   q_ref, kc_ref, kl_ref, vc_ref, vl_ref, *, q_scale):
  h = _rms_mod(x_ref[...], g_ref[...], sc_ref[...], sh_ref[...]).astype(BF16)
  reps = D_MODEL // LANES
  cos = _tile_lanes(cos_ref[...], reps)
  sin = _tile_lanes(sin_ref[...], reps)
  d = D_MODEL
  dot = lambda a, b: jnp.dot(h, w_ref[:, a:b], preferred_element_type=F32)
  q_ref[...] = (dot(0, d) * cos + dot(d, 2 * d) * sin) * q_scale
  i = pl.program_id(0)
  _store_by_pass(i, TM, dot(2 * d, 3 * d) * cos + dot(3 * d, 4 * d) * sin, kc_ref, kl_ref)
  _store_by_pass(i, TM, dot(4 * d, 5 * d), vc_ref, vl_ref)


def _qkv_rope(x, g, sc, sh, w5, cos, sin, q_scale):
  n = x.shape[0]
  rope_spec = pl.BlockSpec((TM, LANES), lambda i: (_rope_blk(i, TM), 0))
  q, kc, kl, vc, vl = pl.pallas_call(
      functools.partial(_qkv_rope_body, q_scale=q_scale),
      grid=(n // TM,),
      in_specs=[_row_spec(D_MODEL), _full_spec((1, D_MODEL)), _mod_spec(), _mod_spec(),
                _full_spec(w5.shape), rope_spec, rope_spec],
      out_specs=[_row_spec(D_MODEL)] + _pass_specs(D_MODEL, TM) * 2,
      out_shape=[jax.ShapeDtypeStruct((n, D_MODEL), F32)] + _pass_shapes(D_MODEL) * 2,
      compiler_params=_cparams(("arbitrary",)),
      name="qkv_rope_proj",
  )(x, g, sc, sh, w5, cos, sin)
  return q, (kc, kl), (vc, vl)


def _qkv_plain_body(x_ref, g_ref, sc_ref, sh_ref, w_ref, q_ref, kc_ref, kl_ref, vc_ref, vl_ref,
                    *, q_scale):
  h = _rms_mod(x_ref[...], g_ref[...], sc_ref[...], sh_ref[...]).astype(BF16)
  d = D_MODEL
  dot = lambda a, b: jnp.dot(h, w_ref[:, a:b], preferred_element_type=F32)
  q_ref[...] = dot(0, d) * q_scale
  i = pl.program_id(0)
  _store_by_pass(i, TM, dot(d, 2 * d), kc_ref, kl_ref)
  _store_by_pass(i, TM, dot(2 * d, 3 * d), vc_ref, vl_ref)


def _qkv_plain(x, g, sc, sh, w3, q_scale):
  n = x.shape[0]
  q, kc, kl, vc, vl = pl.pallas_call(
      functools.partial(_qkv_plain_body, q_scale=q_scale),
      grid=(n // TM,),
      in_specs=[_row_spec(D_MODEL), _full_spec((1, D_MODEL)), _mod_spec(), _mod_spec(),
                _full_spec(w3.shape)],
      out_specs=[_row_spec(D_MODEL)] + _pass_specs(D_MODEL, TM) * 2,
      out_shape=[jax.ShapeDtypeStruct((n, D_MODEL), F32)] + _pass_shapes(D_MODEL) * 2,
      compiler_params=_cparams(("arbitrary",)),
      name="qkv_proj",
  )(x, g, sc, sh, w3)
  return q, (kc, kl), (vc, vl)


def _mla_in_body(x_ref, g_ref, sc_ref, sh_ref, w_ref, qg_ref, kvg_ref, cos_ref, sin_ref,
                 cq_ref, ckv_ref, kpe_ref):
  h = _rms_mod(x_ref[...], g_ref[...], sc_ref[...], sh_ref[...]).astype(BF16)
  z = jnp.dot(h, w_ref[...], preferred_element_type=F32)
  a, b = MLA_Q_RANK, MLA_Q_RANK + MLA_KV_RANK
  cq_ref[...] = _rms(z[:, :a], qg_ref[...]).astype(BF16)
  ckv_ref[...] = _rms(z[:, a:b], kvg_ref[...])
  kpe_ref[...] = z[:, b:b + LANES] * cos_ref[...] + z[:, b + LANES:] * sin_ref[...]


def _mla_in(x, g, sc, sh, w_in, qg, kvg, cos, sin):
  n = x.shape[0]
  rope_spec = pl.BlockSpec((TM, LANES), lambda i: (_rope_blk(i, TM), 0))
  return pl.pallas_call(
      _mla_in_body,
      grid=(n // TM,),
      in_specs=[_row_spec(D_MODEL), _full_spec((1, D_MODEL)), _mod_spec(), _mod_spec(),
                _full_spec(w_in.shape), _full_spec((1, MLA_Q_RANK)),
                _full_spec((1, MLA_KV_RANK)), rope_spec, rope_spec],
      out_specs=[_row_spec(MLA_Q_RANK), _row_spec(MLA_KV_RANK), _row_spec(LANES)],
      out_shape=[jax.ShapeDtypeStruct((n, MLA_Q_RANK), BF16),
                 jax.ShapeDtypeStruct((n, MLA_KV_RANK), F32),
                 jax.ShapeDtypeStruct((n, LANES), F32)],
      compiler_params=_cparams(("parallel",)),
      name="mla_in_proj",
  )(x, g, sc, sh, w_in, qg, kvg, cos, sin)


def _mla_q_body(cq_ref, w_ref, cos_ref, sin_ref, qn_ref, qp_ref):
  z = jnp.dot(cq_ref[...], w_ref[...], preferred_element_type=F32)
  pe = MLA_HEADS * MLA_ROPE
  reps = pe // LANES
  cos = _tile_lanes(cos_ref[...], reps)
  sin = _tile_lanes(sin_ref[...], reps)
  qn_ref[...] = z[:, :D_MODEL]
  qp_ref[...] = z[:, D_MODEL:D_MODEL + pe] * cos + z[:, D_MODEL + pe:] * sin


def _mla_q(cq, w_uq, cos, sin):
  n = cq.shape[0]
  pe = MLA_HEADS * MLA_ROPE
  rope_spec = pl.BlockSpec((TM, LANES), lambda i: (_rope_blk(i, TM), 0))
  return pl.pallas_call(
      _mla_q_body,
      grid=(n // TM,),
      in_specs=[_row_spec(MLA_Q_RANK), _full_spec(w_uq.shape), rope_spec, rope_spec],
      out_specs=[_row_spec(D_MODEL), _row_spec(pe)],
      out_shape=[jax.ShapeDtypeStruct((n, D_MODEL), F32), jax.ShapeDtypeStruct((n, pe), F32)],
      compiler_params=_cparams(("parallel",)),
      name="mla_q_proj",
  )(cq, w_uq, cos, sin)


def _mla_kv_body(c_ref, w_ref, kn_ref, v_ref):
  z = jnp.dot(c_ref[...].astype(BF16), w_ref[...], preferred_element_type=F32)
  kn_ref[...] = z[:, :D_MODEL]
  v_ref[...] = z[:, D_MODEL:]


def _mla_kv(ckv, w_ukv):
  n = ckv.shape[0]
  return pl.pallas_call(
      _mla_kv_body,
      grid=(n // TM,),
      in_specs=[_row_spec(MLA_KV_RANK), _full_spec(w_ukv.shape)],
      out_specs=[_row_spec(D_MODEL)] * 2,
      out_shape=[jax.ShapeDtypeStruct((n, D_MODEL), F32)] * 2,
      compiler_params=_cparams(("parallel",)),
      name="mla_kv_proj",
  )(ckv, w_ukv)


def _online_update(s, m, l, acc, vb):
  m_new = jnp.maximum(m, jnp.max(s, axis=-1, keepdims=True))
  alpha = jnp.exp(m - m_new)
  p = jnp.exp(s - m_new)
  l_new = alpha * l + jnp.sum(p, axis=-1, keepdims=True)
  acc_new = alpha * acc + jnp.dot(p.astype(BF16), vb, preferred_element_type=F32)
  return m_new, l_new, acc_new


def _softmax_state(tq):
  return (jnp.full((tq, 1), NEG_INF, F32), jnp.zeros((tq, 1), F32), jnp.zeros((tq, LANES), F32))


def _stacked_attention(q_pair, kb_s, vb_s):
  tq = q_pair[0].shape[0]
  qq = jnp.concatenate(q_pair, axis=0)
  tk, n_chunks, unroll = _key_chunks(kb_s.shape[0])

  def chunk(c, carry):
    off = pl.multiple_of(c * tk, tk)
    s = lax.dot_general(qq, kb_s[pl.ds(off, tk), :], _NT, preferred_element_type=F32)
    return _online_update(s, *carry, vb_s[pl.ds(off, tk), :])

  _, l, acc = lax.fori_loop(0, n_chunks, chunk, _softmax_state(2 * tq), unroll=unroll)
  o = acc / l
  return o[:tq], o[tq:]


def _pass_view(a, lat):
  w = a.shape[-1]
  if lat:
    return a.reshape(N_TOK // DEC_SEQ, DEC_SEQ, w), N_CTX // DEC_SEQ, DEC_BATCH
  return a.reshape(N_TOK // SEQ, SEQ, w), 0, BATCH


def _stage_keys(dst, lanes, own_ref, cache_ref):
  n_own = own_ref.shape[0]
  dst[:n_own, lanes] = own_ref[...].astype(BF16)
  if cache_ref is not None:
    dst[n_own:, lanes] = cache_ref[...].astype(BF16)


def _key_chunks(n_keys):
  tk = min(TK, n_keys)
  n = n_keys // tk
  return tk, n, (ATTN_UNROLL if n % ATTN_UNROLL == 0 else 1)


def _attn_specs(q, lat, kv_joint=True):
  qv, b0, nb = _pass_view(q, lat)
  kb0 = b0 if kv_joint else 0
  sq = qv.shape[1]
  tq = min(TQ, sq)
  tiles = 1 if lat else D_MODEL // LANES
  width = tiles * LANES
  grid = (nb, D_MODEL // width, sq // tq)
  qspec = pl.BlockSpec((None, tq, width), lambda bi, h, qi: (bi + b0, qi, h))
  kspec = pl.BlockSpec((None, sq, width), lambda bi, h, qi: (bi + kb0, 0, h))
  cspec = pl.BlockSpec((None, PAST_LEN, width), lambda bi, h, qi: (bi, 0, h))
  ospec = pl.BlockSpec((None, tq, width), lambda bi, h, qi: (bi, qi, h))
  n_keys = sq + (PAST_LEN if lat else 0)
  return grid, qspec, kspec, cspec, ospec, n_keys, (nb, sq, D_MODEL), tiles


def _tile(ref, tl):
  return None if ref is None else ref.at[:, tl * LANES:(tl + 1) * LANES]


def _diff_attn_body(*refs, lam_init, cached):
  if cached:
    lam_ref, g_ref, q_ref, k_ref, v_ref, kc_ref, vc_ref, o_ref, kb_s, vb_s = refs
  else:
    lam_ref, g_ref, q_ref, k_ref, v_ref, o_ref, kb_s, vb_s = refs
    kc_ref = vc_ref = None

  @pl.when(pl.program_id(2) == 0)
  def _():
    _stage_keys(kb_s, slice(None), k_ref, kc_ref)
    _stage_keys(vb_s, slice(None), v_ref, vc_ref)

  lv = lam_ref[...]
  lam = (jnp.exp(jnp.sum(lv[0:1] * lv[1:2], axis=-1, keepdims=True))
         - jnp.exp(jnp.sum(lv[2:3] * lv[3:4], axis=-1, keepdims=True)) + lam_init)
  lane = lax.broadcasted_iota(jnp.int32, (q_ref.shape[0], LANES), 1)
  for tl in range(q_ref.shape[1] // LANES):
    q = _tile(q_ref, tl)[...]
    q1 = jnp.where(lane < DA_QK, q, 0.0).astype(BF16)
    q2 = jnp.where(lane >= DA_QK, q, 0.0).astype(BF16)
    o1, o2 = _stacked_attention((q1, q2), _tile(kb_s, tl), _tile(vb_s, tl))
    _tile(o_ref, tl)[...] = _rms(o1 - lam * o2, g_ref[...]) * (1.0 - lam_init)


def _diff_attention(q, k, v, cache, lam_rows, subln_g, layer, lat):
  grid, qspec, kspec, cspec, ospec, n_keys, oshape, tiles = _attn_specs(q, lat, kv_joint=False)
  nb, sq, _ = oshape
  lam_init = 0.8 - 0.6 * math.exp(-0.3 * layer)
  const = lambda shape: pl.BlockSpec(shape, lambda bi, h, qi: (0, 0))
  in_specs = [const((SUBLANES, LANES)), const((1, LANES)), qspec, kspec, kspec]
  args = [lam_rows, subln_g, _pass_view(q, lat)[0], k.reshape(nb, sq, D_MODEL),
          v.reshape(nb, sq, D_MODEL)]
  if lat:
    in_specs += [cspec, cspec]
    args += [cache[0].reshape(DEC_BATCH, PAST_LEN, D_MODEL),
             cache[1].reshape(DEC_BATCH, PAST_LEN, D_MODEL)]
  return pl.pallas_call(
      functools.partial(_diff_attn_body, lam_init=lam_init, cached=lat),
      grid=grid, in_specs=in_specs, out_specs=ospec,
      out_shape=jax.ShapeDtypeStruct(oshape, F32),
      scratch_shapes=[pltpu.VMEM((n_keys, tiles * LANES), BF16)] * 2,
      compiler_params=_cparams(("parallel", "parallel", "arbitrary")),
      name="diff_attention",
  )(*args)


def _pair_attn_body(*refs, scale, with_pe, cached):
  refs = list(refs)
  q_ref = refs.pop(0)
  qp_ref = refs.pop(0) if with_pe else None
  k_ref = refs.pop(0)
  kp_ref = refs.pop(0) if with_pe else None
  v_ref = refs.pop(0)
  kc_ref = refs.pop(0) if cached else None
  kpc_ref = refs.pop(0) if (cached and with_pe) else None
  vc_ref = refs.pop(0) if cached else None
  o_ref, kb_s, vb_s = refs

  tiles = q_ref.shape[1] // LANES
  kw = kb_s.shape[1] // tiles
  key_view = lambda tl: kb_s.at[:, tl * kw:(tl + 1) * kw]

  @pl.when(pl.program_id(2) == 0)
  def _():
    for tl in range(tiles):
      _stage_keys(key_view(tl), slice(0, LANES), _tile(k_ref, tl), _tile(kc_ref, tl))
      if with_pe:
        _stage_keys(key_view(tl), slice(LANES, 2 * LANES), kp_ref, kpc_ref)
    _stage_keys(vb_s, slice(None), v_ref, vc_ref)

  lane = lax.broadcasted_iota(jnp.int32, (q_ref.shape[0], LANES), 1)
  halves = (lane < NA_HD, lane >= NA_HD)
  for tl in range(tiles):
    q = _tile(q_ref, tl)[...] * scale
    qs = [jnp.where(hm, q, 0.0).astype(BF16) for hm in halves]
    if with_pe:
      tile_id = tl if tiles > 1 else pl.program_id(1)
      qp = (_tile(qp_ref, tl // 2) if tiles > 1 else qp_ref)[...] * scale
      base = (tile_id % 2) * (2 * MLA_ROPE)
      qs = [jnp.concatenate(
          [qs[a], jnp.where((lane >= base + a * MLA_ROPE) & (lane < base + (a + 1) * MLA_ROPE),
                            qp, 0.0).astype(BF16)], axis=1) for a in range(2)]
    _tile(o_ref, tl)[...] = jnp.where(
        halves[0], *_stacked_attention(qs, key_view(tl), _tile(vb_s, tl)))


def _pair_attention(q, k, v, lat, cache=None, scale=1.0, q_pe=None, k_pe=None, kv_joint=True):
  grid, qspec, kspec, cspec, ospec, n_keys, oshape, tiles = _attn_specs(q, lat, kv_joint)
  view = lambda a: _pass_view(a, lat)[0]
  b0 = _pass_view(q, lat)[1]
  if not kv_joint:
    k, v = (a.reshape(oshape) for a in (k, v))
  kv_view = view if kv_joint else (lambda a: a)
  with_pe = q_pe is not None
  tq, sq = qspec.block_shape[1], kspec.block_shape[1]
  in_specs, args = [qspec], [view(q)]
  if with_pe:
    if tiles == 1:
      in_specs.append(pl.BlockSpec((None, tq, LANES), lambda bi, h, qi: (bi + b0, qi, h // 2)))
    else:
      in_specs.append(pl.BlockSpec((None, tq, q_pe.shape[-1]), lambda bi, h, qi: (bi + b0, qi, 0)))
    args.append(view(q_pe))
  in_specs.append(kspec)
  args.append(kv_view(k))
  if with_pe:
    in_specs.append(pl.BlockSpec((None, sq, LANES), lambda bi, h, qi: (bi + b0, 0, 0)))
    args.append(view(k_pe))
  in_specs.append(kspec)
  args.append(kv_view(v))
  if lat:
    in_specs.append(cspec)
    args.append(cache[0])
    if with_pe:
      in_specs.append(pl.BlockSpec((None, PAST_LEN, LANES), lambda bi, h, qi: (bi, 0, 0)))
      args.append(cache[1])
    in_specs.append(cspec)
    args.append(cache[-1])
  return pl.pallas_call(
      functools.partial(_pair_attn_body, scale=scale, with_pe=with_pe, cached=lat),
      grid=grid, in_specs=in_specs, out_specs=ospec,
      out_shape=jax.ShapeDtypeStruct(oshape, F32),
      scratch_shapes=[pltpu.VMEM((n_keys, tiles * (2 * LANES if with_pe else LANES)), BF16),
                      pltpu.VMEM((n_keys, tiles * LANES), BF16)],
      compiler_params=_cparams(("parallel", "parallel", "arbitrary")),
      name="pair_attention_pe" if with_pe else "pair_attention",
  )(*args)


def _na_lat_body(q_ref, k_ref, v_ref, kc_ref, vc_ref, bias_ref, o_ref, kb_s, vb_s, kcb_s, vcb_s):
  rb = pl.program_id(2)

  @pl.when(rb == 0)
  def _():
    kb_s[...] = k_ref[...].astype(BF16)
    vb_s[...] = v_ref[...].astype(BF16)
    kcb_s[...] = kc_ref[...].astype(BF16)
    vcb_s[...] = vc_ref[...].astype(BF16)

  n_loc = NA_KH * GRID_W
  lane = lax.broadcasted_iota(jnp.int32, (GRID_W, LANES), 1)
  halves = (lane < NA_HD, lane >= NA_HD)
  kc = kcb_s[...]
  vc = vcb_s[...]
  for rr in range(NA_RB):
    r = rb * NA_RB + rr
    start = jnp.clip(r - NA_KH // 2, 0, GRID_R - NA_KH)
    pat = jnp.where(r < NA_KH // 2, 1 + r,
                    jnp.where(r > GRID_R - NA_KH // 2, r - (GRID_R - NA_KH), 0))
    off = pl.multiple_of(start * GRID_W, GRID_W)
    kw = kb_s[pl.ds(off, n_loc), :]
    vw = vb_s[pl.ds(off, n_loc), :]
    q = q_ref[rr * GRID_W:(rr + 1) * GRID_W, :]
    qq = jnp.concatenate([jnp.where(hm, q, 0.0).astype(BF16) for hm in halves], axis=0)
    bias = bias_ref[pat].reshape(2 * GRID_W, n_loc)
    s_loc = lax.dot_general(qq, kw, _NT, preferred_element_type=F32) + bias
    s_ctx = lax.dot_general(qq, kc, _NT, preferred_element_type=F32)
    m = jnp.maximum(jnp.max(s_loc, axis=-1, keepdims=True),
                    jnp.max(s_ctx, axis=-1, keepdims=True))
    p_loc = jnp.exp(s_loc - m)
    p_ctx = jnp.exp(s_ctx - m)
    l = jnp.sum(p_loc, axis=-1, keepdims=True) + jnp.sum(p_ctx, axis=-1, keepdims=True)
    o = (jnp.dot(p_loc.astype(BF16), vw, preferred_element_type=F32)
         + jnp.dot(p_ctx.astype(BF16), vc, preferred_element_type=F32)) / l
    o_ref[rr * GRID_W:(rr + 1) * GRID_W, :] = jnp.where(halves[0], o[:GRID_W], o[GRID_W:])


def _na_lat_attention(q, k, v, kc, vc, bias):
  qv, b0, b = _pass_view(q, True)
  kv, vv = (a.reshape(b, DEC_SEQ, D_MODEL) for a in (k, v))
  n_pat = bias.shape[0]
  blk = NA_RB * GRID_W
  qspec = pl.BlockSpec((None, blk, LANES), lambda bi, h, r: (bi + b0, r, h))
  kspec = pl.BlockSpec((None, DEC_SEQ, LANES), lambda bi, h, r: (bi, 0, h))
  cspec = pl.BlockSpec((None, PAST_LEN, LANES), lambda bi, h, r: (bi, 0, h))
  bspec = pl.BlockSpec((n_pat, 2, GRID_W, NA_KH * GRID_W), lambda bi, h, r: (0, h, 0, 0))
  return pl.pallas_call(
      _na_lat_body,
      grid=(b, D_MODEL // LANES, GRID_R // NA_RB),
      in_specs=[qspec, kspec, kspec, cspec, cspec, bspec],
      out_specs=pl.BlockSpec((None, blk, LANES), lambda bi, h, r: (bi, r, h)),
      out_shape=jax.ShapeDtypeStruct((b, DEC_SEQ, D_MODEL), F32),
      scratch_shapes=[pltpu.VMEM((DEC_SEQ, LANES), BF16)] * 2
      + [pltpu.VMEM((PAST_LEN, LANES), BF16)] * 2,
      compiler_params=_cparams(("parallel", "parallel", "arbitrary")),
      name="na_lat_attention",
  )(qv, kv, vv, kc, vc, bias)


def _sort_network(n):
  pairs = []
  p = 1
  while p < n:
    k = p
    while k >= 1:
      for j in range(k % p, n - k, 2 * k):
        for i in range(min(k, n - j - k)):
          if (i + j) // (2 * p) == (i + j + k) // (2 * p):
            pairs.append((i + j, i + j + k))
      k //= 2
    p *= 2
  return pairs


def _pop_heads(lists, hit, depth):
  for k in range(depth):
    lists[k] = jnp.where(hit, lists[k + 1], lists[k])


def _top16_sorted(s):
  t = s.shape[1]
  n = s.shape[0] // SUBLANES
  xs = [s[SUBLANES * k:SUBLANES * (k + 1), :] for k in range(n)]
  for i, j in _sort_network(n):
    xs[i], xs[j] = jnp.maximum(xs[i], xs[j]), jnp.minimum(xs[i], xs[j])
  rows = lax.broadcasted_iota(jnp.int32, (PEER_TOPK, t), 0)
  v = jnp.zeros((PEER_TOPK, t), F32)
  for r in range(PEER_TOPK):
    m = jnp.max(xs[0], axis=0, keepdims=True)
    v = jnp.where(rows == r, m, v)
    _pop_heads(xs, xs[0] == m, PEER_TOPK - 1 - r)
  return v


def _router_body(oc_ref, ol_ref, wo_ref, x_ref, g1_ref, g_ref, sc_ref, sh_ref,
                 wq_ref, k1_ref, k2_ref,
                 xn_ref, hb_ref, a1_ref, n1_ref, b2_ref, r2_ref, h_s):
  def first_head(o_ref):
    xn = x_ref[...] + g1_ref[...] * jnp.dot(o_ref[...].astype(BF16), wo_ref[...],
                                            preferred_element_type=F32)
    xn_ref[...] = xn
    h = _rms_mod(xn, g_ref[...], sc_ref[...], sh_ref[...]).astype(BF16)
    h_s[...] = h
    hb_ref[...] = h

  is_first = pl.program_id(1) == 0
  is_ctx = pl.program_id(0) < N_CTX // ROUTER_T
  pl.when(is_first & is_ctx)(lambda: first_head(oc_ref))
  pl.when(is_first & jnp.logical_not(is_ctx))(lambda: first_head(ol_ref))

  qh = jnp.dot(h_s[...], wq_ref[...], preferred_element_type=F32)
  for hh in range(ROUTER_HEADS):
    _route_head(qh[:, hh * PEER_DK:(hh + 1) * PEER_DK], k1_ref[hh], k2_ref[hh],
                a1_ref.at[hh], n1_ref.at[hh], b2_ref.at[hh], r2_ref.at[hh])


def _route_head(qh, k1, k2, a1_ref, n1_ref, b2_ref, r2_ref):
  half = PEER_DK // 2
  s1 = lax.dot_general(k1, qh[:, :half].astype(BF16), _NT, preferred_element_type=F32)
  s2 = lax.dot_general(k2, qh[:, half:].astype(BF16), _NT, preferred_element_type=F32)
  s1 = s1 - jnp.max(s1, axis=0, keepdims=True)
  s2 = s2 - jnp.max(s2, axis=0, keepdims=True)
  v1 = _top16_sorted(s1)
  v2 = _top16_sorted(s2)
  cand = [v1[:SUBLANES] + v2[k:k + 1] for k in range(PEER_TOPK)]
  tail = v1[SUBLANES:] + v2[0:1]
  z = jnp.zeros_like(v2[0:1])
  taken = jnp.zeros_like(cand[0])
  for r in range(PEER_TOPK):
    th = jnp.max(jnp.maximum(cand[0], tail), axis=0, keepdims=True)
    z = z + jnp.exp(th)
    tail = jnp.where(tail == th, NEG_INF, tail)
    hit = cand[0] == th
    taken = taken + jnp.where(hit, 1.0, 0.0)
    _pop_heads(cand, hit, PEER_TOPK - 1 - r)
  inf = jnp.float32(jnp.inf)
  first = jnp.minimum(jnp.min(jnp.where(taken >= 1.0, v1[:SUBLANES], inf), axis=0, keepdims=True),
                      jnp.min(jnp.where(tail == NEG_INF, v1[SUBLANES:], inf), axis=0,
                              keepdims=True))
  n1 = jnp.where(s1 >= first, 1.0, 0.0)
  k_max = PEER_TOPK // 2
  for k in range(2, k_max + 1):
    t_k = jnp.min(jnp.where(taken >= float(k), v1[:SUBLANES], inf), axis=0, keepdims=True)
    n1 = n1 + jnp.where(s1 >= t_k, 1.0, 0.0)
  n1 = n1 + jnp.where(s1 >= v1[0:1], jnp.maximum(taken[0:1] - float(k_max), 0.0), 0.0)
  r2 = jnp.zeros_like(s2)
  for r in range(PEER_TOPK):
    r2 = r2 + jnp.where(v2[r:r + 1] > s2, 1.0, 0.0)
  a1_ref[...] = jnp.exp(s1)
  n1_ref[...] = n1
  b2_ref[...] = pltpu.bitcast((jnp.exp(s2) * (1.0 / z)).astype(BF16), jnp.uint32)
  r2_ref[...] = pltpu.bitcast(r2.astype(BF16), jnp.uint32)


def _peer_router(oc, ol, w_o, x, gate1, g, sc, sh, wq, k1, k2):
  n = x.shape[0]
  tm = ROUTER_T
  nb_ctx = N_CTX // tm
  row = lambda w: pl.BlockSpec((tm, w), lambda i, h: (i, 0))
  full = lambda shape: pl.BlockSpec(shape, lambda i, h: (0,) * len(shape))
  modspec = pl.BlockSpec((None, 1, D_MODEL), lambda i, h: (_mod_row(i, tm), 0, 0))
  hs = ROUTER_HEADS
  keyspec = pl.BlockSpec((hs, PEER_NKEYS, PEER_DK // 2), lambda i, h: (h, 0, 0))
  tspec = pl.BlockSpec((hs, PEER_NKEYS, tm), lambda i, h: (h, 0, i))
  pspec = pl.BlockSpec((hs, PEER_NKEYS // 2, tm), lambda i, h: (h, 0, i))
  return pl.pallas_call(
      _router_body,
      grid=(n // tm, PEER_HEADS // hs),
      in_specs=[pl.BlockSpec((tm, D_MODEL), lambda i, h: (jnp.minimum(i, nb_ctx - 1), 0)),
                pl.BlockSpec((tm, D_MODEL), lambda i, h: (jnp.maximum(i - nb_ctx, 0), 0)),
                full(w_o.shape), row(D_MODEL), modspec,
                full((1, D_MODEL)), modspec, modspec,
                pl.BlockSpec((D_MODEL, hs * PEER_DK), lambda i, h: (0, h)), keyspec, keyspec],
      out_specs=[row(D_MODEL), row(D_MODEL), tspec, tspec, pspec, pspec],
      out_shape=[jax.ShapeDtypeStruct((n, D_MODEL), F32), jax.ShapeDtypeStruct((n, D_MODEL), BF16)]
      + [jax.ShapeDtypeStruct((PEER_HEADS, PEER_NKEYS, n), F32)] * 2
      + [jax.ShapeDtypeStruct((PEER_HEADS, PEER_NKEYS // 2, n), jnp.uint32)] * 2,
      scratch_shapes=[pltpu.VMEM((tm, D_MODEL), BF16)],
      compiler_params=_cparams(("parallel", "arbitrary"), VMEM_LIMIT_PEER),
      name="peer_router",
  )(oc.reshape(N_CTX, D_MODEL), ol.reshape(N_LAT, D_MODEL), w_o, x, gate1, g, sc, sh, wq, k1, k2)


def _gelu(x):
  return 0.5 * x * (1.0 + lax.erf(x * np.float32(math.sqrt(0.5))))


def _peer_mix_body(hb_ref, u_ref, vt_ref, a1_ref, n1_ref, b2_ref, r2_ref, x_ref, gate_ref,
                   fg_ref, y_ref, *scratch, final_norm):
  c = pl.program_id(1)
  n_pieces = PEER_EC // PEER_SUBC
  if final_norm:
    yl_ref, acc_s, *piece_s = scratch
  else:
    acc_s, *piece_s = scratch

  @pl.when(c == 0)
  def _():
    acc_s[...] = jnp.zeros_like(acc_s)

  hb = hb_ref[...]
  t = hb.shape[0]
  pk = BF16_ROWS
  for j in range(n_pieces):
    rows = slice(j * PEER_SUBC, (j + 1) * PEER_SUBC)
    piece_s[j][...] = lax.dot_general(u_ref[rows, :], hb, _NT, preferred_element_type=F32)
  for j in range(n_pieces):
    s_s, p_s = piece_s[j], piece_s[n_pieces + j]
    for ii in range(PEER_SUBC // PEER_NKEYS):
      i1 = j * (PEER_SUBC // PEER_NKEYS) + ii
      w = [jnp.zeros((pk, t), BF16) for _ in range(PEER_NKEYS // pk)]
      for h in range(PEER_HEADS):
        a_row = jnp.broadcast_to(a1_ref[h, i1:i1 + 1, :], (pk, t)).astype(BF16)
        n_row = jnp.broadcast_to(n1_ref[h, i1:i1 + 1, :], (pk, t)).astype(BF16)
        for sub in range(PEER_NKEYS // pk):
          words = slice(sub * pk // 2, (sub + 1) * pk // 2)
          prod = a_row * pltpu.bitcast(b2_ref[h, words, :], BF16)
          rank = pltpu.bitcast(r2_ref[h, words, :], BF16)
          w[sub] = w[sub] + jnp.where(rank < n_row, prod, jnp.zeros_like(prod))
      for sub in range(PEER_NKEYS // pk):
        row0 = ii * PEER_NKEYS + sub * pk
        p_s[row0:row0 + pk, :] = w[sub] * _gelu(s_s[row0:row0 + pk, :]).astype(BF16)
    rows = slice(j * PEER_SUBC, (j + 1) * PEER_SUBC)
    acc_s[...] += jnp.dot(vt_ref[:, rows], p_s[...], preferred_element_type=F32)

  @pl.when(c == pl.num_programs(1) - 1)
  def _():
    y = x_ref[...] + gate_ref[...] * acc_s[...].T
    if final_norm:
      _store_by_pass(pl.program_id(0), t, _rms(y, fg_ref[...]), y_ref, yl_ref)
    else:
      y_ref[...] = y


def _peer_mix(hb, u, vt, a1, n1, b2, r2, x, gate, final_g, final_norm):
  n = x.shape[0]
  t = PEER_T
  n_i1 = PEER_EC // PEER_NKEYS
  row = lambda w: pl.BlockSpec((t, w), lambda i, c: (i, 0))
  i1spec = pl.BlockSpec((PEER_HEADS, n_i1, t), lambda i, c: (0, c, i))
  i2spec = pl.BlockSpec((PEER_HEADS, PEER_NKEYS // 2, t), lambda i, c: (0, 0, i))
  return pl.pallas_call(
      functools.partial(_peer_mix_body, final_norm=final_norm),
      grid=(n // t, PEER_EXPERTS // PEER_EC),
      in_specs=[row(D_MODEL),
                pl.BlockSpec((PEER_EC, D_MODEL), lambda i, c: (c, 0)),
                pl.BlockSpec((None, D_MODEL, PEER_EC), lambda i, c: (c, 0, 0)),
                i1spec, i1spec, i2spec, i2spec,
                row(D_MODEL),
                pl.BlockSpec((None, 1, D_MODEL), lambda i, c: (_mod_row(i, t), 0, 0)),
                pl.BlockSpec((1, D_MODEL), lambda i, c: (0, 0))],
      out_specs=_pass_specs(D_MODEL, t) if final_norm else row(D_MODEL),
      out_shape=_pass_shapes(D_MODEL) if final_norm else jax.ShapeDtypeStruct((n, D_MODEL), F32),
      scratch_shapes=[pltpu.VMEM((D_MODEL, t), F32)]
      + [pltpu.VMEM((PEER_SUBC, t), F32)] * (PEER_EC // PEER_SUBC)
      + [pltpu.VMEM((PEER_SUBC, t), BF16)] * (PEER_EC // PEER_SUBC),
      compiler_params=_cparams(("arbitrary", "arbitrary"), VMEM_LIMIT_PEER),
      name="peer_mix",
  )(hb, u, vt, a1, n1, b2, r2, x, gate, final_g)


def _rotated_tiles(w, group):
  k, n = w.shape
  g = w.reshape(k, n // group, 2, group // 2)
  return jnp.concatenate([-g[:, :, 1:], g[:, :, :1]], axis=2).reshape(k, n)


def _rope_table(rot_dims, reps):
  half = rot_dims // 2
  t = jnp.arange(DEC_SEQ)
  inv = ROPE_BASE ** (-jnp.arange(half, dtype=F32) / half)
  parts_c, parts_s = [], []
  for pos in (t // GRID_W, t % GRID_W):
    ang = pos.astype(F32)[:, None] * inv[None, :]
    parts_c += [jnp.cos(ang), jnp.cos(ang)]
    parts_s += [jnp.sin(ang), jnp.sin(ang)]
  cos = jnp.tile(jnp.concatenate(parts_c, axis=1), (1, reps))
  sin = jnp.tile(jnp.concatenate(parts_s, axis=1), (1, reps))
  cos = jnp.concatenate([cos, jnp.ones((TM, LANES), F32)], axis=0)
  sin = jnp.concatenate([sin, jnp.zeros((TM, LANES), F32)], axis=0)
  return cos, sin


def _na_bias_table(rpb):
  reps = [NA_KH // 2] + list(range(NA_KH // 2)) + list(range(GRID_R - NA_KH // 2 + 1, GRID_R))
  cols = np.arange(GRID_W)
  col_start = np.clip(cols - NA_KW // 2, 0, GRID_W - NA_KW)
  col_mask = (cols[None, :] >= col_start[:, None]) & (cols[None, :] < col_start[:, None] + NA_KW)
  dc = np.clip(cols[None, :] - cols[:, None], -(NA_KW - 1), NA_KW - 1) + (NA_KW - 1)
  dr = np.stack([np.clip(r - NA_KH // 2, 0, GRID_R - NA_KH) + np.arange(NA_KH) - r + (NA_KH - 1)
                 for r in reps])
  onehot = (dc[:, :, None] == np.arange(2 * NA_KW - 1)).astype(np.float32)
  bias = jnp.einsum('hpjc,qwc->phqjw', rpb[:, dr, :], onehot, precision=lax.Precision.HIGHEST)
  bias = jnp.where(col_mask[None, None, :, None, :], bias, NEG_INF)
  return bias.reshape(len(reps), NA_HEADS, GRID_W, NA_KH * GRID_W)


def _ctx_rows(a, *shape):
  return a[:N_CTX].reshape(BATCH, SEQ, *shape)


def _diff_mixer(layer, p, x, sc, sh, cache, tables):
  w = p['w_qkv']
  d = D_MODEL
  wq, wk, wv = w[:, :d], w[:, d:2 * d], w[:, 2 * d:]
  rot = lambda m: _rotated_tiles(m, DA_QK // 2)
  w5 = jnp.concatenate([wq, rot(wq), wk, rot(wk), wv], axis=1).astype(BF16)
  cos, sin = tables['da']
  q, (kc, kl), (vc, vl) = _qkv_rope(x, p['norm1_g'], sc, sh, w5, cos, sin, DA_QK ** -0.5)
  lam_rows = jnp.zeros((SUBLANES, LANES), F32)
  for i, nme in enumerate(('lam_q1', 'lam_k1', 'lam_q2', 'lam_k2')):
    lam_rows = lam_rows.at[i, :DA_QK].set(p[nme])
  g = p['subln_g'].reshape(1, DA_V)
  oc = _diff_attention(q, kc, vc, None, lam_rows, g, layer, lat=False)
  ol = _diff_attention(q, kl, vl, cache, lam_rows, g, layer, lat=True)
  state = (kc.reshape(BATCH, SEQ, DA_HEADS, 2 * DA_QK), vc.reshape(BATCH, SEQ, DA_HEADS, DA_V))
  return oc, ol, state


def _na_mixer(p, x, sc, sh, cache):
  d = D_MODEL
  q, (kc, kl), (vc, vl) = _qkv_plain(x, p['norm1_g'], sc, sh, p['w_qkv'].astype(BF16),
                                     NA_HD ** -0.5)
  oc = _pair_attention(q, kc, vc, lat=False, kv_joint=False)
  bias = _na_bias_table(p['rpb'])
  ol = _na_lat_attention(q, kl, vl, cache[0].reshape(DEC_BATCH, PAST_LEN, d),
                         cache[1].reshape(DEC_BATCH, PAST_LEN, d), bias)
  state = (kc.reshape(BATCH, SEQ, NA_HEADS, NA_HD), vc.reshape(BATCH, SEQ, NA_HEADS, NA_HD))
  return oc, ol, state


def _mla_mixer(p, x, sc, sh, cache, tables):
  a, b = MLA_Q_RANK, MLA_Q_RANK + MLA_KV_RANK
  w_in = p['w_in']
  kpe_w = w_in[:, b:]
  w_in_x = jnp.concatenate([w_in[:, :b], jnp.tile(kpe_w, (1, 4)),
                            jnp.tile(_rotated_tiles(kpe_w, MLA_ROPE // 2), (1, 4))],
                           axis=1).astype(BF16)
  cos, sin = tables['mla']
  cq, ckv, kpe = _mla_in(x, p['norm1_g'], sc, sh, w_in_x, p['q_norm_g'].reshape(1, a),
                         p['kv_norm_g'].reshape(1, MLA_KV_RANK), cos, sin)
  w_uq = p['w_uq'].reshape(a, MLA_HEADS, MLA_NOPE + MLA_ROPE)
  w_qn = w_uq[:, :, :MLA_NOPE].reshape(a, -1)
  w_qp = w_uq[:, :, MLA_NOPE:].reshape(a, -1)
  w_uq_x = jnp.concatenate([w_qn, w_qp, _rotated_tiles(w_qp, MLA_ROPE // 2)], axis=1).astype(BF16)
  qn, qp = _mla_q(cq, w_uq_x, cos, sin)
  w_ukv = p['w_ukv'].reshape(MLA_KV_RANK, MLA_HEADS, MLA_NOPE + MLA_V)
  w_ukv_x = jnp.concatenate([w_ukv[:, :, :MLA_NOPE].reshape(MLA_KV_RANK, -1),
                             w_ukv[:, :, MLA_NOPE:].reshape(MLA_KV_RANK, -1)], axis=1).astype(BF16)
  kn, v = _mla_kv(ckv, w_ukv_x)
  knc, vc = _mla_kv(cache[0].reshape(DEC_BATCH * PAST_LEN, MLA_KV_RANK), w_ukv_x)
  cached = (knc.reshape(DEC_BATCH, PAST_LEN, D_MODEL), jnp.tile(cache[1], (1, 1, LANES // MLA_ROPE)),
            vc.reshape(DEC_BATCH, PAST_LEN, D_MODEL))
  scale = (MLA_NOPE + MLA_ROPE) ** -0.5
  oc = _pair_attention(qn, kn, v, lat=False, scale=scale, q_pe=qp, k_pe=kpe)
  ol = _pair_attention(qn, kn, v, lat=True, cache=cached, scale=scale, q_pe=qp, k_pe=kpe)
  state = (_ctx_rows(ckv, MLA_KV_RANK), _ctx_rows(kpe[:, :MLA_ROPE], MLA_ROPE))
  return oc, ol, state


def _layer(layer, p, x, cond8, cache, tables, final_g):
  sh1, sc1, g1, sh2, sc2, g2 = _modulation(cond8, p['ada_w'], p['ada_b'])
  kind = layer % 3
  if kind == 0:
    oc, ol, state = _diff_mixer(layer, p, x, sc1, sh1, cache, tables)
  elif kind == 1:
    oc, ol, state = _na_mixer(p, x, sc1, sh1, cache)
  else:
    oc, ol, state = _mla_mixer(p, x, sc1, sh1, cache, tables)
  x, hb, a1, n1, b2, r2 = _peer_router(oc, ol, p['w_o'].astype(BF16), x, g1, p['norm2_g'],
                                       sc2, sh2, p['peer_wq'], p['peer_k1'], p['peer_k2'])
  x = _peer_mix(hb, p['peer_u'], p['peer_vt'], a1, n1, b2, r2, x, g2, final_g,
                final_norm=layer == DEPTH - 1)
  return x, state


def kernel(x_prompt, x_sample, cache_l0_k, cache_l0_v, cache_l1_k, cache_l1_v, cache_l2_ckv, cache_l2_kpe, cache_l3_k, cache_l3_v, c, c_ctx, l0_norm1_g, l0_norm2_g, l0_ada_w, l0_ada_b, l0_w_qkv, l0_w_o, l0_lam_q1, l0_lam_k1, l0_lam_q2, l0_lam_k2, l0_subln_g, l0_peer_wq, l0_peer_k1, l0_peer_k2, l0_peer_u, l0_peer_v, l1_norm1_g, l1_norm2_g, l1_ada_w, l1_ada_b, l1_w_qkv, l1_w_o, l1_rpb, l1_peer_wq, l1_peer_k1, l1_peer_k2, l1_peer_u, l1_peer_v, l2_norm1_g, l2_norm2_g, l2_ada_w, l2_ada_b, l2_w_in, l2_q_norm_g, l2_w_uq, l2_kv_norm_g, l2_w_ukv, l2_w_o, l2_peer_wq, l2_peer_k1, l2_peer_k2, l2_peer_u, l2_peer_v, l3_norm1_g, l3_norm2_g, l3_ada_w, l3_ada_b, l3_w_qkv, l3_w_o, l3_lam_q1, l3_lam_k1, l3_lam_q2, l3_lam_k2, l3_subln_g, l3_peer_wq, l3_peer_k1, l3_peer_k2, l3_peer_u, l3_peer_v, final_norm_g):
  common = lambda n1, n2, aw, ab, wq, k1, k2, u, v: dict(
      norm1_g=n1.reshape(1, D_MODEL), norm2_g=n2.reshape(1, D_MODEL), ada_w=aw, ada_b=ab,
      peer_wq=wq.astype(BF16), peer_k1=k1.astype(BF16), peer_k2=k2.astype(BF16),
      peer_u=u.astype(BF16),
      peer_vt=v.reshape(PEER_EXPERTS // PEER_EC, PEER_EC, D_MODEL).transpose(0, 2, 1).astype(BF16))
  p0 = dict(common(l0_norm1_g, l0_norm2_g, l0_ada_w, l0_ada_b, l0_peer_wq, l0_peer_k1, l0_peer_k2,
                   l0_peer_u, l0_peer_v),
            w_qkv=l0_w_qkv, w_o=l0_w_o, lam_q1=l0_lam_q1, lam_k1=l0_lam_k1, lam_q2=l0_lam_q2,
            lam_k2=l0_lam_k2, subln_g=l0_subln_g)
  p1 = dict(common(l1_norm1_g, l1_norm2_g, l1_ada_w, l1_ada_b, l1_peer_wq, l1_peer_k1, l1_peer_k2,
                   l1_peer_u, l1_peer_v),
            w_qkv=l1_w_qkv, w_o=l1_w_o, rpb=l1_rpb)
  p2 = dict(common(l2_norm1_g, l2_norm2_g, l2_ada_w, l2_ada_b, l2_peer_wq, l2_peer_k1, l2_peer_k2,
                   l2_peer_u, l2_peer_v),
            w_in=l2_w_in, q_norm_g=l2_q_norm_g, w_uq=l2_w_uq, kv_norm_g=l2_kv_norm_g,
            w_ukv=l2_w_ukv, w_o=l2_w_o)
  p3 = dict(common(l3_norm1_g, l3_norm2_g, l3_ada_w, l3_ada_b, l3_peer_wq, l3_peer_k1, l3_peer_k2,
                   l3_peer_u, l3_peer_v),
            w_qkv=l3_w_qkv, w_o=l3_w_o, lam_q1=l3_lam_q1, lam_k1=l3_lam_k1, lam_q2=l3_lam_q2,
            lam_k2=l3_lam_k2, subln_g=l3_subln_g)
  params = (p0, p1, p2, p3)
  caches = ((cache_l0_k, cache_l0_v), (cache_l1_k, cache_l1_v),
            (cache_l2_ckv, cache_l2_kpe), (cache_l3_k, cache_l3_v))
  tables = dict(da=_rope_table(DA_QK // 2, LANES // DA_QK),
                mla=_rope_table(MLA_ROPE // 2, LANES // MLA_ROPE))

  cond8 = jnp.zeros((N_COND, D_MODEL), F32).at[0].set(c_ctx).at[1:1 + DEC_BATCH].set(c)
  x = jnp.concatenate([x_prompt.reshape(N_CTX, D_MODEL), x_sample.reshape(N_LAT, D_MODEL)], axis=0)
  states = []
  for layer in range(DEPTH):
    x, st = _layer(layer, params[layer], x, cond8, caches[layer], tables,
                   final_norm_g.reshape(1, D_MODEL))
    states.extend(st)
  y_ctx, y_lat = x
  return (y_ctx.reshape(BATCH, SEQ, D_MODEL), y_lat.reshape(DEC_BATCH, DEC_SEQ, D_MODEL), *states)
```

```python
import functools
import math

import numpy as np
import jax
import jax.numpy as jnp
from jax import lax
from jax.experimental import pallas as pl
from jax.experimental.pallas import tpu as pltpu

F32 = jnp.float32
BF16 = jnp.bfloat16

D_MODEL = 1024
BATCH = 32
SEQ = 256
DEPTH = 4
DEC_BATCH = 2
DEC_SEQ = 4096
PAST_LEN = 512
GRID_W = 64
GRID_R = DEC_SEQ // GRID_W
EPS = 1e-6
ROPE_BASE = 10000.0
NEG_INF = -1e30

DA_HEADS = 8
DA_QK = 64
DA_V = 128
NA_HEADS = 16
NA_HD = 64
NA_KH = 8
NA_KW = 16
MLA_HEADS = 16
MLA_NOPE = 64
MLA_ROPE = 32
MLA_V = 64
MLA_Q_RANK = 384
MLA_KV_RANK = 256
PEER_HEADS = 8
PEER_NKEYS = 128
PEER_EXPERTS = PEER_NKEYS * PEER_NKEYS
PEER_DK = 256
PEER_TOPK = 16

N_CTX = BATCH * SEQ
N_LAT = DEC_BATCH * DEC_SEQ
N_TOK = N_CTX + N_LAT
N_COND = 8

LANES = 128
SUBLANES = 8
BF16_ROWS = 16
VMEM_LIMIT = 48 << 20
VMEM_LIMIT_PEER = 56 << 20

TM = 256
MOD_TN = 1536
TQ = 512
TK = 1536
ATTN_UNROLL = 3
NA_RB = 8
ROUTER_T = 1024
ROUTER_HEADS = 2
PEER_T = 512
PEER_EC = 2048
PEER_SUBC = 256

_NT = (((1,), (1,)), ((), ()))


def _cparams(sem, vmem=VMEM_LIMIT):
  return pltpu.CompilerParams(dimension_semantics=sem, vmem_limit_bytes=vmem)


def _mod_row(i, tm):
  nb_ctx = N_CTX // tm
  nb_bat = DEC_SEQ // tm
  return jnp.where(i < nb_ctx, 0, 1 + (i - nb_ctx) // nb_bat)


def _rope_blk(i, tm):
  nb_ctx = N_CTX // tm
  nb_bat = DEC_SEQ // tm
  return jnp.where(i < nb_ctx, nb_bat, (i - nb_ctx) % nb_bat)


def _rms_mod(x, g, sc, sh):
  y = x * lax.rsqrt(jnp.mean(x * x, axis=-1, keepdims=True) + EPS)
  return (y * g) * (1.0 + sc) + sh


def _rms(x, g):
  return x * lax.rsqrt(jnp.mean(x * x, axis=-1, keepdims=True) + EPS) * g


def _row_spec(width, tm=TM):
  return pl.BlockSpec((tm, width), lambda i: (i, 0))


def _full_spec(shape):
  return pl.BlockSpec(shape, lambda i: (0,) * len(shape))


def _mod_spec(tm=TM):
  return pl.BlockSpec((None, 1, D_MODEL), lambda i: (_mod_row(i, tm), 0, 0))


def _pass_specs(width, tm, index=lambda i, *_: i):
  nb = N_CTX // tm
  return [pl.BlockSpec((tm, width), lambda *g: (jnp.minimum(index(*g), nb - 1), 0)),
          pl.BlockSpec((tm, width), lambda *g: (jnp.maximum(index(*g) - nb, 0), 0))]


def _pass_shapes(width, dtype=F32):
  return [jax.ShapeDtypeStruct((N_CTX, width), dtype), jax.ShapeDtypeStruct((N_LAT, width), dtype)]


def _store_by_pass(i, tm, val, ctx_ref, lat_ref):
  is_ctx = i < N_CTX // tm

  @pl.when(is_ctx)
  def _():
    ctx_ref[...] = val

  @pl.when(jnp.logical_not(is_ctx))
  def _():
    lat_ref[...] = val


def _mod_body(c_ref, w_ref, b_ref, o_ref):
  c = c_ref[...]
  s = c / (1.0 + jnp.exp(-c))
  o_ref[...] = jnp.dot(s, w_ref[...], precision=lax.Precision.HIGHEST,
                       preferred_element_type=F32) + b_ref[...]


def _modulation(cond8, ada_w, ada_b):
  n = ada_w.shape[1]
  tn = MOD_TN
  out = pl.pallas_call(
      _mod_body,
      grid=(n // tn,),
      in_specs=[pl.BlockSpec((N_COND, D_MODEL), lambda j: (0, 0)),
                pl.BlockSpec((D_MODEL, tn), lambda j: (0, j)),
                pl.BlockSpec((1, tn), lambda j: (0, j))],
      out_specs=pl.BlockSpec((N_COND, tn), lambda j: (0, j)),
      out_shape=jax.ShapeDtypeStruct((N_COND, n), F32),
      compiler_params=_cparams(("parallel",)),
      name="modulation",
  )(cond8, ada_w, ada_b.reshape(1, n))
  return [out[:, k * D_MODEL:(k + 1) * D_MODEL].reshape(N_COND, 1, D_MODEL)
          for k in range(n // D_MODEL)]


def _tile_lanes(t, reps):
  return jnp.concatenate([t] * reps, axis=1)


def _qkv_rope_body(x_ref, g_ref, sc_ref, sh_ref, w_ref, cos_ref, sin_ref,
                   q_ref, kc_ref, kl_ref, vc_ref, vl_ref, *, q_scale):
  h = _rms_mod(x_ref[...], g_ref[...], sc_ref[...], sh_ref[...]).astype(BF16)
  reps = D_MODEL // LANES
  cos = _tile_lanes(cos_ref[...], reps)
  sin = _tile_lanes(sin_ref[...], reps)
  d = D_MODEL
  dot = lambda a, b: jnp.dot(h, w_ref[:, a:b], preferred_element_type=F32)
  q_ref[...] = (dot(0, d) * cos + dot(d, 2 * d) * sin) * q_scale
  i = pl.program_id(0)
  _store_by_pass(i, TM, dot(2 * d, 3 * d) * cos + dot(3 * d, 4 * d) * sin, kc_ref, kl_ref)
  _store_by_pass(i, TM, dot(4 * d, 5 * d), vc_ref, vl_ref)


def _qkv_rope(x, g, sc, sh, w5, cos, sin, q_scale):
  n = x.shape[0]
  rope_spec = pl.BlockSpec((TM, LANES), lambda i: (_rope_blk(i, TM), 0))
  q, kc, kl, vc, vl = pl.pallas_call(
      functools.partial(_qkv_rope_body, q_scale=q_scale),
      grid=(n // TM,),
      in_specs=[_row_spec(D_MODEL), _full_spec((1, D_MODEL)), _mod_spec(), _mod_spec(),
                _full_spec(w5.shape), rope_spec, rope_spec],
      out_specs=[_row_spec(D_MODEL)] + _pass_specs(D_MODEL, TM) * 2,
      out_shape=[jax.ShapeDtypeStruct((n, D_MODEL), F32)] + _pass_shapes(D_MODEL) * 2,
      compiler_params=_cparams(("arbitrary",)),
      name="qkv_rope_proj",
  )(x, g, sc, sh, w5, cos, sin)
  return q, (kc, kl), (vc, vl)


def _qkv_plain_body(x_ref, g_ref, sc_ref, sh_ref, w_ref, q_ref, kc_ref, kl_ref, vc_ref, vl_ref,
                    *, q_scale):
  h = _rms_mod(x_ref[...], g_ref[...], sc_ref[...], sh_ref[...]).astype(BF16)
  d = D_MODEL
  dot = lambda a, b: jnp.dot(h, w_ref[:, a:b], preferred_element_type=F32)
  q_ref[...] = dot(0, d) * q_scale
  i = pl.program_id(0)
  _store_by_pass(i, TM, dot(d, 2 * d), kc_ref, kl_ref)
  _store_by_pass(i, TM, dot(2 * d, 3 * d), vc_ref, vl_ref)


def _qkv_plain(x, g, sc, sh, w3, q_scale):
  n = x.shape[0]
  q, kc, kl, vc, vl = pl.pallas_call(
      functools.partial(_qkv_plain_body, q_scale=q_scale),
      grid=(n // TM,),
      in_specs=[_row_spec(D_MODEL), _full_spec((1, D_MODEL)), _mod_spec(), _mod_spec(),
                _full_spec(w3.shape)],
      out_specs=[_row_spec(D_MODEL)] + _pass_specs(D_MODEL, TM) * 2,
      out_shape=[jax.ShapeDtypeStruct((n, D_MODEL), F32)] + _pass_shapes(D_MODEL) * 2,
      compiler_params=_cparams(("arbitrary",)),
      name="qkv_proj",
  )(x, g, sc, sh, w3)
  return q, (kc, kl), (vc, vl)


def _mla_in_body(x_ref, g_ref, sc_ref, sh_ref, w_ref, qg_ref, kvg_ref, cos_ref, sin_ref,
                 cq_ref, ckv_ref, kpe_ref):
  h = _rms_mod(x_ref[...], g_ref[...], sc_ref[...], sh_ref[...]).astype(BF16)
  z = jnp.dot(h, w_ref[...], preferred_element_type=F32)
  a, b = MLA_Q_RANK, MLA_Q_RANK + MLA_KV_RANK
  cq_ref[...] = _rms(z[:, :a], qg_ref[...]).astype(BF16)
  ckv_ref[...] = _rms(z[:, a:b], kvg_ref[...])
  kpe_ref[...] = z[:, b:b + LANES] * cos_ref[...] + z[:, b + LANES:] * sin_ref[...]


def _mla_in(x, g, sc, sh, w_in, qg, kvg, cos, sin):
  n = x.shape[0]
  rope_spec = pl.BlockSpec((TM, LANES), lambda i: (_rope_blk(i, TM), 0))
  return pl.pallas_call(
      _mla_in_body,
      grid=(n // TM,),
      in_specs=[_row_spec(D_MODEL), _full_spec((1, D_MODEL)), _mod_spec(), _mod_spec(),
                _full_spec(w_in.shape), _full_spec((1, MLA_Q_RANK)),
                _full_spec((1, MLA_KV_RANK)), rope_spec, rope_spec],
      out_specs=[_row_spec(MLA_Q_RANK), _row_spec(MLA_KV_RANK), _row_spec(LANES)],
      out_shape=[jax.ShapeDtypeStruct((n, MLA_Q_RANK), BF16),
                 jax.ShapeDtypeStruct((n, MLA_KV_RANK), F32),
                 jax.ShapeDtypeStruct((n, LANES), F32)],
      compiler_params=_cparams(("parallel",)),
      name="mla_in_proj",
  )(x, g, sc, sh, w_in, qg, kvg, cos, sin)


def _mla_q_body(cq_ref, w_ref, cos_ref, sin_ref, qn_ref, qp_ref):
  z = jnp.dot(cq_ref[...], w_ref[...], preferred_element_type=F32)
  pe = MLA_HEADS * MLA_ROPE
  reps = pe // LANES
  cos = _tile_lanes(cos_ref[...], reps)
  sin = _tile_lanes(sin_ref[...], reps)
  qn_ref[...] = z[:, :D_MODEL]
  qp_ref[...] = z[:, D_MODEL:D_MODEL + pe] * cos + z[:, D_MODEL + pe:] * sin


def _mla_q(cq, w_uq, cos, sin):
  n = cq.shape[0]
  pe = MLA_HEADS * MLA_ROPE
  rope_spec = pl.BlockSpec((TM, LANES), lambda i: (_rope_blk(i, TM), 0))
  return pl.pallas_call(
      _mla_q_body,
      grid=(n // TM,),
      in_specs=[_row_spec(MLA_Q_RANK), _full_spec(w_uq.shape), rope_spec, rope_spec],
      out_specs=[_row_spec(D_MODEL), _row_spec(pe)],
      out_shape=[jax.ShapeDtypeStruct((n, D_MODEL), F32), jax.ShapeDtypeStruct((n, pe), F32)],
      compiler_params=_cparams(("parallel",)),
      name="mla_q_proj",
  )(cq, w_uq, cos, sin)


def _mla_kv_body(c_ref, w_ref, kn_ref, v_ref):
  z = jnp.dot(c_ref[...].astype(BF16), w_ref[...], preferred_element_type=F32)
  kn_ref[...] = z[:, :D_MODEL]
  v_ref[...] = z[:, D_MODEL:]


def _mla_kv(ckv, w_ukv):
  n = ckv.shape[0]
  return pl.pallas_call(
      _mla_kv_body,
      grid=(n // TM,),
      in_specs=[_row_spec(MLA_KV_RANK), _full_spec(w_ukv.shape)],
      out_specs=[_row_spec(D_MODEL)] * 2,
      out_shape=[jax.ShapeDtypeStruct((n, D_MODEL), F32)] * 2,
      compiler_params=_cparams(("parallel",)),
      name="mla_kv_proj",
  )(ckv, w_ukv)


def _online_update(s, m, l, acc, vb):
  m_new = jnp.maximum(m, jnp.max(s, axis=-1, keepdims=True))
  alpha = jnp.exp(m - m_new)
  p = jnp.exp(s - m_new)
  l_new = alpha * l + jnp.sum(p, axis=-1, keepdims=True)
  acc_new = alpha * acc + jnp.dot(p.astype(BF16), vb, preferred_element_type=F32)
  return m_new, l_new, acc_new


def _softmax_state(tq):
  return (jnp.full((tq, 1), NEG_INF, F32), jnp.zeros((tq, 1), F32), jnp.zeros((tq, LANES), F32))


def _stacked_attention(q_pair, kb_s, vb_s):
  tq = q_pair[0].shape[0]
  qq = jnp.concatenate(q_pair, axis=0)
  tk, n_chunks, unroll = _key_chunks(kb_s.shape[0])

  def chunk(c, carry):
    off = pl.multiple_of(c * tk, tk)
    s = lax.dot_general(qq, kb_s[pl.ds(off, tk), :], _NT, preferred_element_type=F32)
    return _online_update(s, *carry, vb_s[pl.ds(off, tk), :])

  _, l, acc = lax.fori_loop(0, n_chunks, chunk, _softmax_state(2 * tq), unroll=unroll)
  o = acc / l
  return o[:tq], o[tq:]


def _pass_view(a, lat):
  w = a.shape[-1]
  if lat:
    return a.reshape(N_TOK // DEC_SEQ, DEC_SEQ, w), N_CTX // DEC_SEQ, DEC_BATCH
  return a.reshape(N_TOK // SEQ, SEQ, w), 0, BATCH


def _stage_keys(dst, lanes, own_ref, cache_ref):
  n_own = own_ref.shape[0]
  dst[:n_own, lanes] = own_ref[...].astype(BF16)
  if cache_ref is not None:
    dst[n_own:, lanes] = cache_ref[...].astype(BF16)


def _key_chunks(n_keys):
  tk = min(TK, n_keys)
  n = n_keys // tk
  return tk, n, (ATTN_UNROLL if n % ATTN_UNROLL == 0 else 1)


def _attn_specs(q, lat, kv_joint=True):
  qv, b0, nb = _pass_view(q, lat)
  kb0 = b0 if kv_joint else 0
  sq = qv.shape[1]
  tq = min(TQ, sq)
  tiles = 1 if lat else D_MODEL // LANES
  width = tiles * LANES
  grid = (nb, D_MODEL // width, sq // tq)
  qspec = pl.BlockSpec((None, tq, width), lambda bi, h, qi: (bi + b0, qi, h))
  kspec = pl.BlockSpec((None, sq, width), lambda bi, h, qi: (bi + kb0, 0, h))
  cspec = pl.BlockSpec((None, PAST_LEN, width), lambda bi, h, qi: (bi, 0, h))
  ospec = pl.BlockSpec((None, tq, width), lambda bi, h, qi: (bi, qi, h))
  n_keys = sq + (PAST_LEN if lat else 0)
  return grid, qspec, kspec, cspec, ospec, n_keys, (nb, sq, D_MODEL), tiles


def _tile(ref, tl):
  return None if ref is None else ref.at[:, tl * LANES:(tl + 1) * LANES]


def _diff_attn_body(*refs, lam_init, cached):
  if cached:
    lam_ref, g_ref, q_ref, k_ref, v_ref, kc_ref, vc_ref, o_ref, kb_s, vb_s = refs
  else:
    lam_ref, g_ref, q_ref, k_ref, v_ref, o_ref, kb_s, vb_s = refs
    kc_ref = vc_ref = None

  @pl.when(pl.program_id(2) == 0)
  def _():
    _stage_keys(kb_s, slice(None), k_ref, kc_ref)
    _stage_keys(vb_s, slice(None), v_ref, vc_ref)

  lv = lam_ref[...]
  lam = (jnp.exp(jnp.sum(lv[0:1] * lv[1:2], axis=-1, keepdims=True))
         - jnp.exp(jnp.sum(lv[2:3] * lv[3:4], axis=-1, keepdims=True)) + lam_init)
  lane = lax.broadcasted_iota(jnp.int32, (q_ref.shape[0], LANES), 1)
  for tl in range(q_ref.shape[1] // LANES):
    q = _tile(q_ref, tl)[...]
    q1 = jnp.where(lane < DA_QK, q, 0.0).astype(BF16)
    q2 = jnp.where(lane >= DA_QK, q, 0.0).astype(BF16)
    o1, o2 = _stacked_attention((q1, q2), _tile(kb_s, tl), _tile(vb_s, tl))
    _tile(o_ref, tl)[...] = _rms(o1 - lam * o2, g_ref[...]) * (1.0 - lam_init)


def _diff_attention(q, k, v, cache, lam_rows, subln_g, layer, lat):
  grid, qspec, kspec, cspec, ospec, n_keys, oshape, tiles = _attn_specs(q, lat, kv_joint=False)
  nb, sq, _ = oshape
  lam_init = 0.8 - 0.6 * math.exp(-0.3 * layer)
  const = lambda shape: pl.BlockSpec(shape, lambda bi, h, qi: (0, 0))
  in_specs = [const((SUBLANES, LANES)), const((1, LANES)), qspec, kspec, kspec]
  args = [lam_rows, subln_g, _pass_view(q, lat)[0], k.reshape(nb, sq, D_MODEL),
          v.reshape(nb, sq, D_MODEL)]
  if lat:
    in_specs += [cspec, cspec]
    args += [cache[0].reshape(DEC_BATCH, PAST_LEN, D_MODEL),
             cache[1].reshape(DEC_BATCH, PAST_LEN, D_MODEL)]
  return pl.pallas_call(
      functools.partial(_diff_attn_body, lam_init=lam_init, cached=lat),
      grid=grid, in_specs=in_specs, out_specs=ospec,
      out_shape=jax.ShapeDtypeStruct(oshape, F32),
      scratch_shapes=[pltpu.VMEM((n_keys, tiles * LANES), BF16)] * 2,
      compiler_params=_cparams(("parallel", "parallel", "arbitrary")),
      name="diff_attention",
  )(*args)


def _pair_attn_body(*refs, scale, with_pe, cached):
  refs = list(refs)
  q_ref = refs.pop(0)
  qp_ref = refs.pop(0) if with_pe else None
  k_ref = refs.pop(0)
  kp_ref = refs.pop(0) if with_pe else None
  v_ref = refs.pop(0)
  kc_ref = refs.pop(0) if cached else None
  kpc_ref = refs.pop(0) if (cached and with_pe) else None
  vc_ref = refs.pop(0) if cached else None
  o_ref, kb_s, vb_s = refs

  tiles = q_ref.shape[1] // LANES
  kw = kb_s.shape[1] // tiles
  key_view = lambda tl: kb_s.at[:, tl * kw:(tl + 1) * kw]

  @pl.when(pl.program_id(2) == 0)
  def _():
    for tl in range(tiles):
      _stage_keys(key_view(tl), slice(0, LANES), _tile(k_ref, tl), _tile(kc_ref, tl))
      if with_pe:
        _stage_keys(key_view(tl), slice(LANES, 2 * LANES), kp_ref, kpc_ref)
    _stage_keys(vb_s, slice(None), v_ref, vc_ref)

  lane = lax.broadcasted_iota(jnp.int32, (q_ref.shape[0], LANES), 1)
  halves = (lane < NA_HD, lane >= NA_HD)
  for tl in range(tiles):
    q = _tile(q_ref, tl)[...] * scale
    qs = [jnp.where(hm, q, 0.0).astype(BF16) for hm in halves]
    if with_pe:
      tile_id = tl if tiles > 1 else pl.program_id(1)
      qp = (_tile(qp_ref, tl // 2) if tiles > 1 else qp_ref)[...] * scale
      base = (tile_id % 2) * (2 * MLA_ROPE)
      qs = [jnp.concatenate(
          [qs[a], jnp.where((lane >= base + a * MLA_ROPE) & (lane < base + (a + 1) * MLA_ROPE),
                            qp, 0.0).astype(BF16)], axis=1) for a in range(2)]
    _tile(o_ref, tl)[...] = jnp.where(
        halves[0], *_stacked_attention(qs, key_view(tl), _tile(vb_s, tl)))


def _pair_attention(q, k, v, lat, cache=None, scale=1.0, q_pe=None, k_pe=None, kv_joint=True):
  grid, qspec, kspec, cspec, ospec, n_keys, oshape, tiles = _attn_specs(q, lat, kv_joint)
  view = lambda a: _pass_view(a, lat)[0]
  b0 = _pass_view(q, lat)[1]
  if not kv_joint:
    k, v = (a.reshape(oshape) for a in (k, v))
  kv_view = view if kv_joint else (lambda a: a)
  with_pe = q_pe is not None
  tq, sq = qspec.block_shape[1], kspec.block_shape[1]
  in_specs, args = [qspec], [view(q)]
  if with_pe:
    if tiles == 1:
      in_specs.append(pl.BlockSpec((None, tq, LANES), lambda bi, h, qi: (bi + b0, qi, h // 2)))
    else:
      in_specs.append(pl.BlockSpec((None, tq, q_pe.shape[-1]), lambda bi, h, qi: (bi + b0, qi, 0)))
    args.append(view(q_pe))
  in_specs.append(kspec)
  args.append(kv_view(k))
  if with_pe:
    in_specs.append(pl.BlockSpec((None, sq, LANES), lambda bi, h, qi: (bi + b0, 0, 0)))
    args.append(view(k_pe))
  in_specs.append(kspec)
  args.append(kv_view(v))
  if lat:
    in_specs.append(cspec)
    args.append(cache[0])
    if with_pe:
      in_specs.append(pl.BlockSpec((None, PAST_LEN, LANES), lambda bi, h, qi: (bi, 0, 0)))
      args.append(cache[1])
    in_specs.append(cspec)
    args.append(cache[-1])
  return pl.pallas_call(
      functools.partial(_pair_attn_body, scale=scale, with_pe=with_pe, cached=lat),
      grid=grid, in_specs=in_specs, out_specs=ospec,
      out_shape=jax.ShapeDtypeStruct(oshape, F32),
      scratch_shapes=[pltpu.VMEM((n_keys, tiles * (2 * LANES if with_pe else LANES)), BF16),
                      pltpu.VMEM((n_keys, tiles * LANES), BF16)],
      compiler_params=_cparams(("parallel", "parallel", "arbitrary")),
      name="pair_attention_pe" if with_pe else "pair_attention",
  )(*args)


def _na_lat_body(q_ref, k_ref, v_ref, kc_ref, vc_ref, bias_ref, o_ref, kb_s, vb_s, kcb_s, vcb_s):
  rb = pl.program_id(2)

  @pl.when(rb == 0)
  def _():
    kb_s[...] = k_ref[...].astype(BF16)
    vb_s[...] = v_ref[...].astype(BF16)
    kcb_s[...] = kc_ref[...].astype(BF16)
    vcb_s[...] = vc_ref[...].astype(BF16)

  n_loc = NA_KH * GRID_W
  lane = lax.broadcasted_iota(jnp.int32, (GRID_W, LANES), 1)
  halves = (lane < NA_HD, lane >= NA_HD)
  kc = kcb_s[...]
  vc = vcb_s[...]
  for rr in range(NA_RB):
    r = rb * NA_RB + rr
    start = jnp.clip(r - NA_KH // 2, 0, GRID_R - NA_KH)
    pat = jnp.where(r < NA_KH // 2, 1 + r,
                    jnp.where(r > GRID_R - NA_KH // 2, r - (GRID_R - NA_KH), 0))
    off = pl.multiple_of(start * GRID_W, GRID_W)
    kw = kb_s[pl.ds(off, n_loc), :]
    vw = vb_s[pl.ds(off, n_loc), :]
    q = q_ref[rr * GRID_W:(rr + 1) * GRID_W, :]
    qq = jnp.concatenate([jnp.where(hm, q, 0.0).astype(BF16) for hm in halves], axis=0)
    bias = bias_ref[pat].reshape(2 * GRID_W, n_loc)
    s_loc = lax.dot_general(qq, kw, _NT, preferred_element_type=F32) + bias
    s_ctx = lax.dot_general(qq, kc, _NT, preferred_element_type=F32)
    m = jnp.maximum(jnp.max(s_loc, axis=-1, keepdims=True),
                    jnp.max(s_ctx, axis=-1, keepdims=True))
    p_loc = jnp.exp(s_loc - m)
    p_ctx = jnp.exp(s_ctx - m)
    l = jnp.sum(p_loc, axis=-1, keepdims=True) + jnp.sum(p_ctx, axis=-1, keepdims=True)
    o = (jnp.dot(p_loc.astype(BF16), vw, preferred_element_type=F32)
         + jnp.dot(p_ctx.astype(BF16), vc, preferred_element_type=F32)) / l
    o_ref[rr * GRID_W:(rr + 1) * GRID_W, :] = jnp.where(halves[0], o[:GRID_W], o[GRID_W:])


def _na_lat_attention(q, k, v, kc, vc, bias):
  qv, b0, b = _pass_view(q, True)
  kv, vv = (a.reshape(b, DEC_SEQ, D_MODEL) for a in (k, v))
  n_pat = bias.shape[0]
  blk = NA_RB * GRID_W
  qspec = pl.BlockSpec((None, blk, LANES), lambda bi, h, r: (bi + b0, r, h))
  kspec = pl.BlockSpec((None, DEC_SEQ, LANES), lambda bi, h, r: (bi, 0, h))
  cspec = pl.BlockSpec((None, PAST_LEN, LANES), lambda bi, h, r: (bi, 0, h))
  bspec = pl.BlockSpec((n_pat, 2, GRID_W, NA_KH * GRID_W), lambda bi, h, r: (0, h, 0, 0))
  return pl.pallas_call(
      _na_lat_body,
      grid=(b, D_MODEL // LANES, GRID_R // NA_RB),
      in_specs=[qspec, kspec, kspec, cspec, cspec, bspec],
      out_specs=pl.BlockSpec((None, blk, LANES), lambda bi, h, r: (bi, r, h)),
      out_shape=jax.ShapeDtypeStruct((b, DEC_SEQ, D_MODEL), F32),
      scratch_shapes=[pltpu.VMEM((DEC_SEQ, LANES), BF16)] * 2
      + [pltpu.VMEM((PAST_LEN, LANES), BF16)] * 2,
      compiler_params=_cparams(("parallel", "parallel", "arbitrary")),
      name="na_lat_attention",
  )(qv, kv, vv, kc, vc, bias)


def _sort_network(n):
  pairs = []
  p = 1
  while p < n:
    k = p
    while k >= 1:
      for j in range(k % p, n - k, 2 * k):
        for i in range(min(k, n - j - k)):
          if (i + j) // (2 * p) == (i + j + k) // (2 * p):
            pairs.append((i + j, i + j + k))
      k //= 2
    p *= 2
  return pairs


def _pop_heads(lists, hit, depth):
  for k in range(depth):
    lists[k] = jnp.where(hit, lists[k + 1], lists[k])


def _top16_sorted(s):
  t = s.shape[1]
  n = s.shape[0] // SUBLANES
  xs = [s[SUBLANES * k:SUBLANES * (k + 1), :] for k in range(n)]
  for i, j in _sort_network(n):
    xs[i], xs[j] = jnp.maximum(xs[i], xs[j]), jnp.minimum(xs[i], xs[j])
  rows = lax.broadcasted_iota(jnp.int32, (PEER_TOPK, t), 0)
  v = jnp.zeros((PEER_TOPK, t), F32)
  for r in range(PEER_TOPK):
    m = jnp.max(xs[0], axis=0, keepdims=True)
    v = jnp.where(rows == r, m, v)
    _pop_heads(xs, xs[0] == m, PEER_TOPK - 1 - r)
  return v


def _router_body(oc_ref, ol_ref, wo_ref, x_ref, g1_ref, g_ref, sc_ref, sh_ref,
                 wq_ref, k1_ref, k2_ref,
                 xn_ref, hb_ref, a1_ref, n1_ref, b2_ref, r2_ref, h_s):
  def first_head(o_ref):
    xn = x_ref[...] + g1_ref[...] * jnp.dot(o_ref[...].astype(BF16), wo_ref[...],
                                            preferred_element_type=F32)
    xn_ref[...] = xn
    h = _rms_mod(xn, g_ref[...], sc_ref[...], sh_ref[...])
    h_s[...] = h.astype(BF16)
    hb_ref[...] = h.T.astype(BF16)

  is_first = pl.program_id(1) == 0
  is_ctx = pl.program_id(0) < N_CTX // ROUTER_T
  pl.when(is_first & is_ctx)(lambda: first_head(oc_ref))
  pl.when(is_first & jnp.logical_not(is_ctx))(lambda: first_head(ol_ref))

  qh = jnp.dot(h_s[...], wq_ref[...], preferred_element_type=F32)
  for hh in range(ROUTER_HEADS):
    _route_head(qh[:, hh * PEER_DK:(hh + 1) * PEER_DK], k1_ref[hh], k2_ref[hh],
                a1_ref.at[hh], n1_ref.at[hh], b2_ref.at[hh], r2_ref.at[hh])


def _route_head(qh, k1, k2, a1_ref, n1_ref, b2_ref, r2_ref):
  half = PEER_DK // 2
  s1 = lax.dot_general(k1, qh[:, :half].astype(BF16), _NT, preferred_element_type=F32)
  s2 = lax.dot_general(k2, qh[:, half:].astype(BF16), _NT, preferred_element_type=F32)
  s1 = s1 - jnp.max(s1, axis=0, keepdims=True)
  s2 = s2 - jnp.max(s2, axis=0, keepdims=True)
  v1 = _top16_sorted(s1)
  v2 = _top16_sorted(s2)
  cand = [v1[:SUBLANES] + v2[k:k + 1] for k in range(PEER_TOPK)]
  tail = v1[SUBLANES:] + v2[0:1]
  z = jnp.zeros_like(v2[0:1])
  taken = jnp.zeros_like(cand[0])
  for r in range(PEER_TOPK):
    th = jnp.max(jnp.maximum(cand[0], tail), axis=0, keepdims=True)
    z = z + jnp.exp(th)
    tail = jnp.where(tail == th, NEG_INF, tail)
    hit = cand[0] == th
    taken = taken + jnp.where(hit, 1.0, 0.0)
    _pop_heads(cand, hit, PEER_TOPK - 1 - r)
  inf = jnp.float32(jnp.inf)
  first = jnp.minimum(jnp.min(jnp.where(taken >= 1.0, v1[:SUBLANES], inf), axis=0, keepdims=True),
                      jnp.min(jnp.where(tail == NEG_INF, v1[SUBLANES:], inf), axis=0,
                              keepdims=True))
  n1 = jnp.where(s1 >= first, 1.0, 0.0)
  k_max = PEER_TOPK // 2
  for k in range(2, k_max + 1):
    t_k = jnp.min(jnp.where(taken >= float(k), v1[:SUBLANES], inf), axis=0, keepdims=True)
    n1 = n1 + jnp.where(s1 >= t_k, 1.0, 0.0)
  n1 = n1 + jnp.where(s1 >= v1[0:1], jnp.maximum(taken[0:1] - float(k_max), 0.0), 0.0)
  r2 = jnp.zeros_like(s2)
  for r in range(PEER_TOPK):
    r2 = r2 + jnp.where(v2[r:r + 1] > s2, 1.0, 0.0)
  a1_ref[...] = jnp.exp(s1)
  n1_ref[...] = n1
  b2_ref[...] = pltpu.bitcast((jnp.exp(s2) * (1.0 / z)).astype(BF16), jnp.uint32)
  r2_ref[...] = pltpu.bitcast(r2.astype(BF16), jnp.uint32)


def _peer_router(oc, ol, w_o, x, gate1, g, sc, sh, wq, k1, k2):
  n = x.shape[0]
  tm = ROUTER_T
  nb_ctx = N_CTX // tm
  row = lambda w: pl.BlockSpec((tm, w), lambda i, h: (i, 0))
  full = lambda shape: pl.BlockSpec(shape, lambda i, h: (0,) * len(shape))
  modspec = pl.BlockSpec((None, 1, D_MODEL), lambda i, h: (_mod_row(i, tm), 0, 0))
  hs = ROUTER_HEADS
  keyspec = pl.BlockSpec((hs, PEER_NKEYS, PEER_DK // 2), lambda i, h: (h, 0, 0))
  tspec = pl.BlockSpec((hs, PEER_NKEYS, tm), lambda i, h: (h, 0, i))
  pspec = pl.BlockSpec((hs, PEER_NKEYS // 2, tm), lambda i, h: (h, 0, i))
  return pl.pallas_call(
      _router_body,
      grid=(n // tm, PEER_HEADS // hs),
      in_specs=[pl.BlockSpec((tm, D_MODEL), lambda i, h: (jnp.minimum(i, nb_ctx - 1), 0)),
                pl.BlockSpec((tm, D_MODEL), lambda i, h: (jnp.maximum(i - nb_ctx, 0), 0)),
                full(w_o.shape), row(D_MODEL), modspec,
                full((1, D_MODEL)), modspec, modspec,
                pl.BlockSpec((D_MODEL, hs * PEER_DK), lambda i, h: (0, h)), keyspec, keyspec],
      out_specs=[row(D_MODEL), pl.BlockSpec((D_MODEL, tm), lambda i, h: (0, i)),
                 tspec, tspec, pspec, pspec],
      out_shape=[jax.ShapeDtypeStruct((n, D_MODEL), F32), jax.ShapeDtypeStruct((D_MODEL, n), BF16)]
      + [jax.ShapeDtypeStruct((PEER_HEADS, PEER_NKEYS, n), F32)] * 2
      + [jax.ShapeDtypeStruct((PEER_HEADS, PEER_NKEYS // 2, n), jnp.uint32)] * 2,
      scratch_shapes=[pltpu.VMEM((tm, D_MODEL), BF16)],
      compiler_params=_cparams(("parallel", "arbitrary"), VMEM_LIMIT_PEER),
      name="peer_router",
  )(oc.reshape(N_CTX, D_MODEL), ol.reshape(N_LAT, D_MODEL), w_o, x, gate1, g, sc, sh, wq, k1, k2)


def _gelu(x):
  return 0.5 * x * (1.0 + lax.erf(x * np.float32(math.sqrt(0.5))))


def _peer_mix_body(hb_ref, u_ref, vt_ref, a1_ref, n1_ref, b2_ref, r2_ref, x_ref, gate_ref,
                   fg_ref, y_ref, *scratch, final_norm):
  c = pl.program_id(1)
  n_pieces = PEER_EC // PEER_SUBC
  if final_norm:
    yl_ref, acc_s, *piece_s = scratch
  else:
    acc_s, *piece_s = scratch

  @pl.when(c == 0)
  def _():
    acc_s[...] = jnp.zeros_like(acc_s)

  hbt = hb_ref[...]
  t = hbt.shape[1]
  pk = BF16_ROWS
  for j in range(n_pieces):
    rows = slice(j * PEER_SUBC, (j + 1) * PEER_SUBC)
    piece_s[j][...] = jnp.dot(u_ref[rows, :], hbt, preferred_element_type=F32)
  for j in range(n_pieces):
    s_s, p_s = piece_s[j], piece_s[n_pieces + j]
    for ii in range(PEER_SUBC // PEER_NKEYS):
      i1 = j * (PEER_SUBC // PEER_NKEYS) + ii
      w = [jnp.zeros((pk, t), BF16) for _ in range(PEER_NKEYS // pk)]
      for h in range(PEER_HEADS):
        a_row = jnp.broadcast_to(a1_ref[h, i1:i1 + 1, :], (pk, t)).astype(BF16)
        n_row = jnp.broadcast_to(n1_ref[h, i1:i1 + 1, :], (pk, t)).astype(BF16)
        for sub in range(PEER_NKEYS // pk):
          words = slice(sub * pk // 2, (sub + 1) * pk // 2)
          prod = a_row * pltpu.bitcast(b2_ref[h, words, :], BF16)
          rank = pltpu.bitcast(r2_ref[h, words, :], BF16)
          w[sub] = w[sub] + jnp.where(rank < n_row, prod, jnp.zeros_like(prod))
      for sub in range(PEER_NKEYS // pk):
        row0 = ii * PEER_NKEYS + sub * pk
        p_s[row0:row0 + pk, :] = w[sub] * _gelu(s_s[row0:row0 + pk, :]).astype(BF16)
    rows = slice(j * PEER_SUBC, (j + 1) * PEER_SUBC)
    acc_s[...] += jnp.dot(vt_ref[:, rows], p_s[...], preferred_element_type=F32)

  @pl.when(c == pl.num_programs(1) - 1)
  def _():
    y = x_ref[...] + gate_ref[...] * acc_s[...].T
    if final_norm:
      _store_by_pass(pl.program_id(0), t, _rms(y, fg_ref[...]), y_ref, yl_ref)
    else:
      y_ref[...] = y


def _peer_mix(hb, u, vt, a1, n1, b2, r2, x, gate, final_g, final_norm):
  n = x.shape[0]
  t = PEER_T
  n_i1 = PEER_EC // PEER_NKEYS
  row = lambda w: pl.BlockSpec((t, w), lambda i, c: (i, 0))
  i1spec = pl.BlockSpec((PEER_HEADS, n_i1, t), lambda i, c: (0, c, i))
  i2spec = pl.BlockSpec((PEER_HEADS, PEER_NKEYS // 2, t), lambda i, c: (0, 0, i))
  return pl.pallas_call(
      functools.partial(_peer_mix_body, final_norm=final_norm),
      grid=(n // t, PEER_EXPERTS // PEER_EC),
      in_specs=[pl.BlockSpec((D_MODEL, t), lambda i, c: (0, i)),
                pl.BlockSpec((PEER_EC, D_MODEL), lambda i, c: (c, 0)),
                pl.BlockSpec((None, D_MODEL, PEER_EC), lambda i, c: (c, 0, 0)),
                i1spec, i1spec, i2spec, i2spec,
                row(D_MODEL),
                pl.BlockSpec((None, 1, D_MODEL), lambda i, c: (_mod_row(i, t), 0, 0)),
                pl.BlockSpec((1, D_MODEL), lambda i, c: (0, 0))],
      out_specs=_pass_specs(D_MODEL, t) if final_norm else row(D_MODEL),
      out_shape=_pass_shapes(D_MODEL) if final_norm else jax.ShapeDtypeStruct((n, D_MODEL), F32),
      scratch_shapes=[pltpu.VMEM((D_MODEL, t), F32)]
      + [pltpu.VMEM((PEER_SUBC, t), F32)] * (PEER_EC // PEER_SUBC)
      + [pltpu.VMEM((PEER_SUBC, t), BF16)] * (PEER_EC // PEER_SUBC),
      compiler_params=_cparams(("arbitrary", "arbitrary"), VMEM_LIMIT_PEER),
      name="peer_mix",
  )(hb, u, vt, a1, n1, b2, r2, x, gate, final_g)


def _rotated_tiles(w, group):
  k, n = w.shape
  g = w.reshape(k, n // group, 2, group // 2)
  return jnp.concatenate([-g[:, :, 1:], g[:, :, :1]], axis=2).reshape(k, n)


def _rope_table(rot_dims, reps):
  half = rot_dims // 2
  t = jnp.arange(DEC_SEQ)
  inv = ROPE_BASE ** (-jnp.arange(half, dtype=F32) / half)
  parts_c, parts_s = [], []
  for pos in (t // GRID_W, t % GRID_W):
    ang = pos.astype(F32)[:, None] * inv[None, :]
    parts_c += [jnp.cos(ang), jnp.cos(ang)]
    parts_s += [jnp.sin(ang), jnp.sin(ang)]
  cos = jnp.tile(jnp.concatenate(parts_c, axis=1), (1, reps))
  sin = jnp.tile(jnp.concatenate(parts_s, axis=1), (1, reps))
  cos = jnp.concatenate([cos, jnp.ones((TM, LANES), F32)], axis=0)
  sin = jnp.concatenate([sin, jnp.zeros((TM, LANES), F32)], axis=0)
  return cos, sin


def _na_bias_table(rpb):
  reps = [NA_KH // 2] + list(range(NA_KH // 2)) + list(range(GRID_R - NA_KH // 2 + 1, GRID_R))
  cols = np.arange(GRID_W)
  col_start = np.clip(cols - NA_KW // 2, 0, GRID_W - NA_KW)
  col_mask = (cols[None, :] >= col_start[:, None]) & (cols[None, :] < col_start[:, None] + NA_KW)
  dc = np.clip(cols[None, :] - cols[:, None], -(NA_KW - 1), NA_KW - 1) + (NA_KW - 1)
  dr = np.stack([np.clip(r - NA_KH // 2, 0, GRID_R - NA_KH) + np.arange(NA_KH) - r + (NA_KH - 1)
                 for r in reps])
  onehot = (dc[:, :, None] == np.arange(2 * NA_KW - 1)).astype(np.float32)
  bias = jnp.einsum('hpjc,qwc->phqjw', rpb[:, dr, :], onehot, precision=lax.Precision.HIGHEST)
  bias = jnp.where(col_mask[None, None, :, None, :], bias, NEG_INF)
  return bias.reshape(len(reps), NA_HEADS, GRID_W, NA_KH * GRID_W)


def _ctx_rows(a, *shape):
  return a[:N_CTX].reshape(BATCH, SEQ, *shape)


def _diff_mixer(layer, p, x, sc, sh, cache, tables):
  w = p['w_qkv']
  d = D_MODEL
  wq, wk, wv = w[:, :d], w[:, d:2 * d], w[:, 2 * d:]
  rot = lambda m: _rotated_tiles(m, DA_QK // 2)
  w5 = jnp.concatenate([wq, rot(wq), wk, rot(wk), wv], axis=1).astype(BF16)
  cos, sin = tables['da']
  q, (kc, kl), (vc, vl) = _qkv_rope(x, p['norm1_g'], sc, sh, w5, cos, sin, DA_QK ** -0.5)
  lam_rows = jnp.zeros((SUBLANES, LANES), F32)
  for i, nme in enumerate(('lam_q1', 'lam_k1', 'lam_q2', 'lam_k2')):
    lam_rows = lam_rows.at[i, :DA_QK].set(p[nme])
  g = p['subln_g'].reshape(1, DA_V)
  oc = _diff_attention(q, kc, vc, None, lam_rows, g, layer, lat=False)
  ol = _diff_attention(q, kl, vl, cache, lam_rows, g, layer, lat=True)
  state = (kc.reshape(BATCH, SEQ, DA_HEADS, 2 * DA_QK), vc.reshape(BATCH, SEQ, DA_HEADS, DA_V))
  return oc, ol, state


def _na_mixer(p, x, sc, sh, cache):
  d = D_MODEL
  q, (kc, kl), (vc, vl) = _qkv_plain(x, p['norm1_g'], sc, sh, p['w_qkv'].astype(BF16),
                                     NA_HD ** -0.5)
  oc = _pair_attention(q, kc, vc, lat=False, kv_joint=False)
  bias = _na_bias_table(p['rpb'])
  ol = _na_lat_attention(q, kl, vl, cache[0].reshape(DEC_BATCH, PAST_LEN, d),
                         cache[1].reshape(DEC_BATCH, PAST_LEN, d), bias)
  state = (kc.reshape(BATCH, SEQ, NA_HEADS, NA_HD), vc.reshape(BATCH, SEQ, NA_HEADS, NA_HD))
  return oc, ol, state


def _mla_mixer(p, x, sc, sh, cache, tables):
  a, b = MLA_Q_RANK, MLA_Q_RANK + MLA_KV_RANK
  w_in = p['w_in']
  kpe_w = w_in[:, b:]
  w_in_x = jnp.concatenate([w_in[:, :b], jnp.tile(kpe_w, (1, 4)),
                            jnp.tile(_rotated_tiles(kpe_w, MLA_ROPE // 2), (1, 4))],
                           axis=1).astype(BF16)
  cos, sin = tables['mla']
  cq, ckv, kpe = _mla_in(x, p['norm1_g'], sc, sh, w_in_x, p['q_norm_g'].reshape(1, a),
                         p['kv_norm_g'].reshape(1, MLA_KV_RANK), cos, sin)
  w_uq = p['w_uq'].reshape(a, MLA_HEADS, MLA_NOPE + MLA_ROPE)
  w_qn = w_uq[:, :, :MLA_NOPE].reshape(a, -1)
  w_qp = w_uq[:, :, MLA_NOPE:].reshape(a, -1)
  w_uq_x = jnp.concatenate([w_qn, w_qp, _rotated_tiles(w_qp, MLA_ROPE // 2)], axis=1).astype(BF16)
  qn, qp = _mla_q(cq, w_uq_x, cos, sin)
  w_ukv = p['w_ukv'].reshape(MLA_KV_RANK, MLA_HEADS, MLA_NOPE + MLA_V)
  w_ukv_x = jnp.concatenate([w_ukv[:, :, :MLA_NOPE].reshape(MLA_KV_RANK, -1),
                             w_ukv[:, :, MLA_NOPE:].reshape(MLA_KV_RANK, -1)], axis=1).astype(BF16)
  kn, v = _mla_kv(ckv, w_ukv_x)
  knc, vc = _mla_kv(cache[0].reshape(DEC_BATCH * PAST_LEN, MLA_KV_RANK), w_ukv_x)
  cached = (knc.reshape(DEC_BATCH, PAST_LEN, D_MODEL), jnp.tile(cache[1], (1, 1, LANES // MLA_ROPE)),
            vc.reshape(DEC_BATCH, PAST_LEN, D_MODEL))
  scale = (MLA_NOPE + MLA_ROPE) ** -0.5
  oc = _pair_attention(qn, kn, v, lat=False, scale=scale, q_pe=qp, k_pe=kpe)
  ol = _pair_attention(qn, kn, v, lat=True, cache=cached, scale=scale, q_pe=qp, k_pe=kpe)
  state = (_ctx_rows(ckv, MLA_KV_RANK), _ctx_rows(kpe[:, :MLA_ROPE], MLA_ROPE))
  return oc, ol, state


def _layer(layer, p, x, cond8, cache, tables, final_g):
  sh1, sc1, g1, sh2, sc2, g2 = _modulation(cond8, p['ada_w'], p['ada_b'])
  kind = layer % 3
  if kind == 0:
    oc, ol, state = _diff_mixer(layer, p, x, sc1, sh1, cache, tables)
  elif kind == 1:
    oc, ol, state = _na_mixer(p, x, sc1, sh1, cache)
  else:
    oc, ol, state = _mla_mixer(p, x, sc1, sh1, cache, tables)
  x, hb, a1, n1, b2, r2 = _peer_router(oc, ol, p['w_o'].astype(BF16), x, g1, p['norm2_g'],
                                       sc2, sh2, p['peer_wq'], p['peer_k1'], p['peer_k2'])
  x = _peer_mix(hb, p['peer_u'], p['peer_vt'], a1, n1, b2, r2, x, g2, final_g,
                final_norm=layer == DEPTH - 1)
  return x, state


def kernel(x_prompt, x_sample, cache_l0_k, cache_l0_v, cache_l1_k, cache_l1_v, cache_l2_ckv, cache_l2_kpe, cache_l3_k, cache_l3_v, c, c_ctx, l0_norm1_g, l0_norm2_g, l0_ada_w, l0_ada_b, l0_w_qkv, l0_w_o, l0_lam_q1, l0_lam_k1, l0_lam_q2, l0_lam_k2, l0_subln_g, l0_peer_wq, l0_peer_k1, l0_peer_k2, l0_peer_u, l0_peer_v, l1_norm1_g, l1_norm2_g, l1_ada_w, l1_ada_b, l1_w_qkv, l1_w_o, l1_rpb, l1_peer_wq, l1_peer_k1, l1_peer_k2, l1_peer_u, l1_peer_v, l2_norm1_g, l2_norm2_g, l2_ada_w, l2_ada_b, l2_w_in, l2_q_norm_g, l2_w_uq, l2_kv_norm_g, l2_w_ukv, l2_w_o, l2_peer_wq, l2_peer_k1, l2_peer_k2, l2_peer_u, l2_peer_v, l3_norm1_g, l3_norm2_g, l3_ada_w, l3_ada_b, l3_w_qkv, l3_w_o, l3_lam_q1, l3_lam_k1, l3_lam_q2, l3_lam_k2, l3_subln_g, l3_peer_wq, l3_peer_k1, l3_peer_k2, l3_peer_u, l3_peer_v, final_norm_g):
  common = lambda n1, n2, aw, ab, wq, k1, k2, u, v: dict(
      norm1_g=n1.reshape(1, D_MODEL), norm2_g=n2.reshape(1, D_MODEL), ada_w=aw, ada_b=ab,
      peer_wq=wq.astype(BF16), peer_k1=k1.astype(BF16), peer_k2=k2.astype(BF16),
      peer_u=u.astype(BF16),
      peer_vt=v.reshape(PEER_EXPERTS // PEER_EC, PEER_EC, D_MODEL).transpose(0, 2, 1).astype(BF16))
  p0 = dict(common(l0_norm1_g, l0_norm2_g, l0_ada_w, l0_ada_b, l0_peer_wq, l0_peer_k1, l0_peer_k2,
                   l0_peer_u, l0_peer_v),
            w_qkv=l0_w_qkv, w_o=l0_w_o, lam_q1=l0_lam_q1, lam_k1=l0_lam_k1, lam_q2=l0_lam_q2,
            lam_k2=l0_lam_k2, subln_g=l0_subln_g)
  p1 = dict(common(l1_norm1_g, l1_norm2_g, l1_ada_w, l1_ada_b, l1_peer_wq, l1_peer_k1, l1_peer_k2,
                   l1_peer_u, l1_peer_v),
            w_qkv=l1_w_qkv, w_o=l1_w_o, rpb=l1_rpb)
  p2 = dict(common(l2_norm1_g, l2_norm2_g, l2_ada_w, l2_ada_b, l2_peer_wq, l2_peer_k1, l2_peer_k2,
                   l2_peer_u, l2_peer_v),
            w_in=l2_w_in, q_norm_g=l2_q_norm_g, w_uq=l2_w_uq, kv_norm_g=l2_kv_norm_g,
            w_ukv=l2_w_ukv, w_o=l2_w_o)
  p3 = dict(common(l3_norm1_g, l3_norm2_g, l3_ada_w, l3_ada_b, l3_peer_wq, l3_peer_k1, l3_peer_k2,
                   l3_peer_u, l3_peer_v),
            w_qkv=l3_w_qkv, w_o=l3_w_o, lam_q1=l3_lam_q1, lam_k1=l3_lam_k1, lam_q2=l3_lam_q2,
            lam_k2=l3_lam_k2, subln_g=l3_subln_g)
  params = (p0, p1, p2, p3)
  caches = ((cache_l0_k, cache_l0_v), (cache_l1_k, cache_l1_v),
            (cache_l2_ckv, cache_l2_kpe), (cache_l3_k, cache_l3_v))
  tables = dict(da=_rope_table(DA_QK // 2, LANES // DA_QK),
                mla=_rope_table(MLA_ROPE // 2, LANES // MLA_ROPE))

  cond8 = jnp.zeros((N_COND, D_MODEL), F32).at[0].set(c_ctx).at[1:1 + DEC_BATCH].set(c)
  x = jnp.concatenate([x_prompt.reshape(N_CTX, D_MODEL), x_sample.reshape(N_LAT, D_MODEL)], axis=0)
  states = []
  for layer in range(DEPTH):
    x, st = _layer(layer, params[layer], x, cond8, caches[layer], tables,
                   final_norm_g.reshape(1, D_MODEL))
    states.extend(st)
  y_ctx, y_lat = x
  return (y_ctx.reshape(BATCH, SEQ, D_MODEL), y_lat.reshape(DEC_BATCH, DEC_SEQ, D_MODEL), *states)
```

```python
import functools
import math

import numpy as np
import jax
import jax.numpy as jnp
from jax import lax
from jax.experimental import pallas as pl
from jax.experimental.pallas import tpu as pltpu

F32 = jnp.float32
BF16 = jnp.bfloat16

D_MODEL = 1024
BATCH = 32
SEQ = 256
DEPTH = 4
DEC_BATCH = 2
DEC_SEQ = 4096
PAST_LEN = 512
GRID_W = 64
GRID_R = DEC_SEQ // GRID_W
EPS = 1e-6
ROPE_BASE = 10000.0
NEG_INF = -1e30

DA_HEADS = 8
DA_QK = 64
DA_V = 128
NA_HEADS = 16
NA_HD = 64
NA_KH = 8
NA_KW = 16
MLA_HEADS = 16
MLA_NOPE = 64
MLA_ROPE = 32
MLA_V = 64
MLA_Q_RANK = 384
MLA_KV_RANK = 256
PEER_HEADS = 8
PEER_NKEYS = 128
PEER_EXPERTS = PEER_NKEYS * PEER_NKEYS
PEER_DK = 256
PEER_TOPK = 16

N_CTX = BATCH * SEQ
N_LAT = DEC_BATCH * DEC_SEQ
N_TOK = N_CTX + N_LAT
N_COND = 8

LANES = 128
SUBLANES = 8
BF16_ROWS = 16
VMEM_LIMIT = 48 << 20
VMEM_LIMIT_PEER = 56 << 20

TM = 256
MOD_TN = 1536
TQ = 512
TK = 1536
ATTN_UNROLL = 3
NA_RB = 8
ROUTER_T = 1024
ROUTER_HEADS = 2
PEER_T = 512
PEER_EC = 2048
PEER_SUBC = 256

_NT = (((1,), (1,)), ((), ()))


def _cparams(sem, vmem=VMEM_LIMIT):
  return pltpu.CompilerParams(dimension_semantics=sem, vmem_limit_bytes=vmem)


def _mod_row(i, tm):
  nb_ctx = N_CTX // tm
  nb_bat = DEC_SEQ // tm
  return jnp.where(i < nb_ctx, 0, 1 + (i - nb_ctx) // nb_bat)


def _rope_blk(i, tm):
  nb_ctx = N_CTX // tm
  nb_bat = DEC_SEQ // tm
  return jnp.where(i < nb_ctx, nb_bat, (i - nb_ctx) % nb_bat)


def _rms_mod(x, g, sc, sh):
  y = x * lax.rsqrt(jnp.mean(x * x, axis=-1, keepdims=True) + EPS)
  return (y * g) * (1.0 + sc) + sh


def _rms(x, g):
  return x * lax.rsqrt(jnp.mean(x * x, axis=-1, keepdims=True) + EPS) * g


def _row_spec(width, tm=TM):
  return pl.BlockSpec((tm, width), lambda i: (i, 0))


def _full_spec(shape):
  return pl.BlockSpec(shape, lambda i: (0,) * len(shape))


def _mod_spec(tm=TM):
  return pl.BlockSpec((None, 1, D_MODEL), lambda i: (_mod_row(i, tm), 0, 0))


def _pass_specs(width, tm, index=lambda i, *_: i):
  nb = N_CTX // tm
  return [pl.BlockSpec((tm, width), lambda *g: (jnp.minimum(index(*g), nb - 1), 0)),
          pl.BlockSpec((tm, width), lambda *g: (jnp.maximum(index(*g) - nb, 0), 0))]


def _pass_shapes(width, dtype=F32):
  return [jax.ShapeDtypeStruct((N_CTX, width), dtype), jax.ShapeDtypeStruct((N_LAT, width), dtype)]


def _store_by_pass(i, tm, val, ctx_ref, lat_ref):
  is_ctx = i < N_CTX // tm

  @pl.when(is_ctx)
  def _():
    ctx_ref[...] = val

  @pl.when(jnp.logical_not(is_ctx))
  def _():
    lat_ref[...] = val


def _mod_body(c_ref, w_ref, b_ref, o_ref):
  c = c_ref[...]
  s = c / (1.0 + jnp.exp(-c))
  o_ref[...] = jnp.dot(s, w_ref[...], precision=lax.Precision.HIGHEST,
                       preferred_element_type=F32) + b_ref[...]


def _modulation(cond8, ada_w, ada_b):
  n = ada_w.shape[1]
  tn = MOD_TN
  out = pl.pallas_call(
      _mod_body,
      grid=(n // tn,),
      in_specs=[pl.BlockSpec((N_COND, D_MODEL), lambda j: (0, 0)),
                pl.BlockSpec((D_MODEL, tn), lambda j: (0, j)),
                pl.BlockSpec((1, tn), lambda j: (0, j))],
      out_specs=pl.BlockSpec((N_COND, tn), lambda j: (0, j)),
      out_shape=jax.ShapeDtypeStruct((N_COND, n), F32),
      compiler_params=_cparams(("parallel",)),
      name="modulation",
  )(cond8, ada_w, ada_b.reshape(1, n))
  return [out[:, k * D_MODEL:(k + 1) * D_MODEL].reshape(N_COND, 1, D_MODEL)
          for k in range(n // D_MODEL)]


def _tile_lanes(t, reps):
  return jnp.concatenate([t] * reps, axis=1)


def _qkv_rope_body(x_ref, g_ref, sc_ref, sh_ref, w_ref, cos_ref, sin_ref,
                   q_ref, kc_ref, kl_ref, vc_ref, vl_ref, *, q_scale):
  h = _rms_mod(x_ref[...], g_ref[...], sc_ref[...], sh_ref[...]).astype(BF16)
  reps = D_MODEL // LANES
  cos = _tile_lanes(cos_ref[...], reps)
  sin = _tile_lanes(sin_ref[...], reps)
  d = D_MODEL
  dot = lambda a, b: jnp.dot(h, w_ref[:, a:b], preferred_element_type=F32)
  q_ref[...] = ((dot(0, d) * cos + dot(d, 2 * d) * sin) * q_scale).astype(BF16)
  i = pl.program_id(0)
  _store_by_pass(i, TM, dot(2 * d, 3 * d) * cos + dot(3 * d, 4 * d) * sin, kc_ref, kl_ref)
  _store_by_pass(i, TM, dot(4 * d, 5 * d), vc_ref, vl_ref)


def _qkv_rope(x, g, sc, sh, w5, cos, sin, q_scale):
  n = x.shape[0]
  rope_spec = pl.BlockSpec((TM, LANES), lambda i: (_rope_blk(i, TM), 0))
  q, kc, kl, vc, vl = pl.pallas_call(
      functools.partial(_qkv_rope_body, q_scale=q_scale),
      grid=(n // TM,),
      in_specs=[_row_spec(D_MODEL), _full_spec((1, D_MODEL)), _mod_spec(), _mod_spec(),
                _full_spec(w5.shape), rope_spec, rope_spec],
      out_specs=[_row_spec(D_MODEL)] + _pass_specs(D_MODEL, TM) * 2,
      out_shape=[jax.ShapeDtypeStruct((n, D_MODEL), BF16)] + _pass_shapes(D_MODEL) * 2,
      compiler_params=_cparams(("arbitrary",)),
      name="qkv_rope_proj",
  )(x, g, sc, sh, w5, cos, sin)
  return q, (kc, kl), (vc, vl)


def _qkv_plain_body(x_ref, g_ref, sc_ref, sh_ref, w_ref, q_ref, kc_ref, kl_ref, vc_ref, vl_ref,
                    *, q_scale):
  h = _rms_mod(x_ref[...], g_ref[...], sc_ref[...], sh_ref[...]).astype(BF16)
  d = D_MODEL
  dot = lambda a, b: jnp.dot(h, w_ref[:, a:b], preferred_element_type=F32)
  q_ref[...] = (dot(0, d) * q_scale).astype(BF16)
  i = pl.program_id(0)
  _store_by_pass(i, TM, dot(d, 2 * d), kc_ref, kl_ref)
  _store_by_pass(i, TM, dot(2 * d, 3 * d), vc_ref, vl_ref)


def _qkv_plain(x, g, sc, sh, w3, q_scale):
  n = x.shape[0]
  q, kc, kl, vc, vl = pl.pallas_call(
      functools.partial(_qkv_plain_body, q_scale=q_scale),
      grid=(n // TM,),
      in_specs=[_row_spec(D_MODEL), _full_spec((1, D_MODEL)), _mod_spec(), _mod_spec(),
                _full_spec(w3.shape)],
      out_specs=[_row_spec(D_MODEL)] + _pass_specs(D_MODEL, TM) * 2,
      out_shape=[jax.ShapeDtypeStruct((n, D_MODEL), BF16)] + _pass_shapes(D_MODEL) * 2,
      compiler_params=_cparams(("arbitrary",)),
      name="qkv_proj",
  )(x, g, sc, sh, w3)
  return q, (kc, kl), (vc, vl)


def _mla_in_body(x_ref, g_ref, sc_ref, sh_ref, w_ref, qg_ref, kvg_ref, cos_ref, sin_ref,
                 cq_ref, ckv_ref, kpe_ref):
  h = _rms_mod(x_ref[...], g_ref[...], sc_ref[...], sh_ref[...]).astype(BF16)
  z = jnp.dot(h, w_ref[...], preferred_element_type=F32)
  a, b = MLA_Q_RANK, MLA_Q_RANK + MLA_KV_RANK
  cq_ref[...] = _rms(z[:, :a], qg_ref[...]).astype(BF16)
  ckv_ref[...] = _rms(z[:, a:b], kvg_ref[...])
  kpe_ref[...] = z[:, b:b + LANES] * cos_ref[...] + z[:, b + LANES:] * sin_ref[...]


def _mla_in(x, g, sc, sh, w_in, qg, kvg, cos, sin):
  n = x.shape[0]
  rope_spec = pl.BlockSpec((TM, LANES), lambda i: (_rope_blk(i, TM), 0))
  return pl.pallas_call(
      _mla_in_body,
      grid=(n // TM,),
      in_specs=[_row_spec(D_MODEL), _full_spec((1, D_MODEL)), _mod_spec(), _mod_spec(),
                _full_spec(w_in.shape), _full_spec((1, MLA_Q_RANK)),
                _full_spec((1, MLA_KV_RANK)), rope_spec, rope_spec],
      out_specs=[_row_spec(MLA_Q_RANK), _row_spec(MLA_KV_RANK), _row_spec(LANES)],
      out_shape=[jax.ShapeDtypeStruct((n, MLA_Q_RANK), BF16),
                 jax.ShapeDtypeStruct((n, MLA_KV_RANK), F32),
                 jax.ShapeDtypeStruct((n, LANES), F32)],
      compiler_params=_cparams(("parallel",)),
      name="mla_in_proj",
  )(x, g, sc, sh, w_in, qg, kvg, cos, sin)


def _mla_q_body(cq_ref, w_ref, cos_ref, sin_ref, qn_ref, qp_ref):
  z = jnp.dot(cq_ref[...], w_ref[...], preferred_element_type=F32)
  pe = MLA_HEADS * MLA_ROPE
  reps = pe // LANES
  cos = _tile_lanes(cos_ref[...], reps)
  sin = _tile_lanes(sin_ref[...], reps)
  qn_ref[...] = z[:, :D_MODEL]
  qp_ref[...] = z[:, D_MODEL:D_MODEL + pe] * cos + z[:, D_MODEL + pe:] * sin


def _mla_q(cq, w_uq, cos, sin):
  n = cq.shape[0]
  pe = MLA_HEADS * MLA_ROPE
  rope_spec = pl.BlockSpec((TM, LANES), lambda i: (_rope_blk(i, TM), 0))
  return pl.pallas_call(
      _mla_q_body,
      grid=(n // TM,),
      in_specs=[_row_spec(MLA_Q_RANK), _full_spec(w_uq.shape), rope_spec, rope_spec],
      out_specs=[_row_spec(D_MODEL), _row_spec(pe)],
      out_shape=[jax.ShapeDtypeStruct((n, D_MODEL), F32), jax.ShapeDtypeStruct((n, pe), F32)],
      compiler_params=_cparams(("parallel",)),
      name="mla_q_proj",
  )(cq, w_uq, cos, sin)


def _mla_kv_body(c_ref, w_ref, kn_ref, v_ref):
  z = jnp.dot(c_ref[...].astype(BF16), w_ref[...], preferred_element_type=F32)
  kn_ref[...] = z[:, :D_MODEL]
  v_ref[...] = z[:, D_MODEL:]


def _mla_kv(ckv, w_ukv):
  n = ckv.shape[0]
  return pl.pallas_call(
      _mla_kv_body,
      grid=(n // TM,),
      in_specs=[_row_spec(MLA_KV_RANK), _full_spec(w_ukv.shape)],
      out_specs=[_row_spec(D_MODEL)] * 2,
      out_shape=[jax.ShapeDtypeStruct((n, D_MODEL), F32)] * 2,
      compiler_params=_cparams(("parallel",)),
      name="mla_kv_proj",
  )(ckv, w_ukv)


def _online_update(s, m, l, acc, vb):
  m_new = jnp.maximum(m, jnp.max(s, axis=-1, keepdims=True))
  alpha = jnp.exp(m - m_new)
  p = jnp.exp(s - m_new)
  l_new = alpha * l + jnp.sum(p, axis=-1, keepdims=True)
  acc_new = alpha * acc + jnp.dot(p.astype(BF16), vb, preferred_element_type=F32)
  return m_new, l_new, acc_new


def _softmax_state(tq):
  return (jnp.full((tq, 1), NEG_INF, F32), jnp.zeros((tq, 1), F32), jnp.zeros((tq, LANES), F32))


def _stacked_attention(q_pair, kb_s, vb_s):
  tq = q_pair[0].shape[0]
  qq = jnp.concatenate(q_pair, axis=0)
  tk, n_chunks, unroll = _key_chunks(kb_s.shape[0])

  def chunk(c, carry):
    off = pl.multiple_of(c * tk, tk)
    s = lax.dot_general(qq, kb_s[pl.ds(off, tk), :], _NT, preferred_element_type=F32)
    return _online_update(s, *carry, vb_s[pl.ds(off, tk), :])

  _, l, acc = lax.fori_loop(0, n_chunks, chunk, _softmax_state(2 * tq), unroll=unroll)
  o = acc / l
  return o[:tq], o[tq:]


def _pass_view(a, lat):
  w = a.shape[-1]
  if lat:
    return a.reshape(N_TOK // DEC_SEQ, DEC_SEQ, w), N_CTX // DEC_SEQ, DEC_BATCH
  return a.reshape(N_TOK // SEQ, SEQ, w), 0, BATCH


def _stage_keys(dst, lanes, own_ref, cache_ref):
  n_own = own_ref.shape[0]
  dst[:n_own, lanes] = own_ref[...].astype(BF16)
  if cache_ref is not None:
    dst[n_own:, lanes] = cache_ref[...].astype(BF16)


def _key_chunks(n_keys):
  tk = min(TK, n_keys)
  n = n_keys // tk
  return tk, n, (ATTN_UNROLL if n % ATTN_UNROLL == 0 else 1)


def _attn_specs(q, lat, kv_joint=True):
  qv, b0, nb = _pass_view(q, lat)
  kb0 = b0 if kv_joint else 0
  sq = qv.shape[1]
  tq = min(TQ, sq)
  tiles = 1 if lat else D_MODEL // LANES
  width = tiles * LANES
  grid = (nb, D_MODEL // width, sq // tq)
  qspec = pl.BlockSpec((None, tq, width), lambda bi, h, qi: (bi + b0, qi, h))
  kspec = pl.BlockSpec((None, sq, width), lambda bi, h, qi: (bi + kb0, 0, h))
  cspec = pl.BlockSpec((None, PAST_LEN, width), lambda bi, h, qi: (bi, 0, h))
  ospec = pl.BlockSpec((None, tq, width), lambda bi, h, qi: (bi, qi, h))
  n_keys = sq + (PAST_LEN if lat else 0)
  return grid, qspec, kspec, cspec, ospec, n_keys, (nb, sq, D_MODEL), tiles


def _tile(ref, tl):
  return None if ref is None else ref.at[:, tl * LANES:(tl + 1) * LANES]


def _diff_attn_body(*refs, lam_init, cached):
  if cached:
    lam_ref, g_ref, q_ref, k_ref, v_ref, kc_ref, vc_ref, o_ref, kb_s, vb_s = refs
  else:
    lam_ref, g_ref, q_ref, k_ref, v_ref, o_ref, kb_s, vb_s = refs
    kc_ref = vc_ref = None

  @pl.when(pl.program_id(2) == 0)
  def _():
    _stage_keys(kb_s, slice(None), k_ref, kc_ref)
    _stage_keys(vb_s, slice(None), v_ref, vc_ref)

  lv = lam_ref[...]
  lam = (jnp.exp(jnp.sum(lv[0:1] * lv[1:2], axis=-1, keepdims=True))
         - jnp.exp(jnp.sum(lv[2:3] * lv[3:4], axis=-1, keepdims=True)) + lam_init)
  lane = lax.broadcasted_iota(jnp.int32, (q_ref.shape[0], LANES), 1)
  for tl in range(q_ref.shape[1] // LANES):
    q = _tile(q_ref, tl)[...]
    q1 = jnp.where(lane < DA_QK, q, 0.0).astype(BF16)
    q2 = jnp.where(lane >= DA_QK, q, 0.0).astype(BF16)
    o1, o2 = _stacked_attention((q1, q2), _tile(kb_s, tl), _tile(vb_s, tl))
    _tile(o_ref, tl)[...] = _rms(o1 - lam * o2, g_ref[...]) * (1.0 - lam_init)


def _diff_attention(q, k, v, cache, lam_rows, subln_g, layer, lat):
  grid, qspec, kspec, cspec, ospec, n_keys, oshape, tiles = _attn_specs(q, lat, kv_joint=False)
  nb, sq, _ = oshape
  lam_init = 0.8 - 0.6 * math.exp(-0.3 * layer)
  const = lambda shape: pl.BlockSpec(shape, lambda bi, h, qi: (0, 0))
  in_specs = [const((SUBLANES, LANES)), const((1, LANES)), qspec, kspec, kspec]
  args = [lam_rows, subln_g, _pass_view(q, lat)[0], k.reshape(nb, sq, D_MODEL),
          v.reshape(nb, sq, D_MODEL)]
  if lat:
    in_specs += [cspec, cspec]
    args += [cache[0].reshape(DEC_BATCH, PAST_LEN, D_MODEL),
             cache[1].reshape(DEC_BATCH, PAST_LEN, D_MODEL)]
  return pl.pallas_call(
      functools.partial(_diff_attn_body, lam_init=lam_init, cached=lat),
      grid=grid, in_specs=in_specs, out_specs=ospec,
      out_shape=jax.ShapeDtypeStruct(oshape, F32),
      scratch_shapes=[pltpu.VMEM((n_keys, tiles * LANES), BF16)] * 2,
      compiler_params=_cparams(("parallel", "parallel", "arbitrary")),
      name="diff_attention",
  )(*args)


def _pair_attn_body(*refs, scale, with_pe, cached):
  refs = list(refs)
  q_ref = refs.pop(0)
  qp_ref = refs.pop(0) if with_pe else None
  k_ref = refs.pop(0)
  kp_ref = refs.pop(0) if with_pe else None
  v_ref = refs.pop(0)
  kc_ref = refs.pop(0) if cached else None
  kpc_ref = refs.pop(0) if (cached and with_pe) else None
  vc_ref = refs.pop(0) if cached else None
  o_ref, kb_s, vb_s = refs

  tiles = q_ref.shape[1] // LANES
  kw = kb_s.shape[1] // tiles
  key_view = lambda tl: kb_s.at[:, tl * kw:(tl + 1) * kw]

  @pl.when(pl.program_id(2) == 0)
  def _():
    for tl in range(tiles):
      _stage_keys(key_view(tl), slice(0, LANES), _tile(k_ref, tl), _tile(kc_ref, tl))
      if with_pe:
        _stage_keys(key_view(tl), slice(LANES, 2 * LANES), kp_ref, kpc_ref)
    _stage_keys(vb_s, slice(None), v_ref, vc_ref)

  lane = lax.broadcasted_iota(jnp.int32, (q_ref.shape[0], LANES), 1)
  halves = (lane < NA_HD, lane >= NA_HD)
  for tl in range(tiles):
    q = _tile(q_ref, tl)[...] * scale
    qs = [jnp.where(hm, q, 0.0).astype(BF16) for hm in halves]
    if with_pe:
      tile_id = tl if tiles > 1 else pl.program_id(1)
      qp = (_tile(qp_ref, tl // 2) if tiles > 1 else qp_ref)[...] * scale
      base = (tile_id % 2) * (2 * MLA_ROPE)
      qs = [jnp.concatenate(
          [qs[a], jnp.where((lane >= base + a * MLA_ROPE) & (lane < base + (a + 1) * MLA_ROPE),
                            qp, 0.0).astype(BF16)], axis=1) for a in range(2)]
    _tile(o_ref, tl)[...] = jnp.where(
        halves[0], *_stacked_attention(qs, key_view(tl), _tile(vb_s, tl)))


def _pair_attention(q, k, v, lat, cache=None, scale=1.0, q_pe=None, k_pe=None, kv_joint=True):
  grid, qspec, kspec, cspec, ospec, n_keys, oshape, tiles = _attn_specs(q, lat, kv_joint)
  view = lambda a: _pass_view(a, lat)[0]
  b0 = _pass_view(q, lat)[1]
  if not kv_joint:
    k, v = (a.reshape(oshape) for a in (k, v))
  kv_view = view if kv_joint else (lambda a: a)
  with_pe = q_pe is not None
  tq, sq = qspec.block_shape[1], kspec.block_shape[1]
  in_specs, args = [qspec], [view(q)]
  if with_pe:
    if tiles == 1:
      in_specs.append(pl.BlockSpec((None, tq, LANES), lambda bi, h, qi: (bi + b0, qi, h // 2)))
    else:
      in_specs.append(pl.BlockSpec((None, tq, q_pe.shape[-1]), lambda bi, h, qi: (bi + b0, qi, 0)))
    args.append(view(q_pe))
  in_specs.append(kspec)
  args.append(kv_view(k))
  if with_pe:
    in_specs.append(pl.BlockSpec((None, sq, LANES), lambda bi, h, qi: (bi + b0, 0, 0)))
    args.append(view(k_pe))
  in_specs.append(kspec)
  args.append(kv_view(v))
  if lat:
    in_specs.append(cspec)
    args.append(cache[0])
    if with_pe:
      in_specs.append(pl.BlockSpec((None, PAST_LEN, LANES), lambda bi, h, qi: (bi, 0, 0)))
      args.append(cache[1])
    in_specs.append(cspec)
    args.append(cache[-1])
  return pl.pallas_call(
      functools.partial(_pair_attn_body, scale=scale, with_pe=with_pe, cached=lat),
      grid=grid, in_specs=in_specs, out_specs=ospec,
      out_shape=jax.ShapeDtypeStruct(oshape, F32),
      scratch_shapes=[pltpu.VMEM((n_keys, tiles * (2 * LANES if with_pe else LANES)), BF16),
                      pltpu.VMEM((n_keys, tiles * LANES), BF16)],
      compiler_params=_cparams(("parallel", "parallel", "arbitrary")),
      name="pair_attention_pe" if with_pe else "pair_attention",
  )(*args)


def _na_lat_body(q_ref, k_ref, v_ref, kc_ref, vc_ref, bias_ref, o_ref, kb_s, vb_s, kcb_s, vcb_s):
  rb = pl.program_id(2)

  @pl.when(rb == 0)
  def _():
    kb_s[...] = k_ref[...].astype(BF16)
    vb_s[...] = v_ref[...].astype(BF16)
    kcb_s[...] = kc_ref[...].astype(BF16)
    vcb_s[...] = vc_ref[...].astype(BF16)

  n_loc = NA_KH * GRID_W
  lane = lax.broadcasted_iota(jnp.int32, (GRID_W, LANES), 1)
  halves = (lane < NA_HD, lane >= NA_HD)
  kc = kcb_s[...]
  vc = vcb_s[...]
  for rr in range(NA_RB):
    r = rb * NA_RB + rr
    start = jnp.clip(r - NA_KH // 2, 0, GRID_R - NA_KH)
    pat = jnp.where(r < NA_KH // 2, 1 + r,
                    jnp.where(r > GRID_R - NA_KH // 2, r - (GRID_R - NA_KH), 0))
    off = pl.multiple_of(start * GRID_W, GRID_W)
    kw = kb_s[pl.ds(off, n_loc), :]
    vw = vb_s[pl.ds(off, n_loc), :]
    q = q_ref[rr * GRID_W:(rr + 1) * GRID_W, :]
    qq = jnp.concatenate([jnp.where(hm, q, 0.0).astype(BF16) for hm in halves], axis=0)
    bias = bias_ref[pat].reshape(2 * GRID_W, n_loc)
    s_loc = lax.dot_general(qq, kw, _NT, preferred_element_type=F32) + bias
    s_ctx = lax.dot_general(qq, kc, _NT, preferred_element_type=F32)
    m = jnp.maximum(jnp.max(s_loc, axis=-1, keepdims=True),
                    jnp.max(s_ctx, axis=-1, keepdims=True))
    p_loc = jnp.exp(s_loc - m)
    p_ctx = jnp.exp(s_ctx - m)
    l = jnp.sum(p_loc, axis=-1, keepdims=True) + jnp.sum(p_ctx, axis=-1, keepdims=True)
    o = (jnp.dot(p_loc.astype(BF16), vw, preferred_element_type=F32)
         + jnp.dot(p_ctx.astype(BF16), vc, preferred_element_type=F32)) / l
    o_ref[rr * GRID_W:(rr + 1) * GRID_W, :] = jnp.where(halves[0], o[:GRID_W], o[GRID_W:])


def _na_lat_attention(q, k, v, kc, vc, bias):
  qv, b0, b = _pass_view(q, True)
  kv, vv = (a.reshape(b, DEC_SEQ, D_MODEL) for a in (k, v))
  n_pat = bias.shape[0]
  blk = NA_RB * GRID_W
  qspec = pl.BlockSpec((None, blk, LANES), lambda bi, h, r: (bi + b0, r, h))
  kspec = pl.BlockSpec((None, DEC_SEQ, LANES), lambda bi, h, r: (bi, 0, h))
  cspec = pl.BlockSpec((None, PAST_LEN, LANES), lambda bi, h, r: (bi, 0, h))
  bspec = pl.BlockSpec((n_pat, 2, GRID_W, NA_KH * GRID_W), lambda bi, h, r: (0, h, 0, 0))
  return pl.pallas_call(
      _na_lat_body,
      grid=(b, D_MODEL // LANES, GRID_R // NA_RB),
      in_specs=[qspec, kspec, kspec, cspec, cspec, bspec],
      out_specs=pl.BlockSpec((None, blk, LANES), lambda bi, h, r: (bi, r, h)),
      out_shape=jax.ShapeDtypeStruct((b, DEC_SEQ, D_MODEL), F32),
      scratch_shapes=[pltpu.VMEM((DEC_SEQ, LANES), BF16)] * 2
      + [pltpu.VMEM((PAST_LEN, LANES), BF16)] * 2,
      compiler_params=_cparams(("parallel", "parallel", "arbitrary")),
      name="na_lat_attention",
  )(qv, kv, vv, kc, vc, bias)


def _sort_network(n):
  pairs = []
  p = 1
  while p < n:
    k = p
    while k >= 1:
      for j in range(k % p, n - k, 2 * k):
        for i in range(min(k, n - j - k)):
          if (i + j) // (2 * p) == (i + j + k) // (2 * p):
            pairs.append((i + j, i + j + k))
      k //= 2
    p *= 2
  return pairs


def _pop_heads(lists, hit, depth):
  for k in range(depth):
    lists[k] = jnp.where(hit, lists[k + 1], lists[k])


def _top16_sorted(s):
  t = s.shape[1]
  n = s.shape[0] // SUBLANES
  xs = [s[SUBLANES * k:SUBLANES * (k + 1), :] for k in range(n)]
  for i, j in _sort_network(n):
    xs[i], xs[j] = jnp.maximum(xs[i], xs[j]), jnp.minimum(xs[i], xs[j])
  rows = lax.broadcasted_iota(jnp.int32, (PEER_TOPK, t), 0)
  v = jnp.zeros((PEER_TOPK, t), F32)
  for r in range(PEER_TOPK):
    m = jnp.max(xs[0], axis=0, keepdims=True)
    v = jnp.where(rows == r, m, v)
    _pop_heads(xs, xs[0] == m, PEER_TOPK - 1 - r)
  return v


def _router_body(oc_ref, ol_ref, wo_ref, x_ref, g1_ref, g_ref, sc_ref, sh_ref,
                 wq_ref, k1_ref, k2_ref,
                 xn_ref, hb_ref, a1_ref, n1_ref, b2_ref, r2_ref, h_s):
  def first_head(o_ref):
    xn = x_ref[...] + g1_ref[...] * jnp.dot(o_ref[...].astype(BF16), wo_ref[...],
                                            preferred_element_type=F32)
    xn_ref[...] = xn
    h = _rms_mod(xn, g_ref[...], sc_ref[...], sh_ref[...]).astype(BF16)
    h_s[...] = h
    hb_ref[...] = h

  is_first = pl.program_id(1) == 0
  is_ctx = pl.program_id(0) < N_CTX // ROUTER_T
  pl.when(is_first & is_ctx)(lambda: first_head(oc_ref))
  pl.when(is_first & jnp.logical_not(is_ctx))(lambda: first_head(ol_ref))

  qh = jnp.dot(h_s[...], wq_ref[...], preferred_element_type=F32)
  for hh in range(ROUTER_HEADS):
    _route_head(qh[:, hh * PEER_DK:(hh + 1) * PEER_DK], k1_ref[hh], k2_ref[hh],
                a1_ref.at[hh], n1_ref.at[hh], b2_ref.at[hh], r2_ref.at[hh])


def _route_head(qh, k1, k2, a1_ref, n1_ref, b2_ref, r2_ref):
  half = PEER_DK // 2
  s1 = lax.dot_general(k1, qh[:, :half].astype(BF16), _NT, preferred_element_type=F32)
  s2 = lax.dot_general(k2, qh[:, half:].astype(BF16), _NT, preferred_element_type=F32)
  s1 = s1 - jnp.max(s1, axis=0, keepdims=True)
  s2 = s2 - jnp.max(s2, axis=0, keepdims=True)
  v1 = _top16_sorted(s1)
  v2 = _top16_sorted(s2)
  cand = [v1[:SUBLANES] + v2[k:k + 1] for k in range(PEER_TOPK)]
  tail = v1[SUBLANES:] + v2[0:1]
  z = jnp.zeros_like(v2[0:1])
  taken = jnp.zeros_like(cand[0])
  for r in range(PEER_TOPK):
    th = jnp.max(jnp.maximum(cand[0], tail), axis=0, keepdims=True)
    z = z + jnp.exp(th)
    tail = jnp.where(tail == th, NEG_INF, tail)
    hit = cand[0] == th
    taken = taken + jnp.where(hit, 1.0, 0.0)
    _pop_heads(cand, hit, PEER_TOPK - 1 - r)
  inf = jnp.float32(jnp.inf)
  first = jnp.minimum(jnp.min(jnp.where(taken >= 1.0, v1[:SUBLANES], inf), axis=0, keepdims=True),
                      jnp.min(jnp.where(tail == NEG_INF, v1[SUBLANES:], inf), axis=0,
                              keepdims=True))
  n1 = jnp.where(s1 >= first, 1.0, 0.0)
  k_max = PEER_TOPK // 2
  for k in range(2, k_max + 1):
    t_k = jnp.min(jnp.where(taken >= float(k), v1[:SUBLANES], inf), axis=0, keepdims=True)
    n1 = n1 + jnp.where(s1 >= t_k, 1.0, 0.0)
  n1 = n1 + jnp.where(s1 >= v1[0:1], jnp.maximum(taken[0:1] - float(k_max), 0.0), 0.0)
  r2 = jnp.zeros_like(s2)
  for r in range(PEER_TOPK):
    r2 = r2 + jnp.where(v2[r:r + 1] > s2, 1.0, 0.0)
  a1_ref[...] = jnp.exp(s1)
  n1_ref[...] = n1
  b2_ref[...] = pltpu.bitcast((jnp.exp(s2) * (1.0 / z)).astype(BF16), jnp.uint32)
  r2_ref[...] = pltpu.bitcast(r2.astype(BF16), jnp.uint32)


def _peer_router(oc, ol, w_o, x, gate1, g, sc, sh, wq, k1, k2):
  n = x.shape[0]
  tm = ROUTER_T
  nb_ctx = N_CTX // tm
  row = lambda w: pl.BlockSpec((tm, w), lambda i, h: (i, 0))
  full = lambda shape: pl.BlockSpec(shape, lambda i, h: (0,) * len(shape))
  modspec = pl.BlockSpec((None, 1, D_MODEL), lambda i, h: (_mod_row(i, tm), 0, 0))
  hs = ROUTER_HEADS
  keyspec = pl.BlockSpec((hs, PEER_NKEYS, PEER_DK // 2), lambda i, h: (h, 0, 0))
  tspec = pl.BlockSpec((hs, PEER_NKEYS, tm), lambda i, h: (h, 0, i))
  pspec = pl.BlockSpec((hs, PEER_NKEYS // 2, tm), lambda i, h: (h, 0, i))
  return pl.pallas_call(
      _router_body,
      grid=(n // tm, PEER_HEADS // hs),
      in_specs=[pl.BlockSpec((tm, D_MODEL), lambda i, h: (jnp.minimum(i, nb_ctx - 1), 0)),
                pl.BlockSpec((tm, D_MODEL), lambda i, h: (jnp.maximum(i - nb_ctx, 0), 0)),
                full(w_o.shape), row(D_MODEL), modspec,
                full((1, D_MODEL)), modspec, modspec,
                pl.BlockSpec((D_MODEL, hs * PEER_DK), lambda i, h: (0, h)), keyspec, keyspec],
      out_specs=[row(D_MODEL), row(D_MODEL), tspec, tspec, pspec, pspec],
      out_shape=[jax.ShapeDtypeStruct((n, D_MODEL), F32), jax.ShapeDtypeStruct((n, D_MODEL), BF16)]
      + [jax.ShapeDtypeStruct((PEER_HEADS, PEER_NKEYS, n), F32)] * 2
      + [jax.ShapeDtypeStruct((PEER_HEADS, PEER_NKEYS // 2, n), jnp.uint32)] * 2,
      scratch_shapes=[pltpu.VMEM((tm, D_MODEL), BF16)],
      compiler_params=_cparams(("parallel", "arbitrary"), VMEM_LIMIT_PEER),
      name="peer_router",
  )(oc.reshape(N_CTX, D_MODEL), ol.reshape(N_LAT, D_MODEL), w_o, x, gate1, g, sc, sh, wq, k1, k2)


def _gelu(x):
  return 0.5 * x * (1.0 + lax.erf(x * np.float32(math.sqrt(0.5))))


def _peer_mix_body(hb_ref, u_ref, vt_ref, a1_ref, n1_ref, b2_ref, r2_ref, x_ref, gate_ref,
                   fg_ref, y_ref, *scratch, final_norm):
  c = pl.program_id(1)
  n_pieces = PEER_EC // PEER_SUBC
  if final_norm:
    yl_ref, acc_s, *piece_s = scratch
  else:
    acc_s, *piece_s = scratch

  @pl.when(c == 0)
  def _():
    acc_s[...] = jnp.zeros_like(acc_s)

  hb = hb_ref[...]
  t = hb.shape[0]
  pk = BF16_ROWS
  for j in range(n_pieces):
    rows = slice(j * PEER_SUBC, (j + 1) * PEER_SUBC)
    piece_s[j][...] = lax.dot_general(u_ref[rows, :], hb, _NT, preferred_element_type=F32)
  for j in range(n_pieces):
    s_s, p_s = piece_s[j], piece_s[n_pieces + j]
    for ii in range(PEER_SUBC // PEER_NKEYS):
      i1 = j * (PEER_SUBC // PEER_NKEYS) + ii
      w = [jnp.zeros((pk, t), BF16) for _ in range(PEER_NKEYS // pk)]
      for h in range(PEER_HEADS):
        a_row = jnp.broadcast_to(a1_ref[h, i1:i1 + 1, :], (pk, t)).astype(BF16)
        n_row = jnp.broadcast_to(n1_ref[h, i1:i1 + 1, :], (pk, t)).astype(BF16)
        for sub in range(PEER_NKEYS // pk):
          words = slice(sub * pk // 2, (sub + 1) * pk // 2)
          prod = a_row * pltpu.bitcast(b2_ref[h, words, :], BF16)
          rank = pltpu.bitcast(r2_ref[h, words, :], BF16)
          w[sub] = w[sub] + jnp.where(rank < n_row, prod, jnp.zeros_like(prod))
      for sub in range(PEER_NKEYS // pk):
        row0 = ii * PEER_NKEYS + sub * pk
        p_s[row0:row0 + pk, :] = w[sub] * _gelu(s_s[row0:row0 + pk, :]).astype(BF16)
    rows = slice(j * PEER_SUBC, (j + 1) * PEER_SUBC)
    acc_s[...] += jnp.dot(vt_ref[:, rows], p_s[...], preferred_element_type=F32)

  @pl.when(c == pl.num_programs(1) - 1)
  def _():
    y = x_ref[...] + gate_ref[...] * acc_s[...].T
    if final_norm:
      _store_by_pass(pl.program_id(0), t, _rms(y, fg_ref[...]), y_ref, yl_ref)
    else:
      y_ref[...] = y


def _peer_mix(hb, u, vt, a1, n1, b2, r2, x, gate, final_g, final_norm):
  n = x.shape[0]
  t = PEER_T
  n_i1 = PEER_EC // PEER_NKEYS
  row = lambda w: pl.BlockSpec((t, w), lambda i, c: (i, 0))
  i1spec = pl.BlockSpec((PEER_HEADS, n_i1, t), lambda i, c: (0, c, i))
  i2spec = pl.BlockSpec((PEER_HEADS, PEER_NKEYS // 2, t), lambda i, c: (0, 0, i))
  return pl.pallas_call(
      functools.partial(_peer_mix_body, final_norm=final_norm),
      grid=(n // t, PEER_EXPERTS // PEER_EC),
      in_specs=[row(D_MODEL),
                pl.BlockSpec((PEER_EC, D_MODEL), lambda i, c: (c, 0)),
                pl.BlockSpec((None, D_MODEL, PEER_EC), lambda i, c: (c, 0, 0)),
                i1spec, i1spec, i2spec, i2spec,
                row(D_MODEL),
                pl.BlockSpec((None, 1, D_MODEL), lambda i, c: (_mod_row(i, t), 0, 0)),
                pl.BlockSpec((1, D_MODEL), lambda i, c: (0, 0))],
      out_specs=_pass_specs(D_MODEL, t) if final_norm else row(D_MODEL),
      out_shape=_pass_shapes(D_MODEL) if final_norm else jax.ShapeDtypeStruct((n, D_MODEL), F32),
      scratch_shapes=[pltpu.VMEM((D_MODEL, t), F32)]
      + [pltpu.VMEM((PEER_SUBC, t), F32)] * (PEER_EC // PEER_SUBC)
      + [pltpu.VMEM((PEER_SUBC, t), BF16)] * (PEER_EC // PEER_SUBC),
      compiler_params=_cparams(("arbitrary", "arbitrary"), VMEM_LIMIT_PEER),
      name="peer_mix",
  )(hb, u, vt, a1, n1, b2, r2, x, gate, final_g)


def _rotated_tiles(w, group):
  k, n = w.shape
  g = w.reshape(k, n // group, 2, group // 2)
  return jnp.concatenate([-g[:, :, 1:], g[:, :, :1]], axis=2).reshape(k, n)


def _rope_table(rot_dims, reps):
  half = rot_dims // 2
  t = jnp.arange(DEC_SEQ)
  inv = ROPE_BASE ** (-jnp.arange(half, dtype=F32) / half)
  parts_c, parts_s = [], []
  for pos in (t // GRID_W, t % GRID_W):
    ang = pos.astype(F32)[:, None] * inv[None, :]
    parts_c += [jnp.cos(ang), jnp.cos(ang)]
    parts_s += [jnp.sin(ang), jnp.sin(ang)]
  cos = jnp.tile(jnp.concatenate(parts_c, axis=1), (1, reps))
  sin = jnp.tile(jnp.concatenate(parts_s, axis=1), (1, reps))
  cos = jnp.concatenate([cos, jnp.ones((TM, LANES), F32)], axis=0)
  sin = jnp.concatenate([sin, jnp.zeros((TM, LANES), F32)], axis=0)
  return cos, sin


def _na_bias_table(rpb):
  reps = [NA_KH // 2] + list(range(NA_KH // 2)) + list(range(GRID_R - NA_KH // 2 + 1, GRID_R))
  cols = np.arange(GRID_W)
  col_start = np.clip(cols - NA_KW // 2, 0, GRID_W - NA_KW)
  col_mask = (cols[None, :] >= col_start[:, None]) & (cols[None, :] < col_start[:, None] + NA_KW)
  dc = np.clip(cols[None, :] - cols[:, None], -(NA_KW - 1), NA_KW - 1) + (NA_KW - 1)
  dr = np.stack([np.clip(r - NA_KH // 2, 0, GRID_R - NA_KH) + np.arange(NA_KH) - r + (NA_KH - 1)
                 for r in reps])
  onehot = (dc[:, :, None] == np.arange(2 * NA_KW - 1)).astype(np.float32)
  bias = jnp.einsum('hpjc,qwc->phqjw', rpb[:, dr, :], onehot, precision=lax.Precision.HIGHEST)
  bias = jnp.where(col_mask[None, None, :, None, :], bias, NEG_INF)
  return bias.reshape(len(reps), NA_HEADS, GRID_W, NA_KH * GRID_W)


def _ctx_rows(a, *shape):
  return a[:N_CTX].reshape(BATCH, SEQ, *shape)


def _diff_mixer(layer, p, x, sc, sh, cache, tables):
  w = p['w_qkv']
  d = D_MODEL
  wq, wk, wv = w[:, :d], w[:, d:2 * d], w[:, 2 * d:]
  rot = lambda m: _rotated_tiles(m, DA_QK // 2)
  w5 = jnp.concatenate([wq, rot(wq), wk, rot(wk), wv], axis=1).astype(BF16)
  cos, sin = tables['da']
  q, (kc, kl), (vc, vl) = _qkv_rope(x, p['norm1_g'], sc, sh, w5, cos, sin, DA_QK ** -0.5)
  lam_rows = jnp.zeros((SUBLANES, LANES), F32)
  for i, nme in enumerate(('lam_q1', 'lam_k1', 'lam_q2', 'lam_k2')):
    lam_rows = lam_rows.at[i, :DA_QK].set(p[nme])
  g = p['subln_g'].reshape(1, DA_V)
  oc = _diff_attention(q, kc, vc, None, lam_rows, g, layer, lat=False)
  ol = _diff_attention(q, kl, vl, cache, lam_rows, g, layer, lat=True)
  state = (kc.reshape(BATCH, SEQ, DA_HEADS, 2 * DA_QK), vc.reshape(BATCH, SEQ, DA_HEADS, DA_V))
  return oc, ol, state


def _na_mixer(p, x, sc, sh, cache):
  d = D_MODEL
  q, (kc, kl), (vc, vl) = _qkv_plain(x, p['norm1_g'], sc, sh, p['w_qkv'].astype(BF16),
                                     NA_HD ** -0.5)
  oc = _pair_attention(q, kc, vc, lat=False, kv_joint=False)
  bias = _na_bias_table(p['rpb'])
  ol = _na_lat_attention(q, kl, vl, cache[0].reshape(DEC_BATCH, PAST_LEN, d),
                         cache[1].reshape(DEC_BATCH, PAST_LEN, d), bias)
  state = (kc.reshape(BATCH, SEQ, NA_HEADS, NA_HD), vc.reshape(BATCH, SEQ, NA_HEADS, NA_HD))
  return oc, ol, state


def _mla_mixer(p, x, sc, sh, cache, tables):
  a, b = MLA_Q_RANK, MLA_Q_RANK + MLA_KV_RANK
  w_in = p['w_in']
  kpe_w = w_in[:, b:]
  w_in_x = jnp.concatenate([w_in[:, :b], jnp.tile(kpe_w, (1, 4)),
                            jnp.tile(_rotated_tiles(kpe_w, MLA_ROPE // 2), (1, 4))],
                           axis=1).astype(BF16)
  cos, sin = tables['mla']
  cq, ckv, kpe = _mla_in(x, p['norm1_g'], sc, sh, w_in_x, p['q_norm_g'].reshape(1, a),
                         p['kv_norm_g'].reshape(1, MLA_KV_RANK), cos, sin)
  w_uq = p['w_uq'].reshape(a, MLA_HEADS, MLA_NOPE + MLA_ROPE)
  w_qn = w_uq[:, :, :MLA_NOPE].reshape(a, -1)
  w_qp = w_uq[:, :, MLA_NOPE:].reshape(a, -1)
  w_uq_x = jnp.concatenate([w_qn, w_qp, _rotated_tiles(w_qp, MLA_ROPE // 2)], axis=1).astype(BF16)
  qn, qp = _mla_q(cq, w_uq_x, cos, sin)
  w_ukv = p['w_ukv'].reshape(MLA_KV_RANK, MLA_HEADS, MLA_NOPE + MLA_V)
  w_ukv_x = jnp.concatenate([w_ukv[:, :, :MLA_NOPE].reshape(MLA_KV_RANK, -1),
                             w_ukv[:, :, MLA_NOPE:].reshape(MLA_KV_RANK, -1)], axis=1).astype(BF16)
  kn, v = _mla_kv(ckv, w_ukv_x)
  knc, vc = _mla_kv(cache[0].reshape(DEC_BATCH * PAST_LEN, MLA_KV_RANK), w_ukv_x)
  cached = (knc.reshape(DEC_BATCH, PAST_LEN, D_MODEL), jnp.tile(cache[1], (1, 1, LANES // MLA_ROPE)),
            vc.reshape(DEC_BATCH, PAST_LEN, D_MODEL))
  scale = (MLA_NOPE + MLA_ROPE) ** -0.5
  oc = _pair_attention(qn, kn, v, lat=False, scale=scale, q_pe=qp, k_pe=kpe)
  ol = _pair_attention(qn, kn, v, lat=True, cache=cached, scale=scale, q_pe=qp, k_pe=kpe)
  state = (_ctx_rows(ckv, MLA_KV_RANK), _ctx_rows(kpe[:, :MLA_ROPE], MLA_ROPE))
  return oc, ol, state


def _layer(layer, p, x, cond8, cache, tables, final_g):
  sh1, sc1, g1, sh2, sc2, g2 = _modulation(cond8, p['ada_w'], p['ada_b'])
  kind = layer % 3
  if kind == 0:
    oc, ol, state = _diff_mixer(layer, p, x, sc1, sh1, cache, tables)
  elif kind == 1:
    oc, ol, state = _na_mixer(p, x, sc1, sh1, cache)
  else:
    oc, ol, state = _mla_mixer(p, x, sc1, sh1, cache, tables)
  x, hb, a1, n1, b2, r2 = _peer_router(oc, ol, p['w_o'].astype(BF16), x, g1, p['norm2_g'],
                                       sc2, sh2, p['peer_wq'], p['peer_k1'], p['peer_k2'])
  x = _peer_mix(hb, p['peer_u'], p['peer_vt'], a1, n1, b2, r2, x, g2, final_g,
                final_norm=layer == DEPTH - 1)
  return x, state


def kernel(x_prompt, x_sample, cache_l0_k, cache_l0_v, cache_l1_k, cache_l1_v, cache_l2_ckv, cache_l2_kpe, cache_l3_k, cache_l3_v, c, c_ctx, l0_norm1_g, l0_norm2_g, l0_ada_w, l0_ada_b, l0_w_qkv, l0_w_o, l0_lam_q1, l0_lam_k1, l0_lam_q2, l0_lam_k2, l0_subln_g, l0_peer_wq, l0_peer_k1, l0_peer_k2, l0_peer_u, l0_peer_v, l1_norm1_g, l1_norm2_g, l1_ada_w, l1_ada_b, l1_w_qkv, l1_w_o, l1_rpb, l1_peer_wq, l1_peer_k1, l1_peer_k2, l1_peer_u, l1_peer_v, l2_norm1_g, l2_norm2_g, l2_ada_w, l2_ada_b, l2_w_in, l2_q_norm_g, l2_w_uq, l2_kv_norm_g, l2_w_ukv, l2_w_o, l2_peer_wq, l2_peer_k1, l2_peer_k2, l2_peer_u, l2_peer_v, l3_norm1_g, l3_norm2_g, l3_ada_w, l3_ada_b, l3_w_qkv, l3_w_o, l3_lam_q1, l3_lam_k1, l3_lam_q2, l3_lam_k2, l3_subln_g, l3_peer_wq, l3_peer_k1, l3_peer_k2, l3_peer_u, l3_peer_v, final_norm_g):
  common = lambda n1, n2, aw, ab, wq, k1, k2, u, v: dict(
      norm1_g=n1.reshape(1, D_MODEL), norm2_g=n2.reshape(1, D_MODEL), ada_w=aw, ada_b=ab,
      peer_wq=wq.astype(BF16), peer_k1=k1.astype(BF16), peer_k2=k2.astype(BF16),
      peer_u=u.astype(BF16),
      peer_vt=v.reshape(PEER_EXPERTS // PEER_EC, PEER_EC, D_MODEL).transpose(0, 2, 1).astype(BF16))
  p0 = dict(common(l0_norm1_g, l0_norm2_g, l0_ada_w, l0_ada_b, l0_peer_wq, l0_peer_k1, l0_peer_k2,
                   l0_peer_u, l0_peer_v),
            w_qkv=l0_w_qkv, w_o=l0_w_o, lam_q1=l0_lam_q1, lam_k1=l0_lam_k1, lam_q2=l0_lam_q2,
            lam_k2=l0_lam_k2, subln_g=l0_subln_g)
  p1 = dict(common(l1_norm1_g, l1_norm2_g, l1_ada_w, l1_ada_b, l1_peer_wq, l1_peer_k1, l1_peer_k2,
                   l1_peer_u, l1_peer_v),
            w_qkv=l1_w_qkv, w_o=l1_w_o, rpb=l1_rpb)
  p2 = dict(common(l2_norm1_g, l2_norm2_g, l2_ada_w, l2_ada_b, l2_peer_wq, l2_peer_k1, l2_peer_k2,
                   l2_peer_u, l2_peer_v),
            w_in=l2_w_in, q_norm_g=l2_q_norm_g, w_uq=l2_w_uq, kv_norm_g=l2_kv_norm_g,
            w_ukv=l2_w_ukv, w_o=l2_w_o)
  p3 = dict(common(l3_norm1_g, l3_norm2_g, l3_ada_w, l3_ada_b, l3_peer_wq, l3_peer_k1, l3_peer_k2,
                   l3_peer_u, l3_peer_v),
            w_qkv=l3_w_qkv, w_o=l3_w_o, lam_q1=l3_lam_q1, lam_k1=l3_lam_k1, lam_q2=l3_lam_q2,
            lam_k2=l3_lam_k2, subln_g=l3_subln_g)
  params = (p0, p1, p2, p3)
  caches = ((cache_l0_k, cache_l0_v), (cache_l1_k, cache_l1_v),
            (cache_l2_ckv, cache_l2_kpe), (cache_l3_k, cache_l3_v))
  tables = dict(da=_rope_table(DA_QK // 2, LANES // DA_QK),
                mla=_rope_table(MLA_ROPE // 2, LANES // MLA_ROPE))

  cond8 = jnp.zeros((N_COND, D_MODEL), F32).at[0].set(c_ctx).at[1:1 + DEC_BATCH].set(c)
  x = jnp.concatenate([x_prompt.reshape(N_CTX, D_MODEL), x_sample.reshape(N_LAT, D_MODEL)], axis=0)
  states = []
  for layer in range(DEPTH):
    x, st = _layer(layer, params[layer], x, cond8, caches[layer], tables,
                   final_norm_g.reshape(1, D_MODEL))
    states.extend(st)
  y_ctx, y_lat = x
  return (y_ctx.reshape(BATCH, SEQ, D_MODEL), y_lat.reshape(DEC_BATCH, DEC_SEQ, D_MODEL), *states)
```
